```python
import jax, jax.numpy as jnp
from jax import lax
import numpy as np

D_MODEL = 2048
BATCH = 32
SEQ = 256
DEPTH = 4
DEC_BATCH = 4
DEC_SEQ = 2048
PAST_LEN = 256

GRID_W = 64
HEAD_DIM = 128
N_HEADS_A = 8
N_KV_A = 2
N_HEADS_B = 8
NOPE_B = 128
ROPE_B = 64
VDIM_B = 128
KV_RANK = 256
D_FF = ((8 * D_MODEL + 3 * 256 - 1) // (3 * 256)) * 256
Q_BLOCK = 128
ROPE_THETA = 10000.0
EPS = 1e-6

W_QA = N_HEADS_A * HEAD_DIM
W_KA = N_KV_A * HEAD_DIM
W_QB = N_HEADS_B * (NOPE_B + ROPE_B)
IN_COLS = W_QA + 2 * W_KA + W_QB + KV_RANK + ROPE_B
D_MIX = N_HEADS_A * HEAD_DIM + N_HEADS_B * VDIM_B

kernel_name = "hybrid_gqa_mla_diffusion_step"


def rmsnorm(x, g):
    xf = x.astype(jnp.float32)
    y = xf * lax.rsqrt(jnp.mean(xf * xf, axis=-1, keepdims=True) + EPS)
    return (y * g.astype(jnp.float32)).astype(x.dtype)


def grid_angles(n_tokens, dim):
    n_rows = n_tokens // GRID_W
    row = jnp.repeat(jnp.arange(n_rows, dtype=jnp.float32), GRID_W)
    col = jnp.tile(jnp.arange(GRID_W, dtype=jnp.float32), n_rows)
    n_freq = dim // 4
    inv = ROPE_THETA ** (-jnp.arange(n_freq, dtype=jnp.float32) / n_freq)
    ang = jnp.concatenate([row[:, None] * inv, col[:, None] * inv], axis=-1)
    return jnp.cos(ang), jnp.sin(ang)


def apply_rope(x, cos, sin):
    xf = x.astype(jnp.float32)
    half = x.shape[-1] // 2
    x1, x2 = xf[..., :half], xf[..., half:]
    cs, sn = cos[None, :, None, :], sin[None, :, None, :]
    return jnp.concatenate([x1 * cs - x2 * sn, x2 * cs + x1 * sn], axis=-1).astype(x.dtype)


def attention(q, k, v):
    B, Lq, H, dq = q.shape
    Hk, dv = k.shape[2], v.shape[-1]
    G = H // Hk
    nb = Lq // Q_BLOCK
    qb = q.reshape(B, nb, Q_BLOCK, Hk, G, dq).transpose(1, 0, 2, 3, 4, 5)
    kf = k.astype(jnp.float32)
    vf = v.astype(jnp.float32)
    scale = dq ** -0.5

    def one_block(qblk):
        s = jnp.einsum('bqhgd,bkhd->bhgqk', qblk.astype(jnp.float32), kf) * scale
        p = jax.nn.softmax(s, axis=-1)
        return jnp.einsum('bhgqk,bkhe->bqhge', p, vf)

    out = lax.map(one_block, qb)
    return out.transpose(1, 0, 2, 3, 4, 5).reshape(B, Lq, H, dv).astype(q.dtype)


def project(h, w_in_l, qnorm_l, knorm_l, kvnorm_l):
    B, L, _ = h.shape
    p = jnp.einsum('bld,de->ble', h, w_in_l)
    o1 = W_QA
    o2 = o1 + W_KA
    o3 = o2 + W_KA
    o4 = o3 + W_QB
    o5 = o4 + KV_RANK
    q_a = rmsnorm(p[..., :o1].reshape(B, L, N_HEADS_A, HEAD_DIM), qnorm_l)
    k_a = rmsnorm(p[..., o1:o2].reshape(B, L, N_KV_A, HEAD_DIM), knorm_l)
    v_a = p[..., o2:o3].reshape(B, L, N_KV_A, HEAD_DIM)
    q_b = p[..., o3:o4].reshape(B, L, N_HEADS_B, NOPE_B + ROPE_B)
    ckv = rmsnorm(p[..., o4:o5], kvnorm_l)
    krope = p[..., o5:]
    return q_a, k_a, v_a, q_b, ckv, krope


def mla_expand(ckv, krope, w_uk_l, w_uv_l):
    B, L, _ = ckv.shape
    k_nope = jnp.einsum('blr,rn->bln', ckv, w_uk_l).reshape(B, L, N_HEADS_B, NOPE_B)
    v = jnp.einsum('blr,rn->bln', ckv, w_uv_l).reshape(B, L, N_HEADS_B, VDIM_B)
    k_r = jnp.broadcast_to(krope[:, :, None, :], (B, L, N_HEADS_B, ROPE_B))
    return jnp.concatenate([k_nope, k_r], axis=-1), v


def mix_out(o_a, o_b, w_o_l):
    B, L = o_a.shape[:2]
    o = jnp.concatenate([o_a.reshape(B, L, -1), o_b.reshape(B, L, -1)], axis=-1)
    return jnp.einsum('ble,ed->bld', o, w_o_l)


def swiglu(h, wg, wu, wd):
    a = jnp.einsum('bld,df->blf', h, wg)
    b = jnp.einsum('bld,df->blf', h, wu)
    return jnp.einsum('blf,fd->bld', jax.nn.silu(a) * b, wd)


def modulate(x, shift, scale):
    return x * (1 + scale) + shift


def setup_inputs(seed: int = 0) -> dict:
    key = jax.random.key(seed)
    ks = jax.random.split(key, 24)
    f32 = jnp.float32
    nrm = lambda k, shape, s: jax.random.normal(k, shape, f32) * s
    return {
        "x_prompt": nrm(ks[0], (BATCH, SEQ, D_MODEL), 1.0),
        "x_sample": nrm(ks[1], (DEC_BATCH, DEC_SEQ, D_MODEL), 1.0),
        "cache_k_a": nrm(ks[2], (DEC_BATCH, DEPTH, PAST_LEN, N_KV_A, HEAD_DIM), 1.0),
        "cache_v_a": nrm(ks[3], (DEC_BATCH, DEPTH, PAST_LEN, N_KV_A, HEAD_DIM), 1.0),
        "cache_ckv_b": nrm(ks[4], (DEC_BATCH, DEPTH, PAST_LEN, KV_RANK), 1.0),
        "cache_krope_b": nrm(ks[5], (DEC_BATCH, DEPTH, PAST_LEN, ROPE_B), 1.0),
        "c": nrm(ks[6], (DEC_BATCH, D_MODEL), 1.0),
        "c_ctx": nrm(ks[7], (D_MODEL,), 1.0),
        "w_ada": nrm(ks[8], (DEPTH, D_MODEL, 6 * D_MODEL), 0.5 * D_MODEL ** -0.5),
        "b_ada": nrm(ks[9], (DEPTH, 6 * D_MODEL), 0.02),
        "norm_attn": 1.0 + nrm(ks[10], (DEPTH, D_MODEL), 0.02),
        "norm_ffn": 1.0 + nrm(ks[11], (DEPTH, D_MODEL), 0.02),
        "w_in": nrm(ks[12], (DEPTH, D_MODEL, IN_COLS), D_MODEL ** -0.5),
        "qnorm_a": 1.0 + nrm(ks[13], (DEPTH, HEAD_DIM), 0.02),
        "knorm_a": 1.0 + nrm(ks[14], (DEPTH, HEAD_DIM), 0.02),
        "kvnorm_b": 1.0 + nrm(ks[15], (DEPTH, KV_RANK), 0.02),
        "w_uk_b": nrm(ks[16], (DEPTH, KV_RANK, N_HEADS_B * NOPE_B), KV_RANK ** -0.5),
        "w_uv_b": nrm(ks[17], (DEPTH, KV_RANK, N_HEADS_B * VDIM_B), KV_RANK ** -0.5),
        "w_o": nrm(ks[18], (DEPTH, D_MIX, D_MODEL), D_MIX ** -0.5),
        "w_gate": nrm(ks[19], (DEPTH, D_MODEL, D_FF), D_MODEL ** -0.5),
        "w_up": nrm(ks[20], (DEPTH, D_MODEL, D_FF), D_MODEL ** -0.5),
        "w_down": nrm(ks[21], (DEPTH, D_FF, D_MODEL), D_FF ** -0.5),
        "norm_final": 1.0 + nrm(ks[22], (D_MODEL,), 0.02),
    }


def reference(x_prompt, x_sample, cache_k_a, cache_v_a, cache_ckv_b, cache_krope_b, c,
              c_ctx, w_ada, b_ada, norm_attn, norm_ffn, w_in, qnorm_a, knorm_a, kvnorm_b,
              w_uk_b, w_uv_b, w_o, w_gate, w_up, w_down, norm_final):
    xp = x_prompt
    xs = x_sample
    cos_a, sin_a = grid_angles(xs.shape[1], HEAD_DIM)
    cos_b, sin_b = grid_angles(xs.shape[1], ROPE_B)
    silu_ctx = jax.nn.silu(c_ctx)
    silu_c = jax.nn.silu(c)
    ks_a, vs_a, ckvs_b, kropes_b = [], [], [], []

    for l in range(DEPTH):
        mc = jnp.einsum('d,de->e', silu_ctx, w_ada[l]) + b_ada[l]
        sh1, sc1, g1, sh2, sc2, g2 = jnp.split(mc, 6, axis=-1)
        h = modulate(rmsnorm(xp, norm_attn[l]), sh1, sc1)
        q_a, k_a, v_a, q_b, ckv, krope = project(h, w_in[l], qnorm_a[l], knorm_a[l], kvnorm_b[l])
        k_b, v_b = mla_expand(ckv, krope, w_uk_b[l], w_uv_b[l])
        o_a = attention(q_a, k_a, v_a)
        o_b = attention(q_b, k_b, v_b)
        xp = xp + g1 * mix_out(o_a, o_b, w_o[l])
        h = modulate(rmsnorm(xp, norm_ffn[l]), sh2, sc2)
        xp = xp + g2 * swiglu(h, w_gate[l], w_up[l], w_down[l])
        ks_a.append(k_a)
        vs_a.append(v_a)
        ckvs_b.append(ckv)
        kropes_b.append(krope)

        ms = (jnp.einsum('bd,de->be', silu_c, w_ada[l]) + b_ada[l])[:, None, :]
        sh1, sc1, g1, sh2, sc2, g2 = jnp.split(ms, 6, axis=-1)
        h = modulate(rmsnorm(xs, norm_attn[l]), sh1, sc1)
        q_a, k_a, v_a, q_b, ckv, krope = project(h, w_in[l], qnorm_a[l], knorm_a[l], kvnorm_b[l])
        q_a = apply_rope(q_a, cos_a, sin_a)
        k_a = apply_rope(k_a, cos_a, sin_a)
        q_b = jnp.concatenate([q_b[..., :NOPE_B], apply_rope(q_b[..., NOPE_B:], cos_b, sin_b)], axis=-1)
        krope = apply_rope(krope[:, :, None, :], cos_b, sin_b)[:, :, 0, :]
        k_a_all = jnp.concatenate([cache_k_a[:, l], k_a], axis=1)
        v_a_all = jnp.concatenate([cache_v_a[:, l], v_a], axis=1)
        ckv_all = jnp.concatenate([cache_ckv_b[:, l], ckv], axis=1)
        krope_all = jnp.concatenate([cache_krope_b[:, l], krope], axis=1)
        k_b, v_b = mla_expand(ckv_all, krope_all, w_uk_b[l], w_uv_b[l])
        o_a = attention(q_a, k_a_all, v_a_all)
        o_b = attention(q_b, k_b, v_b)
        xs = xs + g1 * mix_out(o_a, o_b, w_o[l])
        h = modulate(rmsnorm(xs, norm_ffn[l]), sh2, sc2)
        xs = xs + g2 * swiglu(h, w_gate[l], w_up[l], w_down[l])

    y_prompt = rmsnorm(xp, norm_final)
    y_sample = rmsnorm(xs, norm_final)
    new_k_a = jnp.stack(ks_a, axis=1)
    new_v_a = jnp.stack(vs_a, axis=1)
    new_ckv_b = jnp.stack(ckvs_b, axis=1)
    new_krope_b = jnp.stack(kropes_b, axis=1)
    return (y_prompt, y_sample, new_k_a, new_v_a, new_ckv_b, new_krope_b)
```

```python
import functools

import jax
import jax.numpy as jnp
import numpy as np
from jax import lax
from jax.experimental import pallas as pl
from jax.experimental.pallas import tpu as pltpu

D_MODEL = 2048
DEPTH = 4
GRID_W = 64
HEAD_DIM = 128
N_HEADS_A = 8
N_KV_A = 2
GROUP_A = N_HEADS_A // N_KV_A
N_HEADS_B = 8
NOPE_B = 128
ROPE_B = 64
VDIM_B = 128
KV_RANK = 256
D_FF = 5632
ROPE_THETA = 10000.0
EPS = 1e-6

W_QA = N_HEADS_A * HEAD_DIM
W_KA = N_KV_A * HEAD_DIM
W_QB = N_HEADS_B * (NOPE_B + ROPE_B)
QB_PAD = 256
KR_PAD = 128
W_QB_PAD = N_HEADS_B * QB_PAD
W_VB = N_HEADS_B * VDIM_B
N_MOD = 6
MOD_ROWS = 8

C_QA = 0
C_KA = C_QA + W_QA
C_VA = C_KA + W_KA
C_QB = C_VA + W_KA
C_CKV = C_QB + W_QB_PAD
C_KR = C_CKV + KV_RANK
IN_COLS_PAD = C_KR + KR_PAD

SCALE_A = HEAD_DIM ** -0.5
SCALE_B = (NOPE_B + ROPE_B) ** -0.5

VMEM_LIMIT = 56 * 1024 * 1024

BF16 = jnp.bfloat16
F32 = jnp.float32


def _cparams(n_axes):
    return pltpu.CompilerParams(
        dimension_semantics=("arbitrary",) * n_axes, vmem_limit_bytes=VMEM_LIMIT)


def _const_spec(shape):
    nd = len(shape)
    return pl.BlockSpec(shape, lambda *_: (0,) * nd, pipeline_mode=pl.Buffered(1))


def _dot(a, b):
    return jnp.dot(a, b, preferred_element_type=F32)


def _dot_t(a, b):
    return lax.dot_general(a, b, (((1,), (1,)), ((), ())), preferred_element_type=F32)


def _rms(x):
    return x * lax.rsqrt(jnp.mean(x * x, axis=-1, keepdims=True) + EPS)


ADA_TN = 1024


def _adaln_kernel(s_ref, w_ref, b_ref, o_ref):
    s = s_ref[...]
    s = s * jax.nn.sigmoid(s)
    o_ref[...] = _dot(s.astype(BF16), w_ref[...].astype(BF16)) + b_ref[...]


def _adaln(s, w_ada, b_ada):
    n = 6 * D_MODEL
    return pl.pallas_call(
        _adaln_kernel,
        grid=(DEPTH, n // ADA_TN),
        in_specs=[
            pl.BlockSpec((MOD_ROWS, D_MODEL), lambda l, j: (0, 0)),
            pl.BlockSpec((None, D_MODEL, ADA_TN), lambda l, j: (l, 0, j)),
            pl.BlockSpec((None, 1, ADA_TN), lambda l, j: (l, 0, j)),
        ],
        out_specs=pl.BlockSpec((None, MOD_ROWS, ADA_TN), lambda l, j: (l, 0, j)),
        out_shape=jax.ShapeDtypeStruct((DEPTH, MOD_ROWS, n), F32),
        compiler_params=_cparams(2),
        name="adaln",
    )(s, w_ada, b_ada.reshape(DEPTH, 1, n))


def _cache_kernel(ka_ref, va_ref, ckv_ref, kr_ref, wuk_ref, wuv_ref,
                  ka_o, va_o, kb_o, vb_o):
    ka_o[...] = ka_ref[...].astype(BF16)
    va_o[...] = va_ref[...].astype(BF16)
    ckv = ckv_ref[...].astype(BF16)
    kn = _dot(ckv, wuk_ref[...])
    vb_o[...] = _dot(ckv, wuv_ref[...]).astype(BF16)
    kr = kr_ref[...].astype(BF16)
    for h in range(N_HEADS_B):
        kb_o[:, h * QB_PAD:h * QB_PAD + NOPE_B] = kn[:, h * NOPE_B:(h + 1) * NOPE_B].astype(BF16)
        kb_o[:, h * QB_PAD + NOPE_B:(h + 1) * QB_PAD] = kr


def _cache_prep(cache_k_a, cache_v_a, cache_ckv_b, krope_pad, wuk, wuv):
    nb, _, past = cache_ckv_b.shape[:3]
    ka = cache_k_a.reshape(nb, DEPTH, past, W_KA)
    va = cache_v_a.reshape(nb, DEPTH, past, W_KA)

    def in4(width):
        return pl.BlockSpec((None, None, past, width), lambda l, b: (b, l, 0, 0))

    def out4(width):
        return pl.BlockSpec((None, None, past, width), lambda l, b: (l, b, 0, 0))

    def w3(width):
        return pl.BlockSpec((None, KV_RANK, width), lambda l, b: (l, 0, 0))

    return pl.pallas_call(
        _cache_kernel,
        grid=(DEPTH, nb),
        in_specs=[in4(W_KA), in4(W_KA), in4(KV_RANK), in4(KR_PAD),
                  w3(N_HEADS_B * NOPE_B), w3(W_VB)],
        out_specs=[out4(W_KA), out4(W_KA), out4(W_QB_PAD), out4(W_VB)],
        out_shape=[
            jax.ShapeDtypeStruct((DEPTH, nb, past, W_KA), BF16),
            jax.ShapeDtypeStruct((DEPTH, nb, past, W_KA), BF16),
            jax.ShapeDtypeStruct((DEPTH, nb, past, W_QB_PAD), BF16),
            jax.ShapeDtypeStruct((DEPTH, nb, past, W_VB), BF16),
        ],
        compiler_params=_cparams(2),
        name="cache_prep",
    )(ka, va, cache_ckv_b, krope_pad, wuk, wuv)


PROJ_TM = 256


def _proj_kernel(*refs, rope, emit_new):
    it = iter(refs)
    x_ref, mod_ref, g_ref, w_ref, qn_ref, kn_ref, kvn_ref, wuk_ref, wuv_ref = (
        next(it) for _ in range(9))
    if rope:
        ca_ref, sa_ref, cb_ref, sb_ref = (next(it) for _ in range(4))
    qa_o, ka_o, va_o, qb_o, kb_o, vb_o = (next(it) for _ in range(6))
    if emit_new:
        nk_o, nv_o, nckv_o, nkr_o = (next(it) for _ in range(4))

    x = x_ref[...]
    h = _rms(x) * g_ref[...]
    h = h * (1.0 + mod_ref[1:2, :]) + mod_ref[0:1, :]
    hb = h.astype(BF16)

    def rope_a(t):
        if not rope:
            return t
        return t * ca_ref[...] + pltpu.roll(t, HEAD_DIM // 2, 1) * sa_ref[...]

    def rope_b(t):
        if not rope:
            return t
        return t * cb_ref[...] + pltpu.roll(t, KR_PAD // 2, 1) * sb_ref[...]

    pq = _dot(hb, w_ref[:, C_QA:C_QA + W_QA])
    qn = qn_ref[...]
    for i in range(N_HEADS_A):
        sl = slice(i * HEAD_DIM, (i + 1) * HEAD_DIM)
        t = rope_a(_rms(pq[:, sl]) * qn)
        qa_o[:, sl] = (t * SCALE_A).astype(BF16)

    pk = _dot(hb, w_ref[:, C_KA:C_KA + W_KA])
    pv = _dot(hb, w_ref[:, C_VA:C_VA + W_KA])
    kn = kn_ref[...]
    for i in range(N_KV_A):
        sl = slice(i * HEAD_DIM, (i + 1) * HEAD_DIM)
        t = _rms(pk[:, sl]) * kn
        if emit_new:
            nk_o[:, sl] = t
        ka_o[:, sl] = rope_a(t).astype(BF16)
    va_o[...] = pv.astype(BF16)
    if emit_new:
        nv_o[...] = pv

    pqb = _dot(hb, w_ref[:, C_QB:C_QB + W_QB_PAD])
    for i in range(N_HEADS_B):
        s0 = i * QB_PAD
        qb_o[:, s0:s0 + NOPE_B] = (pqb[:, s0:s0 + NOPE_B] * SCALE_B).astype(BF16)
        t = rope_b(pqb[:, s0 + NOPE_B:s0 + QB_PAD])
        qb_o[:, s0 + NOPE_B:s0 + QB_PAD] = (t * SCALE_B).astype(BF16)

    pc = _dot(hb, w_ref[:, C_CKV:C_CKV + KV_RANK])
    ckv = _rms(pc) * kvn_ref[...]
    pkr = _dot(hb, w_ref[:, C_KR:C_KR + KR_PAD])
    if emit_new:
        nckv_o[...] = ckv
        nkr_o[...] = pkr
    ckvb = ckv.astype(BF16)
    krb = rope_b(pkr).astype(BF16)
    kno = _dot(ckvb, wuk_ref[...])
    vb_o[...] = _dot(ckvb, wuv_ref[...]).astype(BF16)
    for i in range(N_HEADS_B):
        s0 = i * QB_PAD
        kb_o[:, s0:s0 + NOPE_B] = kno[:, i * NOPE_B:(i + 1) * NOPE_B].astype(BF16)
        kb_o[:, s0 + NOPE_B:s0 + QB_PAD] = krb


def _proj(x, mods_l, mod_map, g, w_in, qn, kn, kvn, wuk, wuv, rope_tabs, emit_new, tm=PROJ_TM):
    t = x.shape[0]
    rope = rope_tabs is not None

    def row(width):
        return pl.BlockSpec((tm, width), lambda i: (i, 0))

    in_specs = [
        row(D_MODEL),
        pl.BlockSpec((None, N_MOD, D_MODEL), lambda i: (mod_map(i), 0, 0)),
        _const_spec((1, D_MODEL)),
        _const_spec((D_MODEL, IN_COLS_PAD)),
        _const_spec((1, HEAD_DIM)),
        _const_spec((1, HEAD_DIM)),
        _const_spec((1, KV_RANK)),
        _const_spec((KV_RANK, N_HEADS_B * NOPE_B)),
        _const_spec((KV_RANK, W_VB)),
    ]
    args = [x, mods_l, g, w_in, qn, kn, kvn, wuk, wuv]
    if rope:
        seq_tiles = rope_tabs[0].shape[0] // tm
        for tab in rope_tabs:
            in_specs.append(pl.BlockSpec((tm, tab.shape[1]), lambda i: (i % seq_tiles, 0)))
            args.append(tab)
    widths = [W_QA, W_KA, W_KA, W_QB_PAD, W_QB_PAD, W_VB]
    out_specs = [row(w) for w in widths]
    out_shape = [jax.ShapeDtypeStruct((t, w), BF16) for w in widths]
    if emit_new:
        nw = [W_KA, W_KA, KV_RANK, KR_PAD]
        out_specs += [row(w) for w in nw]
        out_shape += [jax.ShapeDtypeStruct((t, w), F32) for w in nw]
    return pl.pallas_call(
        functools.partial(_proj_kernel, rope=rope, emit_new=emit_new),
        grid=(t // tm,),
        in_specs=in_specs,
        out_specs=out_specs,
        out_shape=out_shape,
        compiler_params=_cparams(1),
        name="proj_lat" if rope else "proj_ctx",
    )(*args)


def _softmax_pv(s_parts, v_parts):
    m = s_parts[0].max(axis=-1, keepdims=True)
    for s in s_parts[1:]:
        m = jnp.maximum(m, s.max(axis=-1, keepdims=True))
    den = None
    acc = None
    for s, v in zip(s_parts, v_parts):
        p = jnp.exp(s - m)
        d = p.sum(axis=-1, keepdims=True)
        o = _dot(p.astype(BF16), v)
        den = d if den is None else den + d
        acc = o if acc is None else acc + o
    return acc / den


def _attn_ctx_kernel(qa_ref, ka_ref, va_ref, qb_ref, kb_ref, vb_ref, oa_ref, ob_ref):
    for hq in range(N_HEADS_A):
        g = hq // GROUP_A
        ksl = slice(g * HEAD_DIM, (g + 1) * HEAD_DIM)
        qsl = slice(hq * HEAD_DIM, (hq + 1) * HEAD_DIM)
        s = _dot_t(qa_ref[:, qsl], ka_ref[:, ksl])
        oa_ref[:, qsl] = _softmax_pv([s], [va_ref[:, ksl]]).astype(BF16)
    for hq in range(N_HEADS_B):
        qsl = slice(hq * QB_PAD, (hq + 1) * QB_PAD)
        vsl = slice(hq * VDIM_B, (hq + 1) * VDIM_B)
        s = _dot_t(qb_ref[:, qsl], kb_ref[:, qsl])
        ob_ref[:, vsl] = _softmax_pv([s], [vb_ref[:, vsl]]).astype(BF16)


def _attn_ctx(qa, ka, va, qb, kb, vb, seq):
    t = qa.shape[0]

    def row(width):
        return pl.BlockSpec((seq, width), lambda b: (b, 0))

    return pl.pallas_call(
        _attn_ctx_kernel,
        grid=(t // seq,),
        in_specs=[row(W_QA), row(W_KA), row(W_KA), row(W_QB_PAD), row(W_QB_PAD), row(W_VB)],
        out_specs=[row(W_QA), row(W_VB)],
        out_shape=[jax.ShapeDtypeStruct((t, W_QA), BF16), jax.ShapeDtypeStruct((t, W_VB), BF16)],
        compiler_params=_cparams(1),
        name="attn_ctx",
    )(qa, ka, va, qb, kb, vb)


ATTN_TQ = 512


def _attn_lat_kernel(q_ref, kc_ref, kn_ref, vc_ref, vn_ref, o_ref, *, n_q, dq, dv):
    kc = kc_ref[...]
    kn = kn_ref[...]
    vc = vc_ref[...]
    vn = vn_ref[...]
    for j in range(n_q):
        q = q_ref[:, j * dq:(j + 1) * dq]
        s = [_dot_t(q, kc), _dot_t(q, kn)]
        o_ref[:, j * dv:(j + 1) * dv] = _softmax_pv(s, [vc, vn]).astype(BF16)


def _attn_lat(q, kc, kn, vc, vn, n_units, n_q, dq, dk, dv, name, tq=ATTN_TQ):
    nb, seq, _ = q.shape
    past = kc.shape[1]
    assert dq == dk
    return pl.pallas_call(
        functools.partial(_attn_lat_kernel, n_q=n_q, dq=dq, dv=dv),
        grid=(nb, n_units, seq // tq),
        in_specs=[
            pl.BlockSpec((None, tq, n_q * dq), lambda b, u, i: (b, i, u)),
            pl.BlockSpec((None, past, dk), lambda b, u, i: (b, 0, u)),
            pl.BlockSpec((None, seq, dk), lambda b, u, i: (b, 0, u)),
            pl.BlockSpec((None, past, dv), lambda b, u, i: (b, 0, u)),
            pl.BlockSpec((None, seq, dv), lambda b, u, i: (b, 0, u)),
        ],
        out_specs=pl.BlockSpec((None, tq, n_q * dv), lambda b, u, i: (b, i, u)),
        out_shape=jax.ShapeDtypeStruct((nb, seq, n_units * n_q * dv), BF16),
        compiler_params=_cparams(3),
        name=name,
    )(q, kc, kn, vc, vn)


OUT_TM = 512


def _outproj_kernel(x_ref, oa_ref, ob_ref, mod_ref, w_ref, y_ref):
    y = _dot(oa_ref[...], w_ref[0:W_QA, :]) + _dot(ob_ref[...], w_ref[W_QA:, :])
    y_ref[...] = x_ref[...] + mod_ref[2:3, :] * y


def _outproj(x, oa, ob, mods_l, mod_map, w_o, tm=OUT_TM):
    t = x.shape[0]
    return pl.pallas_call(
        _outproj_kernel,
        grid=(t // tm,),
        in_specs=[
            pl.BlockSpec((tm, D_MODEL), lambda i: (i, 0)),
            pl.BlockSpec((tm, W_QA), lambda i: (i, 0)),
            pl.BlockSpec((tm, W_VB), lambda i: (i, 0)),
            pl.BlockSpec((None, N_MOD, D_MODEL), lambda i: (mod_map(i), 0, 0)),
            _const_spec((W_QA + W_VB, D_MODEL)),
        ],
        out_specs=pl.BlockSpec((tm, D_MODEL), lambda i: (i, 0)),
        out_shape=jax.ShapeDtypeStruct((t, D_MODEL), F32),
        compiler_params=_cparams(1),
        name="outproj",
    )(x, oa, ob, mods_l, w_o)


FFN_TM = 512
FFN_TF = 512


def _ffn_kernel(x_ref, mod_ref, g_ref, wg_ref, wu_ref, wd_ref, y_ref, h_ref, acc_ref):
    j = pl.program_id(1)

    @pl.when(j == 0)
    def _():
        h = _rms(x_ref[...]) * g_ref[...]
        h = h * (1.0 + mod_ref[4:5, :]) + mod_ref[3:4, :]
        h_ref[...] = h.astype(BF16)

    hb = h_ref[...]
    a = _dot(hb, wg_ref[...])
    b = _dot(hb, wu_ref[...])
    t = (a * jax.nn.sigmoid(a) * b).astype(BF16)
    d = _dot(t, wd_ref[...])

    @pl.when(j == 0)
    def _():
        acc_ref[...] = d

    @pl.when(j > 0)
    def _():
        acc_ref[...] += d

    @pl.when(j == pl.num_programs(1) - 1)
    def _():
        y_ref[...] = x_ref[...] + mod_ref[5:6, :] * acc_ref[...]


def _ffn(x, mods_l, mod_map, g, wg, wu, wd, tm=FFN_TM, tf=FFN_TF):
    t = x.shape[0]
    return pl.pallas_call(
        _ffn_kernel,
        grid=(t // tm, D_FF // tf),
        in_specs=[
            pl.BlockSpec((tm, D_MODEL), lambda i, j: (i, 0)),
            pl.BlockSpec((None, N_MOD, D_MODEL), lambda i, j: (mod_map(i), 0, 0)),
            pl.BlockSpec((1, D_MODEL), lambda i, j: (0, 0)),
            pl.BlockSpec((D_MODEL, tf), lambda i, j: (0, j)),
            pl.BlockSpec((D_MODEL, tf), lambda i, j: (0, j)),
            pl.BlockSpec((tf, D_MODEL), lambda i, j: (j, 0)),
        ],
        out_specs=pl.BlockSpec((tm, D_MODEL), lambda i, j: (i, 0)),
        out_shape=jax.ShapeDtypeStruct((t, D_MODEL), F32),
        scratch_shapes=[pltpu.VMEM((tm, D_MODEL), BF16), pltpu.VMEM((tm, D_MODEL), F32)],
        compiler_params=_cparams(2),
        name="ffn",
    )(x, mods_l, g, wg, wu, wd)


NORM_TM = 512


def _final_norm_kernel(x_ref, g_ref, y_ref):
    y_ref[...] = _rms(x_ref[...]) * g_ref[...]


def _final_norm(x, g, tm=NORM_TM):
    t = x.shape[0]
    return pl.pallas_call(
        _final_norm_kernel,
        grid=(t // tm,),
        in_specs=[pl.BlockSpec((tm, D_MODEL), lambda i: (i, 0)),
                  pl.BlockSpec((1, D_MODEL), lambda i: (0, 0))],
        out_specs=pl.BlockSpec((tm, D_MODEL), lambda i: (i, 0)),
        out_shape=jax.ShapeDtypeStruct((t, D_MODEL), F32),
        compiler_params=_cparams(1),
        name="final_norm",
    )(x, g)


def _pad_rope_cols(w):
    half = ROPE_B // 2
    z = jnp.zeros(w.shape[:-1] + (half,), w.dtype)
    return jnp.concatenate([w[..., :half], z, w[..., half:], z], axis=-1)


def _relayout_w_in(w_in):
    o1 = W_QA + 2 * W_KA
    o2 = o1 + W_QB
    o3 = o2 + KV_RANK
    qb = w_in[..., o1:o2].reshape(DEPTH, D_MODEL, N_HEADS_B, NOPE_B + ROPE_B)
    qb = jnp.concatenate([qb[..., :NOPE_B], _pad_rope_cols(qb[..., NOPE_B:])], axis=-1)
    qb = qb.reshape(DEPTH, D_MODEL, W_QB_PAD)
    return jnp.concatenate(
        [w_in[..., :o1], qb, w_in[..., o2:o3], _pad_rope_cols(w_in[..., o3:])], axis=-1).astype(BF16)


def _rope_tables(n_tokens):
    n_rows = n_tokens // GRID_W
    row = jnp.repeat(jnp.arange(n_rows, dtype=F32), GRID_W)
    col = jnp.tile(jnp.arange(GRID_W, dtype=F32), n_rows)

    def cs(dim):
        n_freq = dim // 4
        inv = ROPE_THETA ** (-jnp.arange(n_freq, dtype=F32) / n_freq)
        ang = jnp.concatenate([row[:, None] * inv, col[:, None] * inv], axis=-1)
        return jnp.cos(ang), jnp.sin(ang)

    cos_a, sin_a = cs(HEAD_DIM)
    cos_b, sin_b = cs(ROPE_B)
    one = jnp.ones_like(cos_b)
    zero = jnp.zeros_like(sin_b)
    return (jnp.concatenate([cos_a, cos_a], axis=-1),
            jnp.concatenate([-sin_a, sin_a], axis=-1),
            jnp.concatenate([cos_b, one, cos_b, one], axis=-1),
            jnp.concatenate([-sin_b, zero, sin_b, zero], axis=-1))


def _unpad_rope_cols(x):
    half = ROPE_B // 2
    return jnp.concatenate([x[..., :half], x[..., 2 * half:3 * half]], axis=-1)


def kernel(x_prompt, x_sample, cache_k_a, cache_v_a, cache_ckv_b, cache_krope_b, c, c_ctx, w_ada, b_ada, norm_attn, norm_ffn, w_in, qnorm_a, knorm_a, kvnorm_b, w_uk_b, w_uv_b, w_o, w_gate, w_up, w_down, norm_final):
    batch, seq, _ = x_prompt.shape
    dec_batch, dec_seq, _ = x_sample.shape
    assert dec_batch + 1 <= MOD_ROWS

    w_in_b = _relayout_w_in(w_in)
    wuk_b = w_uk_b.astype(BF16)
    wuv_b = w_uv_b.astype(BF16)
    w_o_b = w_o.astype(BF16)
    wg_b = w_gate.astype(BF16)
    wu_b = w_up.astype(BF16)
    wd_b = w_down.astype(BF16)
    rope_tabs = _rope_tables(dec_seq)

    cond = jnp.concatenate(
        [c_ctx[None, :], c, jnp.zeros((MOD_ROWS - 1 - dec_batch, D_MODEL), F32)], axis=0)
    mods = _adaln(cond, w_ada, b_ada).reshape(DEPTH, MOD_ROWS, N_MOD, D_MODEL)

    kc_a, vc_a, kc_b, vc_b = _cache_prep(
        cache_k_a, cache_v_a, cache_ckv_b, _pad_rope_cols(cache_krope_b), wuk_b, wuv_b)

    xp = x_prompt.reshape(batch * seq, D_MODEL)
    xs = x_sample.reshape(dec_batch * dec_seq, D_MODEL)
    ctx_mod = lambda i: 0
    new_k, new_v, new_ckv, new_kr = [], [], [], []

    for l in range(DEPTH):
        g1 = norm_attn[l].reshape(1, D_MODEL)
        g2 = norm_ffn[l].reshape(1, D_MODEL)
        qn = qnorm_a[l].reshape(1, HEAD_DIM)
        kn = knorm_a[l].reshape(1, HEAD_DIM)
        kvn = kvnorm_b[l].reshape(1, KV_RANK)

        qa, ka, va, qb, kb, vb, nk, nv, nckv, nkr = _proj(
            xp, mods[l], ctx_mod, g1, w_in_b[l], qn, kn, kvn, wuk_b[l], wuv_b[l], None, True)
        new_k.append(nk)
        new_v.append(nv)
        new_ckv.append(nckv)
        new_kr.append(nkr)
        oa, ob = _attn_ctx(qa, ka, va, qb, kb, vb, seq)
        xp = _outproj(xp, oa, ob, mods[l], ctx_mod, w_o_b[l])
        xp = _ffn(xp, mods[l], ctx_mod, g2, wg_b[l], wu_b[l], wd_b[l])

        def lat_mod(tm):
            return lambda i: 1 + i // (dec_seq // tm)

        qa, ka, va, qb, kb, vb = _proj(
            xs, mods[l], lat_mod(PROJ_TM), g1, w_in_b[l], qn, kn, kvn, wuk_b[l], wuv_b[l],
            rope_tabs, False)
        r3 = lambda a: a.reshape(dec_batch, dec_seq, a.shape[-1])
        oa = _attn_lat(r3(qa), kc_a[l], r3(ka), vc_a[l], r3(va),
                       N_KV_A, GROUP_A, HEAD_DIM, HEAD_DIM, HEAD_DIM, "attn_lat_a")
        ob = _attn_lat(r3(qb), kc_b[l], r3(kb), vc_b[l], r3(vb),
                       N_HEADS_B, 1, QB_PAD, QB_PAD, VDIM_B, "attn_lat_b")
        xs = _outproj(xs, oa.reshape(-1, W_QA), ob.reshape(-1, W_VB), mods[l], lat_mod(OUT_TM), w_o_b[l])
        xs = _ffn(xs, mods[l], lat_mod(FFN_TM), g2, wg_b[l], wu_b[l], wd_b[l])

    gf = norm_final.reshape(1, D_MODEL)
    y_prompt = _final_norm(xp, gf).reshape(batch, seq, D_MODEL)
    y_sample = _final_norm(xs, gf).reshape(dec_batch, dec_seq, D_MODEL)

    def stack(parts, tail):
        return jnp.stack([p.reshape((batch, seq) + tail) for p in parts], axis=1)

    new_k_a = stack(new_k, (N_KV_A, HEAD_DIM))
    new_v_a = stack(new_v, (N_KV_A, HEAD_DIM))
    new_ckv_b = stack(new_ckv, (KV_RANK,))
    new_krope_b = stack([_unpad_rope_cols(p) for p in new_kr], (ROPE_B,))
    return (y_prompt, y_sample, new_k_a, new_v_a, new_ckv_b, new_krope_b)
```

```python
import functools

import jax
import jax.numpy as jnp
import numpy as np
from jax import lax
from jax.experimental import pallas as pl
from jax.experimental.pallas import tpu as pltpu

D_MODEL = 2048
DEPTH = 4
GRID_W = 64
HEAD_DIM = 128
N_HEADS_A = 8
N_KV_A = 2
GROUP_A = N_HEADS_A // N_KV_A
N_HEADS_B = 8
NOPE_B = 128
ROPE_B = 64
VDIM_B = 128
KV_RANK = 256
D_FF = 5632
ROPE_THETA = 10000.0
EPS = 1e-6

W_QA = N_HEADS_A * HEAD_DIM
W_KA = N_KV_A * HEAD_DIM
QB_DIM = NOPE_B + ROPE_B
W_QB = N_HEADS_B * QB_DIM
QB_PAD = 256
KR_PAD = 128
W_QB_PAD = N_HEADS_B * QB_PAD
W_KNOPE = N_HEADS_B * NOPE_B
W_VB = N_HEADS_B * VDIM_B
N_MOD = 6
MOD_ROWS = 8

O_QB = W_QA + 2 * W_KA
O_CKV = O_QB + W_QB
O_KR = O_CKV + KV_RANK
IN_COLS = O_KR + ROPE_B

C_QA = 0
C_KA = C_QA + W_QA
C_VA = C_KA + W_KA
C_QB = C_VA + W_KA
C_CKV = C_QB + W_QB_PAD
C_KR = C_CKV + KV_RANK
IN_COLS_PAD = C_KR + KR_PAD

SCALE_A = HEAD_DIM ** -0.5
SCALE_B = QB_DIM ** -0.5

VMEM_LIMIT = 56 * 1024 * 1024

BF16 = jnp.bfloat16
F32 = jnp.float32


def _cparams(n_axes):
    return pltpu.CompilerParams(
        dimension_semantics=("arbitrary",) * n_axes, vmem_limit_bytes=VMEM_LIMIT)


def _const_spec(shape):
    nd = len(shape)
    return pl.BlockSpec(shape, lambda *_: (0,) * nd, pipeline_mode=pl.Buffered(1))


def _dot(a, b):
    return jnp.dot(a, b, preferred_element_type=F32)


def _dot_t(a, b):
    return lax.dot_general(a, b, (((1,), (1,)), ((), ())), preferred_element_type=F32)


def _rms(x):
    return x * lax.rsqrt(jnp.mean(x * x, axis=-1, keepdims=True) + EPS)


ADA_TN = 1024


def _adaln_kernel(s_ref, w_ref, b_ref, o_ref):
    s = s_ref[...]
    s = s * jax.nn.sigmoid(s)
    o_ref[...] = _dot(s.astype(BF16), w_ref[...].astype(BF16)) + b_ref[...]


def _adaln(s, w_ada, b_ada):
    n = N_MOD * D_MODEL
    return pl.pallas_call(
        _adaln_kernel,
        grid=(DEPTH, n // ADA_TN),
        in_specs=[
            pl.BlockSpec((MOD_ROWS, D_MODEL), lambda l, j: (0, 0)),
            pl.BlockSpec((None, D_MODEL, ADA_TN), lambda l, j: (l, 0, j)),
            pl.BlockSpec((None, 1, ADA_TN), lambda l, j: (l, 0, j)),
        ],
        out_specs=pl.BlockSpec((None, MOD_ROWS, ADA_TN), lambda l, j: (l, 0, j)),
        out_shape=jax.ShapeDtypeStruct((DEPTH, MOD_ROWS, n), F32),
        compiler_params=_cparams(2),
        name="adaln",
    )(s, w_ada, b_ada.reshape(DEPTH, 1, n))


WPREP_TR = 512
PAIR_IN = 2 * QB_DIM
PAIR_OUT = 2 * QB_PAD


def _wprep_kernel(w_ref, pq_ref, pk_ref, o_ref):
    o_ref[:, 0:O_QB] = w_ref[:, 0:O_QB].astype(BF16)
    pq = pq_ref[...]
    for i in range(N_HEADS_B // 2):
        src = w_ref[:, O_QB + i * PAIR_IN:O_QB + (i + 1) * PAIR_IN].astype(BF16)
        o_ref[:, C_QB + i * PAIR_OUT:C_QB + (i + 1) * PAIR_OUT] = _dot(src, pq).astype(BF16)
    o_ref[:, C_CKV:C_CKV + KV_RANK] = w_ref[:, O_CKV:O_CKV + KV_RANK].astype(BF16)
    kr = w_ref[:, O_KR:O_KR + ROPE_B].astype(BF16)
    o_ref[:, C_KR:C_KR + KR_PAD] = _dot(kr, pk_ref[...]).astype(BF16)


def _rope_pad_src():
    half = ROPE_B // 2
    src = np.full((KR_PAD,), -1, np.int32)
    src[0:half] = np.arange(half)
    src[2 * half:3 * half] = half + np.arange(half)
    return src


def _selection(src, n_in):
    return (jnp.arange(n_in, dtype=jnp.int32)[:, None] == jnp.asarray(src)[None, :]).astype(BF16)


def _wprep(w_in):
    rsrc = _rope_pad_src()
    head = np.concatenate([np.arange(NOPE_B), np.where(rsrc >= 0, NOPE_B + rsrc, -1)])
    pair = np.concatenate([head, np.where(head >= 0, QB_DIM + head, -1)]).astype(np.int32)
    pq = _selection(pair, PAIR_IN)
    pk = _selection(rsrc, ROPE_B)
    return pl.pallas_call(
        _wprep_kernel,
        grid=(DEPTH, D_MODEL // WPREP_TR),
        in_specs=[
            pl.BlockSpec((None, WPREP_TR, IN_COLS), lambda l, r: (l, r, 0)),
            pl.BlockSpec((PAIR_IN, PAIR_OUT), lambda l, r: (0, 0)),
            pl.BlockSpec((ROPE_B, KR_PAD), lambda l, r: (0, 0)),
        ],
        out_specs=pl.BlockSpec((None, WPREP_TR, IN_COLS_PAD), lambda l, r: (l, r, 0)),
        out_shape=jax.ShapeDtypeStruct((DEPTH, D_MODEL, IN_COLS_PAD), BF16),
        compiler_params=_cparams(2),
        name="wprep",
    )(w_in, pq, pk)


def _cache_kernel(ka_ref, va_ref, ckv_ref, kr_ref, wuk_ref, wuv_ref, pk_ref,
                  ka_o, va_o, kb_o, vb_o):
    ka_o[...] = ka_ref[...].astype(BF16)
    va_o[...] = va_ref[...].astype(BF16)
    ckv = ckv_ref[...].astype(BF16)
    kn = _dot(ckv, wuk_ref[...])
    vb_o[...] = _dot(ckv, wuv_ref[...]).astype(BF16)
    kr = _dot(kr_ref[...].astype(BF16), pk_ref[...]).astype(BF16)
    for h in range(N_HEADS_B):
        kb_o[:, h * QB_PAD:h * QB_PAD + NOPE_B] = kn[:, h * NOPE_B:(h + 1) * NOPE_B].astype(BF16)
        kb_o[:, h * QB_PAD + NOPE_B:(h + 1) * QB_PAD] = kr


def _cache_prep(cache_k_a, cache_v_a, cache_ckv_b, cache_krope_b, wuk, wuv):
    nb, _, past = cache_ckv_b.shape[:3]
    ka = cache_k_a.reshape(nb, DEPTH, past, W_KA)
    va = cache_v_a.reshape(nb, DEPTH, past, W_KA)
    pk = _selection(_rope_pad_src(), ROPE_B)

    def in4(width):
        return pl.BlockSpec((None, None, past, width), lambda l, b: (b, l, 0, 0))

    def out4(width):
        return pl.BlockSpec((None, None, past, width), lambda l, b: (l, b, 0, 0))

    def w3(width):
        return pl.BlockSpec((None, KV_RANK, width), lambda l, b: (l, 0, 0))

    return pl.pallas_call(
        _cache_kernel,
        grid=(DEPTH, nb),
        in_specs=[in4(W_KA), in4(W_KA), in4(KV_RANK), in4(ROPE_B), w3(W_KNOPE), w3(W_VB),
                  pl.BlockSpec((ROPE_B, KR_PAD), lambda l, b: (0, 0))],
        out_specs=[out4(W_KA), out4(W_KA), out4(W_QB_PAD), out4(W_VB)],
        out_shape=[
            jax.ShapeDtypeStruct((DEPTH, nb, past, W_KA), BF16),
            jax.ShapeDtypeStruct((DEPTH, nb, past, W_KA), BF16),
            jax.ShapeDtypeStruct((DEPTH, nb, past, W_QB_PAD), BF16),
            jax.ShapeDtypeStruct((DEPTH, nb, past, W_VB), BF16),
        ],
        compiler_params=_cparams(2),
        name="cache_prep",
    )(ka, va, cache_ckv_b, cache_krope_b, wuk, wuv, pk)


PROJ_TM = 256
N_NEW = 4


def _proj_kernel(*refs, rope, emit_new, n_alias):
    it = iter(refs)
    x_ref, mod_ref, g_ref, w_ref, qn_ref, kn_ref, kvn_ref, wuk_ref, wuv_ref = (
        next(it) for _ in range(9))
    if rope:
        ca_ref, sa_ref, cb_ref, sb_ref = (next(it) for _ in range(4))
    for _ in range(n_alias):
        next(it)
    qa_o, ka_o, va_o, qb_o, kb_o, vb_o = (next(it) for _ in range(6))
    if emit_new:
        nk_o, nv_o, nckv_o, nkr_o = (next(it) for _ in range(N_NEW))

    x = x_ref[...]
    h = _rms(x) * g_ref[...]
    h = h * (1.0 + mod_ref[1:2, :]) + mod_ref[0:1, :]
    hb = h.astype(BF16)

    def rope_a(t):
        if not rope:
            return t
        return t * ca_ref[...] + pltpu.roll(t, HEAD_DIM // 2, 1) * sa_ref[...]

    def rope_b(t):
        if not rope:
            return t
        return t * cb_ref[...] + pltpu.roll(t, KR_PAD // 2, 1) * sb_ref[...]

    pq = _dot(hb, w_ref[:, C_QA:C_QA + W_QA])
    qn = qn_ref[...]
    for i in range(N_HEADS_A):
        sl = slice(i * HEAD_DIM, (i + 1) * HEAD_DIM)
        t = rope_a(_rms(pq[:, sl]) * qn)
        qa_o[:, sl] = (t * SCALE_A).astype(BF16)

    pk = _dot(hb, w_ref[:, C_KA:C_KA + W_KA])
    pv = _dot(hb, w_ref[:, C_VA:C_VA + W_KA])
    kn = kn_ref[...]
    for i in range(N_KV_A):
        sl = slice(i * HEAD_DIM, (i + 1) * HEAD_DIM)
        t = _rms(pk[:, sl]) * kn
        if emit_new:
            nk_o[:, sl] = t
        ka_o[:, sl] = rope_a(t).astype(BF16)
    va_o[...] = pv.astype(BF16)
    if emit_new:
        nv_o[...] = pv

    pqb = _dot(hb, w_ref[:, C_QB:C_QB + W_QB_PAD])
    for i in range(N_HEADS_B):
        s0 = i * QB_PAD
        qb_o[:, s0:s0 + NOPE_B] = (pqb[:, s0:s0 + NOPE_B] * SCALE_B).astype(BF16)
        t = rope_b(pqb[:, s0 + NOPE_B:s0 + QB_PAD])
        qb_o[:, s0 + NOPE_B:s0 + QB_PAD] = (t * SCALE_B).astype(BF16)

    pc = _dot(hb, w_ref[:, C_CKV:C_CKV + KV_RANK])
    ckv = _rms(pc) * kvn_ref[...]
    pkr = _dot(hb, w_ref[:, C_KR:C_KR + KR_PAD])
    if emit_new:
        half = ROPE_B // 2
        nckv_o[...] = ckv
        nkr_o[...] = jnp.concatenate([pkr[:, 0:half], pkr[:, 2 * half:3 * half]], axis=-1)
    ckvb = ckv.astype(BF16)
    krb = rope_b(pkr).astype(BF16)
    kno = _dot(ckvb, wuk_ref[...])
    vb_o[...] = _dot(ckvb, wuv_ref[...]).astype(BF16)
    for i in range(N_HEADS_B):
        s0 = i * QB_PAD
        kb_o[:, s0:s0 + NOPE_B] = kno[:, i * NOPE_B:(i + 1) * NOPE_B].astype(BF16)
        kb_o[:, s0 + NOPE_B:s0 + QB_PAD] = krb


def _proj(x, mods_l, mod_map, g, w_in, qn, kn, kvn, wuk, wuv, rope_tabs=None,
          new_layer=None, new_bufs=None, new_shape=None, tm=PROJ_TM):
    t = x.shape[0]
    rope = rope_tabs is not None
    emit_new = new_layer is not None

    def row(width):
        return pl.BlockSpec((tm, width), lambda i: (i, 0))

    in_specs = [
        row(D_MODEL),
        pl.BlockSpec((None, N_MOD, D_MODEL), lambda i: (mod_map(i), 0, 0)),
        _const_spec((1, D_MODEL)),
        _const_spec((D_MODEL, IN_COLS_PAD)),
        _const_spec((1, HEAD_DIM)),
        _const_spec((1, HEAD_DIM)),
        _const_spec((1, KV_RANK)),
        _const_spec((KV_RANK, W_KNOPE)),
        _const_spec((KV_RANK, W_VB)),
    ]
    args = [x, mods_l, g, w_in, qn, kn, kvn, wuk, wuv]
    if rope:
        seq_tiles = rope_tabs[0].shape[0] // tm
        for tab in rope_tabs:
            in_specs.append(pl.BlockSpec((tm, tab.shape[1]), lambda i: (i % seq_tiles, 0)))
            args.append(tab)
    widths = [W_QA, W_KA, W_KA, W_QB_PAD, W_QB_PAD, W_VB]
    out_specs = [row(w) for w in widths]
    out_shape = [jax.ShapeDtypeStruct((t, w), BF16) for w in widths]
    aliases = {}
    n_alias = 0
    if emit_new:
        batch, seq = new_shape
        assert tm == seq and t == batch * seq
        nw = [W_KA, W_KA, KV_RANK, ROPE_B]
        if new_bufs is not None:
            n_alias = N_NEW
            for k, buf in enumerate(new_bufs):
                aliases[len(args)] = len(widths) + k
                in_specs.append(pl.BlockSpec(memory_space=pl.ANY))
                args.append(buf)
        out_specs += [pl.BlockSpec((None, None, seq, w), lambda i: (i, new_layer, 0, 0)) for w in nw]
        out_shape += [jax.ShapeDtypeStruct((batch, DEPTH, seq, w), F32) for w in nw]
    return pl.pallas_call(
        functools.partial(_proj_kernel, rope=rope, emit_new=emit_new, n_alias=n_alias),
        grid=(t // tm,),
        in_specs=in_specs,
        out_specs=out_specs,
        out_shape=out_shape,
        input_output_aliases=aliases,
        compiler_params=_cparams(1),
        name="proj_lat" if rope else "proj_ctx",
    )(*args)


def _softmax_pv(s_parts, v_parts):
    m = s_parts[0].max(axis=-1, keepdims=True)
    for s in s_parts[1:]:
        m = jnp.maximum(m, s.max(axis=-1, keepdims=True))
    den = None
    acc = None
    for s, v in zip(s_parts, v_parts):
        p = jnp.exp(s - m)
        d = p.sum(axis=-1, keepdims=True)
        o = _dot(p.astype(BF16), v)
        den = d if den is None else den + d
        acc = o if acc is None else acc + o
    return acc / den


def _attn_ctx_kernel(qa_ref, ka_ref, va_ref, qb_ref, kb_ref, vb_ref, oa_ref, ob_ref):
    for hq in range(N_HEADS_A):
        g = hq // GROUP_A
        ksl = slice(g * HEAD_DIM, (g + 1) * HEAD_DIM)
        qsl = slice(hq * HEAD_DIM, (hq + 1) * HEAD_DIM)
        s = _dot_t(qa_ref[:, qsl], ka_ref[:, ksl])
        oa_ref[:, qsl] = _softmax_pv([s], [va_ref[:, ksl]]).astype(BF16)
    for hq in range(N_HEADS_B):
        qsl = slice(hq * QB_PAD, (hq + 1) * QB_PAD)
        vsl = slice(hq * VDIM_B, (hq + 1) * VDIM_B)
        s = _dot_t(qb_ref[:, qsl], kb_ref[:, qsl])
        ob_ref[:, vsl] = _softmax_pv([s], [vb_ref[:, vsl]]).astype(BF16)


def _attn_ctx(qa, ka, va, qb, kb, vb, seq):
    t = qa.shape[0]

    def row(width):
        return pl.BlockSpec((seq, width), lambda b: (b, 0))

    return pl.pallas_call(
        _attn_ctx_kernel,
        grid=(t // seq,),
        in_specs=[row(W_QA), row(W_KA), row(W_KA), row(W_QB_PAD), row(W_QB_PAD), row(W_VB)],
        out_specs=[row(W_QA), row(W_VB)],
        out_shape=[jax.ShapeDtypeStruct((t, W_QA), BF16), jax.ShapeDtypeStruct((t, W_VB), BF16)],
        compiler_params=_cparams(1),
        name="attn_ctx",
    )(qa, ka, va, qb, kb, vb)


ATTN_TQ = 512
ATTN_HEADS_PER_STEP = 4


def _attn_lat_kernel(q_ref, kc_ref, kn_ref, vc_ref, vn_ref, o_ref, *, n_q, n_kv, dq, dv):
    for j in range(n_q):
        jk = j if n_kv > 1 else 0
        ksl = slice(jk * dq, (jk + 1) * dq)
        vsl = slice(jk * dv, (jk + 1) * dv)
        q = q_ref[:, j * dq:(j + 1) * dq]
        s = [_dot_t(q, kc_ref[:, ksl]), _dot_t(q, kn_ref[:, ksl])]
        o = _softmax_pv(s, [vc_ref[:, vsl], vn_ref[:, vsl]])
        o_ref[:, j * dv:(j + 1) * dv] = o.astype(BF16)


def _attn_lat(q, kc, kn, vc, vn, n_units, n_q, n_kv, dq, dv, name, tq=ATTN_TQ):
    nb, seq, _ = q.shape
    past = kc.shape[1]
    return pl.pallas_call(
        functools.partial(_attn_lat_kernel, n_q=n_q, n_kv=n_kv, dq=dq, dv=dv),
        grid=(nb, n_units, seq // tq),
        in_specs=[
            pl.BlockSpec((None, tq, n_q * dq), lambda b, u, i: (b, i, u)),
            pl.BlockSpec((None, past, n_kv * dq), lambda b, u, i: (b, 0, u)),
            pl.BlockSpec((None, seq, n_kv * dq), lambda b, u, i: (b, 0, u)),
            pl.BlockSpec((None, past, n_kv * dv), lambda b, u, i: (b, 0, u)),
            pl.BlockSpec((None, seq, n_kv * dv), lambda b, u, i: (b, 0, u)),
        ],
        out_specs=pl.BlockSpec((None, tq, n_q * dv), lambda b, u, i: (b, i, u)),
        out_shape=jax.ShapeDtypeStruct((nb, seq, n_units * n_q * dv), BF16),
        compiler_params=_cparams(3),
        name=name,
    )(q, kc, kn, vc, vn)


OUT_TM = 512


def _outproj_kernel(x_ref, oa_ref, ob_ref, mod_ref, w_ref, y_ref):
    y = _dot(oa_ref[...], w_ref[0:W_QA, :]) + _dot(ob_ref[...], w_ref[W_QA:, :])
    y_ref[...] = x_ref[...] + mod_ref[2:3, :] * y


def _outproj(x, oa, ob, mods_l, mod_map, w_o, tm=OUT_TM):
    t = x.shape[0]
    return pl.pallas_call(
        _outproj_kernel,
        grid=(t // tm,),
        in_specs=[
            pl.BlockSpec((tm, D_MODEL), lambda i: (i, 0)),
            pl.BlockSpec((tm, W_QA), lambda i: (i, 0)),
            pl.BlockSpec((tm, W_VB), lambda i: (i, 0)),
            pl.BlockSpec((None, N_MOD, D_MODEL), lambda i: (mod_map(i), 0, 0)),
            _const_spec((W_QA + W_VB, D_MODEL)),
        ],
        out_specs=pl.BlockSpec((tm, D_MODEL), lambda i: (i, 0)),
        out_shape=jax.ShapeDtypeStruct((t, D_MODEL), F32),
        compiler_params=_cparams(1),
        name="outproj",
    )(x, oa, ob, mods_l, w_o)


FFN_TM = 512
FFN_TF = 512
FFN_TN = 512


def _ffn_kernel(x_ref, xc_ref, mod_ref, modc_ref, g_ref, wg_ref, wu_ref, wd_ref, y_ref,
                h_ref, t_ref, *, nf, tf):
    j = pl.program_id(1)

    @pl.when(j == 0)
    def _():
        h = _rms(x_ref[...]) * g_ref[...]
        h = h * (1.0 + mod_ref[4:5, :]) + mod_ref[3:4, :]
        h_ref[...] = h.astype(BF16)

    @pl.when(j < nf)
    def _():
        hb = h_ref[...]
        a = _dot(hb, wg_ref[...])
        b = _dot(hb, wu_ref[...])
        t_ref[j] = (a * jax.nn.sigmoid(a) * b).astype(BF16)

    @pl.when(j >= nf)
    def _():
        d = _dot(t_ref[0], wd_ref[0:tf, :])
        for k in range(1, nf):
            d = d + _dot(t_ref[k], wd_ref[k * tf:(k + 1) * tf, :])
        y_ref[...] = xc_ref[...] + modc_ref[5:6, :] * d


def _ffn(x, mods_l, mod_map, g, wg, wu, wd, tm=FFN_TM, tf=FFN_TF, tn=FFN_TN):
    t = x.shape[0]
    nf = D_FF // tf
    col = lambda j: jnp.maximum(j - nf, 0)
    return pl.pallas_call(
        functools.partial(_ffn_kernel, nf=nf, tf=tf),
        grid=(t // tm, nf + D_MODEL // tn),
        in_specs=[
            pl.BlockSpec((tm, D_MODEL), lambda i, j: (i, 0)),
            pl.BlockSpec((tm, tn), lambda i, j: (i, col(j))),
            pl.BlockSpec((None, N_MOD, D_MODEL), lambda i, j: (mod_map(i), 0, 0)),
            pl.BlockSpec((None, N_MOD, tn), lambda i, j: (mod_map(i), 0, col(j))),
            pl.BlockSpec((1, D_MODEL), lambda i, j: (0, 0)),
            pl.BlockSpec((D_MODEL, tf), lambda i, j: (0, jnp.minimum(j, nf - 1))),
            pl.BlockSpec((D_MODEL, tf), lambda i, j: (0, jnp.minimum(j, nf - 1))),
            pl.BlockSpec((D_FF, tn), lambda i, j: (0, col(j))),
        ],
        out_specs=pl.BlockSpec((tm, tn), lambda i, j: (i, col(j))),
        out_shape=jax.ShapeDtypeStruct((t, D_MODEL), F32),
        scratch_shapes=[pltpu.VMEM((tm, D_MODEL), BF16), pltpu.VMEM((nf, tm, tf), BF16)],
        compiler_params=_cparams(2),
        name="ffn",
    )(x, x, mods_l, mods_l, g, wg, wu, wd)


NORM_TM = 512


def _final_norm_kernel(x_ref, g_ref, y_ref):
    y_ref[...] = _rms(x_ref[...]) * g_ref[...]


def _final_norm(x, g, tm=NORM_TM):
    t = x.shape[0]
    return pl.pallas_call(
        _final_norm_kernel,
        grid=(t // tm,),
        in_specs=[pl.BlockSpec((tm, D_MODEL), lambda i: (i, 0)),
                  pl.BlockSpec((1, D_MODEL), lambda i: (0, 0))],
        out_specs=pl.BlockSpec((tm, D_MODEL), lambda i: (i, 0)),
        out_shape=jax.ShapeDtypeStruct((t, D_MODEL), F32),
        compiler_params=_cparams(1),
        name="final_norm",
    )(x, g)


def _rope_tables(n_tokens):
    n_rows = n_tokens // GRID_W
    row = jnp.repeat(jnp.arange(n_rows, dtype=F32), GRID_W)
    col = jnp.tile(jnp.arange(GRID_W, dtype=F32), n_rows)

    def cs(dim):
        n_freq = dim // 4
        inv = ROPE_THETA ** (-jnp.arange(n_freq, dtype=F32) / n_freq)
        ang = jnp.concatenate([row[:, None] * inv, col[:, None] * inv], axis=-1)
        return jnp.cos(ang), jnp.sin(ang)

    cos_a, sin_a = cs(HEAD_DIM)
    cos_b, sin_b = cs(ROPE_B)
    one = jnp.ones_like(cos_b)
    zero = jnp.zeros_like(sin_b)
    return (jnp.concatenate([cos_a, cos_a], axis=-1),
            jnp.concatenate([-sin_a, sin_a], axis=-1),
            jnp.concatenate([cos_b, one, cos_b, one], axis=-1),
            jnp.concatenate([-sin_b, zero, sin_b, zero], axis=-1))


def kernel(x_prompt, x_sample, cache_k_a, cache_v_a, cache_ckv_b, cache_krope_b, c, c_ctx, w_ada, b_ada, norm_attn, norm_ffn, w_in, qnorm_a, knorm_a, kvnorm_b, w_uk_b, w_uv_b, w_o, w_gate, w_up, w_down, norm_final):
    batch, seq, _ = x_prompt.shape
    dec_batch, dec_seq, _ = x_sample.shape
    assert dec_batch + 1 <= MOD_ROWS

    w_in_b = _wprep(w_in)
    wuk_b = w_uk_b.astype(BF16)
    wuv_b = w_uv_b.astype(BF16)
    w_o_b = w_o.astype(BF16)
    wg_b = w_gate.astype(BF16)
    wu_b = w_up.astype(BF16)
    wd_b = w_down.astype(BF16)
    rope_tabs = _rope_tables(dec_seq)

    cond = jnp.concatenate(
        [c_ctx[None, :], c, jnp.zeros((MOD_ROWS - 1 - dec_batch, D_MODEL), F32)], axis=0)
    mods = _adaln(cond, w_ada, b_ada).reshape(DEPTH, MOD_ROWS, N_MOD, D_MODEL)

    kc_a, vc_a, kc_b, vc_b = _cache_prep(
        cache_k_a, cache_v_a, cache_ckv_b, cache_krope_b, wuk_b, wuv_b)

    xp = x_prompt.reshape(batch * seq, D_MODEL)
    xs = x_sample.reshape(dec_batch * dec_seq, D_MODEL)
    ctx_mod = lambda i: 0
    new_bufs = None

    for l in range(DEPTH):
        g1 = norm_attn[l].reshape(1, D_MODEL)
        g2 = norm_ffn[l].reshape(1, D_MODEL)
        qn = qnorm_a[l].reshape(1, HEAD_DIM)
        kn = knorm_a[l].reshape(1, HEAD_DIM)
        kvn = kvnorm_b[l].reshape(1, KV_RANK)

        outs = _proj(xp, mods[l], ctx_mod, g1, w_in_b[l], qn, kn, kvn, wuk_b[l], wuv_b[l],
                     new_layer=l, new_bufs=new_bufs, new_shape=(batch, seq))
        qa, ka, va, qb, kb, vb = outs[:6]
        new_bufs = outs[6:]
        oa, ob = _attn_ctx(qa, ka, va, qb, kb, vb, seq)
        xp = _outproj(xp, oa, ob, mods[l], ctx_mod, w_o_b[l])
        xp = _ffn(xp, mods[l], ctx_mod, g2, wg_b[l], wu_b[l], wd_b[l])

        def lat_mod(tm):
            return lambda i: 1 + i // (dec_seq // tm)

        qa, ka, va, qb, kb, vb = _proj(
            xs, mods[l], lat_mod(PROJ_TM), g1, w_in_b[l], qn, kn, kvn, wuk_b[l], wuv_b[l],
            rope_tabs=rope_tabs)
        r3 = lambda a: a.reshape(dec_batch, dec_seq, a.shape[-1])
        oa = _attn_lat(r3(qa), kc_a[l], r3(ka), vc_a[l], r3(va),
                       N_KV_A, GROUP_A, 1, HEAD_DIM, HEAD_DIM, "attn_lat_a")
        hb = ATTN_HEADS_PER_STEP
        ob = _attn_lat(r3(qb), kc_b[l], r3(kb), vc_b[l], r3(vb),
                       N_HEADS_B // hb, hb, hb, QB_PAD, VDIM_B, "attn_lat_b")
        xs = _outproj(xs, oa.reshape(-1, W_QA), ob.reshape(-1, W_VB), mods[l], lat_mod(OUT_TM), w_o_b[l])
        xs = _ffn(xs, mods[l], lat_mod(FFN_TM), g2, wg_b[l], wu_b[l], wd_b[l])

    gf = norm_final.reshape(1, D_MODEL)
    y_prompt = _final_norm(xp, gf).reshape(batch, seq, D_MODEL)
    y_sample = _final_norm(xs, gf).reshape(dec_batch, dec_seq, D_MODEL)
    new_k, new_v, new_ckv, new_kr = new_bufs
    return (y_prompt, y_sample,
            new_k.reshape(batch, DEPTH, seq, N_KV_A, HEAD_DIM),
            new_v.reshape(batch, DEPTH, seq, N_KV_A, HEAD_DIM),
            new_ckv, new_kr)
```

```python
import functools

import jax
import jax.numpy as jnp
import numpy as np
from jax import lax
from jax.experimental import pallas as pl
from jax.experimental.pallas import tpu as pltpu

D_MODEL = 2048
DEPTH = 4
GRID_W = 64
HEAD_DIM = 128
N_HEADS_A = 8
N_KV_A = 2
GROUP_A = N_HEADS_A // N_KV_A
N_HEADS_B = 8
NOPE_B = 128
ROPE_B = 64
VDIM_B = 128
KV_RANK = 256
D_FF = 5632
ROPE_THETA = 10000.0
EPS = 1e-6

W_QA = N_HEADS_A * HEAD_DIM
W_KA = N_KV_A * HEAD_DIM
QB_DIM = NOPE_B + ROPE_B
W_QB = N_HEADS_B * QB_DIM
QB_PAD = 256
KR_PAD = 128
W_QB_PAD = N_HEADS_B * QB_PAD
W_KNOPE = N_HEADS_B * NOPE_B
W_VB = N_HEADS_B * VDIM_B
N_MOD = 6
MOD_ROWS = 8

O_QB = W_QA + 2 * W_KA
O_CKV = O_QB + W_QB
O_KR = O_CKV + KV_RANK
IN_COLS = O_KR + ROPE_B

C_QA = 0
C_KA = C_QA + W_QA
C_VA = C_KA + W_KA
C_QB = C_VA + W_KA
C_CKV = C_QB + W_QB_PAD
C_KR = C_CKV + KV_RANK
IN_COLS_PAD = C_KR + KR_PAD

SCALE_A = HEAD_DIM ** -0.5
SCALE_B = QB_DIM ** -0.5

VMEM_LIMIT = 56 * 1024 * 1024

BF16 = jnp.bfloat16
F32 = jnp.float32


def _cparams(n_axes):
    return pltpu.CompilerParams(
        dimension_semantics=("arbitrary",) * n_axes, vmem_limit_bytes=VMEM_LIMIT)


def _layer_spec(shape, layer):
    nd = len(shape)
    return pl.BlockSpec((None,) + shape, lambda *_: (layer,) + (0,) * nd,
                        pipeline_mode=pl.Buffered(1))


def _mod_spec(layer, mod_map, width, col_map=None):
    if col_map is None:
        return pl.BlockSpec((None, None, N_MOD, width), lambda i, *_: (layer, mod_map(i), 0, 0))
    return pl.BlockSpec((None, None, N_MOD, width),
                        lambda i, j: (layer, mod_map(i), 0, col_map(j)))


def _dot(a, b):
    return jnp.dot(a, b, preferred_element_type=F32)


def _dot_t(a, b):
    return lax.dot_general(a, b, (((1,), (1,)), ((), ())), preferred_element_type=F32)


def _rms(x):
    return x * lax.rsqrt(jnp.mean(x * x, axis=-1, keepdims=True) + EPS)


ADA_TN = 1024


def _adaln_kernel(s_ref, w_ref, b_ref, o_ref):
    s = s_ref[...]
    s = s * jax.nn.sigmoid(s)
    o_ref[...] = _dot(s.astype(BF16), w_ref[...].astype(BF16)) + b_ref[...]


def _adaln(s, w_ada, b_ada):
    n = N_MOD * D_MODEL
    return pl.pallas_call(
        _adaln_kernel,
        grid=(DEPTH, n // ADA_TN),
        in_specs=[
            pl.BlockSpec((MOD_ROWS, D_MODEL), lambda l, j: (0, 0)),
            pl.BlockSpec((None, D_MODEL, ADA_TN), lambda l, j: (l, 0, j)),
            pl.BlockSpec((None, 1, ADA_TN), lambda l, j: (l, 0, j)),
        ],
        out_specs=pl.BlockSpec((None, MOD_ROWS, ADA_TN), lambda l, j: (l, 0, j)),
        out_shape=jax.ShapeDtypeStruct((DEPTH, MOD_ROWS, n), F32),
        compiler_params=_cparams(2),
        name="adaln",
    )(s, w_ada, b_ada.reshape(DEPTH, 1, n))


WPREP_TR = 512
PAIR_IN = 2 * QB_DIM
PAIR_OUT = 2 * QB_PAD


def _wprep_kernel(w_ref, pq_ref, pk_ref, o_ref):
    o_ref[:, 0:O_QB] = w_ref[:, 0:O_QB].astype(BF16)
    pq = pq_ref[...]
    for i in range(N_HEADS_B // 2):
        src = w_ref[:, O_QB + i * PAIR_IN:O_QB + (i + 1) * PAIR_IN].astype(BF16)
        o_ref[:, C_QB + i * PAIR_OUT:C_QB + (i + 1) * PAIR_OUT] = _dot(src, pq).astype(BF16)
    o_ref[:, C_CKV:C_CKV + KV_RANK] = w_ref[:, O_CKV:O_CKV + KV_RANK].astype(BF16)
    kr = w_ref[:, O_KR:O_KR + ROPE_B].astype(BF16)
    o_ref[:, C_KR:C_KR + KR_PAD] = _dot(kr, pk_ref[...]).astype(BF16)


def _rope_pad_src():
    half = ROPE_B // 2
    src = np.full((KR_PAD,), -1, np.int32)
    src[0:half] = np.arange(half)
    src[2 * half:3 * half] = half + np.arange(half)
    return src


def _selection(src, n_in):
    return (jnp.arange(n_in, dtype=jnp.int32)[:, None] == jnp.asarray(src)[None, :]).astype(BF16)


def _wprep(w_in):
    rsrc = _rope_pad_src()
    head = np.concatenate([np.arange(NOPE_B), np.where(rsrc >= 0, NOPE_B + rsrc, -1)])
    pair = np.concatenate([head, np.where(head >= 0, QB_DIM + head, -1)]).astype(np.int32)
    pq = _selection(pair, PAIR_IN)
    pk = _selection(rsrc, ROPE_B)
    return pl.pallas_call(
        _wprep_kernel,
        grid=(DEPTH, D_MODEL // WPREP_TR),
        in_specs=[
            pl.BlockSpec((None, WPREP_TR, IN_COLS), lambda l, r: (l, r, 0)),
            pl.BlockSpec((PAIR_IN, PAIR_OUT), lambda l, r: (0, 0)),
            pl.BlockSpec((ROPE_B, KR_PAD), lambda l, r: (0, 0)),
        ],
        out_specs=pl.BlockSpec((None, WPREP_TR, IN_COLS_PAD), lambda l, r: (l, r, 0)),
        out_shape=jax.ShapeDtypeStruct((DEPTH, D_MODEL, IN_COLS_PAD), BF16),
        compiler_params=_cparams(2),
        name="wprep",
    )(w_in, pq, pk)


def _cache_kernel(ka_ref, va_ref, ckv_ref, kr_ref, wuk_ref, wuv_ref, pk_ref,
                  ka_o, va_o, kb_o, vb_o):
    ka_o[...] = ka_ref[...].astype(BF16)
    va_o[...] = va_ref[...].astype(BF16)
    ckv = ckv_ref[...].astype(BF16)
    kn = _dot(ckv, wuk_ref[...])
    vb_o[...] = _dot(ckv, wuv_ref[...]).astype(BF16)
    kr = _dot(kr_ref[...].astype(BF16), pk_ref[...]).astype(BF16)
    for h in range(N_HEADS_B):
        kb_o[:, h * QB_PAD:h * QB_PAD + NOPE_B] = kn[:, h * NOPE_B:(h + 1) * NOPE_B].astype(BF16)
        kb_o[:, h * QB_PAD + NOPE_B:(h + 1) * QB_PAD] = kr


def _cache_prep(cache_k_a, cache_v_a, cache_ckv_b, cache_krope_b, wuk, wuv):
    nb, _, past = cache_ckv_b.shape[:3]
    ka = cache_k_a.reshape(nb, DEPTH, past, W_KA)
    va = cache_v_a.reshape(nb, DEPTH, past, W_KA)
    pk = _selection(_rope_pad_src(), ROPE_B)

    def in4(width):
        return pl.BlockSpec((None, None, past, width), lambda l, b: (b, l, 0, 0))

    def out4(width):
        return pl.BlockSpec((None, None, past, width), lambda l, b: (l, b, 0, 0))

    def w3(width):
        return pl.BlockSpec((None, KV_RANK, width), lambda l, b: (l, 0, 0))

    return pl.pallas_call(
        _cache_kernel,
        grid=(DEPTH, nb),
        in_specs=[in4(W_KA), in4(W_KA), in4(KV_RANK), in4(ROPE_B), w3(W_KNOPE), w3(W_VB),
                  pl.BlockSpec((ROPE_B, KR_PAD), lambda l, b: (0, 0))],
        out_specs=[out4(W_KA), out4(W_KA), out4(W_QB_PAD), out4(W_VB)],
        out_shape=[
            jax.ShapeDtypeStruct((DEPTH, nb, past, W_KA), BF16),
            jax.ShapeDtypeStruct((DEPTH, nb, past, W_KA), BF16),
            jax.ShapeDtypeStruct((DEPTH, nb, past, W_QB_PAD), BF16),
            jax.ShapeDtypeStruct((DEPTH, nb, past, W_VB), BF16),
        ],
        compiler_params=_cparams(2),
        name="cache_prep",
    )(ka, va, cache_ckv_b, cache_krope_b, wuk, wuv, pk)


PROJ_TM = 256
N_NEW = 4


def _proj_kernel(*refs, rope, emit_new, n_alias):
    it = iter(refs)
    x_ref, mod_ref, g_ref, w_ref, qn_ref, kn_ref, kvn_ref, wuk_ref, wuv_ref = (
        next(it) for _ in range(9))
    if rope:
        ca_ref, sa_ref, cb_ref, sb_ref = (next(it) for _ in range(4))
    for _ in range(n_alias):
        next(it)
    qa_o, ka_o, va_o, qb_o, kb_o, vb_o = (next(it) for _ in range(6))
    if emit_new:
        nk_o, nv_o, nckv_o, nkr_o = (next(it) for _ in range(N_NEW))

    x = x_ref[...]
    h = _rms(x) * g_ref[...]
    h = h * (1.0 + mod_ref[1:2, :]) + mod_ref[0:1, :]
    hb = h.astype(BF16)

    def rope_a(t):
        if not rope:
            return t
        return t * ca_ref[...] + pltpu.roll(t, HEAD_DIM // 2, 1) * sa_ref[...]

    def rope_b(t):
        if not rope:
            return t
        return t * cb_ref[...] + pltpu.roll(t, KR_PAD // 2, 1) * sb_ref[...]

    pq = _dot(hb, w_ref[:, C_QA:C_QA + W_QA])
    qn = qn_ref[...]
    for i in range(N_HEADS_A):
        sl = slice(i * HEAD_DIM, (i + 1) * HEAD_DIM)
        t = rope_a(_rms(pq[:, sl]) * qn)
        qa_o[:, sl] = (t * SCALE_A).astype(BF16)

    pk = _dot(hb, w_ref[:, C_KA:C_KA + W_KA])
    pv = _dot(hb, w_ref[:, C_VA:C_VA + W_KA])
    kn = kn_ref[...]
    for i in range(N_KV_A):
        sl = slice(i * HEAD_DIM, (i + 1) * HEAD_DIM)
        t = _rms(pk[:, sl]) * kn
        if emit_new:
            nk_o[:, sl] = t
        ka_o[:, sl] = rope_a(t).astype(BF16)
    va_o[...] = pv.astype(BF16)
    if emit_new:
        nv_o[...] = pv

    pqb = _dot(hb, w_ref[:, C_QB:C_QB + W_QB_PAD])
    for i in range(N_HEADS_B):
        s0 = i * QB_PAD
        qb_o[:, s0:s0 + NOPE_B] = (pqb[:, s0:s0 + NOPE_B] * SCALE_B).astype(BF16)
        t = rope_b(pqb[:, s0 + NOPE_B:s0 + QB_PAD])
        qb_o[:, s0 + NOPE_B:s0 + QB_PAD] = (t * SCALE_B).astype(BF16)

    pc = _dot(hb, w_ref[:, C_CKV:C_CKV + KV_RANK])
    ckv = _rms(pc) * kvn_ref[...]
    pkr = _dot(hb, w_ref[:, C_KR:C_KR + KR_PAD])
    if emit_new:
        half = ROPE_B // 2
        nckv_o[...] = ckv
        nkr_o[...] = jnp.concatenate([pkr[:, 0:half], pkr[:, 2 * half:3 * half]], axis=-1)
    ckvb = ckv.astype(BF16)
    krb = rope_b(pkr).astype(BF16)
    kno = _dot(ckvb, wuk_ref[...])
    vb_o[...] = _dot(ckvb, wuv_ref[...]).astype(BF16)
    for i in range(N_HEADS_B):
        s0 = i * QB_PAD
        kb_o[:, s0:s0 + NOPE_B] = kno[:, i * NOPE_B:(i + 1) * NOPE_B].astype(BF16)
        kb_o[:, s0 + NOPE_B:s0 + QB_PAD] = krb


def _proj(x, layer, mods, mod_map, g, w_in, qn, kn, kvn, wuk, wuv, rope_tabs=None,
          new_shape=None, new_bufs=None, tm=PROJ_TM):
    t = x.shape[0]
    rope = rope_tabs is not None
    emit_new = new_shape is not None

    def row(width):
        return pl.BlockSpec((tm, width), lambda i: (i, 0))

    in_specs = [
        row(D_MODEL),
        _mod_spec(layer, mod_map, D_MODEL),
        _layer_spec((1, D_MODEL), layer),
        _layer_spec((D_MODEL, IN_COLS_PAD), layer),
        _layer_spec((1, HEAD_DIM), layer),
        _layer_spec((1, HEAD_DIM), layer),
        _layer_spec((1, KV_RANK), layer),
        _layer_spec((KV_RANK, W_KNOPE), layer),
        _layer_spec((KV_RANK, W_VB), layer),
    ]
    args = [x, mods, g, w_in, qn, kn, kvn, wuk, wuv]
    if rope:
        seq_tiles = rope_tabs[0].shape[0] // tm
        for tab in rope_tabs:
            in_specs.append(pl.BlockSpec((tm, tab.shape[1]), lambda i: (i % seq_tiles, 0)))
            args.append(tab)
    widths = [W_QA, W_KA, W_KA, W_QB_PAD, W_QB_PAD, W_VB]
    out_specs = [row(w) for w in widths]
    out_shape = [jax.ShapeDtypeStruct((t, w), BF16) for w in widths]
    aliases = {}
    n_alias = 0
    if emit_new:
        batch, seq = new_shape
        assert tm == seq and t == batch * seq
        nw = [W_KA, W_KA, KV_RANK, ROPE_B]
        if new_bufs is not None:
            n_alias = N_NEW
            for k, buf in enumerate(new_bufs):
                aliases[len(args)] = len(widths) + k
                in_specs.append(pl.BlockSpec(memory_space=pl.ANY))
                args.append(buf)
        out_specs += [pl.BlockSpec((None, None, seq, w), lambda i: (i, layer, 0, 0)) for w in nw]
        out_shape += [jax.ShapeDtypeStruct((batch, DEPTH, seq, w), F32) for w in nw]
    return pl.pallas_call(
        functools.partial(_proj_kernel, rope=rope, emit_new=emit_new, n_alias=n_alias),
        grid=(t // tm,),
        in_specs=in_specs,
        out_specs=out_specs,
        out_shape=out_shape,
        input_output_aliases=aliases,
        compiler_params=_cparams(1),
        name="proj_lat" if rope else "proj_ctx",
    )(*args)


def _softmax_pv(s_parts, v_parts):
    m = s_parts[0].max(axis=-1, keepdims=True)
    for s in s_parts[1:]:
        m = jnp.maximum(m, s.max(axis=-1, keepdims=True))
    den = None
    acc = None
    for s, v in zip(s_parts, v_parts):
        p = jnp.exp(s - m)
        d = p.sum(axis=-1, keepdims=True)
        o = _dot(p.astype(BF16), v)
        den = d if den is None else den + d
        acc = o if acc is None else acc + o
    return acc / den


def _attn_ctx_kernel(qa_ref, ka_ref, va_ref, qb_ref, kb_ref, vb_ref, oa_ref, ob_ref):
    for hq in range(N_HEADS_A):
        g = hq // GROUP_A
        ksl = slice(g * HEAD_DIM, (g + 1) * HEAD_DIM)
        qsl = slice(hq * HEAD_DIM, (hq + 1) * HEAD_DIM)
        s = _dot_t(qa_ref[:, qsl], ka_ref[:, ksl])
        oa_ref[:, qsl] = _softmax_pv([s], [va_ref[:, ksl]]).astype(BF16)
    for hq in range(N_HEADS_B):
        qsl = slice(hq * QB_PAD, (hq + 1) * QB_PAD)
        vsl = slice(hq * VDIM_B, (hq + 1) * VDIM_B)
        s = _dot_t(qb_ref[:, qsl], kb_ref[:, qsl])
        ob_ref[:, vsl] = _softmax_pv([s], [vb_ref[:, vsl]]).astype(BF16)


def _attn_ctx(qa, ka, va, qb, kb, vb, seq):
    t = qa.shape[0]

    def row(width):
        return pl.BlockSpec((seq, width), lambda b: (b, 0))

    return pl.pallas_call(
        _attn_ctx_kernel,
        grid=(t // seq,),
        in_specs=[row(W_QA), row(W_KA), row(W_KA), row(W_QB_PAD), row(W_QB_PAD), row(W_VB)],
        out_specs=[row(W_QA), row(W_VB)],
        out_shape=[jax.ShapeDtypeStruct((t, W_QA), BF16), jax.ShapeDtypeStruct((t, W_VB), BF16)],
        compiler_params=_cparams(1),
        name="attn_ctx",
    )(qa, ka, va, qb, kb, vb)


ATTN_TQ = 512
ATTN_HEADS_PER_STEP = 4


def _attn_lat_kernel(q_ref, kc_ref, kn_ref, vc_ref, vn_ref, o_ref, *, n_q, n_kv, dq, dv):
    for j in range(n_q):
        jk = j if n_kv > 1 else 0
        ksl = slice(jk * dq, (jk + 1) * dq)
        vsl = slice(jk * dv, (jk + 1) * dv)
        q = q_ref[:, j * dq:(j + 1) * dq]
        s = [_dot_t(q, kc_ref[:, ksl]), _dot_t(q, kn_ref[:, ksl])]
        o = _softmax_pv(s, [vc_ref[:, vsl], vn_ref[:, vsl]])
        o_ref[:, j * dv:(j + 1) * dv] = o.astype(BF16)


def _attn_lat(q, layer, kc, kn, vc, vn, n_units, n_q, n_kv, dq, dv, name, tq=ATTN_TQ):
    nb, seq, _ = q.shape
    past = kc.shape[2]
    return pl.pallas_call(
        functools.partial(_attn_lat_kernel, n_q=n_q, n_kv=n_kv, dq=dq, dv=dv),
        grid=(nb, n_units, seq // tq),
        in_specs=[
            pl.BlockSpec((None, tq, n_q * dq), lambda b, u, i: (b, i, u)),
            pl.BlockSpec((None, None, past, n_kv * dq), lambda b, u, i: (layer, b, 0, u)),
            pl.BlockSpec((None, seq, n_kv * dq), lambda b, u, i: (b, 0, u)),
            pl.BlockSpec((None, None, past, n_kv * dv), lambda b, u, i: (layer, b, 0, u)),
            pl.BlockSpec((None, seq, n_kv * dv), lambda b, u, i: (b, 0, u)),
        ],
        out_specs=pl.BlockSpec((None, tq, n_q * dv), lambda b, u, i: (b, i, u)),
        out_shape=jax.ShapeDtypeStruct((nb, seq, n_units * n_q * dv), BF16),
        compiler_params=_cparams(3),
        name=name,
    )(q, kc, kn, vc, vn)


OUT_TM = 512


def _outproj_kernel(x_ref, oa_ref, ob_ref, mod_ref, w_ref, y_ref):
    y = _dot(oa_ref[...], w_ref[0:W_QA, :]) + _dot(ob_ref[...], w_ref[W_QA:, :])
    y_ref[...] = x_ref[...] + mod_ref[2:3, :] * y


def _outproj(x, oa, ob, layer, mods, mod_map, w_o, tm=OUT_TM):
    t = x.shape[0]
    return pl.pallas_call(
        _outproj_kernel,
        grid=(t // tm,),
        in_specs=[
            pl.BlockSpec((tm, D_MODEL), lambda i: (i, 0)),
            pl.BlockSpec((tm, W_QA), lambda i: (i, 0)),
            pl.BlockSpec((tm, W_VB), lambda i: (i, 0)),
            _mod_spec(layer, mod_map, D_MODEL),
            _layer_spec((W_QA + W_VB, D_MODEL), layer),
        ],
        out_specs=pl.BlockSpec((tm, D_MODEL), lambda i: (i, 0)),
        out_shape=jax.ShapeDtypeStruct((t, D_MODEL), F32),
        compiler_params=_cparams(1),
        name="outproj",
    )(x, oa, ob, mods, w_o)


FFN_TM = 1024
FFN_TF = 512
FFN_TN = 256


def _ffn_kernel(x_ref, xc_ref, mod_ref, modc_ref, g_ref, wg_ref, wu_ref, wd_ref, y_ref,
                h_ref, t_ref, *, nf, tf):
    j = pl.program_id(1)

    @pl.when(j == 0)
    def _():
        h = _rms(x_ref[...]) * g_ref[...]
        h = h * (1.0 + mod_ref[4:5, :]) + mod_ref[3:4, :]
        h_ref[...] = h.astype(BF16)

    @pl.when(j < nf)
    def _():
        hb = h_ref[...]
        a = _dot(hb, wg_ref[...])
        b = _dot(hb, wu_ref[...])
        t_ref[j] = (a * jax.nn.sigmoid(a) * b).astype(BF16)

    @pl.when(j >= nf)
    def _():
        d = _dot(t_ref[0], wd_ref[0:tf, :])
        for k in range(1, nf):
            d = d + _dot(t_ref[k], wd_ref[k * tf:(k + 1) * tf, :])
        y_ref[...] = xc_ref[...] + modc_ref[5:6, :] * d


def _ffn(x, layer, mods, mod_map, g, wg, wu, wd, tm=FFN_TM, tf=FFN_TF, tn=FFN_TN):
    t = x.shape[0]
    nf = D_FF // tf
    col = lambda j: jnp.maximum(j - nf, 0)
    ftile = lambda j: jnp.minimum(j, nf - 1)
    return pl.pallas_call(
        functools.partial(_ffn_kernel, nf=nf, tf=tf),
        grid=(t // tm, nf + D_MODEL // tn),
        in_specs=[
            pl.BlockSpec((tm, D_MODEL), lambda i, j: (i, 0), pipeline_mode=pl.Buffered(1)),
            pl.BlockSpec((tm, tn), lambda i, j: (i, col(j))),
            _mod_spec(layer, mod_map, D_MODEL),
            _mod_spec(layer, mod_map, tn, col),
            _layer_spec((1, D_MODEL), layer),
            pl.BlockSpec((None, D_MODEL, tf), lambda i, j: (layer, 0, ftile(j))),
            pl.BlockSpec((None, D_MODEL, tf), lambda i, j: (layer, 0, ftile(j))),
            pl.BlockSpec((None, D_FF, tn), lambda i, j: (layer, 0, col(j))),
        ],
        out_specs=pl.BlockSpec((tm, tn), lambda i, j: (i, col(j))),
        out_shape=jax.ShapeDtypeStruct((t, D_MODEL), F32),
        scratch_shapes=[pltpu.VMEM((tm, D_MODEL), BF16), pltpu.VMEM((nf, tm, tf), BF16)],
        compiler_params=_cparams(2),
        name="ffn",
    )(x, x, mods, mods, g, wg, wu, wd)


NORM_TM = 512


def _final_norm_kernel(x_ref, g_ref, y_ref):
    y_ref[...] = _rms(x_ref[...]) * g_ref[...]


def _final_norm(x, g, tm=NORM_TM):
    t = x.shape[0]
    return pl.pallas_call(
        _final_norm_kernel,
        grid=(t // tm,),
        in_specs=[pl.BlockSpec((tm, D_MODEL), lambda i: (i, 0)),
                  pl.BlockSpec((1, D_MODEL), lambda i: (0, 0))],
        out_specs=pl.BlockSpec((tm, D_MODEL), lambda i: (i, 0)),
        out_shape=jax.ShapeDtypeStruct((t, D_MODEL), F32),
        compiler_params=_cparams(1),
        name="final_norm",
    )(x, g)


def _rope_tables(n_tokens):
    n_rows = n_tokens // GRID_W
    row = jnp.repeat(jnp.arange(n_rows, dtype=F32), GRID_W)
    col = jnp.tile(jnp.arange(GRID_W, dtype=F32), n_rows)

    def cs(dim):
        n_freq = dim // 4
        inv = ROPE_THETA ** (-jnp.arange(n_freq, dtype=F32) / n_freq)
        ang = jnp.concatenate([row[:, None] * inv, col[:, None] * inv], axis=-1)
        return jnp.cos(ang), jnp.sin(ang)

    cos_a, sin_a = cs(HEAD_DIM)
    cos_b, sin_b = cs(ROPE_B)
    one = jnp.ones_like(cos_b)
    zero = jnp.zeros_like(sin_b)
    return (jnp.concatenate([cos_a, cos_a], axis=-1),
            jnp.concatenate([-sin_a, sin_a], axis=-1),
            jnp.concatenate([cos_b, one, cos_b, one], axis=-1),
            jnp.concatenate([-sin_b, zero, sin_b, zero], axis=-1))


def kernel(x_prompt, x_sample, cache_k_a, cache_v_a, cache_ckv_b, cache_krope_b, c, c_ctx, w_ada, b_ada, norm_attn, norm_ffn, w_in, qnorm_a, knorm_a, kvnorm_b, w_uk_b, w_uv_b, w_o, w_gate, w_up, w_down, norm_final):
    batch, seq, _ = x_prompt.shape
    dec_batch, dec_seq, _ = x_sample.shape
    assert dec_batch + 1 <= MOD_ROWS

    w_in_b = _wprep(w_in)
    wuk_b = w_uk_b.astype(BF16)
    wuv_b = w_uv_b.astype(BF16)
    w_o_b = w_o.astype(BF16)
    wg_b = w_gate.astype(BF16)
    wu_b = w_up.astype(BF16)
    wd_b = w_down.astype(BF16)
    rope_tabs = _rope_tables(dec_seq)

    cond = jnp.concatenate(
        [c_ctx[None, :], c, jnp.zeros((MOD_ROWS - 1 - dec_batch, D_MODEL), F32)], axis=0)
    mods = _adaln(cond, w_ada, b_ada).reshape(DEPTH, MOD_ROWS, N_MOD, D_MODEL)

    kc_a, vc_a, kc_b, vc_b = _cache_prep(
        cache_k_a, cache_v_a, cache_ckv_b, cache_krope_b, wuk_b, wuv_b)

    xp = x_prompt.reshape(batch * seq, D_MODEL)
    xs = x_sample.reshape(dec_batch * dec_seq, D_MODEL)
    g1 = norm_attn.reshape(DEPTH, 1, D_MODEL)
    g2 = norm_ffn.reshape(DEPTH, 1, D_MODEL)
    qn = qnorm_a.reshape(DEPTH, 1, HEAD_DIM)
    kn = knorm_a.reshape(DEPTH, 1, HEAD_DIM)
    kvn = kvnorm_b.reshape(DEPTH, 1, KV_RANK)
    ctx_mod = lambda i: 0

    def lat_mod(tm):
        return lambda i: 1 + i // (dec_seq // tm)

    r3 = lambda a: a.reshape(dec_batch, dec_seq, a.shape[-1])
    hb = ATTN_HEADS_PER_STEP
    new_bufs = None

    for l in range(DEPTH):
        outs = _proj(xp, l, mods, ctx_mod, g1, w_in_b, qn, kn, kvn, wuk_b, wuv_b,
                     new_shape=(batch, seq), new_bufs=new_bufs)
        qa, ka, va, qb, kb, vb = outs[:6]
        new_bufs = outs[6:]
        oa, ob = _attn_ctx(qa, ka, va, qb, kb, vb, seq)
        xp = _outproj(xp, oa, ob, l, mods, ctx_mod, w_o_b)
        xp = _ffn(xp, l, mods, ctx_mod, g2, wg_b, wu_b, wd_b)

        qa, ka, va, qb, kb, vb = _proj(
            xs, l, mods, lat_mod(PROJ_TM), g1, w_in_b, qn, kn, kvn, wuk_b, wuv_b,
            rope_tabs=rope_tabs)
        oa = _attn_lat(r3(qa), l, kc_a, r3(ka), vc_a, r3(va),
                       N_KV_A, GROUP_A, 1, HEAD_DIM, HEAD_DIM, "attn_lat_a")
        ob = _attn_lat(r3(qb), l, kc_b, r3(kb), vc_b, r3(vb),
                       N_HEADS_B // hb, hb, hb, QB_PAD, VDIM_B, "attn_lat_b")
        xs = _outproj(xs, oa.reshape(-1, W_QA), ob.reshape(-1, W_VB), l, mods, lat_mod(OUT_TM), w_o_b)
        xs = _ffn(xs, l, mods, lat_mod(FFN_TM), g2, wg_b, wu_b, wd_b)

    gf = norm_final.reshape(1, D_MODEL)
    y_prompt = _final_norm(xp, gf).reshape(batch, seq, D_MODEL)
    y_sample = _final_norm(xs, gf).reshape(dec_batch, dec_seq, D_MODEL)
    new_k, new_v, new_ckv, new_kr = new_bufs
    return (y_prompt, y_sample,
            new_k.reshape(batch, DEPTH, seq, N_KV_A, HEAD_DIM),
            new_v.reshape(batch, DEPTH, seq, N_KV_A, HEAD_DIM),
            new_ckv, new_kr)
```

```python
import functools

import jax
import jax.numpy as jnp
import numpy as np
from jax import lax
from jax.experimental import pallas as pl
from jax.experimental.pallas import tpu as pltpu

D_MODEL = 2048
DEPTH = 4
GRID_W = 64
HEAD_DIM = 128
N_HEADS_A = 8
N_KV_A = 2
GROUP_A = N_HEADS_A // N_KV_A
N_HEADS_B = 8
NOPE_B = 128
ROPE_B = 64
VDIM_B = 128
KV_RANK = 256
D_FF = 5632
ROPE_THETA = 10000.0
EPS = 1e-6

W_QA = N_HEADS_A * HEAD_DIM
W_KA = N_KV_A * HEAD_DIM
QB_DIM = NOPE_B + ROPE_B
W_QB = N_HEADS_B * QB_DIM
QB_PAD = 256
KR_PAD = 128
W_QB_PAD = N_HEADS_B * QB_PAD
W_KNOPE = N_HEADS_B * NOPE_B
W_VB = N_HEADS_B * VDIM_B
N_MOD = 6
MOD_ROWS = 8

O_QB = W_QA + 2 * W_KA
O_CKV = O_QB + W_QB
O_KR = O_CKV + KV_RANK
IN_COLS = O_KR + ROPE_B

C_QA = 0
C_KA = C_QA + W_QA
C_VA = C_KA + W_KA
C_QB = C_VA + W_KA
C_CKV = C_QB + W_QB_PAD
C_KR = C_CKV + KV_RANK
IN_COLS_PAD = C_KR + KR_PAD

SCALE_A = HEAD_DIM ** -0.5
SCALE_B = QB_DIM ** -0.5

VMEM_LIMIT = 56 * 1024 * 1024

BF16 = jnp.bfloat16
F32 = jnp.float32


def _cparams(n_axes):
    return pltpu.CompilerParams(
        dimension_semantics=("arbitrary",) * n_axes, vmem_limit_bytes=VMEM_LIMIT)


def _layer_spec(shape, layer):
    nd = len(shape)
    return pl.BlockSpec((None,) + shape, lambda *_: (layer,) + (0,) * nd,
                        pipeline_mode=pl.Buffered(1))


def _mod_spec(layer, mod_map, width, col_map=None):
    if col_map is None:
        return pl.BlockSpec((None, None, N_MOD, width), lambda i, *_: (layer, mod_map(i), 0, 0))
    return pl.BlockSpec((None, None, N_MOD, width),
                        lambda i, j: (layer, mod_map(i), 0, col_map(j)))


def _dot(a, b):
    return jnp.dot(a, b, preferred_element_type=F32)


def _dot_t(a, b):
    return lax.dot_general(a, b, (((1,), (1,)), ((), ())), preferred_element_type=F32)


def _rms(x):
    return x * lax.rsqrt(jnp.mean(x * x, axis=-1, keepdims=True) + EPS)


ADA_TN = 1024


def _adaln_kernel(s_ref, w_ref, b_ref, o_ref):
    s = s_ref[...]
    s = s * jax.nn.sigmoid(s)
    o_ref[...] = _dot(s.astype(BF16), w_ref[...].astype(BF16)) + b_ref[...]


def _adaln(s, w_ada, b_ada):
    n = N_MOD * D_MODEL
    return pl.pallas_call(
        _adaln_kernel,
        grid=(DEPTH, n // ADA_TN),
        in_specs=[
            pl.BlockSpec((MOD_ROWS, D_MODEL), lambda l, j: (0, 0)),
            pl.BlockSpec((None, D_MODEL, ADA_TN), lambda l, j: (l, 0, j)),
            pl.BlockSpec((None, 1, ADA_TN), lambda l, j: (l, 0, j)),
        ],
        out_specs=pl.BlockSpec((None, MOD_ROWS, ADA_TN), lambda l, j: (l, 0, j)),
        out_shape=jax.ShapeDtypeStruct((DEPTH, MOD_ROWS, n), F32),
        compiler_params=_cparams(2),
        name="adaln",
    )(s, w_ada, b_ada.reshape(DEPTH, 1, n))


WPREP_TR = 512
PAIR_IN = 2 * QB_DIM
PAIR_OUT = 2 * QB_PAD


def _wprep_kernel(w_ref, pq_ref, pk_ref, o_ref):
    o_ref[:, 0:O_QB] = w_ref[:, 0:O_QB].astype(BF16)
    pq = pq_ref[...]
    for i in range(N_HEADS_B // 2):
        src = w_ref[:, O_QB + i * PAIR_IN:O_QB + (i + 1) * PAIR_IN].astype(BF16)
        o_ref[:, C_QB + i * PAIR_OUT:C_QB + (i + 1) * PAIR_OUT] = _dot(src, pq).astype(BF16)
    o_ref[:, C_CKV:C_CKV + KV_RANK] = w_ref[:, O_CKV:O_CKV + KV_RANK].astype(BF16)
    kr = w_ref[:, O_KR:O_KR + ROPE_B].astype(BF16)
    o_ref[:, C_KR:C_KR + KR_PAD] = _dot(kr, pk_ref[...]).astype(BF16)


def _rope_pad_src():
    half = ROPE_B // 2
    src = np.full((KR_PAD,), -1, np.int32)
    src[0:half] = np.arange(half)
    src[2 * half:3 * half] = half + np.arange(half)
    return src


def _selection(src, n_in):
    return (jnp.arange(n_in, dtype=jnp.int32)[:, None] == jnp.asarray(src)[None, :]).astype(BF16)


def _wprep(w_in):
    rsrc = _rope_pad_src()
    head = np.concatenate([np.arange(NOPE_B), np.where(rsrc >= 0, NOPE_B + rsrc, -1)])
    pair = np.concatenate([head, np.where(head >= 0, QB_DIM + head, -1)]).astype(np.int32)
    pq = _selection(pair, PAIR_IN)
    pk = _selection(rsrc, ROPE_B)
    return pl.pallas_call(
        _wprep_kernel,
        grid=(DEPTH, D_MODEL // WPREP_TR),
        in_specs=[
            pl.BlockSpec((None, WPREP_TR, IN_COLS), lambda l, r: (l, r, 0)),
            pl.BlockSpec((PAIR_IN, PAIR_OUT), lambda l, r: (0, 0)),
            pl.BlockSpec((ROPE_B, KR_PAD), lambda l, r: (0, 0)),
        ],
        out_specs=pl.BlockSpec((None, WPREP_TR, IN_COLS_PAD), lambda l, r: (l, r, 0)),
        out_shape=jax.ShapeDtypeStruct((DEPTH, D_MODEL, IN_COLS_PAD), BF16),
        compiler_params=_cparams(2),
        name="wprep",
    )(w_in, pq, pk)


def _cache_kernel(ka_ref, va_ref, ckv_ref, kr_ref, wuk_ref, wuv_ref, pk_ref,
                  ka_o, va_o, kb_o, vb_o):
    ka_o[...] = ka_ref[...].astype(BF16)
    va_o[...] = va_ref[...].astype(BF16)
    ckv = ckv_ref[...].astype(BF16)
    kn = _dot(ckv, wuk_ref[...])
    vb_o[...] = _dot(ckv, wuv_ref[...]).astype(BF16)
    kr = _dot(kr_ref[...].astype(BF16), pk_ref[...]).astype(BF16)
    for h in range(N_HEADS_B):
        kb_o[:, h * QB_PAD:h * QB_PAD + NOPE_B] = kn[:, h * NOPE_B:(h + 1) * NOPE_B].astype(BF16)
        kb_o[:, h * QB_PAD + NOPE_B:(h + 1) * QB_PAD] = kr


def _cache_prep(cache_k_a, cache_v_a, cache_ckv_b, cache_krope_b, wuk, wuv):
    nb, _, past = cache_ckv_b.shape[:3]
    ka = cache_k_a.reshape(nb, DEPTH, past, W_KA)
    va = cache_v_a.reshape(nb, DEPTH, past, W_KA)
    pk = _selection(_rope_pad_src(), ROPE_B)

    def in4(width):
        return pl.BlockSpec((None, None, past, width), lambda l, b: (b, l, 0, 0))

    def out4(width):
        return pl.BlockSpec((None, None, past, width), lambda l, b: (l, b, 0, 0))

    def w3(width):
        return pl.BlockSpec((None, KV_RANK, width), lambda l, b: (l, 0, 0))

    return pl.pallas_call(
        _cache_kernel,
        grid=(DEPTH, nb),
        in_specs=[in4(W_KA), in4(W_KA), in4(KV_RANK), in4(ROPE_B), w3(W_KNOPE), w3(W_VB),
                  pl.BlockSpec((ROPE_B, KR_PAD), lambda l, b: (0, 0))],
        out_specs=[out4(W_KA), out4(W_KA), out4(W_QB_PAD), out4(W_VB)],
        out_shape=[
            jax.ShapeDtypeStruct((DEPTH, nb, past, W_KA), BF16),
            jax.ShapeDtypeStruct((DEPTH, nb, past, W_KA), BF16),
            jax.ShapeDtypeStruct((DEPTH, nb, past, W_QB_PAD), BF16),
            jax.ShapeDtypeStruct((DEPTH, nb, past, W_VB), BF16),
        ],
        compiler_params=_cparams(2),
        name="cache_prep",
    )(ka, va, cache_ckv_b, cache_krope_b, wuk, wuv, pk)


PROJ_TM = 256
N_NEW = 4


def _proj_kernel(*refs, rope, emit_new, n_alias):
    it = iter(refs)
    x_ref, mod_ref, g_ref, w_ref, qn_ref, kn_ref, kvn_ref, wuk_ref, wuv_ref = (
        next(it) for _ in range(9))
    if rope:
        ca_ref, sa_ref, cb_ref, sb_ref = (next(it) for _ in range(4))
    for _ in range(n_alias):
        next(it)
    qa_o, ka_o, va_o, qb_o, kb_o, vb_o = (next(it) for _ in range(6))
    if emit_new:
        nk_o, nv_o, nckv_o, nkr_o = (next(it) for _ in range(N_NEW))

    x = x_ref[...]
    h = _rms(x) * g_ref[...]
    h = h * (1.0 + mod_ref[1:2, :]) + mod_ref[0:1, :]
    hb = h.astype(BF16)

    def rope_a(t):
        if not rope:
            return t
        return t * ca_ref[...] + pltpu.roll(t, HEAD_DIM // 2, 1) * sa_ref[...]

    def rope_b(t):
        if not rope:
            return t
        return t * cb_ref[...] + pltpu.roll(t, KR_PAD // 2, 1) * sb_ref[...]

    pq = _dot(hb, w_ref[:, C_QA:C_QA + W_QA])
    qn = qn_ref[...]
    for i in range(N_HEADS_A):
        sl = slice(i * HEAD_DIM, (i + 1) * HEAD_DIM)
        t = rope_a(_rms(pq[:, sl]) * qn)
        qa_o[:, sl] = (t * SCALE_A).astype(BF16)

    pk = _dot(hb, w_ref[:, C_KA:C_KA + W_KA])
    pv = _dot(hb, w_ref[:, C_VA:C_VA + W_KA])
    kn = kn_ref[...]
    for i in range(N_KV_A):
        sl = slice(i * HEAD_DIM, (i + 1) * HEAD_DIM)
        t = _rms(pk[:, sl]) * kn
        if emit_new:
            nk_o[:, sl] = t
        ka_o[:, sl] = rope_a(t).astype(BF16)
    va_o[...] = pv.astype(BF16)
    if emit_new:
        nv_o[...] = pv

    pqb = _dot(hb, w_ref[:, C_QB:C_QB + W_QB_PAD])
    for i in range(N_HEADS_B):
        s0 = i * QB_PAD
        qb_o[:, s0:s0 + NOPE_B] = (pqb[:, s0:s0 + NOPE_B] * SCALE_B).astype(BF16)
        t = rope_b(pqb[:, s0 + NOPE_B:s0 + QB_PAD])
        qb_o[:, s0 + NOPE_B:s0 + QB_PAD] = (t * SCALE_B).astype(BF16)

    pc = _dot(hb, w_ref[:, C_CKV:C_CKV + KV_RANK])
    ckv = _rms(pc) * kvn_ref[...]
    pkr = _dot(hb, w_ref[:, C_KR:C_KR + KR_PAD])
    if emit_new:
        half = ROPE_B // 2
        nckv_o[...] = ckv
        nkr_o[...] = jnp.concatenate([pkr[:, 0:half], pkr[:, 2 * half:3 * half]], axis=-1)
    ckvb = ckv.astype(BF16)
    krb = rope_b(pkr).astype(BF16)
    kno = _dot(ckvb, wuk_ref[...])
    vb_o[...] = _dot(ckvb, wuv_ref[...]).astype(BF16)
    for i in range(N_HEADS_B):
        s0 = i * QB_PAD
        kb_o[:, s0:s0 + NOPE_B] = kno[:, i * NOPE_B:(i + 1) * NOPE_B].astype(BF16)
        kb_o[:, s0 + NOPE_B:s0 + QB_PAD] = krb


def _proj(x, layer, mods, mod_map, g, w_in, qn, kn, kvn, wuk, wuv, rope_tabs=None,
          new_shape=None, new_bufs=None, tm=PROJ_TM):
    t = x.shape[0]
    rope = rope_tabs is not None
    emit_new = new_shape is not None

    def row(width):
        return pl.BlockSpec((tm, width), lambda i: (i, 0))

    in_specs = [
        row(D_MODEL),
        _mod_spec(layer, mod_map, D_MODEL),
        _layer_spec((1, D_MODEL), layer),
        _layer_spec((D_MODEL, IN_COLS_PAD), layer),
        _layer_spec((1, HEAD_DIM), layer),
        _layer_spec((1, HEAD_DIM), layer),
        _layer_spec((1, KV_RANK), layer),
        _layer_spec((KV_RANK, W_KNOPE), layer),
        _layer_spec((KV_RANK, W_VB), layer),
    ]
    args = [x, mods, g, w_in, qn, kn, kvn, wuk, wuv]
    if rope:
        seq_tiles = rope_tabs[0].shape[0] // tm
        for tab in rope_tabs:
            in_specs.append(pl.BlockSpec((tm, tab.shape[1]), lambda i: (i % seq_tiles, 0)))
            args.append(tab)
    widths = [W_QA, W_KA, W_KA, W_QB_PAD, W_QB_PAD, W_VB]
    out_specs = [row(w) for w in widths]
    out_shape = [jax.ShapeDtypeStruct((t, w), BF16) for w in widths]
    aliases = {}
    n_alias = 0
    if emit_new:
        batch, seq = new_shape
        assert tm == seq and t == batch * seq
        nw = [W_KA, W_KA, KV_RANK, ROPE_B]
        if new_bufs is not None:
            n_alias = N_NEW
            for k, buf in enumerate(new_bufs):
                aliases[len(args)] = len(widths) + k
                in_specs.append(pl.BlockSpec(memory_space=pl.ANY))
                args.append(buf)
        out_specs += [pl.BlockSpec((None, None, seq, w), lambda i: (i, layer, 0, 0)) for w in nw]
        out_shape += [jax.ShapeDtypeStruct((batch, DEPTH, seq, w), F32) for w in nw]
    return pl.pallas_call(
        functools.partial(_proj_kernel, rope=rope, emit_new=emit_new, n_alias=n_alias),
        grid=(t // tm,),
        in_specs=in_specs,
        out_specs=out_specs,
        out_shape=out_shape,
        input_output_aliases=aliases,
        compiler_params=_cparams(1),
        name="proj_lat" if rope else "proj_ctx",
    )(*args)


def _attend(q, k, v_ext, dv):
    s = _dot_t(q, k)
    p = jnp.exp(s - s.max(axis=-1, keepdims=True))
    o = _dot(p.astype(BF16), v_ext)
    return (o[:, 0:dv] / o[:, dv:2 * dv]).astype(BF16)


def _fill_v_ext(v_ext_ref, rows, v_ref, n_kv, dv):
    ones = jnp.ones((v_ref.shape[0], dv), BF16)
    for jk in range(n_kv):
        v_ext_ref[rows, 2 * jk * dv:(2 * jk + 1) * dv] = v_ref[:, jk * dv:(jk + 1) * dv]
        v_ext_ref[rows, (2 * jk + 1) * dv:(2 * jk + 2) * dv] = ones


def _attn_ctx_kernel(qa_ref, ka_ref, va_ref, qb_ref, kb_ref, vb_ref, oa_ref, ob_ref,
                     vae_ref, vbe_ref):
    _fill_v_ext(vae_ref, slice(None), va_ref, N_KV_A, HEAD_DIM)
    _fill_v_ext(vbe_ref, slice(None), vb_ref, N_HEADS_B, VDIM_B)
    for hq in range(N_HEADS_A):
        g = hq // GROUP_A
        qsl = slice(hq * HEAD_DIM, (hq + 1) * HEAD_DIM)
        oa_ref[:, qsl] = _attend(qa_ref[:, qsl], ka_ref[:, g * HEAD_DIM:(g + 1) * HEAD_DIM],
                                 vae_ref[:, 2 * g * HEAD_DIM:(2 * g + 2) * HEAD_DIM], HEAD_DIM)
    for hq in range(N_HEADS_B):
        qsl = slice(hq * QB_PAD, (hq + 1) * QB_PAD)
        ob_ref[:, hq * VDIM_B:(hq + 1) * VDIM_B] = _attend(
            qb_ref[:, qsl], kb_ref[:, qsl],
            vbe_ref[:, 2 * hq * VDIM_B:(2 * hq + 2) * VDIM_B], VDIM_B)


def _attn_ctx(qa, ka, va, qb, kb, vb, seq):
    t = qa.shape[0]

    def row(width):
        return pl.BlockSpec((seq, width), lambda b: (b, 0))

    return pl.pallas_call(
        _attn_ctx_kernel,
        grid=(t // seq,),
        in_specs=[row(W_QA), row(W_KA), row(W_KA), row(W_QB_PAD), row(W_QB_PAD), row(W_VB)],
        out_specs=[row(W_QA), row(W_VB)],
        out_shape=[jax.ShapeDtypeStruct((t, W_QA), BF16), jax.ShapeDtypeStruct((t, W_VB), BF16)],
        scratch_shapes=[pltpu.VMEM((seq, 2 * W_KA), BF16), pltpu.VMEM((seq, 2 * W_VB), BF16)],
        compiler_params=_cparams(1),
        name="attn_ctx",
    )(qa, ka, va, qb, kb, vb)


ATTN_TQ = 512
ATTN_HEADS_PER_STEP = 4
ATTN_ROW_SPLIT = 2


def _attn_lat_kernel(q_ref, kc_ref, kn_ref, vc_ref, vn_ref, o_ref, k_ref, v_ref,
                     *, n_q, n_kv, dq, dv, past):
    @pl.when(pl.program_id(2) == 0)
    def _():
        k_ref[0:past, :] = kc_ref[...]
        k_ref[past:, :] = kn_ref[...]
        _fill_v_ext(v_ref, slice(0, past), vc_ref, n_kv, dv)
        _fill_v_ext(v_ref, slice(past, None), vn_ref, n_kv, dv)

    rows = q_ref.shape[0] // ATTN_ROW_SPLIT
    for j in range(n_q):
        jk = j * n_kv // n_q
        for r in range(ATTN_ROW_SPLIT):
            rsl = slice(r * rows, (r + 1) * rows)
            o_ref[rsl, j * dv:(j + 1) * dv] = _attend(
                q_ref[rsl, j * dq:(j + 1) * dq], k_ref[:, jk * dq:(jk + 1) * dq],
                v_ref[:, 2 * jk * dv:(2 * jk + 2) * dv], dv)


def _attn_lat(q, layer, kc, kn, vc, vn, n_units, n_q, n_kv, dq, dv, name, tq=ATTN_TQ):
    nb, seq, _ = q.shape
    past = kc.shape[2]
    return pl.pallas_call(
        functools.partial(_attn_lat_kernel, n_q=n_q, n_kv=n_kv, dq=dq, dv=dv, past=past),
        grid=(nb, n_units, seq // tq),
        in_specs=[
            pl.BlockSpec((None, tq, n_q * dq), lambda b, u, i: (b, i, u)),
            pl.BlockSpec((None, None, past, n_kv * dq), lambda b, u, i: (layer, b, 0, u)),
            pl.BlockSpec((None, seq, n_kv * dq), lambda b, u, i: (b, 0, u)),
            pl.BlockSpec((None, None, past, n_kv * dv), lambda b, u, i: (layer, b, 0, u)),
            pl.BlockSpec((None, seq, n_kv * dv), lambda b, u, i: (b, 0, u)),
        ],
        out_specs=pl.BlockSpec((None, tq, n_q * dv), lambda b, u, i: (b, i, u)),
        out_shape=jax.ShapeDtypeStruct((nb, seq, n_units * n_q * dv), BF16),
        scratch_shapes=[pltpu.VMEM((past + seq, n_kv * dq), BF16),
                        pltpu.VMEM((past + seq, n_kv * 2 * dv), BF16)],
        compiler_params=_cparams(3),
        name=name,
    )(q, kc, kn, vc, vn)


OUT_TM = 512


def _outproj_kernel(x_ref, oa_ref, ob_ref, mod_ref, w_ref, y_ref):
    y = _dot(oa_ref[...], w_ref[0:W_QA, :]) + _dot(ob_ref[...], w_ref[W_QA:, :])
    y_ref[...] = x_ref[...] + mod_ref[2:3, :] * y


def _outproj(x, oa, ob, layer, mods, mod_map, w_o, tm=OUT_TM):
    t = x.shape[0]
    return pl.pallas_call(
        _outproj_kernel,
        grid=(t // tm,),
        in_specs=[
            pl.BlockSpec((tm, D_MODEL), lambda i: (i, 0)),
            pl.BlockSpec((tm, W_QA), lambda i: (i, 0)),
            pl.BlockSpec((tm, W_VB), lambda i: (i, 0)),
            _mod_spec(layer, mod_map, D_MODEL),
            _layer_spec((W_QA + W_VB, D_MODEL), layer),
        ],
        out_specs=pl.BlockSpec((tm, D_MODEL), lambda i: (i, 0)),
        out_shape=jax.ShapeDtypeStruct((t, D_MODEL), F32),
        compiler_params=_cparams(1),
        name="outproj",
    )(x, oa, ob, mods, w_o)


FFN_TM = 1024
FFN_TF = 512
FFN_TN = 512


def _ffn_kernel(xr_ref, xc_ref, mod_ref, modc_ref, g_ref, wg_ref, wu_ref, wd_ref, y_ref,
                h_ref, t_ref, *, nf, tf, rc, n_tiles):
    i = pl.program_id(0)
    j = pl.program_id(1)
    cur = (i + 1) % 2
    nxt = i % 2

    def norm_chunk():
        h = _rms(xr_ref[...]) * g_ref[...]
        h = h * (1.0 + mod_ref[4:5, :]) + mod_ref[3:4, :]
        r0 = pl.multiple_of((j - nf) * rc, rc)
        h_ref[nxt, pl.ds(r0, rc), :] = h.astype(BF16)

    def down_proj():
        d = _dot(t_ref[0], wd_ref[0:tf, :])
        for k in range(1, nf):
            d = d + _dot(t_ref[k], wd_ref[k * tf:(k + 1) * tf, :])
        y_ref[...] = xc_ref[...] + modc_ref[5:6, :] * d

    @pl.when((j < nf) & (i >= 1))
    def _():
        hb = h_ref[cur]
        a = _dot(hb, wg_ref[...])
        b = _dot(hb, wu_ref[...])
        t_ref[j] = (a * jax.nn.sigmoid(a) * b).astype(BF16)

    @pl.when((j >= nf) & (i >= 1) & (i < n_tiles))
    def _():
        norm_chunk()
        down_proj()

    @pl.when((j >= nf) & (i == 0))
    def _():
        norm_chunk()

    @pl.when((j >= nf) & (i == n_tiles))
    def _():
        down_proj()


def _ffn(x, layer, mods, mod_map, g, wg, wu, wd, tm=FFN_TM, tf=FFN_TF, tn=FFN_TN):
    t = x.shape[0]
    n_tiles = t // tm
    nf = D_FF // tf
    nn = D_MODEL // tn
    rc = tm // nn
    col = lambda j: jnp.maximum(j - nf, 0)
    comp = lambda i: jnp.maximum(i - 1, 0)
    norm = lambda i: jnp.minimum(i, n_tiles - 1)
    ccol = lambda i, j: jnp.where(i == 0, 0, col(j))
    ftile = lambda i, j: jnp.where(i == 0, 0, jnp.minimum(j, nf - 1))
    return pl.pallas_call(
        functools.partial(_ffn_kernel, nf=nf, tf=tf, rc=rc, n_tiles=n_tiles),
        grid=(n_tiles + 1, nf + nn),
        in_specs=[
            pl.BlockSpec((rc, D_MODEL), lambda i, j: (norm(i) * nn + col(j), 0)),
            pl.BlockSpec((tm, tn), lambda i, j: (comp(i), ccol(i, j))),
            pl.BlockSpec((None, None, N_MOD, D_MODEL),
                         lambda i, j: (layer, mod_map(norm(i)), 0, 0)),
            pl.BlockSpec((None, None, N_MOD, tn),
                         lambda i, j: (layer, mod_map(comp(i)), 0, ccol(i, j))),
            _layer_spec((1, D_MODEL), layer),
            pl.BlockSpec((None, D_MODEL, tf), lambda i, j: (layer, 0, ftile(i, j))),
            pl.BlockSpec((None, D_MODEL, tf), lambda i, j: (layer, 0, ftile(i, j))),
            pl.BlockSpec((None, D_FF, tn), lambda i, j: (layer, 0, ccol(i, j))),
        ],
        out_specs=pl.BlockSpec((tm, tn), lambda i, j: (comp(i), ccol(i, j))),
        out_shape=jax.ShapeDtypeStruct((t, D_MODEL), F32),
        scratch_shapes=[pltpu.VMEM((2, tm, D_MODEL), BF16), pltpu.VMEM((nf, tm, tf), BF16)],
        compiler_params=_cparams(2),
        name="ffn",
    )(x, x, mods, mods, g, wg, wu, wd)


NORM_TM = 512


def _final_norm_kernel(x_ref, g_ref, y_ref):
    y_ref[...] = _rms(x_ref[...]) * g_ref[...]


def _final_norm(x, g, tm=NORM_TM):
    t = x.shape[0]
    return pl.pallas_call(
        _final_norm_kernel,
        grid=(t // tm,),
        in_specs=[pl.BlockSpec((tm, D_MODEL), lambda i: (i, 0)),
                  pl.BlockSpec((1, D_MODEL), lambda i: (0, 0))],
        out_specs=pl.BlockSpec((tm, D_MODEL), lambda i: (i, 0)),
        out_shape=jax.ShapeDtypeStruct((t, D_MODEL), F32),
        compiler_params=_cparams(1),
        name="final_norm",
    )(x, g)


def _rope_tables(n_tokens):
    n_rows = n_tokens // GRID_W
    row = jnp.repeat(jnp.arange(n_rows, dtype=F32), GRID_W)
    col = jnp.tile(jnp.arange(GRID_W, dtype=F32), n_rows)

    def cs(dim):
        n_freq = dim // 4
        inv = ROPE_THETA ** (-jnp.arange(n_freq, dtype=F32) / n_freq)
        ang = jnp.concatenate([row[:, None] * inv, col[:, None] * inv], axis=-1)
        return jnp.cos(ang), jnp.sin(ang)

    cos_a, sin_a = cs(HEAD_DIM)
    cos_b, sin_b = cs(ROPE_B)
    one = jnp.ones_like(cos_b)
    zero = jnp.zeros_like(sin_b)
    return (jnp.concatenate([cos_a, cos_a], axis=-1),
            jnp.concatenate([-sin_a, sin_a], axis=-1),
            jnp.concatenate([cos_b, one, cos_b, one], axis=-1),
            jnp.concatenate([-sin_b, zero, sin_b, zero], axis=-1))


def kernel(x_prompt, x_sample, cache_k_a, cache_v_a, cache_ckv_b, cache_krope_b, c, c_ctx, w_ada, b_ada, norm_attn, norm_ffn, w_in, qnorm_a, knorm_a, kvnorm_b, w_uk_b, w_uv_b, w_o, w_gate, w_up, w_down, norm_final):
    batch, seq, _ = x_prompt.shape
    dec_batch, dec_seq, _ = x_sample.shape
    assert dec_batch + 1 <= MOD_ROWS

    w_in_b = _wprep(w_in)
    wuk_b = w_uk_b.astype(BF16)
    wuv_b = w_uv_b.astype(BF16)
    w_o_b = w_o.astype(BF16)
    wg_b = w_gate.astype(BF16)
    wu_b = w_up.astype(BF16)
    wd_b = w_down.astype(BF16)
    rope_tabs = _rope_tables(dec_seq)

    cond = jnp.concatenate(
        [c_ctx[None, :], c, jnp.zeros((MOD_ROWS - 1 - dec_batch, D_MODEL), F32)], axis=0)
    mods = _adaln(cond, w_ada, b_ada).reshape(DEPTH, MOD_ROWS, N_MOD, D_MODEL)

    kc_a, vc_a, kc_b, vc_b = _cache_prep(
        cache_k_a, cache_v_a, cache_ckv_b, cache_krope_b, wuk_b, wuv_b)

    xp = x_prompt.reshape(batch * seq, D_MODEL)
    xs = x_sample.reshape(dec_batch * dec_seq, D_MODEL)
    g1 = norm_attn.reshape(DEPTH, 1, D_MODEL)
    g2 = norm_ffn.reshape(DEPTH, 1, D_MODEL)
    qn = qnorm_a.reshape(DEPTH, 1, HEAD_DIM)
    kn = knorm_a.reshape(DEPTH, 1, HEAD_DIM)
    kvn = kvnorm_b.reshape(DEPTH, 1, KV_RANK)
    ctx_mod = lambda i: 0

    def lat_mod(tm):
        return lambda i: 1 + i // (dec_seq // tm)

    r3 = lambda a: a.reshape(dec_batch, dec_seq, a.shape[-1])
    hb = ATTN_HEADS_PER_STEP
    new_bufs = None

    for l in range(DEPTH):
        outs = _proj(xp, l, mods, ctx_mod, g1, w_in_b, qn, kn, kvn, wuk_b, wuv_b,
                     new_shape=(batch, seq), new_bufs=new_bufs)
        qa, ka, va, qb, kb, vb = outs[:6]
        new_bufs = outs[6:]
        oa, ob = _attn_ctx(qa, ka, va, qb, kb, vb, seq)
        xp = _outproj(xp, oa, ob, l, mods, ctx_mod, w_o_b)
        xp = _ffn(xp, l, mods, ctx_mod, g2, wg_b, wu_b, wd_b)

        qa, ka, va, qb, kb, vb = _proj(
            xs, l, mods, lat_mod(PROJ_TM), g1, w_in_b, qn, kn, kvn, wuk_b, wuv_b,
            rope_tabs=rope_tabs)
        oa = _attn_lat(r3(qa), l, kc_a, r3(ka), vc_a, r3(va),
                       1, N_HEADS_A, N_KV_A, HEAD_DIM, HEAD_DIM, "attn_lat_a")
        ob = _attn_lat(r3(qb), l, kc_b, r3(kb), vc_b, r3(vb),
                       N_HEADS_B // hb, hb, hb, QB_PAD, VDIM_B, "attn_lat_b")
        xs = _outproj(xs, oa.reshape(-1, W_QA), ob.reshape(-1, W_VB), l, mods, lat_mod(OUT_TM), w_o_b)
        xs = _ffn(xs, l, mods, lat_mod(FFN_TM), g2, wg_b, wu_b, wd_b)

    gf = norm_final.reshape(1, D_MODEL)
    y_prompt = _final_norm(xp, gf).reshape(batch, seq, D_MODEL)
    y_sample = _final_norm(xs, gf).reshape(dec_batch, dec_seq, D_MODEL)
    new_k, new_v, new_ckv, new_kr = new_bufs
    return (y_prompt, y_sample,
            new_k.reshape(batch, DEPTH, seq, N_KV_A, HEAD_DIM),
            new_v.reshape(batch, DEPTH, seq, N_KV_A, HEAD_DIM),
            new_ckv, new_kr)
```

```python
import functools

import jax
import jax.numpy as jnp
import numpy as np
from jax import lax
from jax.experimental import pallas as pl
from jax.experimental.pallas import tpu as pltpu

D_MODEL = 2048
DEPTH = 4
GRID_W = 64
HEAD_DIM = 128
N_HEADS_A = 8
N_KV_A = 2
GROUP_A = N_HEADS_A // N_KV_A
N_HEADS_B = 8
NOPE_B = 128
ROPE_B = 64
VDIM_B = 128
KV_RANK = 256
D_FF = 5632
ROPE_THETA = 10000.0
EPS = 1e-6

W_QA = N_HEADS_A * HEAD_DIM
W_KA = N_KV_A * HEAD_DIM
QB_DIM = NOPE_B + ROPE_B
W_QB = N_HEADS_B * QB_DIM
QB_PAD = 256
KR_PAD = 128
W_QB_PAD = N_HEADS_B * QB_PAD
W_KNOPE = N_HEADS_B * NOPE_B
W_VB = N_HEADS_B * VDIM_B
N_MOD = 6
MOD_ROWS = 8

O_QB = W_QA + 2 * W_KA
O_CKV = O_QB + W_QB
O_KR = O_CKV + KV_RANK
IN_COLS = O_KR + ROPE_B

C_QA = 0
C_KA = C_QA + W_QA
C_VA = C_KA + W_KA
C_QB = C_VA + W_KA
C_CKV = C_QB + W_QB_PAD
C_KR = C_CKV + KV_RANK
IN_COLS_PAD = C_KR + KR_PAD

SCALE_A = HEAD_DIM ** -0.5
SCALE_B = QB_DIM ** -0.5

VMEM_LIMIT = 56 * 1024 * 1024

BF16 = jnp.bfloat16
F32 = jnp.float32


def _cparams(n_axes):
    return pltpu.CompilerParams(
        dimension_semantics=("arbitrary",) * n_axes, vmem_limit_bytes=VMEM_LIMIT)


def _layer_spec(shape, layer):
    nd = len(shape)
    return pl.BlockSpec((None,) + shape, lambda *_: (layer,) + (0,) * nd,
                        pipeline_mode=pl.Buffered(1))


def _mod_spec(layer, mod_map, width, col_map=None):
    if col_map is None:
        return pl.BlockSpec((None, None, N_MOD, width), lambda i, *_: (layer, mod_map(i), 0, 0))
    return pl.BlockSpec((None, None, N_MOD, width),
                        lambda i, j: (layer, mod_map(i), 0, col_map(j)))


def _dot(a, b):
    return jnp.dot(a, b, preferred_element_type=F32)


def _dot_t(a, b):
    return lax.dot_general(a, b, (((1,), (1,)), ((), ())), preferred_element_type=F32)


def _rms(x):
    return x * lax.rsqrt(jnp.mean(x * x, axis=-1, keepdims=True) + EPS)


ADA_TN = 1024


def _adaln_kernel(s_ref, w_ref, b_ref, o_ref):
    s = s_ref[...]
    s = s * jax.nn.sigmoid(s)
    o_ref[...] = _dot(s.astype(BF16), w_ref[...].astype(BF16)) + b_ref[...]


def _adaln(s, w_ada, b_ada):
    n = N_MOD * D_MODEL
    return pl.pallas_call(
        _adaln_kernel,
        grid=(DEPTH, n // ADA_TN),
        in_specs=[
            pl.BlockSpec((MOD_ROWS, D_MODEL), lambda l, j: (0, 0)),
            pl.BlockSpec((None, D_MODEL, ADA_TN), lambda l, j: (l, 0, j)),
            pl.BlockSpec((None, 1, ADA_TN), lambda l, j: (l, 0, j)),
        ],
        out_specs=pl.BlockSpec((None, MOD_ROWS, ADA_TN), lambda l, j: (l, 0, j)),
        out_shape=jax.ShapeDtypeStruct((DEPTH, MOD_ROWS, n), F32),
        compiler_params=_cparams(2),
        name="adaln",
    )(s, w_ada, b_ada.reshape(DEPTH, 1, n))


WPREP_TR = 512
PAIR_IN = 2 * QB_DIM
PAIR_OUT = 2 * QB_PAD


def _wprep_kernel(w_ref, pq_ref, pk_ref, o_ref):
    o_ref[:, 0:O_QB] = w_ref[:, 0:O_QB]
    pq = pq_ref[...]
    for i in range(N_HEADS_B // 2):
        src = w_ref[:, O_QB + i * PAIR_IN:O_QB + (i + 1) * PAIR_IN]
        o_ref[:, C_QB + i * PAIR_OUT:C_QB + (i + 1) * PAIR_OUT] = _dot(src, pq).astype(BF16)
    o_ref[:, C_CKV:C_CKV + KV_RANK] = w_ref[:, O_CKV:O_CKV + KV_RANK]
    o_ref[:, C_KR:C_KR + KR_PAD] = _dot(w_ref[:, O_KR:O_KR + ROPE_B], pk_ref[...]).astype(BF16)


def _rope_pad_src():
    half = ROPE_B // 2
    src = np.full((KR_PAD,), -1, np.int32)
    src[0:half] = np.arange(half)
    src[2 * half:3 * half] = half + np.arange(half)
    return src


def _selection(src, n_in):
    return (jnp.arange(n_in, dtype=jnp.int32)[:, None] == jnp.asarray(src)[None, :]).astype(BF16)


def _wprep(w_in):
    rsrc = _rope_pad_src()
    head = np.concatenate([np.arange(NOPE_B), np.where(rsrc >= 0, NOPE_B + rsrc, -1)])
    pair = np.concatenate([head, np.where(head >= 0, QB_DIM + head, -1)]).astype(np.int32)
    pq = _selection(pair, PAIR_IN)
    pk = _selection(rsrc, ROPE_B)
    return pl.pallas_call(
        _wprep_kernel,
        grid=(DEPTH, D_MODEL // WPREP_TR),
        in_specs=[
            pl.BlockSpec((None, WPREP_TR, IN_COLS), lambda l, r: (l, r, 0)),
            pl.BlockSpec((PAIR_IN, PAIR_OUT), lambda l, r: (0, 0)),
            pl.BlockSpec((ROPE_B, KR_PAD), lambda l, r: (0, 0)),
        ],
        out_specs=pl.BlockSpec((None, WPREP_TR, IN_COLS_PAD), lambda l, r: (l, r, 0)),
        out_shape=jax.ShapeDtypeStruct((DEPTH, D_MODEL, IN_COLS_PAD), BF16),
        compiler_params=_cparams(2),
        name="wprep",
    )(w_in, pq, pk)


def _cache_kernel(ka_ref, va_ref, ckv_ref, kr_ref, wuk_ref, wuv_ref, pk_ref,
                  ka_o, va_o, kb_o, vb_o):
    ka_o[...] = ka_ref[...].astype(BF16)
    va_o[...] = va_ref[...].astype(BF16)
    ckv = ckv_ref[...].astype(BF16)
    kn = _dot(ckv, wuk_ref[...])
    vb_o[...] = _dot(ckv, wuv_ref[...]).astype(BF16)
    kr = _dot(kr_ref[...].astype(BF16), pk_ref[...]).astype(BF16)
    for h in range(N_HEADS_B):
        kb_o[:, h * QB_PAD:h * QB_PAD + NOPE_B] = kn[:, h * NOPE_B:(h + 1) * NOPE_B].astype(BF16)
        kb_o[:, h * QB_PAD + NOPE_B:(h + 1) * QB_PAD] = kr


def _cache_prep(cache_k_a, cache_v_a, cache_ckv_b, cache_krope_b, wuk, wuv):
    nb, _, past = cache_ckv_b.shape[:3]
    ka = cache_k_a.reshape(nb, DEPTH, past, W_KA)
    va = cache_v_a.reshape(nb, DEPTH, past, W_KA)
    pk = _selection(_rope_pad_src(), ROPE_B)

    def in4(width):
        return pl.BlockSpec((None, None, past, width), lambda l, b: (b, l, 0, 0))

    def out4(width):
        return pl.BlockSpec((None, None, past, width), lambda l, b: (l, b, 0, 0))

    def w3(width):
        return pl.BlockSpec((None, KV_RANK, width), lambda l, b: (l, 0, 0))

    return pl.pallas_call(
        _cache_kernel,
        grid=(DEPTH, nb),
        in_specs=[in4(W_KA), in4(W_KA), in4(KV_RANK), in4(ROPE_B), w3(W_KNOPE), w3(W_VB),
                  pl.BlockSpec((ROPE_B, KR_PAD), lambda l, b: (0, 0))],
        out_specs=[out4(W_KA), out4(W_KA), out4(W_QB_PAD), out4(W_VB)],
        out_shape=[
            jax.ShapeDtypeStruct((DEPTH, nb, past, W_KA), BF16),
            jax.ShapeDtypeStruct((DEPTH, nb, past, W_KA), BF16),
            jax.ShapeDtypeStruct((DEPTH, nb, past, W_QB_PAD), BF16),
            jax.ShapeDtypeStruct((DEPTH, nb, past, W_VB), BF16),
        ],
        compiler_params=_cparams(2),
        name="cache_prep",
    )(ka, va, cache_ckv_b, cache_krope_b, wuk, wuv, pk)


PROJ_TM = 512
PROJ_ROWS = 256
N_NEW = 4


def _proj_kernel(*refs, rope, emit_new, n_alias):
    it = iter(refs)
    x_ref, mod_ref, g_ref, w_ref, qn_ref, kn_ref, kvn_ref, wuk_ref, wuv_ref = (
        next(it) for _ in range(9))
    if rope:
        ca_ref, sa_ref, cb_ref, sb_ref = (next(it) for _ in range(4))
    for _ in range(n_alias):
        next(it)
    qa_o, ka_o, va_o, qb_o, kb_o, vb_o = (next(it) for _ in range(6))
    if emit_new:
        nk_o, nv_o, nckv_o, nkr_o = (next(it) for _ in range(N_NEW))

    for r in range(x_ref.shape[0] // PROJ_ROWS):
        rows = slice(r * PROJ_ROWS, (r + 1) * PROJ_ROWS)

        h = _rms(x_ref[rows, :]) * g_ref[...]
        h = h * (1.0 + mod_ref[1:2, :]) + mod_ref[0:1, :]
        hb = h.astype(BF16)

        def rope_a(t):
            if not rope:
                return t
            return t * ca_ref[rows, :] + pltpu.roll(t, HEAD_DIM // 2, 1) * sa_ref[rows, :]

        def rope_b(t):
            if not rope:
                return t
            return t * cb_ref[rows, :] + pltpu.roll(t, KR_PAD // 2, 1) * sb_ref[rows, :]

        pq = _dot(hb, w_ref[:, C_QA:C_QA + W_QA])
        qn = qn_ref[...]
        for i in range(N_HEADS_A):
            sl = slice(i * HEAD_DIM, (i + 1) * HEAD_DIM)
            t = rope_a(_rms(pq[:, sl]) * qn)
            qa_o[rows, sl] = (t * SCALE_A).astype(BF16)

        pk = _dot(hb, w_ref[:, C_KA:C_KA + W_KA])
        pv = _dot(hb, w_ref[:, C_VA:C_VA + W_KA])
        kn = kn_ref[...]
        for i in range(N_KV_A):
            sl = slice(i * HEAD_DIM, (i + 1) * HEAD_DIM)
            t = _rms(pk[:, sl]) * kn
            if emit_new:
                nk_o[r, :, sl] = t
            ka_o[rows, sl] = rope_a(t).astype(BF16)
        va_o[rows, :] = pv.astype(BF16)
        if emit_new:
            nv_o[r] = pv

        pqb = _dot(hb, w_ref[:, C_QB:C_QB + W_QB_PAD])
        for i in range(N_HEADS_B):
            s0 = i * QB_PAD
            qb_o[rows, s0:s0 + NOPE_B] = (pqb[:, s0:s0 + NOPE_B] * SCALE_B).astype(BF16)
            t = rope_b(pqb[:, s0 + NOPE_B:s0 + QB_PAD])
            qb_o[rows, s0 + NOPE_B:s0 + QB_PAD] = (t * SCALE_B).astype(BF16)

        pc = _dot(hb, w_ref[:, C_CKV:C_CKV + KV_RANK])
        ckv = _rms(pc) * kvn_ref[...]
        pkr = _dot(hb, w_ref[:, C_KR:C_KR + KR_PAD])
        if emit_new:
            half = ROPE_B // 2
            nckv_o[r] = ckv
            nkr_o[r] = jnp.concatenate([pkr[:, 0:half], pkr[:, 2 * half:3 * half]], axis=-1)
        ckvb = ckv.astype(BF16)
        krb = rope_b(pkr).astype(BF16)
        kno = _dot(ckvb, wuk_ref[...])
        vb_o[rows, :] = _dot(ckvb, wuv_ref[...]).astype(BF16)
        for i in range(N_HEADS_B):
            s0 = i * QB_PAD
            kb_o[rows, s0:s0 + NOPE_B] = kno[:, i * NOPE_B:(i + 1) * NOPE_B].astype(BF16)
            kb_o[rows, s0 + NOPE_B:s0 + QB_PAD] = krb


def _proj(x, layer, mods, mod_map, g, w_in, qn, kn, kvn, wuk, wuv, rope_tabs=None,
          new_shape=None, new_bufs=None, tm=PROJ_TM):
    t = x.shape[0]
    rope = rope_tabs is not None
    emit_new = new_shape is not None

    def row(width):
        return pl.BlockSpec((tm, width), lambda i: (i, 0))

    in_specs = [
        row(D_MODEL),
        _mod_spec(layer, mod_map, D_MODEL),
        _layer_spec((1, D_MODEL), layer),
        _layer_spec((D_MODEL, IN_COLS_PAD), layer),
        _layer_spec((1, HEAD_DIM), layer),
        _layer_spec((1, HEAD_DIM), layer),
        _layer_spec((1, KV_RANK), layer),
        _layer_spec((KV_RANK, W_KNOPE), layer),
        _layer_spec((KV_RANK, W_VB), layer),
    ]
    args = [x, mods, g, w_in, qn, kn, kvn, wuk, wuv]
    if rope:
        seq_tiles = rope_tabs[0].shape[0] // tm
        for tab in rope_tabs:
            in_specs.append(pl.BlockSpec((tm, tab.shape[1]), lambda i: (i % seq_tiles, 0)))
            args.append(tab)
    widths = [W_QA, W_KA, W_KA, W_QB_PAD, W_QB_PAD, W_VB]
    out_specs = [row(w) for w in widths]
    out_shape = [jax.ShapeDtypeStruct((t, w), BF16) for w in widths]
    aliases = {}
    n_alias = 0
    if emit_new:
        batch, seq = new_shape
        assert seq == PROJ_ROWS and t == batch * seq
        nb = tm // seq
        nw = [W_KA, W_KA, KV_RANK, ROPE_B]
        if new_bufs is not None:
            n_alias = N_NEW
            for k, buf in enumerate(new_bufs):
                aliases[len(args)] = len(widths) + k
                in_specs.append(pl.BlockSpec(memory_space=pl.ANY))
                args.append(buf)
        out_specs += [pl.BlockSpec((nb, None, seq, w), lambda i: (i, layer, 0, 0)) for w in nw]
        out_shape += [jax.ShapeDtypeStruct((batch, DEPTH, seq, w), F32) for w in nw]
    return pl.pallas_call(
        functools.partial(_proj_kernel, rope=rope, emit_new=emit_new, n_alias=n_alias),
        grid=(t // tm,),
        in_specs=in_specs,
        out_specs=out_specs,
        out_shape=out_shape,
        input_output_aliases=aliases,
        compiler_params=_cparams(1),
        name="proj_lat" if rope else "proj_ctx",
    )(*args)


def _attend(q, k, v_ext, dv):
    s = _dot_t(q, k)
    p = jnp.exp(s - s.max(axis=-1, keepdims=True))
    o = _dot(p.astype(BF16), v_ext)
    return (o[:, 0:dv] / o[:, dv:2 * dv]).astype(BF16)


def _fill_v_ext(v_ext_ref, rows, v_ref, n_kv, dv):
    ones = jnp.ones((v_ref.shape[0], dv), BF16)
    for jk in range(n_kv):
        v_ext_ref[rows, 2 * jk * dv:(2 * jk + 1) * dv] = v_ref[:, jk * dv:(jk + 1) * dv]
        v_ext_ref[rows, (2 * jk + 1) * dv:(2 * jk + 2) * dv] = ones


def _attn_ctx_kernel(qa_ref, ka_ref, va_ref, qb_ref, kb_ref, vb_ref, oa_ref, ob_ref,
                     vae_ref, vbe_ref):
    _fill_v_ext(vae_ref, slice(None), va_ref, N_KV_A, HEAD_DIM)
    _fill_v_ext(vbe_ref, slice(None), vb_ref, N_HEADS_B, VDIM_B)
    for hq in range(N_HEADS_A):
        g = hq // GROUP_A
        qsl = slice(hq * HEAD_DIM, (hq + 1) * HEAD_DIM)
        oa_ref[:, qsl] = _attend(qa_ref[:, qsl], ka_ref[:, g * HEAD_DIM:(g + 1) * HEAD_DIM],
                                 vae_ref[:, 2 * g * HEAD_DIM:(2 * g + 2) * HEAD_DIM], HEAD_DIM)
    for hq in range(N_HEADS_B):
        qsl = slice(hq * QB_PAD, (hq + 1) * QB_PAD)
        ob_ref[:, hq * VDIM_B:(hq + 1) * VDIM_B] = _attend(
            qb_ref[:, qsl], kb_ref[:, qsl],
            vbe_ref[:, 2 * hq * VDIM_B:(2 * hq + 2) * VDIM_B], VDIM_B)


def _attn_ctx(qa, ka, va, qb, kb, vb, seq):
    t = qa.shape[0]

    def row(width):
        return pl.BlockSpec((seq, width), lambda b: (b, 0))

    return pl.pallas_call(
        _attn_ctx_kernel,
        grid=(t // seq,),
        in_specs=[row(W_QA), row(W_KA), row(W_KA), row(W_QB_PAD), row(W_QB_PAD), row(W_VB)],
        out_specs=[row(W_QA), row(W_VB)],
        out_shape=[jax.ShapeDtypeStruct((t, W_QA), BF16), jax.ShapeDtypeStruct((t, W_VB), BF16)],
        scratch_shapes=[pltpu.VMEM((seq, 2 * W_KA), BF16), pltpu.VMEM((seq, 2 * W_VB), BF16)],
        compiler_params=_cparams(1),
        name="attn_ctx",
    )(qa, ka, va, qb, kb, vb)


ATTN_TQ = 512
ATTN_HEADS_PER_STEP = 4
ATTN_ROW_SPLIT = 2


def _attn_lat_kernel(q_ref, kc_ref, kn_ref, vc_ref, vn_ref, o_ref, k_ref, v_ref,
                     *, n_q, n_kv, dq, dv, past):
    @pl.when(pl.program_id(2) == 0)
    def _():
        k_ref[0:past, :] = kc_ref[...]
        k_ref[past:, :] = kn_ref[...]
        _fill_v_ext(v_ref, slice(0, past), vc_ref, n_kv, dv)
        _fill_v_ext(v_ref, slice(past, None), vn_ref, n_kv, dv)

    rows = q_ref.shape[0] // ATTN_ROW_SPLIT
    for j in range(n_q):
        jk = j * n_kv // n_q
        for r in range(ATTN_ROW_SPLIT):
            rsl = slice(r * rows, (r + 1) * rows)
            o_ref[rsl, j * dv:(j + 1) * dv] = _attend(
                q_ref[rsl, j * dq:(j + 1) * dq], k_ref[:, jk * dq:(jk + 1) * dq],
                v_ref[:, 2 * jk * dv:(2 * jk + 2) * dv], dv)


def _attn_lat(q, layer, kc, kn, vc, vn, n_units, n_q, n_kv, dq, dv, name, tq=ATTN_TQ):
    nb, seq, _ = q.shape
    past = kc.shape[2]
    return pl.pallas_call(
        functools.partial(_attn_lat_kernel, n_q=n_q, n_kv=n_kv, dq=dq, dv=dv, past=past),
        grid=(nb, n_units, seq // tq),
        in_specs=[
            pl.BlockSpec((None, tq, n_q * dq), lambda b, u, i: (b, i, u)),
            pl.BlockSpec((None, None, past, n_kv * dq), lambda b, u, i: (layer, b, 0, u)),
            pl.BlockSpec((None, seq, n_kv * dq), lambda b, u, i: (b, 0, u)),
            pl.BlockSpec((None, None, past, n_kv * dv), lambda b, u, i: (layer, b, 0, u)),
            pl.BlockSpec((None, seq, n_kv * dv), lambda b, u, i: (b, 0, u)),
        ],
        out_specs=pl.BlockSpec((None, tq, n_q * dv), lambda b, u, i: (b, i, u)),
        out_shape=jax.ShapeDtypeStruct((nb, seq, n_units * n_q * dv), BF16),
        scratch_shapes=[pltpu.VMEM((past + seq, n_kv * dq), BF16),
                        pltpu.VMEM((past + seq, n_kv * 2 * dv), BF16)],
        compiler_params=_cparams(3),
        name=name,
    )(q, kc, kn, vc, vn)


OUT_TM = 512


def _outproj_kernel(x_ref, oa_ref, ob_ref, mod_ref, w_ref, y_ref):
    y = _dot(oa_ref[...], w_ref[0:W_QA, :]) + _dot(ob_ref[...], w_ref[W_QA:, :])
    y_ref[...] = x_ref[...] + mod_ref[2:3, :] * y


def _outproj(x, oa, ob, layer, mods, mod_map, w_o, tm=OUT_TM):
    t = x.shape[0]
    return pl.pallas_call(
        _outproj_kernel,
        grid=(t // tm,),
        in_specs=[
            pl.BlockSpec((tm, D_MODEL), lambda i: (i, 0)),
            pl.BlockSpec((tm, W_QA), lambda i: (i, 0)),
            pl.BlockSpec((tm, W_VB), lambda i: (i, 0)),
            _mod_spec(layer, mod_map, D_MODEL),
            _layer_spec((W_QA + W_VB, D_MODEL), layer),
        ],
        out_specs=pl.BlockSpec((tm, D_MODEL), lambda i: (i, 0)),
        out_shape=jax.ShapeDtypeStruct((t, D_MODEL), F32),
        compiler_params=_cparams(1),
        name="outproj",
    )(x, oa, ob, mods, w_o)


FFN_TM = 1024
FFN_TF = 512
FFN_TN = 512
FFN_ROWS = 256


def _ffn_kernel(xr_ref, xc_ref, mod_ref, modc_ref, g_ref, wg_ref, wu_ref, wd_ref, y_ref,
                h_ref, t_ref, *, nf, tf, rc, n_tiles):
    i = pl.program_id(0)
    j = pl.program_id(1)
    cur = (i + 1) % 2
    nxt = i % 2

    def norm_chunk():
        h = _rms(xr_ref[...]) * g_ref[...]
        h = h * (1.0 + mod_ref[4:5, :]) + mod_ref[3:4, :]
        r0 = pl.multiple_of((j - nf) * rc, rc)
        h_ref[nxt, pl.ds(r0, rc), :] = h.astype(BF16)

    row_blocks = [slice(m, m + FFN_ROWS) for m in range(0, y_ref.shape[0], FFN_ROWS)]

    def down_proj():
        for rows in row_blocks:
            d = _dot(t_ref[0, rows, :], wd_ref[0:tf, :])
            for k in range(1, nf):
                d = d + _dot(t_ref[k, rows, :], wd_ref[k * tf:(k + 1) * tf, :])
            y_ref[rows, :] = xc_ref[rows, :] + modc_ref[5:6, :] * d

    @pl.when((j < nf) & (i >= 1))
    def _():
        for rows in row_blocks:
            hb = h_ref[cur, rows, :]
            a = _dot(hb, wg_ref[...])
            b = _dot(hb, wu_ref[...])
            t_ref[j, rows, :] = (a * jax.nn.sigmoid(a) * b).astype(BF16)

    @pl.when((j >= nf) & (i >= 1) & (i < n_tiles))
    def _():
        norm_chunk()
        down_proj()

    @pl.when((j >= nf) & (i == 0))
    def _():
        norm_chunk()

    @pl.when((j >= nf) & (i == n_tiles))
    def _():
        down_proj()


def _ffn(x, layer, mods, mod_map, g, wg, wu, wd, tm=FFN_TM, tf=FFN_TF, tn=FFN_TN):
    t = x.shape[0]
    n_tiles = t // tm
    nf = D_FF // tf
    nn = D_MODEL // tn
    rc = tm // nn
    col = lambda j: jnp.maximum(j - nf, 0)
    comp = lambda i: jnp.maximum(i - 1, 0)
    norm = lambda i: jnp.minimum(i, n_tiles - 1)
    ccol = lambda i, j: jnp.where(i == 0, 0, col(j))
    ftile = lambda i, j: jnp.where(i == 0, 0, jnp.minimum(j, nf - 1))
    return pl.pallas_call(
        functools.partial(_ffn_kernel, nf=nf, tf=tf, rc=rc, n_tiles=n_tiles),
        grid=(n_tiles + 1, nf + nn),
        in_specs=[
            pl.BlockSpec((rc, D_MODEL), lambda i, j: (norm(i) * nn + col(j), 0)),
            pl.BlockSpec((tm, tn), lambda i, j: (comp(i), ccol(i, j))),
            pl.BlockSpec((None, None, N_MOD, D_MODEL),
                         lambda i, j: (layer, mod_map(norm(i)), 0, 0)),
            pl.BlockSpec((None, None, N_MOD, tn),
                         lambda i, j: (layer, mod_map(comp(i)), 0, ccol(i, j))),
            _layer_spec((1, D_MODEL), layer),
            pl.BlockSpec((None, D_MODEL, tf), lambda i, j: (layer, 0, ftile(i, j))),
            pl.BlockSpec((None, D_MODEL, tf), lambda i, j: (layer, 0, ftile(i, j))),
            pl.BlockSpec((None, D_FF, tn), lambda i, j: (layer, 0, ccol(i, j))),
        ],
        out_specs=pl.BlockSpec((tm, tn), lambda i, j: (comp(i), ccol(i, j))),
        out_shape=jax.ShapeDtypeStruct((t, D_MODEL), F32),
        scratch_shapes=[pltpu.VMEM((2, tm, D_MODEL), BF16), pltpu.VMEM((nf, tm, tf), BF16)],
        compiler_params=_cparams(2),
        name="ffn",
    )(x, x, mods, mods, g, wg, wu, wd)


NORM_TM = 512


def _final_norm_kernel(x_ref, g_ref, y_ref):
    y_ref[...] = _rms(x_ref[...]) * g_ref[...]


def _final_norm(x, g, tm=NORM_TM):
    t = x.shape[0]
    return pl.pallas_call(
        _final_norm_kernel,
        grid=(t // tm,),
        in_specs=[pl.BlockSpec((tm, D_MODEL), lambda i: (i, 0)),
                  pl.BlockSpec((1, D_MODEL), lambda i: (0, 0))],
        out_specs=pl.BlockSpec((tm, D_MODEL), lambda i: (i, 0)),
        out_shape=jax.ShapeDtypeStruct((t, D_MODEL), F32),
        compiler_params=_cparams(1),
        name="final_norm",
    )(x, g)


def _rope_tables(n_tokens):
    n_rows = n_tokens // GRID_W
    row = jnp.repeat(jnp.arange(n_rows, dtype=F32), GRID_W)
    col = jnp.tile(jnp.arange(GRID_W, dtype=F32), n_rows)

    def cs(dim):
        n_freq = dim // 4
        inv = ROPE_THETA ** (-jnp.arange(n_freq, dtype=F32) / n_freq)
        ang = jnp.concatenate([row[:, None] * inv, col[:, None] * inv], axis=-1)
        return jnp.cos(ang), jnp.sin(ang)

    cos_a, sin_a = cs(HEAD_DIM)
    cos_b, sin_b = cs(ROPE_B)
    one = jnp.ones_like(cos_b)
    zero = jnp.zeros_like(sin_b)
    return (jnp.concatenate([cos_a, cos_a], axis=-1),
            jnp.concatenate([-sin_a, sin_a], axis=-1),
            jnp.concatenate([cos_b, one, cos_b, one], axis=-1),
            jnp.concatenate([-sin_b, zero, sin_b, zero], axis=-1))


def kernel(x_prompt, x_sample, cache_k_a, cache_v_a, cache_ckv_b, cache_krope_b, c, c_ctx, w_ada, b_ada, norm_attn, norm_ffn, w_in, qnorm_a, knorm_a, kvnorm_b, w_uk_b, w_uv_b, w_o, w_gate, w_up, w_down, norm_final):
    batch, seq, _ = x_prompt.shape
    dec_batch, dec_seq, _ = x_sample.shape
    assert dec_batch + 1 <= MOD_ROWS

    w_in_b = _wprep(w_in.astype(BF16))
    wuk_b = w_uk_b.astype(BF16)
    wuv_b = w_uv_b.astype(BF16)
    w_o_b = w_o.astype(BF16)
    wg_b = w_gate.astype(BF16)
    wu_b = w_up.astype(BF16)
    wd_b = w_down.astype(BF16)
    rope_tabs = _rope_tables(dec_seq)

    cond = jnp.concatenate(
        [c_ctx[None, :], c, jnp.zeros((MOD_ROWS - 1 - dec_batch, D_MODEL), F32)], axis=0)
    mods = _adaln(cond, w_ada, b_ada).reshape(DEPTH, MOD_ROWS, N_MOD, D_MODEL)

    kc_a, vc_a, kc_b, vc_b = _cache_prep(
        cache_k_a, cache_v_a, cache_ckv_b, cache_krope_b, wuk_b, wuv_b)

    xp = x_prompt.reshape(batch * seq, D_MODEL)
    xs = x_sample.reshape(dec_batch * dec_seq, D_MODEL)
    g1 = norm_attn.reshape(DEPTH, 1, D_MODEL)
    g2 = norm_ffn.reshape(DEPTH, 1, D_MODEL)
    qn = qnorm_a.reshape(DEPTH, 1, HEAD_DIM)
    kn = knorm_a.reshape(DEPTH, 1, HEAD_DIM)
    kvn = kvnorm_b.reshape(DEPTH, 1, KV_RANK)
    ctx_mod = lambda i: 0

    def lat_mod(tm):
        return lambda i: 1 + i // (dec_seq // tm)

    r3 = lambda a: a.reshape(dec_batch, dec_seq, a.shape[-1])
    hb = ATTN_HEADS_PER_STEP
    new_bufs = None

    for l in range(DEPTH):
        outs = _proj(xp, l, mods, ctx_mod, g1, w_in_b, qn, kn, kvn, wuk_b, wuv_b,
                     new_shape=(batch, seq), new_bufs=new_bufs)
        qa, ka, va, qb, kb, vb = outs[:6]
        new_bufs = outs[6:]
        oa, ob = _attn_ctx(qa, ka, va, qb, kb, vb, seq)
        xp = _outproj(xp, oa, ob, l, mods, ctx_mod, w_o_b)
        xp = _ffn(xp, l, mods, ctx_mod, g2, wg_b, wu_b, wd_b)

        qa, ka, va, qb, kb, vb = _proj(
            xs, l, mods, lat_mod(PROJ_TM), g1, w_in_b, qn, kn, kvn, wuk_b, wuv_b,
            rope_tabs=rope_tabs)
        oa = _attn_lat(r3(qa), l, kc_a, r3(ka), vc_a, r3(va),
                       1, N_HEADS_A, N_KV_A, HEAD_DIM, HEAD_DIM, "attn_lat_a")
        ob = _attn_lat(r3(qb), l, kc_b, r3(kb), vc_b, r3(vb),
                       N_HEADS_B // hb, hb, hb, QB_PAD, VDIM_B, "attn_lat_b")
        xs = _outproj(xs, oa.reshape(-1, W_QA), ob.reshape(-1, W_VB), l, mods, lat_mod(OUT_TM), w_o_b)
        xs = _ffn(xs, l, mods, lat_mod(FFN_TM), g2, wg_b, wu_b, wd_b)

    gf = norm_final.reshape(1, D_MODEL)
    y_prompt = _final_norm(xp, gf).reshape(batch, seq, D_MODEL)
    y_sample = _final_norm(xs, gf).reshape(dec_batch, dec_seq, D_MODEL)
    new_k, new_v, new_ckv, new_kr = new_bufs
    return (y_prompt, y_sample,
            new_k.reshape(batch, DEPTH, seq, N_KV_A, HEAD_DIM),
            new_v.reshape(batch, DEPTH, seq, N_KV_A, HEAD_DIM),
            new_ckv, new_kr)
```

```python
import functools

import jax
import jax.numpy as jnp
import numpy as np
from jax import lax
from jax.experimental import pallas as pl
from jax.experimental.pallas import tpu as pltpu

D_MODEL = 2048
DEPTH = 4
GRID_W = 64
HEAD_DIM = 128
N_HEADS_A = 8
N_KV_A = 2
GROUP_A = N_HEADS_A // N_KV_A
N_HEADS_B = 8
NOPE_B = 128
ROPE_B = 64
VDIM_B = 128
KV_RANK = 256
D_FF = 5632
ROPE_THETA = 10000.0
EPS = 1e-6

W_QA = N_HEADS_A * HEAD_DIM
W_KA = N_KV_A * HEAD_DIM
QB_DIM = NOPE_B + ROPE_B
W_QB = N_HEADS_B * QB_DIM
QB_PAD = 256
KR_PAD = 128
W_QB_PAD = N_HEADS_B * QB_PAD
W_KNOPE = N_HEADS_B * NOPE_B
W_VB = N_HEADS_B * VDIM_B
N_MOD = 6
MOD_ROWS = 8

O_QB = W_QA + 2 * W_KA
O_CKV = O_QB + W_QB
O_KR = O_CKV + KV_RANK
IN_COLS = O_KR + ROPE_B

C_QA = 0
C_KA = C_QA + W_QA
C_VA = C_KA + W_KA
C_QB = C_VA + W_KA
C_CKV = C_QB + W_QB_PAD
C_KR = C_CKV + KV_RANK
IN_COLS_PAD = C_KR + KR_PAD

SCALE_A = HEAD_DIM ** -0.5
SCALE_B = QB_DIM ** -0.5

VMEM_LIMIT = 56 * 1024 * 1024

BF16 = jnp.bfloat16
F32 = jnp.float32


def _cparams(n_axes):
    return pltpu.CompilerParams(
        dimension_semantics=("arbitrary",) * n_axes, vmem_limit_bytes=VMEM_LIMIT)


def _layer_spec(shape, layer):
    nd = len(shape)
    return pl.BlockSpec((None,) + shape, lambda *_: (layer,) + (0,) * nd,
                        pipeline_mode=pl.Buffered(1))


def _mod_spec(layer, mod_map, width, col_map=None):
    if col_map is None:
        return pl.BlockSpec((None, None, N_MOD, width), lambda i, *_: (layer, mod_map(i), 0, 0))
    return pl.BlockSpec((None, None, N_MOD, width),
                        lambda i, j: (layer, mod_map(i), 0, col_map(j)))


def _dot(a, b):
    return jnp.dot(a, b, preferred_element_type=F32)


def _dot_t(a, b):
    return lax.dot_general(a, b, (((1,), (1,)), ((), ())), preferred_element_type=F32)


def _rms(x):
    return x * lax.rsqrt(jnp.mean(x * x, axis=-1, keepdims=True) + EPS)


ADA_TN = 1024


def _adaln_kernel(s_ref, w_ref, b_ref, o_ref):
    s = s_ref[...]
    s = s * jax.nn.sigmoid(s)
    o_ref[...] = _dot(s.astype(BF16), w_ref[...].astype(BF16)) + b_ref[...]


def _adaln(s, w_ada, b_ada):
    n = N_MOD * D_MODEL
    return pl.pallas_call(
        _adaln_kernel,
        grid=(DEPTH, n // ADA_TN),
        in_specs=[
            pl.BlockSpec((MOD_ROWS, D_MODEL), lambda l, j: (0, 0)),
            pl.BlockSpec((None, D_MODEL, ADA_TN), lambda l, j: (l, 0, j)),
            pl.BlockSpec((None, 1, ADA_TN), lambda l, j: (l, 0, j)),
        ],
        out_specs=pl.BlockSpec((None, MOD_ROWS, ADA_TN), lambda l, j: (l, 0, j)),
        out_shape=jax.ShapeDtypeStruct((DEPTH, MOD_ROWS, n), F32),
        compiler_params=_cparams(2),
        name="adaln",
    )(s, w_ada, b_ada.reshape(DEPTH, 1, n))


WPREP_TR = 512
PAIR_IN = 2 * QB_DIM
PAIR_OUT = 2 * QB_PAD


def _wprep_kernel(w_ref, pq_ref, pk_ref, o_ref):
    o_ref[:, 0:O_QB] = w_ref[:, 0:O_QB]
    pq = pq_ref[...]
    for i in range(N_HEADS_B // 2):
        src = w_ref[:, O_QB + i * PAIR_IN:O_QB + (i + 1) * PAIR_IN]
        o_ref[:, C_QB + i * PAIR_OUT:C_QB + (i + 1) * PAIR_OUT] = _dot(src, pq).astype(BF16)
    o_ref[:, C_CKV:C_CKV + KV_RANK] = w_ref[:, O_CKV:O_CKV + KV_RANK]
    o_ref[:, C_KR:C_KR + KR_PAD] = _dot(w_ref[:, O_KR:O_KR + ROPE_B], pk_ref[...]).astype(BF16)


def _rope_pad_src():
    half = ROPE_B // 2
    src = np.full((KR_PAD,), -1, np.int32)
    src[0:half] = np.arange(half)
    src[2 * half:3 * half] = half + np.arange(half)
    return src


def _selection(src, n_in):
    return (jnp.arange(n_in, dtype=jnp.int32)[:, None] == jnp.asarray(src)[None, :]).astype(BF16)


def _wprep(w_in):
    rsrc = _rope_pad_src()
    head = np.concatenate([np.arange(NOPE_B), np.where(rsrc >= 0, NOPE_B + rsrc, -1)])
    pair = np.concatenate([head, np.where(head >= 0, QB_DIM + head, -1)]).astype(np.int32)
    pq = _selection(pair, PAIR_IN)
    pk = _selection(rsrc, ROPE_B)
    return pl.pallas_call(
        _wprep_kernel,
        grid=(DEPTH, D_MODEL // WPREP_TR),
        in_specs=[
            pl.BlockSpec((None, WPREP_TR, IN_COLS), lambda l, r: (l, r, 0)),
            pl.BlockSpec((PAIR_IN, PAIR_OUT), lambda l, r: (0, 0)),
            pl.BlockSpec((ROPE_B, KR_PAD), lambda l, r: (0, 0)),
        ],
        out_specs=pl.BlockSpec((None, WPREP_TR, IN_COLS_PAD), lambda l, r: (l, r, 0)),
        out_shape=jax.ShapeDtypeStruct((DEPTH, D_MODEL, IN_COLS_PAD), BF16),
        compiler_params=_cparams(2),
        name="wprep",
    )(w_in, pq, pk)


def _cache_kernel(ka_ref, va_ref, ckv_ref, kr_ref, wuk_ref, wuv_ref, pk_ref,
                  ka_o, va_o, kb_o, vb_o):
    ka_o[...] = ka_ref[...].astype(BF16)
    va_o[...] = va_ref[...].astype(BF16)
    ckv = ckv_ref[...].astype(BF16)
    kn = _dot(ckv, wuk_ref[...])
    vb_o[...] = _dot(ckv, wuv_ref[...]).astype(BF16)
    kr = _dot(kr_ref[...].astype(BF16), pk_ref[...]).astype(BF16)
    for h in range(N_HEADS_B):
        kb_o[:, h * QB_PAD:h * QB_PAD + NOPE_B] = kn[:, h * NOPE_B:(h + 1) * NOPE_B].astype(BF16)
        kb_o[:, h * QB_PAD + NOPE_B:(h + 1) * QB_PAD] = kr


def _cache_prep(cache_k_a, cache_v_a, cache_ckv_b, cache_krope_b, wuk, wuv):
    nb, _, past = cache_ckv_b.shape[:3]
    ka = cache_k_a.reshape(nb, DEPTH, past, W_KA)
    va = cache_v_a.reshape(nb, DEPTH, past, W_KA)
    pk = _selection(_rope_pad_src(), ROPE_B)

    def in4(width):
        return pl.BlockSpec((None, None, past, width), lambda l, b: (b, l, 0, 0))

    def out4(width):
        return pl.BlockSpec((None, None, past, width), lambda l, b: (l, b, 0, 0))

    def w3(width):
        return pl.BlockSpec((None, KV_RANK, width), lambda l, b: (l, 0, 0))

    return pl.pallas_call(
        _cache_kernel,
        grid=(DEPTH, nb),
        in_specs=[in4(W_KA), in4(W_KA), in4(KV_RANK), in4(ROPE_B), w3(W_KNOPE), w3(W_VB),
                  pl.BlockSpec((ROPE_B, KR_PAD), lambda l, b: (0, 0))],
        out_specs=[out4(W_KA), out4(W_KA), out4(W_QB_PAD), out4(W_VB)],
        out_shape=[
            jax.ShapeDtypeStruct((DEPTH, nb, past, W_KA), BF16),
            jax.ShapeDtypeStruct((DEPTH, nb, past, W_KA), BF16),
            jax.ShapeDtypeStruct((DEPTH, nb, past, W_QB_PAD), BF16),
            jax.ShapeDtypeStruct((DEPTH, nb, past, W_VB), BF16),
        ],
        compiler_params=_cparams(2),
        name="cache_prep",
    )(ka, va, cache_ckv_b, cache_krope_b, wuk, wuv, pk)


PROJ_TM = 512
PROJ_ROWS = 256
N_NEW = 4


def _proj_kernel(*refs, rope, emit_new, n_alias):
    it = iter(refs)
    x_ref, mod_ref, g_ref, w_ref, qn_ref, kn_ref, kvn_ref, wuk_ref, wuv_ref = (
        next(it) for _ in range(9))
    if rope:
        ca_ref, sa_ref, cb_ref, sb_ref = (next(it) for _ in range(4))
    for _ in range(n_alias):
        next(it)
    qa_o, ka_o, va_o, qb_o, kb_o, vb_o = (next(it) for _ in range(6))
    if emit_new:
        nk_o, nv_o, nckv_o, nkr_o = (next(it) for _ in range(N_NEW))

    for r in range(x_ref.shape[0] // PROJ_ROWS):
        rows = slice(r * PROJ_ROWS, (r + 1) * PROJ_ROWS)

        h = _rms(x_ref[rows, :]) * g_ref[...]
        h = h * (1.0 + mod_ref[1:2, :]) + mod_ref[0:1, :]
        hb = h.astype(BF16)

        def rope_a(t):
            if not rope:
                return t
            return t * ca_ref[rows, :] + pltpu.roll(t, HEAD_DIM // 2, 1) * sa_ref[rows, :]

        def rope_b(t):
            if not rope:
                return t
            return t * cb_ref[rows, :] + pltpu.roll(t, KR_PAD // 2, 1) * sb_ref[rows, :]

        pq = _dot(hb, w_ref[:, C_QA:C_QA + W_QA])
        qn = qn_ref[...]
        for i in range(N_HEADS_A):
            sl = slice(i * HEAD_DIM, (i + 1) * HEAD_DIM)
            t = rope_a(_rms(pq[:, sl]) * qn)
            qa_o[rows, sl] = (t * SCALE_A).astype(BF16)

        pk = _dot(hb, w_ref[:, C_KA:C_KA + W_KA])
        pv = _dot(hb, w_ref[:, C_VA:C_VA + W_KA])
        kn = kn_ref[...]
        for i in range(N_KV_A):
            sl = slice(i * HEAD_DIM, (i + 1) * HEAD_DIM)
            t = _rms(pk[:, sl]) * kn
            if emit_new:
                nk_o[r, :, i, :] = t
            ka_o[rows, sl] = rope_a(t).astype(BF16)
        va_o[rows, :] = pv.astype(BF16)
        if emit_new:
            for i in range(N_KV_A):
                nv_o[r, :, i, :] = pv[:, i * HEAD_DIM:(i + 1) * HEAD_DIM]

        pqb = _dot(hb, w_ref[:, C_QB:C_QB + W_QB_PAD])
        for i in range(N_HEADS_B):
            s0 = i * QB_PAD
            qb_o[rows, s0:s0 + NOPE_B] = (pqb[:, s0:s0 + NOPE_B] * SCALE_B).astype(BF16)
            t = rope_b(pqb[:, s0 + NOPE_B:s0 + QB_PAD])
            qb_o[rows, s0 + NOPE_B:s0 + QB_PAD] = (t * SCALE_B).astype(BF16)

        pc = _dot(hb, w_ref[:, C_CKV:C_CKV + KV_RANK])
        ckv = _rms(pc) * kvn_ref[...]
        pkr = _dot(hb, w_ref[:, C_KR:C_KR + KR_PAD])
        if emit_new:
            half = ROPE_B // 2
            nckv_o[r] = ckv
            nkr_o[r] = jnp.concatenate([pkr[:, 0:half], pkr[:, 2 * half:3 * half]], axis=-1)
        ckvb = ckv.astype(BF16)
        krb = rope_b(pkr).astype(BF16)
        kno = _dot(ckvb, wuk_ref[...])
        vb_o[rows, :] = _dot(ckvb, wuv_ref[...]).astype(BF16)
        for i in range(N_HEADS_B):
            s0 = i * QB_PAD
            kb_o[rows, s0:s0 + NOPE_B] = kno[:, i * NOPE_B:(i + 1) * NOPE_B].astype(BF16)
            kb_o[rows, s0 + NOPE_B:s0 + QB_PAD] = krb


def _proj(x, layer, mods, mod_map, g, w_in, qn, kn, kvn, wuk, wuv, rope_tabs=None,
          new_shape=None, new_bufs=None, tm=PROJ_TM):
    t = x.shape[0]
    rope = rope_tabs is not None
    emit_new = new_shape is not None

    def row(width):
        return pl.BlockSpec((tm, width), lambda i: (i, 0))

    in_specs = [
        row(D_MODEL),
        _mod_spec(layer, mod_map, D_MODEL),
        _layer_spec((1, D_MODEL), layer),
        _layer_spec((D_MODEL, IN_COLS_PAD), layer),
        _layer_spec((1, HEAD_DIM), layer),
        _layer_spec((1, HEAD_DIM), layer),
        _layer_spec((1, KV_RANK), layer),
        _layer_spec((KV_RANK, W_KNOPE), layer),
        _layer_spec((KV_RANK, W_VB), layer),
    ]
    args = [x, mods, g, w_in, qn, kn, kvn, wuk, wuv]
    if rope:
        seq_tiles = rope_tabs[0].shape[0] // tm
        for tab in rope_tabs:
            in_specs.append(pl.BlockSpec((tm, tab.shape[1]), lambda i: (i % seq_tiles, 0)))
            args.append(tab)
    widths = [W_QA, W_KA, W_KA, W_QB_PAD, W_QB_PAD, W_VB]
    out_specs = [row(w) for w in widths]
    out_shape = [jax.ShapeDtypeStruct((t, w), BF16) for w in widths]
    aliases = {}
    n_alias = 0
    if emit_new:
        batch, seq = new_shape
        assert seq == PROJ_ROWS and t == batch * seq
        nb = tm // seq
        tails = [(N_KV_A, HEAD_DIM), (N_KV_A, HEAD_DIM), (KV_RANK,), (ROPE_B,)]
        if new_bufs is not None:
            n_alias = N_NEW
            for k, buf in enumerate(new_bufs):
                aliases[len(args)] = len(widths) + k
                in_specs.append(pl.BlockSpec(memory_space=pl.ANY))
                args.append(buf)
        for tail in tails:
            zeros = (0,) * len(tail)
            out_specs.append(pl.BlockSpec((nb, None, seq) + tail,
                                          lambda i, zeros=zeros: (i, layer, 0) + zeros))
            out_shape.append(jax.ShapeDtypeStruct((batch, DEPTH, seq) + tail, F32))
    return pl.pallas_call(
        functools.partial(_proj_kernel, rope=rope, emit_new=emit_new, n_alias=n_alias),
        grid=(t // tm,),
        in_specs=in_specs,
        out_specs=out_specs,
        out_shape=out_shape,
        input_output_aliases=aliases,
        compiler_params=_cparams(1),
        name="proj_lat" if rope else "proj_ctx",
    )(*args)


def _attend(q, k, v_ext, dv):
    s = _dot_t(q, k)
    p = jnp.exp(s - s.max(axis=-1, keepdims=True))
    o = _dot(p.astype(BF16), v_ext)
    return (o[:, 0:dv] / o[:, dv:2 * dv]).astype(BF16)


def _attend_parts(q, k_parts, v_ext_ref, dv):
    s = [_dot_t(q, k) for k in k_parts]
    m = s[0].max(axis=-1, keepdims=True)
    for sp in s[1:]:
        m = jnp.maximum(m, sp.max(axis=-1, keepdims=True))
    o = None
    r0 = 0
    for sp in s:
        r1 = r0 + sp.shape[1]
        op = _dot(jnp.exp(sp - m).astype(BF16), v_ext_ref[r0:r1, :])
        o = op if o is None else o + op
        r0 = r1
    return (o[:, 0:dv] / o[:, dv:2 * dv]).astype(BF16)


def _fill_v_ext(v_ext_ref, rows, v_ref, n_kv, dv):
    ones = jnp.ones((v_ref.shape[0], dv), BF16)
    for jk in range(n_kv):
        v_ext_ref[rows, 2 * jk * dv:(2 * jk + 1) * dv] = v_ref[:, jk * dv:(jk + 1) * dv]
        v_ext_ref[rows, (2 * jk + 1) * dv:(2 * jk + 2) * dv] = ones


def _attn_ctx_kernel(qa_ref, ka_ref, va_ref, qb_ref, kb_ref, vb_ref, oa_ref, ob_ref,
                     vae_ref, vbe_ref):
    _fill_v_ext(vae_ref, slice(None), va_ref, N_KV_A, HEAD_DIM)
    _fill_v_ext(vbe_ref, slice(None), vb_ref, N_HEADS_B, VDIM_B)
    for hq in range(N_HEADS_A):
        g = hq // GROUP_A
        qsl = slice(hq * HEAD_DIM, (hq + 1) * HEAD_DIM)
        oa_ref[:, qsl] = _attend(qa_ref[:, qsl], ka_ref[:, g * HEAD_DIM:(g + 1) * HEAD_DIM],
                                 vae_ref[:, 2 * g * HEAD_DIM:(2 * g + 2) * HEAD_DIM], HEAD_DIM)
    for hq in range(N_HEADS_B):
        qsl = slice(hq * QB_PAD, (hq + 1) * QB_PAD)
        ob_ref[:, hq * VDIM_B:(hq + 1) * VDIM_B] = _attend(
            qb_ref[:, qsl], kb_ref[:, qsl],
            vbe_ref[:, 2 * hq * VDIM_B:(2 * hq + 2) * VDIM_B], VDIM_B)


def _attn_ctx(qa, ka, va, qb, kb, vb, seq):
    t = qa.shape[0]

    def row(width):
        return pl.BlockSpec((seq, width), lambda b: (b, 0))

    return pl.pallas_call(
        _attn_ctx_kernel,
        grid=(t // seq,),
        in_specs=[row(W_QA), row(W_KA), row(W_KA), row(W_QB_PAD), row(W_QB_PAD), row(W_VB)],
        out_specs=[row(W_QA), row(W_VB)],
        out_shape=[jax.ShapeDtypeStruct((t, W_QA), BF16), jax.ShapeDtypeStruct((t, W_VB), BF16)],
        scratch_shapes=[pltpu.VMEM((seq, 2 * W_KA), BF16), pltpu.VMEM((seq, 2 * W_VB), BF16)],
        compiler_params=_cparams(1),
        name="attn_ctx",
    )(qa, ka, va, qb, kb, vb)


ATTN_TQ = 512
ATTN_HEADS_PER_STEP = 8
ATTN_ROW_SPLIT = 2


def _attn_lat_kernel(q_ref, kc_ref, kn_ref, vc_ref, vn_ref, o_ref, v_ref, *k_scratch,
                     n_q, n_kv, dq, dv, past):
    @pl.when(pl.program_id(2) == 0)
    def _():
        _fill_v_ext(v_ref, slice(0, past), vc_ref, n_kv, dv)
        _fill_v_ext(v_ref, slice(past, None), vn_ref, n_kv, dv)
        if k_scratch:
            k_scratch[0][0:past, :] = kc_ref[...]
            k_scratch[0][past:, :] = kn_ref[...]

    rows = q_ref.shape[0] // ATTN_ROW_SPLIT
    for j in range(n_q):
        jk = j * n_kv // n_q
        ksl = slice(jk * dq, (jk + 1) * dq)
        vsl = slice(2 * jk * dv, (2 * jk + 2) * dv)
        for r in range(ATTN_ROW_SPLIT):
            rsl = slice(r * rows, (r + 1) * rows)
            q = q_ref[rsl, j * dq:(j + 1) * dq]
            if k_scratch:
                o = _attend(q, k_scratch[0][:, ksl], v_ref[:, vsl], dv)
            else:
                o = _attend_parts(q, [kc_ref[:, ksl], kn_ref[:, ksl]], v_ref.at[:, vsl], dv)
            o_ref[rsl, j * dv:(j + 1) * dv] = o


def _attn_lat(q, layer, kc, kn, vc, vn, n_units, n_q, n_kv, dq, dv, name, join_k, tq=ATTN_TQ):
    nb, seq, _ = q.shape
    past = kc.shape[2]
    scratch = [pltpu.VMEM((past + seq, n_kv * 2 * dv), BF16)]
    if join_k:
        scratch.append(pltpu.VMEM((past + seq, n_kv * dq), BF16))
    return pl.pallas_call(
        functools.partial(_attn_lat_kernel, n_q=n_q, n_kv=n_kv, dq=dq, dv=dv, past=past),
        grid=(nb, n_units, seq // tq),
        in_specs=[
            pl.BlockSpec((None, tq, n_q * dq), lambda b, u, i: (b, i, u)),
            pl.BlockSpec((None, None, past, n_kv * dq), lambda b, u, i: (layer, b, 0, u)),
            pl.BlockSpec((None, seq, n_kv * dq), lambda b, u, i: (b, 0, u)),
            pl.BlockSpec((None, None, past, n_kv * dv), lambda b, u, i: (layer, b, 0, u)),
            pl.BlockSpec((None, seq, n_kv * dv), lambda b, u, i: (b, 0, u)),
        ],
        out_specs=pl.BlockSpec((None, tq, n_q * dv), lambda b, u, i: (b, i, u)),
        out_shape=jax.ShapeDtypeStruct((nb, seq, n_units * n_q * dv), BF16),
        scratch_shapes=scratch,
        compiler_params=_cparams(3),
        name=name,
    )(q, kc, kn, vc, vn)


OUT_TM = 512


def _outproj_kernel(x_ref, oa_ref, ob_ref, mod_ref, w_ref, y_ref):
    y = _dot(oa_ref[...], w_ref[0:W_QA, :]) + _dot(ob_ref[...], w_ref[W_QA:, :])
    y_ref[...] = x_ref[...] + mod_ref[2:3, :] * y


def _outproj(x, oa, ob, layer, mods, mod_map, w_o, tm=OUT_TM):
    t = x.shape[0]
    return pl.pallas_call(
        _outproj_kernel,
        grid=(t // tm,),
        in_specs=[
            pl.BlockSpec((tm, D_MODEL), lambda i: (i, 0)),
            pl.BlockSpec((tm, W_QA), lambda i: (i, 0)),
            pl.BlockSpec((tm, W_VB), lambda i: (i, 0)),
            _mod_spec(layer, mod_map, D_MODEL),
            _layer_spec((W_QA + W_VB, D_MODEL), layer),
        ],
        out_specs=pl.BlockSpec((tm, D_MODEL), lambda i: (i, 0)),
        out_shape=jax.ShapeDtypeStruct((t, D_MODEL), F32),
        compiler_params=_cparams(1),
        name="outproj",
    )(x, oa, ob, mods, w_o)


FFN_TM = 1024
FFN_TF = 512
FFN_TN = 512
FFN_ROWS = 256


def _ffn_kernel(xr_ref, xc_ref, mod_ref, modc_ref, g_ref, wg_ref, wu_ref, wd_ref, y_ref,
                h_ref, t_ref, *, nf, tf, rc, n_tiles):
    i = pl.program_id(0)
    j = pl.program_id(1)
    cur = (i + 1) % 2
    nxt = i % 2

    def norm_chunk():
        h = _rms(xr_ref[...]) * g_ref[...]
        h = h * (1.0 + mod_ref[4:5, :]) + mod_ref[3:4, :]
        r0 = pl.multiple_of((j - nf) * rc, rc)
        h_ref[nxt, pl.ds(r0, rc), :] = h.astype(BF16)

    row_blocks = [slice(m, m + FFN_ROWS) for m in range(0, y_ref.shape[0], FFN_ROWS)]

    def down_proj():
        for rows in row_blocks:
            d = _dot(t_ref[0, rows, :], wd_ref[0:tf, :])
            for k in range(1, nf):
                d = d + _dot(t_ref[k, rows, :], wd_ref[k * tf:(k + 1) * tf, :])
            y_ref[rows, :] = xc_ref[rows, :] + modc_ref[5:6, :] * d

    @pl.when((j < nf) & (i >= 1))
    def _():
        for rows in row_blocks:
            hb = h_ref[cur, rows, :]
            a = _dot(hb, wg_ref[...])
            b = _dot(hb, wu_ref[...])
            t_ref[j, rows, :] = (a * jax.nn.sigmoid(a) * b).astype(BF16)

    @pl.when((j >= nf) & (i >= 1) & (i < n_tiles))
    def _():
        norm_chunk()
        down_proj()

    @pl.when((j >= nf) & (i == 0))
    def _():
        norm_chunk()

    @pl.when((j >= nf) & (i == n_tiles))
    def _():
        down_proj()


def _ffn(x, layer, mods, mod_map, g, wg, wu, wd, tm=FFN_TM, tf=FFN_TF, tn=FFN_TN):
    t = x.shape[0]
    n_tiles = t // tm
    nf = D_FF // tf
    nn = D_MODEL // tn
    rc = tm // nn
    col = lambda j: jnp.maximum(j - nf, 0)
    comp = lambda i: jnp.maximum(i - 1, 0)
    norm = lambda i: jnp.minimum(i, n_tiles - 1)
    ccol = lambda i, j: jnp.where(i == 0, 0, col(j))
    ftile = lambda i, j: jnp.where(i == 0, 0, jnp.minimum(j, nf - 1))
    return pl.pallas_call(
        functools.partial(_ffn_kernel, nf=nf, tf=tf, rc=rc, n_tiles=n_tiles),
        grid=(n_tiles + 1, nf + nn),
        in_specs=[
            pl.BlockSpec((rc, D_MODEL), lambda i, j: (norm(i) * nn + col(j), 0)),
            pl.BlockSpec((tm, tn), lambda i, j: (comp(i), ccol(i, j))),
            pl.BlockSpec((None, None, N_MOD, D_MODEL),
                         lambda i, j: (layer, mod_map(norm(i)), 0, 0)),
            pl.BlockSpec((None, None, N_MOD, tn),
                         lambda i, j: (layer, mod_map(comp(i)), 0, ccol(i, j))),
            _layer_spec((1, D_MODEL), layer),
            pl.BlockSpec((None, D_MODEL, tf), lambda i, j: (layer, 0, ftile(i, j))),
            pl.BlockSpec((None, D_MODEL, tf), lambda i, j: (layer, 0, ftile(i, j))),
            pl.BlockSpec((None, D_FF, tn), lambda i, j: (layer, 0, ccol(i, j))),
        ],
        out_specs=pl.BlockSpec((tm, tn), lambda i, j: (comp(i), ccol(i, j))),
        out_shape=jax.ShapeDtypeStruct((t, D_MODEL), F32),
        scratch_shapes=[pltpu.VMEM((2, tm, D_MODEL), BF16), pltpu.VMEM((nf, tm, tf), BF16)],
        compiler_params=_cparams(2),
        name="ffn",
    )(x, x, mods, mods, g, wg, wu, wd)


NORM_TM = 512


def _final_norm_kernel(x_ref, g_ref, y_ref):
    y_ref[...] = _rms(x_ref[...]) * g_ref[...]


def _final_norm(x, g, tm=NORM_TM):
    t = x.shape[0]
    return pl.pallas_call(
        _final_norm_kernel,
        grid=(t // tm,),
        in_specs=[pl.BlockSpec((tm, D_MODEL), lambda i: (i, 0)),
                  pl.BlockSpec((1, D_MODEL), lambda i: (0, 0))],
        out_specs=pl.BlockSpec((tm, D_MODEL), lambda i: (i, 0)),
        out_shape=jax.ShapeDtypeStruct((t, D_MODEL), F32),
        compiler_params=_cparams(1),
        name="final_norm",
    )(x, g)


def _rope_tables(n_tokens):
    n_rows = n_tokens // GRID_W
    row = jnp.repeat(jnp.arange(n_rows, dtype=F32), GRID_W)
    col = jnp.tile(jnp.arange(GRID_W, dtype=F32), n_rows)

    def cs(dim):
        n_freq = dim // 4
        inv = ROPE_THETA ** (-jnp.arange(n_freq, dtype=F32) / n_freq)
        ang = jnp.concatenate([row[:, None] * inv, col[:, None] * inv], axis=-1)
        return jnp.cos(ang), jnp.sin(ang)

    cos_a, sin_a = cs(HEAD_DIM)
    cos_b, sin_b = cs(ROPE_B)
    one = jnp.ones_like(cos_b)
    zero = jnp.zeros_like(sin_b)
    return (jnp.concatenate([cos_a, cos_a], axis=-1),
            jnp.concatenate([-sin_a, sin_a], axis=-1),
            jnp.concatenate([cos_b, one, cos_b, one], axis=-1),
            jnp.concatenate([-sin_b, zero, sin_b, zero], axis=-1))


def kernel(x_prompt, x_sample, cache_k_a, cache_v_a, cache_ckv_b, cache_krope_b, c, c_ctx, w_ada, b_ada, norm_attn, norm_ffn, w_in, qnorm_a, knorm_a, kvnorm_b, w_uk_b, w_uv_b, w_o, w_gate, w_up, w_down, norm_final):
    batch, seq, _ = x_prompt.shape
    dec_batch, dec_seq, _ = x_sample.shape
    assert dec_batch + 1 <= MOD_ROWS

    w_in_b = _wprep(w_in.astype(BF16))
    wuk_b = w_uk_b.astype(BF16)
    wuv_b = w_uv_b.astype(BF16)
    w_o_b = w_o.astype(BF16)
    wg_b = w_gate.astype(BF16)
    wu_b = w_up.astype(BF16)
    wd_b = w_down.astype(BF16)
    rope_tabs = _rope_tables(dec_seq)

    cond = jnp.concatenate(
        [c_ctx[None, :], c, jnp.zeros((MOD_ROWS - 1 - dec_batch, D_MODEL), F32)], axis=0)
    mods = _adaln(cond, w_ada, b_ada).reshape(DEPTH, MOD_ROWS, N_MOD, D_MODEL)

    kc_a, vc_a, kc_b, vc_b = _cache_prep(
        cache_k_a, cache_v_a, cache_ckv_b, cache_krope_b, wuk_b, wuv_b)

    xp = x_prompt.reshape(batch * seq, D_MODEL)
    xs = x_sample.reshape(dec_batch * dec_seq, D_MODEL)
    g1 = norm_attn.reshape(DEPTH, 1, D_MODEL)
    g2 = norm_ffn.reshape(DEPTH, 1, D_MODEL)
    qn = qnorm_a.reshape(DEPTH, 1, HEAD_DIM)
    kn = knorm_a.reshape(DEPTH, 1, HEAD_DIM)
    kvn = kvnorm_b.reshape(DEPTH, 1, KV_RANK)
    ctx_mod = lambda i: 0

    def lat_mod(tm):
        return lambda i: 1 + i // (dec_seq // tm)

    r3 = lambda a: a.reshape(dec_batch, dec_seq, a.shape[-1])
    hb = ATTN_HEADS_PER_STEP
    new_bufs = None

    for l in range(DEPTH):
        outs = _proj(xp, l, mods, ctx_mod, g1, w_in_b, qn, kn, kvn, wuk_b, wuv_b,
                     new_shape=(batch, seq), new_bufs=new_bufs)
        qa, ka, va, qb, kb, vb = outs[:6]
        new_bufs = outs[6:]
        oa, ob = _attn_ctx(qa, ka, va, qb, kb, vb, seq)
        xp = _outproj(xp, oa, ob, l, mods, ctx_mod, w_o_b)
        xp = _ffn(xp, l, mods, ctx_mod, g2, wg_b, wu_b, wd_b)

        qa, ka, va, qb, kb, vb = _proj(
            xs, l, mods, lat_mod(PROJ_TM), g1, w_in_b, qn, kn, kvn, wuk_b, wuv_b,
            rope_tabs=rope_tabs)
        oa = _attn_lat(r3(qa), l, kc_a, r3(ka), vc_a, r3(va),
                       1, N_HEADS_A, N_KV_A, HEAD_DIM, HEAD_DIM, "attn_lat_a", join_k=True)
        ob = _attn_lat(r3(qb), l, kc_b, r3(kb), vc_b, r3(vb),
                       N_HEADS_B // hb, hb, hb, QB_PAD, VDIM_B, "attn_lat_b", join_k=False)
        xs = _outproj(xs, oa.reshape(-1, W_QA), ob.reshape(-1, W_VB), l, mods, lat_mod(OUT_TM), w_o_b)
        xs = _ffn(xs, l, mods, lat_mod(FFN_TM), g2, wg_b, wu_b, wd_b)

    gf = norm_final.reshape(1, D_MODEL)
    y_prompt = _final_norm(xp, gf).reshape(batch, seq, D_MODEL)
    y_sample = _final_norm(xs, gf).reshape(dec_batch, dec_seq, D_MODEL)
    new_k, new_v, new_ckv, new_kr = new_bufs
    return (y_prompt, y_sample,
            new_k, new_v, new_ckv, new_kr)
```

```python
import functools

import jax
import jax.numpy as jnp
import numpy as np
from jax import lax
from jax.experimental import pallas as pl
from jax.experimental.pallas import tpu as pltpu

D_MODEL = 2048
DEPTH = 4
GRID_W = 64
HEAD_DIM = 128
N_HEADS_A = 8
N_KV_A = 2
GROUP_A = N_HEADS_A // N_KV_A
N_HEADS_B = 8
NOPE_B = 128
ROPE_B = 64
VDIM_B = 128
KV_RANK = 256
D_FF = 5632
ROPE_THETA = 10000.0
EPS = 1e-6

W_QA = N_HEADS_A * HEAD_DIM
W_KA = N_KV_A * HEAD_DIM
QB_DIM = NOPE_B + ROPE_B
W_QB = N_HEADS_B * QB_DIM
QB_PAD = 256
KR_PAD = 128
W_QB_PAD = N_HEADS_B * QB_PAD
W_KNOPE = N_HEADS_B * NOPE_B
W_VB = N_HEADS_B * VDIM_B
N_MOD = 6
MOD_ROWS = 8

O_QB = W_QA + 2 * W_KA
O_CKV = O_QB + W_QB
O_KR = O_CKV + KV_RANK
IN_COLS = O_KR + ROPE_B

C_QA = 0
C_KA = C_QA + W_QA
C_VA = C_KA + W_KA
C_QB = C_VA + W_KA
C_CKV = C_QB + W_QB_PAD
C_KR = C_CKV + KV_RANK
IN_COLS_PAD = C_KR + KR_PAD

SCALE_A = HEAD_DIM ** -0.5
SCALE_B = QB_DIM ** -0.5

VMEM_LIMIT = 56 * 1024 * 1024

BF16 = jnp.bfloat16
F32 = jnp.float32


def _cparams(n_axes):
    return pltpu.CompilerParams(
        dimension_semantics=("arbitrary",) * n_axes, vmem_limit_bytes=VMEM_LIMIT)


def _layer_spec(shape, layer):
    nd = len(shape)
    return pl.BlockSpec((None,) + shape, lambda *_: (layer,) + (0,) * nd,
                        pipeline_mode=pl.Buffered(1))


def _mod_spec(layer, mod_map, width, col_map=None):
    if col_map is None:
        return pl.BlockSpec((None, None, N_MOD, width), lambda i, *_: (layer, mod_map(i), 0, 0))
    return pl.BlockSpec((None, None, N_MOD, width),
                        lambda i, j: (layer, mod_map(i), 0, col_map(j)))


def _dot(a, b):
    return jnp.dot(a, b, preferred_element_type=F32)


def _dot_t(a, b):
    return lax.dot_general(a, b, (((1,), (1,)), ((), ())), preferred_element_type=F32)


def _rms(x):
    return x * lax.rsqrt(jnp.mean(x * x, axis=-1, keepdims=True) + EPS)


ADA_TN = 1024


def _adaln_kernel(s_ref, w_ref, b_ref, o_ref):
    s = s_ref[...]
    s = s * jax.nn.sigmoid(s)
    o_ref[...] = _dot(s.astype(BF16), w_ref[...].astype(BF16)) + b_ref[...]


def _adaln(s, w_ada, b_ada):
    n = N_MOD * D_MODEL
    return pl.pallas_call(
        _adaln_kernel,
        grid=(DEPTH, n // ADA_TN),
        in_specs=[
            pl.BlockSpec((MOD_ROWS, D_MODEL), lambda l, j: (0, 0)),
            pl.BlockSpec((None, D_MODEL, ADA_TN), lambda l, j: (l, 0, j)),
            pl.BlockSpec((None, 1, ADA_TN), lambda l, j: (l, 0, j)),
        ],
        out_specs=pl.BlockSpec((None, MOD_ROWS, ADA_TN), lambda l, j: (l, 0, j)),
        out_shape=jax.ShapeDtypeStruct((DEPTH, MOD_ROWS, n), F32),
        compiler_params=_cparams(2),
        name="adaln",
    )(s, w_ada, b_ada.reshape(DEPTH, 1, n))


WPREP_TR = 512
PAIR_IN = 2 * QB_DIM
PAIR_OUT = 2 * QB_PAD


def _wprep_kernel(w_ref, pq_ref, pk_ref, o_ref):
    o_ref[:, 0:O_QB] = w_ref[:, 0:O_QB]
    pq = pq_ref[...]
    for i in range(N_HEADS_B // 2):
        src = w_ref[:, O_QB + i * PAIR_IN:O_QB + (i + 1) * PAIR_IN]
        o_ref[:, C_QB + i * PAIR_OUT:C_QB + (i + 1) * PAIR_OUT] = _dot(src, pq).astype(BF16)
    o_ref[:, C_CKV:C_CKV + KV_RANK] = w_ref[:, O_CKV:O_CKV + KV_RANK]
    o_ref[:, C_KR:C_KR + KR_PAD] = _dot(w_ref[:, O_KR:O_KR + ROPE_B], pk_ref[...]).astype(BF16)


def _rope_pad_src():
    half = ROPE_B // 2
    src = np.full((KR_PAD,), -1, np.int32)
    src[0:half] = np.arange(half)
    src[2 * half:3 * half] = half + np.arange(half)
    return src


def _selection(src, n_in):
    return (jnp.arange(n_in, dtype=jnp.int32)[:, None] == jnp.asarray(src)[None, :]).astype(BF16)


def _wprep(w_in):
    rsrc = _rope_pad_src()
    head = np.concatenate([np.arange(NOPE_B), np.where(rsrc >= 0, NOPE_B + rsrc, -1)])
    pair = np.concatenate([head, np.where(head >= 0, QB_DIM + head, -1)]).astype(np.int32)
    pq = _selection(pair, PAIR_IN)
    pk = _selection(rsrc, ROPE_B)
    return pl.pallas_call(
        _wprep_kernel,
        grid=(DEPTH, D_MODEL // WPREP_TR),
        in_specs=[
            pl.BlockSpec((None, WPREP_TR, IN_COLS), lambda l, r: (l, r, 0)),
            pl.BlockSpec((PAIR_IN, PAIR_OUT), lambda l, r: (0, 0)),
            pl.BlockSpec((ROPE_B, KR_PAD), lambda l, r: (0, 0)),
        ],
        out_specs=pl.BlockSpec((None, WPREP_TR, IN_COLS_PAD), lambda l, r: (l, r, 0)),
        out_shape=jax.ShapeDtypeStruct((DEPTH, D_MODEL, IN_COLS_PAD), BF16),
        compiler_params=_cparams(2),
        name="wprep",
    )(w_in, pq, pk)


def _cache_kernel(ka_ref, va_ref, ckv_ref, kr_ref, wuk_ref, wuv_ref, pk_ref,
                  ka_o, va_o, kb_o, vb_o):
    ka_o[...] = ka_ref[...].astype(BF16)
    va_o[...] = va_ref[...].astype(BF16)
    ckv = ckv_ref[...].astype(BF16)
    kn = _dot(ckv, wuk_ref[...])
    vb_o[...] = _dot(ckv, wuv_ref[...]).astype(BF16)
    kr = _dot(kr_ref[...].astype(BF16), pk_ref[...]).astype(BF16)
    for h in range(N_HEADS_B):
        kb_o[:, h * QB_PAD:h * QB_PAD + NOPE_B] = kn[:, h * NOPE_B:(h + 1) * NOPE_B].astype(BF16)
        kb_o[:, h * QB_PAD + NOPE_B:(h + 1) * QB_PAD] = kr


def _cache_prep(cache_k_a, cache_v_a, cache_ckv_b, cache_krope_b, wuk, wuv):
    nb, _, past = cache_ckv_b.shape[:3]
    ka = cache_k_a.reshape(nb, DEPTH, past, W_KA)
    va = cache_v_a.reshape(nb, DEPTH, past, W_KA)
    pk = _selection(_rope_pad_src(), ROPE_B)

    def in4(width):
        return pl.BlockSpec((None, None, past, width), lambda l, b: (b, l, 0, 0))

    def out4(width):
        return pl.BlockSpec((None, None, past, width), lambda l, b: (l, b, 0, 0))

    def w3(width):
        return pl.BlockSpec((None, KV_RANK, width), lambda l, b: (l, 0, 0))

    return pl.pallas_call(
        _cache_kernel,
        grid=(DEPTH, nb),
        in_specs=[in4(W_KA), in4(W_KA), in4(KV_RANK), in4(ROPE_B), w3(W_KNOPE), w3(W_VB),
                  pl.BlockSpec((ROPE_B, KR_PAD), lambda l, b: (0, 0))],
        out_specs=[out4(W_KA), out4(W_KA), out4(W_QB_PAD), out4(W_VB)],
        out_shape=[
            jax.ShapeDtypeStruct((DEPTH, nb, past, W_KA), BF16),
            jax.ShapeDtypeStruct((DEPTH, nb, past, W_KA), BF16),
            jax.ShapeDtypeStruct((DEPTH, nb, past, W_QB_PAD), BF16),
            jax.ShapeDtypeStruct((DEPTH, nb, past, W_VB), BF16),
        ],
        compiler_params=_cparams(2),
        name="cache_prep",
    )(ka, va, cache_ckv_b, cache_krope_b, wuk, wuv, pk)


PROJ_TM = 512
PROJ_ROWS = 256
N_NEW = 4


def _proj_kernel(*refs, rope, emit_new, n_alias):
    it = iter(refs)
    x_ref, mod_ref, g_ref, w_ref, qn_ref, kn_ref, kvn_ref, wuk_ref, wuv_ref = (
        next(it) for _ in range(9))
    if rope:
        ca_ref, sa_ref, cb_ref, sb_ref = (next(it) for _ in range(4))
    for _ in range(n_alias):
        next(it)
    qa_o, ka_o, va_o, qb_o, kb_o, vb_o = (next(it) for _ in range(6))
    if emit_new:
        nk_o, nv_o, nckv_o, nkr_o = (next(it) for _ in range(N_NEW))

    for r in range(x_ref.shape[0] // PROJ_ROWS):
        rows = slice(r * PROJ_ROWS, (r + 1) * PROJ_ROWS)

        h = _rms(x_ref[rows, :]) * g_ref[...]
        h = h * (1.0 + mod_ref[1:2, :]) + mod_ref[0:1, :]
        hb = h.astype(BF16)

        def rope_a(t):
            if not rope:
                return t
            return t * ca_ref[rows, :] + pltpu.roll(t, HEAD_DIM // 2, 1) * sa_ref[rows, :]

        def rope_b(t):
            if not rope:
                return t
            return t * cb_ref[rows, :] + pltpu.roll(t, KR_PAD // 2, 1) * sb_ref[rows, :]

        pq = _dot(hb, w_ref[:, C_QA:C_QA + W_QA])
        qn = qn_ref[...]
        for i in range(N_HEADS_A):
            sl = slice(i * HEAD_DIM, (i + 1) * HEAD_DIM)
            t = rope_a(_rms(pq[:, sl]) * qn)
            qa_o[rows, sl] = (t * SCALE_A).astype(BF16)

        pk = _dot(hb, w_ref[:, C_KA:C_KA + W_KA])
        pv = _dot(hb, w_ref[:, C_VA:C_VA + W_KA])
        kn = kn_ref[...]
        for i in range(N_KV_A):
            sl = slice(i * HEAD_DIM, (i + 1) * HEAD_DIM)
            t = _rms(pk[:, sl]) * kn
            if emit_new:
                nk_o[r, :, i, :] = t
            ka_o[rows, sl] = rope_a(t).astype(BF16)
        va_o[rows, :] = pv.astype(BF16)
        if emit_new:
            for i in range(N_KV_A):
                nv_o[r, :, i, :] = pv[:, i * HEAD_DIM:(i + 1) * HEAD_DIM]

        pqb = _dot(hb, w_ref[:, C_QB:C_QB + W_QB_PAD])
        for i in range(N_HEADS_B):
            s0 = i * QB_PAD
            qb_o[rows, s0:s0 + NOPE_B] = (pqb[:, s0:s0 + NOPE_B] * SCALE_B).astype(BF16)
            t = rope_b(pqb[:, s0 + NOPE_B:s0 + QB_PAD])
            qb_o[rows, s0 + NOPE_B:s0 + QB_PAD] = (t * SCALE_B).astype(BF16)

        pc = _dot(hb, w_ref[:, C_CKV:C_CKV + KV_RANK])
        ckv = _rms(pc) * kvn_ref[...]
        pkr = _dot(hb, w_ref[:, C_KR:C_KR + KR_PAD])
        if emit_new:
            half = ROPE_B // 2
            nckv_o[r] = ckv
            nkr_o[r] = jnp.concatenate([pkr[:, 0:half], pkr[:, 2 * half:3 * half]], axis=-1)
        ckvb = ckv.astype(BF16)
        krb = rope_b(pkr).astype(BF16)
        kno = _dot(ckvb, wuk_ref[...])
        vb_o[rows, :] = _dot(ckvb, wuv_ref[...]).astype(BF16)
        for i in range(N_HEADS_B):
            s0 = i * QB_PAD
            kb_o[rows, s0:s0 + NOPE_B] = kno[:, i * NOPE_B:(i + 1) * NOPE_B].astype(BF16)
            kb_o[rows, s0 + NOPE_B:s0 + QB_PAD] = krb


def _proj(x, layer, mods, mod_map, g, w_in, qn, kn, kvn, wuk, wuv, rope_tabs=None,
          new_shape=None, new_bufs=None, tm=PROJ_TM):
    t = x.shape[0]
    rope = rope_tabs is not None
    emit_new = new_shape is not None

    def row(width):
        return pl.BlockSpec((tm, width), lambda i: (i, 0))

    in_specs = [
        row(D_MODEL),
        _mod_spec(layer, mod_map, D_MODEL),
        _layer_spec((1, D_MODEL), layer),
        _layer_spec((D_MODEL, IN_COLS_PAD), layer),
        _layer_spec((1, HEAD_DIM), layer),
        _layer_spec((1, HEAD_DIM), layer),
        _layer_spec((1, KV_RANK), layer),
        _layer_spec((KV_RANK, W_KNOPE), layer),
        _layer_spec((KV_RANK, W_VB), layer),
    ]
    args = [x, mods, g, w_in, qn, kn, kvn, wuk, wuv]
    if rope:
        seq_tiles = rope_tabs[0].shape[0] // tm
        for tab in rope_tabs:
            in_specs.append(pl.BlockSpec((tm, tab.shape[1]), lambda i: (i % seq_tiles, 0)))
            args.append(tab)
    widths = [W_QA, W_KA, W_KA, W_QB_PAD, W_QB_PAD, W_VB]
    out_specs = [row(w) for w in widths]
    out_shape = [jax.ShapeDtypeStruct((t, w), BF16) for w in widths]
    aliases = {}
    n_alias = 0
    if emit_new:
        batch, seq = new_shape
        assert seq == PROJ_ROWS and t == batch * seq
        nb = tm // seq
        tails = [(N_KV_A, HEAD_DIM), (N_KV_A, HEAD_DIM), (KV_RANK,), (ROPE_B,)]
        if new_bufs is not None:
            n_alias = N_NEW
            for k, buf in enumerate(new_bufs):
                aliases[len(args)] = len(widths) + k
                in_specs.append(pl.BlockSpec(memory_space=pl.ANY))
                args.append(buf)
        for tail in tails:
            zeros = (0,) * len(tail)
            out_specs.append(pl.BlockSpec((nb, None, seq) + tail,
                                          lambda i, zeros=zeros: (i, layer, 0) + zeros))
            out_shape.append(jax.ShapeDtypeStruct((batch, DEPTH, seq) + tail, F32))
    return pl.pallas_call(
        functools.partial(_proj_kernel, rope=rope, emit_new=emit_new, n_alias=n_alias),
        grid=(t // tm,),
        in_specs=in_specs,
        out_specs=out_specs,
        out_shape=out_shape,
        input_output_aliases=aliases,
        compiler_params=_cparams(1),
        name="proj_lat" if rope else "proj_ctx",
    )(*args)


def _attend(q, k, v_ext, dv):
    s = _dot_t(q, k)
    p = jnp.exp(s - s.max(axis=-1, keepdims=True))
    o = _dot(p.astype(BF16), v_ext)
    return (o[:, 0:dv] / o[:, dv:2 * dv]).astype(BF16)


def _attend_parts(q, k_parts, v_ext_ref, dv):
    s = [_dot_t(q, k) for k in k_parts]
    m = s[0].max(axis=-1, keepdims=True)
    for sp in s[1:]:
        m = jnp.maximum(m, sp.max(axis=-1, keepdims=True))
    o = None
    r0 = 0
    for sp in s:
        r1 = r0 + sp.shape[1]
        op = _dot(jnp.exp(sp - m).astype(BF16), v_ext_ref[r0:r1, :])
        o = op if o is None else o + op
        r0 = r1
    return (o[:, 0:dv] / o[:, dv:2 * dv]).astype(BF16)


def _fill_v_ext(v_ext_ref, rows, v_ref, n_kv, dv):
    ones = jnp.ones((v_ref.shape[0], dv), BF16)
    for jk in range(n_kv):
        v_ext_ref[rows, 2 * jk * dv:(2 * jk + 1) * dv] = v_ref[:, jk * dv:(jk + 1) * dv]
        v_ext_ref[rows, (2 * jk + 1) * dv:(2 * jk + 2) * dv] = ones


def _attn_ctx_kernel(qa_ref, ka_ref, va_ref, qb_ref, kb_ref, vb_ref, oa_ref, ob_ref,
                     vae_ref, vbe_ref):
    _fill_v_ext(vae_ref, slice(None), va_ref, N_KV_A, HEAD_DIM)
    _fill_v_ext(vbe_ref, slice(None), vb_ref, N_HEADS_B, VDIM_B)
    for hq in range(N_HEADS_A):
        g = hq // GROUP_A
        qsl = slice(hq * HEAD_DIM, (hq + 1) * HEAD_DIM)
        oa_ref[:, qsl] = _attend(qa_ref[:, qsl], ka_ref[:, g * HEAD_DIM:(g + 1) * HEAD_DIM],
                                 vae_ref[:, 2 * g * HEAD_DIM:(2 * g + 2) * HEAD_DIM], HEAD_DIM)
    for hq in range(N_HEADS_B):
        qsl = slice(hq * QB_PAD, (hq + 1) * QB_PAD)
        ob_ref[:, hq * VDIM_B:(hq + 1) * VDIM_B] = _attend(
            qb_ref[:, qsl], kb_ref[:, qsl],
            vbe_ref[:, 2 * hq * VDIM_B:(2 * hq + 2) * VDIM_B], VDIM_B)


def _attn_ctx(qa, ka, va, qb, kb, vb, seq):
    t = qa.shape[0]

    def row(width):
        return pl.BlockSpec((seq, width), lambda b: (b, 0))

    return pl.pallas_call(
        _attn_ctx_kernel,
        grid=(t // seq,),
        in_specs=[row(W_QA), row(W_KA), row(W_KA), row(W_QB_PAD), row(W_QB_PAD), row(W_VB)],
        out_specs=[row(W_QA), row(W_VB)],
        out_shape=[jax.ShapeDtypeStruct((t, W_QA), BF16), jax.ShapeDtypeStruct((t, W_VB), BF16)],
        scratch_shapes=[pltpu.VMEM((seq, 2 * W_KA), BF16), pltpu.VMEM((seq, 2 * W_VB), BF16)],
        compiler_params=_cparams(1),
        name="attn_ctx",
    )(qa, ka, va, qb, kb, vb)


ATTN_TQ = 512
ATTN_HEADS_PER_STEP = 8
ATTN_ROW_SPLIT = 2


def _attn_lat_kernel(q_ref, kc_ref, kn_ref, vc_ref, vn_ref, o_ref, v_ref, *k_scratch,
                     n_q, n_kv, dq, dv, past):
    @pl.when(pl.program_id(2) == 0)
    def _():
        _fill_v_ext(v_ref, slice(0, past), vc_ref, n_kv, dv)
        _fill_v_ext(v_ref, slice(past, None), vn_ref, n_kv, dv)
        if k_scratch:
            k_scratch[0][0:past, :] = kc_ref[...]
            k_scratch[0][past:, :] = kn_ref[...]

    rows = q_ref.shape[0] // ATTN_ROW_SPLIT
    for j in range(n_q):
        jk = j * n_kv // n_q
        ksl = slice(jk * dq, (jk + 1) * dq)
        vsl = slice(2 * jk * dv, (2 * jk + 2) * dv)
        for r in range(ATTN_ROW_SPLIT):
            rsl = slice(r * rows, (r + 1) * rows)
            q = q_ref[rsl, j * dq:(j + 1) * dq]
            if k_scratch:
                o = _attend(q, k_scratch[0][:, ksl], v_ref[:, vsl], dv)
            else:
                o = _attend_parts(q, [kc_ref[:, ksl], kn_ref[:, ksl]], v_ref.at[:, vsl], dv)
            o_ref[rsl, j * dv:(j + 1) * dv] = o


def _attn_lat(q, layer, kc, kn, vc, vn, n_units, n_q, n_kv, dq, dv, name, join_k, tq=ATTN_TQ):
    nb, seq, _ = q.shape
    past = kc.shape[2]
    scratch = [pltpu.VMEM((past + seq, n_kv * 2 * dv), BF16)]
    if join_k:
        scratch.append(pltpu.VMEM((past + seq, n_kv * dq), BF16))
    return pl.pallas_call(
        functools.partial(_attn_lat_kernel, n_q=n_q, n_kv=n_kv, dq=dq, dv=dv, past=past),
        grid=(nb, n_units, seq // tq),
        in_specs=[
            pl.BlockSpec((None, tq, n_q * dq), lambda b, u, i: (b, i, u)),
            pl.BlockSpec((None, None, past, n_kv * dq), lambda b, u, i: (layer, b, 0, u)),
            pl.BlockSpec((None, seq, n_kv * dq), lambda b, u, i: (b, 0, u)),
            pl.BlockSpec((None, None, past, n_kv * dv), lambda b, u, i: (layer, b, 0, u)),
            pl.BlockSpec((None, seq, n_kv * dv), lambda b, u, i: (b, 0, u)),
        ],
        out_specs=pl.BlockSpec((None, tq, n_q * dv), lambda b, u, i: (b, i, u)),
        out_shape=jax.ShapeDtypeStruct((nb, seq, n_units * n_q * dv), BF16),
        scratch_shapes=scratch,
        compiler_params=_cparams(3),
        name=name,
    )(q, kc, kn, vc, vn)


OUT_TM = 512


def _outproj_kernel(x_ref, oa_ref, ob_ref, mod_ref, w_ref, y_ref):
    y = _dot(oa_ref[...], w_ref[0:W_QA, :]) + _dot(ob_ref[...], w_ref[W_QA:, :])
    y_ref[...] = x_ref[...] + mod_ref[2:3, :] * y


def _outproj(x, oa, ob, layer, mods, mod_map, w_o, tm=OUT_TM):
    t = x.shape[0]
    return pl.pallas_call(
        _outproj_kernel,
        grid=(t // tm,),
        in_specs=[
            pl.BlockSpec((tm, D_MODEL), lambda i: (i, 0)),
            pl.BlockSpec((tm, W_QA), lambda i: (i, 0)),
            pl.BlockSpec((tm, W_VB), lambda i: (i, 0)),
            _mod_spec(layer, mod_map, D_MODEL),
            _layer_spec((W_QA + W_VB, D_MODEL), layer),
        ],
        out_specs=pl.BlockSpec((tm, D_MODEL), lambda i: (i, 0)),
        out_shape=jax.ShapeDtypeStruct((t, D_MODEL), F32),
        compiler_params=_cparams(1),
        name="outproj",
    )(x, oa, ob, mods, w_o)


FFN_TM = 1024
FFN_TF = 512
FFN_TN = 512
FFN_ROWS = 256


def _ffn_kernel(*refs, nf, tf, rc, n_tiles, n_cast):
    xr_ref, xc_ref, mod_ref, modc_ref, g_ref, wg_ref, wu_ref, wd_ref = refs[:8]
    cast_in = refs[8:8 + n_cast]
    y_ref = refs[8 + n_cast]
    cast_out = refs[9 + n_cast:9 + 2 * n_cast]
    h_ref, t_ref = refs[9 + 2 * n_cast:]
    i = pl.program_id(0)
    j = pl.program_id(1)

    def cast_chunks():
        for src, dst in zip(cast_in, cast_out):
            dst[...] = src[...].astype(BF16)
    cur = (i + 1) % 2
    nxt = i % 2

    def norm_chunk():
        h = _rms(xr_ref[...]) * g_ref[...]
        h = h * (1.0 + mod_ref[4:5, :]) + mod_ref[3:4, :]
        r0 = pl.multiple_of((j - nf) * rc, rc)
        h_ref[nxt, pl.ds(r0, rc), :] = h.astype(BF16)

    row_blocks = [slice(m, m + FFN_ROWS) for m in range(0, y_ref.shape[0], FFN_ROWS)]

    def down_proj():
        for rows in row_blocks:
            d = _dot(t_ref[0, rows, :], wd_ref[0:tf, :])
            for k in range(1, nf):
                d = d + _dot(t_ref[k, rows, :], wd_ref[k * tf:(k + 1) * tf, :])
            y_ref[rows, :] = xc_ref[rows, :] + modc_ref[5:6, :] * d

    @pl.when((j < nf) & (i == 0))
    def _():
        cast_chunks()

    @pl.when((j < nf) & (i >= 1))
    def _():
        cast_chunks()
        for rows in row_blocks:
            hb = h_ref[cur, rows, :]
            a = _dot(hb, wg_ref[...])
            b = _dot(hb, wu_ref[...])
            t_ref[j, rows, :] = (a * jax.nn.sigmoid(a) * b).astype(BF16)

    @pl.when((j >= nf) & (i >= 1) & (i < n_tiles))
    def _():
        cast_chunks()
        norm_chunk()
        down_proj()

    @pl.when((j >= nf) & (i == 0))
    def _():
        cast_chunks()
        norm_chunk()

    @pl.when((j >= nf) & (i == n_tiles))
    def _():
        cast_chunks()
        down_proj()


def _cast_rows(n_rows, n_steps):
    rows = 16
    while n_rows % rows or n_rows // rows > n_steps:
        rows += 16
    return rows


def _ffn(x, layer, mods, mod_map, g, wg, wu, wd, cast=(), tm=FFN_TM, tf=FFN_TF, tn=FFN_TN):
    t = x.shape[0]
    n_tiles = t // tm
    nf = D_FF // tf
    nn = D_MODEL // tn
    n_steps = (n_tiles + 1) * (nf + nn)
    cast_specs_in, cast_specs_out, cast_shapes = [], [], []
    for src, src_layer in cast:
        _, n_rows, n_cols = src.shape
        rows = _cast_rows(n_rows, n_steps)
        chunk = lambda i, j, last=n_rows // rows - 1: jnp.minimum(i * (nf + nn) + j, last)
        cast_specs_in.append(pl.BlockSpec(
            (None, rows, n_cols), lambda i, j, c=chunk, l=src_layer: (l, c(i, j), 0)))
        cast_specs_out.append(pl.BlockSpec((rows, n_cols), lambda i, j, c=chunk: (c(i, j), 0)))
        cast_shapes.append(jax.ShapeDtypeStruct((n_rows, n_cols), BF16))
    rc = tm // nn
    col = lambda j: jnp.maximum(j - nf, 0)
    comp = lambda i: jnp.maximum(i - 1, 0)
    norm = lambda i: jnp.minimum(i, n_tiles - 1)
    ccol = lambda i, j: jnp.where(i == 0, 0, col(j))
    ftile = lambda i, j: jnp.where(i == 0, 0, jnp.minimum(j, nf - 1))
    outs = pl.pallas_call(
        functools.partial(_ffn_kernel, nf=nf, tf=tf, rc=rc, n_tiles=n_tiles, n_cast=len(cast)),
        grid=(n_tiles + 1, nf + nn),
        in_specs=[
            pl.BlockSpec((rc, D_MODEL), lambda i, j: (norm(i) * nn + col(j), 0)),
            pl.BlockSpec((tm, tn), lambda i, j: (comp(i), ccol(i, j))),
            pl.BlockSpec((None, None, N_MOD, D_MODEL),
                         lambda i, j: (layer, mod_map(norm(i)), 0, 0)),
            pl.BlockSpec((None, None, N_MOD, tn),
                         lambda i, j: (layer, mod_map(comp(i)), 0, ccol(i, j))),
            _layer_spec((1, D_MODEL), layer),
            pl.BlockSpec((D_MODEL, tf), lambda i, j: (0, ftile(i, j))),
            pl.BlockSpec((D_MODEL, tf), lambda i, j: (0, ftile(i, j))),
            pl.BlockSpec((D_FF, tn), lambda i, j: (0, ccol(i, j))),
        ] + cast_specs_in,
        out_specs=[pl.BlockSpec((tm, tn), lambda i, j: (comp(i), ccol(i, j)))] + cast_specs_out,
        out_shape=[jax.ShapeDtypeStruct((t, D_MODEL), F32)] + cast_shapes,
        scratch_shapes=[pltpu.VMEM((2, tm, D_MODEL), BF16), pltpu.VMEM((nf, tm, tf), BF16)],
        compiler_params=_cparams(2),
        name="ffn",
    )(x, x, mods, mods, g, wg, wu, wd, *[src for src, _ in cast])
    return outs[0], outs[1:]


NORM_TM = 512


def _final_norm_kernel(x_ref, g_ref, y_ref):
    y_ref[...] = _rms(x_ref[...]) * g_ref[...]


def _final_norm(x, g, tm=NORM_TM):
    t = x.shape[0]
    return pl.pallas_call(
        _final_norm_kernel,
        grid=(t // tm,),
        in_specs=[pl.BlockSpec((tm, D_MODEL), lambda i: (i, 0)),
                  pl.BlockSpec((1, D_MODEL), lambda i: (0, 0))],
        out_specs=pl.BlockSpec((tm, D_MODEL), lambda i: (i, 0)),
        out_shape=jax.ShapeDtypeStruct((t, D_MODEL), F32),
        compiler_params=_cparams(1),
        name="final_norm",
    )(x, g)


def _rope_tables(n_tokens):
    n_rows = n_tokens // GRID_W
    row = jnp.repeat(jnp.arange(n_rows, dtype=F32), GRID_W)
    col = jnp.tile(jnp.arange(GRID_W, dtype=F32), n_rows)

    def cs(dim):
        n_freq = dim // 4
        inv = ROPE_THETA ** (-jnp.arange(n_freq, dtype=F32) / n_freq)
        ang = jnp.concatenate([row[:, None] * inv, col[:, None] * inv], axis=-1)
        return jnp.cos(ang), jnp.sin(ang)

    cos_a, sin_a = cs(HEAD_DIM)
    cos_b, sin_b = cs(ROPE_B)
    one = jnp.ones_like(cos_b)
    zero = jnp.zeros_like(sin_b)
    return (jnp.concatenate([cos_a, cos_a], axis=-1),
            jnp.concatenate([-sin_a, sin_a], axis=-1),
            jnp.concatenate([cos_b, one, cos_b, one], axis=-1),
            jnp.concatenate([-sin_b, zero, sin_b, zero], axis=-1))


def kernel(x_prompt, x_sample, cache_k_a, cache_v_a, cache_ckv_b, cache_krope_b, c, c_ctx, w_ada, b_ada, norm_attn, norm_ffn, w_in, qnorm_a, knorm_a, kvnorm_b, w_uk_b, w_uv_b, w_o, w_gate, w_up, w_down, norm_final):
    batch, seq, _ = x_prompt.shape
    dec_batch, dec_seq, _ = x_sample.shape
    assert dec_batch + 1 <= MOD_ROWS

    w_in_b = _wprep(w_in.astype(BF16))
    wuk_b = w_uk_b.astype(BF16)
    wuv_b = w_uv_b.astype(BF16)
    w_o_b = w_o.astype(BF16)
    wg_l = w_gate[0].astype(BF16)
    wu_l = w_up[0].astype(BF16)
    wd_l = w_down[0].astype(BF16)
    rope_tabs = _rope_tables(dec_seq)

    cond = jnp.concatenate(
        [c_ctx[None, :], c, jnp.zeros((MOD_ROWS - 1 - dec_batch, D_MODEL), F32)], axis=0)
    mods = _adaln(cond, w_ada, b_ada).reshape(DEPTH, MOD_ROWS, N_MOD, D_MODEL)

    kc_a, vc_a, kc_b, vc_b = _cache_prep(
        cache_k_a, cache_v_a, cache_ckv_b, cache_krope_b, wuk_b, wuv_b)

    xp = x_prompt.reshape(batch * seq, D_MODEL)
    xs = x_sample.reshape(dec_batch * dec_seq, D_MODEL)
    g1 = norm_attn.reshape(DEPTH, 1, D_MODEL)
    g2 = norm_ffn.reshape(DEPTH, 1, D_MODEL)
    qn = qnorm_a.reshape(DEPTH, 1, HEAD_DIM)
    kn = knorm_a.reshape(DEPTH, 1, HEAD_DIM)
    kvn = kvnorm_b.reshape(DEPTH, 1, KV_RANK)
    ctx_mod = lambda i: 0

    def lat_mod(tm):
        return lambda i: 1 + i // (dec_seq // tm)

    r3 = lambda a: a.reshape(dec_batch, dec_seq, a.shape[-1])
    hb = ATTN_HEADS_PER_STEP
    new_bufs = None

    for l in range(DEPTH):
        outs = _proj(xp, l, mods, ctx_mod, g1, w_in_b, qn, kn, kvn, wuk_b, wuv_b,
                     new_shape=(batch, seq), new_bufs=new_bufs)
        qa, ka, va, qb, kb, vb = outs[:6]
        new_bufs = outs[6:]
        oa, ob = _attn_ctx(qa, ka, va, qb, kb, vb, seq)
        xp = _outproj(xp, oa, ob, l, mods, ctx_mod, w_o_b)
        nxt = l + 1 < DEPTH
        xp, gu_next = _ffn(xp, l, mods, ctx_mod, g2, wg_l, wu_l, wd_l,
                           cast=((w_gate, l + 1), (w_up, l + 1)) if nxt else ())

        qa, ka, va, qb, kb, vb = _proj(
            xs, l, mods, lat_mod(PROJ_TM), g1, w_in_b, qn, kn, kvn, wuk_b, wuv_b,
            rope_tabs=rope_tabs)
        oa = _attn_lat(r3(qa), l, kc_a, r3(ka), vc_a, r3(va),
                       1, N_HEADS_A, N_KV_A, HEAD_DIM, HEAD_DIM, "attn_lat_a", join_k=True)
        ob = _attn_lat(r3(qb), l, kc_b, r3(kb), vc_b, r3(vb),
                       N_HEADS_B // hb, hb, hb, QB_PAD, VDIM_B, "attn_lat_b", join_k=False)
        xs = _outproj(xs, oa.reshape(-1, W_QA), ob.reshape(-1, W_VB), l, mods, lat_mod(OUT_TM), w_o_b)
        xs, d_next = _ffn(xs, l, mods, lat_mod(FFN_TM), g2, wg_l, wu_l, wd_l,
                          cast=((w_down, l + 1),) if nxt else ())
        if nxt:
            (wg_l, wu_l), (wd_l,) = gu_next, d_next

    gf = norm_final.reshape(1, D_MODEL)
    y_prompt = _final_norm(xp, gf).reshape(batch, seq, D_MODEL)
    y_sample = _final_norm(xs, gf).reshape(dec_batch, dec_seq, D_MODEL)
    new_k, new_v, new_ckv, new_kr = new_bufs
    return (y_prompt, y_sample,
            new_k, new_v, new_ckv, new_kr)
```

```python
import functools

import jax
import jax.numpy as jnp
import numpy as np
from jax import lax
from jax.experimental import pallas as pl
from jax.experimental.pallas import tpu as pltpu

D_MODEL = 2048
DEPTH = 4
GRID_W = 64
HEAD_DIM = 128
N_HEADS_A = 8
N_KV_A = 2
GROUP_A = N_HEADS_A // N_KV_A
N_HEADS_B = 8
NOPE_B = 128
ROPE_B = 64
VDIM_B = 128
KV_RANK = 256
D_FF = 5632
ROPE_THETA = 10000.0
EPS = 1e-6

W_QA = N_HEADS_A * HEAD_DIM
W_KA = N_KV_A * HEAD_DIM
QB_DIM = NOPE_B + ROPE_B
W_QB = N_HEADS_B * QB_DIM
QB_PAD = 256
KR_PAD = 128
W_QB_PAD = N_HEADS_B * QB_PAD
W_KNOPE = N_HEADS_B * NOPE_B
W_VB = N_HEADS_B * VDIM_B
N_MOD = 6
MOD_ROWS = 8

O_QB = W_QA + 2 * W_KA
O_CKV = O_QB + W_QB
O_KR = O_CKV + KV_RANK
IN_COLS = O_KR + ROPE_B

C_QA = 0
C_KA = C_QA + W_QA
C_VA = C_KA + W_KA
C_QB = C_VA + W_KA
C_CKV = C_QB + W_QB_PAD
C_KR = C_CKV + KV_RANK
IN_COLS_PAD = C_KR + KR_PAD

LOG2_E = 1.4426950408889634
SCALE_A = HEAD_DIM ** -0.5 * LOG2_E
SCALE_B = QB_DIM ** -0.5 * LOG2_E

VMEM_LIMIT = 56 * 1024 * 1024

BF16 = jnp.bfloat16
F32 = jnp.float32


def _cparams(n_axes):
    return pltpu.CompilerParams(
        dimension_semantics=("arbitrary",) * n_axes, vmem_limit_bytes=VMEM_LIMIT)


def _layer_spec(shape, layer):
    nd = len(shape)
    return pl.BlockSpec((None,) + shape, lambda *_: (layer,) + (0,) * nd,
                        pipeline_mode=pl.Buffered(1))


def _mod_spec(layer, mod_map, width, col_map=None):
    if col_map is None:
        return pl.BlockSpec((None, None, N_MOD, width), lambda i, *_: (layer, mod_map(i), 0, 0))
    return pl.BlockSpec((None, None, N_MOD, width),
                        lambda i, j: (layer, mod_map(i), 0, col_map(j)))


def _dot(a, b):
    return jnp.dot(a, b, preferred_element_type=F32)


def _dot_t(a, b):
    return lax.dot_general(a, b, (((1,), (1,)), ((), ())), preferred_element_type=F32)


def _rms(x):
    return x * lax.rsqrt(jnp.mean(x * x, axis=-1, keepdims=True) + EPS)


ADA_TN = 1024


def _adaln_kernel(s_ref, w_ref, b_ref, o_ref):
    s = s_ref[...]
    s = s * jax.nn.sigmoid(s)
    o_ref[...] = _dot(s.astype(BF16), w_ref[...].astype(BF16)) + b_ref[...]


def _adaln(s, w_ada, b_ada):
    n = N_MOD * D_MODEL
    return pl.pallas_call(
        _adaln_kernel,
        grid=(DEPTH, n // ADA_TN),
        in_specs=[
            pl.BlockSpec((MOD_ROWS, D_MODEL), lambda l, j: (0, 0)),
            pl.BlockSpec((None, D_MODEL, ADA_TN), lambda l, j: (l, 0, j)),
            pl.BlockSpec((None, 1, ADA_TN), lambda l, j: (l, 0, j)),
        ],
        out_specs=pl.BlockSpec((None, MOD_ROWS, ADA_TN), lambda l, j: (l, 0, j)),
        out_shape=jax.ShapeDtypeStruct((DEPTH, MOD_ROWS, n), F32),
        compiler_params=_cparams(2),
        name="adaln",
    )(s, w_ada, b_ada.reshape(DEPTH, 1, n))


WPREP_TR = 512
PAIR_IN = 2 * QB_DIM
PAIR_OUT = 2 * QB_PAD


def _wprep_kernel(w_ref, pq_ref, pk_ref, o_ref):
    o_ref[:, 0:O_QB] = w_ref[:, 0:O_QB]
    pq = pq_ref[...]
    for i in range(N_HEADS_B // 2):
        src = w_ref[:, O_QB + i * PAIR_IN:O_QB + (i + 1) * PAIR_IN]
        o_ref[:, C_QB + i * PAIR_OUT:C_QB + (i + 1) * PAIR_OUT] = _dot(src, pq).astype(BF16)
    o_ref[:, C_CKV:C_CKV + KV_RANK] = w_ref[:, O_CKV:O_CKV + KV_RANK]
    o_ref[:, C_KR:C_KR + KR_PAD] = _dot(w_ref[:, O_KR:O_KR + ROPE_B], pk_ref[...]).astype(BF16)


def _rope_pad_src():
    half = ROPE_B // 2
    src = np.full((KR_PAD,), -1, np.int32)
    src[0:half] = np.arange(half)
    src[2 * half:3 * half] = half + np.arange(half)
    return src


def _selection(src, n_in):
    return (jnp.arange(n_in, dtype=jnp.int32)[:, None] == jnp.asarray(src)[None, :]).astype(BF16)


def _wprep(w_in):
    rsrc = _rope_pad_src()
    head = np.concatenate([np.arange(NOPE_B), np.where(rsrc >= 0, NOPE_B + rsrc, -1)])
    pair = np.concatenate([head, np.where(head >= 0, QB_DIM + head, -1)]).astype(np.int32)
    pq = _selection(pair, PAIR_IN)
    pk = _selection(rsrc, ROPE_B)
    return pl.pallas_call(
        _wprep_kernel,
        grid=(DEPTH, D_MODEL // WPREP_TR),
        in_specs=[
            pl.BlockSpec((None, WPREP_TR, IN_COLS), lambda l, r: (l, r, 0)),
            pl.BlockSpec((PAIR_IN, PAIR_OUT), lambda l, r: (0, 0)),
            pl.BlockSpec((ROPE_B, KR_PAD), lambda l, r: (0, 0)),
        ],
        out_specs=pl.BlockSpec((None, WPREP_TR, IN_COLS_PAD), lambda l, r: (l, r, 0)),
        out_shape=jax.ShapeDtypeStruct((DEPTH, D_MODEL, IN_COLS_PAD), BF16),
        compiler_params=_cparams(2),
        name="wprep",
    )(w_in, pq, pk)


def _cache_kernel(ka_ref, va_ref, ckv_ref, kr_ref, wuk_ref, wuv_ref, pk_ref,
                  ka_o, va_o, kb_o, vb_o):
    ka_o[...] = ka_ref[...].astype(BF16)
    va_o[...] = va_ref[...].astype(BF16)
    ckv = ckv_ref[...].astype(BF16)
    kn = _dot(ckv, wuk_ref[...])
    vb_o[...] = _dot(ckv, wuv_ref[...]).astype(BF16)
    kr = _dot(kr_ref[...].astype(BF16), pk_ref[...]).astype(BF16)
    for h in range(N_HEADS_B):
        kb_o[:, h * QB_PAD:h * QB_PAD + NOPE_B] = kn[:, h * NOPE_B:(h + 1) * NOPE_B].astype(BF16)
        kb_o[:, h * QB_PAD + NOPE_B:(h + 1) * QB_PAD] = kr


def _cache_prep(cache_k_a, cache_v_a, cache_ckv_b, cache_krope_b, wuk, wuv):
    nb, _, past = cache_ckv_b.shape[:3]
    ka = cache_k_a.reshape(nb, DEPTH, past, W_KA)
    va = cache_v_a.reshape(nb, DEPTH, past, W_KA)
    pk = _selection(_rope_pad_src(), ROPE_B)

    def in4(width):
        return pl.BlockSpec((None, None, past, width), lambda l, b: (b, l, 0, 0))

    def out4(width):
        return pl.BlockSpec((None, None, past, width), lambda l, b: (l, b, 0, 0))

    def w3(width):
        return pl.BlockSpec((None, KV_RANK, width), lambda l, b: (l, 0, 0))

    return pl.pallas_call(
        _cache_kernel,
        grid=(DEPTH, nb),
        in_specs=[in4(W_KA), in4(W_KA), in4(KV_RANK), in4(ROPE_B), w3(W_KNOPE), w3(W_VB),
                  pl.BlockSpec((ROPE_B, KR_PAD), lambda l, b: (0, 0))],
        out_specs=[out4(W_KA), out4(W_KA), out4(W_QB_PAD), out4(W_VB)],
        out_shape=[
            jax.ShapeDtypeStruct((DEPTH, nb, past, W_KA), BF16),
            jax.ShapeDtypeStruct((DEPTH, nb, past, W_KA), BF16),
            jax.ShapeDtypeStruct((DEPTH, nb, past, W_QB_PAD), BF16),
            jax.ShapeDtypeStruct((DEPTH, nb, past, W_VB), BF16),
        ],
        compiler_params=_cparams(2),
        name="cache_prep",
    )(ka, va, cache_ckv_b, cache_krope_b, wuk, wuv, pk)


PROJ_TM = 512
PROJ_ROWS = 256
N_NEW = 4


def _proj_kernel(*refs, rope, emit_new, n_alias):
    it = iter(refs)
    x_ref, mod_ref, g_ref, w_ref, qn_ref, kn_ref, kvn_ref, wuk_ref, wuv_ref = (
        next(it) for _ in range(9))
    if rope:
        ca_ref, sa_ref, cb_ref, sb_ref = (next(it) for _ in range(4))
    for _ in range(n_alias):
        next(it)
    if emit_new:
        oa_o, ob_o, nk_o, nv_o, nckv_o, nkr_o = (next(it) for _ in range(2 + N_NEW))
    qa_o, ka_o, va_o, qb_o, kb_o, vb_o = (next(it) for _ in range(6))
    if emit_new:
        vae_s, vbe_s = next(it), next(it)

    for r in range(x_ref.shape[0] // PROJ_ROWS):
        rows = slice(r * PROJ_ROWS, (r + 1) * PROJ_ROWS)

        h = _rms(x_ref[rows, :]) * g_ref[...]
        h = h * (1.0 + mod_ref[1:2, :]) + mod_ref[0:1, :]
        hb = h.astype(BF16)

        def rope_a(t):
            if not rope:
                return t
            return t * ca_ref[rows, :] + pltpu.roll(t, HEAD_DIM // 2, 1) * sa_ref[rows, :]

        def rope_b(t):
            if not rope:
                return t
            return t * cb_ref[rows, :] + pltpu.roll(t, KR_PAD // 2, 1) * sb_ref[rows, :]

        pq = _dot(hb, w_ref[:, C_QA:C_QA + W_QA])
        qn = qn_ref[...]
        for i in range(N_HEADS_A):
            sl = slice(i * HEAD_DIM, (i + 1) * HEAD_DIM)
            t = rope_a(_rms(pq[:, sl]) * qn)
            qa_o[rows, sl] = (t * SCALE_A).astype(BF16)

        pk = _dot(hb, w_ref[:, C_KA:C_KA + W_KA])
        pv = _dot(hb, w_ref[:, C_VA:C_VA + W_KA])
        kn = kn_ref[...]
        for i in range(N_KV_A):
            sl = slice(i * HEAD_DIM, (i + 1) * HEAD_DIM)
            t = _rms(pk[:, sl]) * kn
            if emit_new:
                nk_o[r, :, i, :] = t
            ka_o[rows, sl] = rope_a(t).astype(BF16)
        va_o[rows, :] = pv.astype(BF16)
        if emit_new:
            for i in range(N_KV_A):
                nv_o[r, :, i, :] = pv[:, i * HEAD_DIM:(i + 1) * HEAD_DIM]

        pqb = _dot(hb, w_ref[:, C_QB:C_QB + W_QB_PAD])
        for i in range(N_HEADS_B):
            s0 = i * QB_PAD
            qb_o[rows, s0:s0 + NOPE_B] = (pqb[:, s0:s0 + NOPE_B] * SCALE_B).astype(BF16)
            t = rope_b(pqb[:, s0 + NOPE_B:s0 + QB_PAD])
            qb_o[rows, s0 + NOPE_B:s0 + QB_PAD] = (t * SCALE_B).astype(BF16)

        pc = _dot(hb, w_ref[:, C_CKV:C_CKV + KV_RANK])
        ckv = _rms(pc) * kvn_ref[...]
        pkr = _dot(hb, w_ref[:, C_KR:C_KR + KR_PAD])
        if emit_new:
            half = ROPE_B // 2
            nckv_o[r] = ckv
            nkr_o[r] = jnp.concatenate([pkr[:, 0:half], pkr[:, 2 * half:3 * half]], axis=-1)
        ckvb = ckv.astype(BF16)
        krb = rope_b(pkr).astype(BF16)
        kno = _dot(ckvb, wuk_ref[...])
        vb_o[rows, :] = _dot(ckvb, wuv_ref[...]).astype(BF16)
        for i in range(N_HEADS_B):
            s0 = i * QB_PAD
            kb_o[rows, s0:s0 + NOPE_B] = kno[:, i * NOPE_B:(i + 1) * NOPE_B].astype(BF16)
            kb_o[rows, s0 + NOPE_B:s0 + QB_PAD] = krb

        if emit_new:
            _fill_v_ext(vae_s, rows, va_o.at[rows, :], N_KV_A, HEAD_DIM)
            _fill_v_ext(vbe_s, rows, vb_o.at[rows, :], N_HEADS_B, VDIM_B)
            for hq in range(N_HEADS_A):
                g = hq // GROUP_A
                qsl = slice(hq * HEAD_DIM, (hq + 1) * HEAD_DIM)
                oa_o[rows, qsl] = _attend(
                    qa_o[rows, qsl], ka_o[rows, g * HEAD_DIM:(g + 1) * HEAD_DIM],
                    vae_s[rows, 2 * g * HEAD_DIM:(2 * g + 2) * HEAD_DIM], HEAD_DIM)
            for hq in range(N_HEADS_B):
                qsl = slice(hq * QB_PAD, (hq + 1) * QB_PAD)
                ob_o[rows, hq * VDIM_B:(hq + 1) * VDIM_B] = _attend(
                    qb_o[rows, qsl], kb_o[rows, qsl],
                    vbe_s[rows, 2 * hq * VDIM_B:(2 * hq + 2) * VDIM_B], VDIM_B)


def _proj(x, layer, mods, mod_map, g, w_in, qn, kn, kvn, wuk, wuv, rope_tabs=None,
          new_shape=None, new_bufs=None, tm=PROJ_TM):
    t = x.shape[0]
    rope = rope_tabs is not None
    emit_new = new_shape is not None

    def row(width):
        return pl.BlockSpec((tm, width), lambda i: (i, 0))

    in_specs = [
        row(D_MODEL),
        _mod_spec(layer, mod_map, D_MODEL),
        _layer_spec((1, D_MODEL), layer),
        _layer_spec((D_MODEL, IN_COLS_PAD), layer),
        _layer_spec((1, HEAD_DIM), layer),
        _layer_spec((1, HEAD_DIM), layer),
        _layer_spec((1, KV_RANK), layer),
        _layer_spec((KV_RANK, W_KNOPE), layer),
        _layer_spec((KV_RANK, W_VB), layer),
    ]
    args = [x, mods, g, w_in, qn, kn, kvn, wuk, wuv]
    if rope:
        seq_tiles = rope_tabs[0].shape[0] // tm
        for tab in rope_tabs:
            in_specs.append(pl.BlockSpec((tm, tab.shape[1]), lambda i: (i % seq_tiles, 0)))
            args.append(tab)
    qkv_widths = [W_QA, W_KA, W_KA, W_QB_PAD, W_QB_PAD, W_VB]
    widths = [W_QA, W_VB] if emit_new else qkv_widths
    out_specs = [row(w) for w in widths]
    out_shape = [jax.ShapeDtypeStruct((t, w), BF16) for w in widths]
    scratch = []
    aliases = {}
    n_alias = 0
    if emit_new:
        scratch = [pltpu.VMEM((tm, w), BF16) for w in qkv_widths + [2 * W_KA, 2 * W_VB]]
        batch, seq = new_shape
        assert seq == PROJ_ROWS and t == batch * seq
        nb = tm // seq
        tails = [(N_KV_A, HEAD_DIM), (N_KV_A, HEAD_DIM), (KV_RANK,), (ROPE_B,)]
        if new_bufs is not None:
            n_alias = N_NEW
            for k, buf in enumerate(new_bufs):
                aliases[len(args)] = len(widths) + k
                in_specs.append(pl.BlockSpec(memory_space=pl.ANY))
                args.append(buf)
        for tail in tails:
            zeros = (0,) * len(tail)
            out_specs.append(pl.BlockSpec((nb, None, seq) + tail,
                                          lambda i, zeros=zeros: (i, layer, 0) + zeros))
            out_shape.append(jax.ShapeDtypeStruct((batch, DEPTH, seq) + tail, F32))
    return pl.pallas_call(
        functools.partial(_proj_kernel, rope=rope, emit_new=emit_new, n_alias=n_alias),
        grid=(t // tm,),
        in_specs=in_specs,
        out_specs=out_specs,
        out_shape=out_shape,
        input_output_aliases=aliases,
        scratch_shapes=scratch,
        compiler_params=_cparams(1),
        name="proj_lat" if rope else "proj_attn_ctx",
    )(*args)


def _attend(q, k, v_ext, dv):
    s = _dot_t(q, k)
    p = jnp.exp2(s - s.max(axis=-1, keepdims=True))
    o = _dot(p.astype(BF16), v_ext)
    return (o[:, 0:dv] / o[:, dv:2 * dv]).astype(BF16)


def _attend_parts(q, k_parts, v_ext_ref, dv):
    s = [_dot_t(q, k) for k in k_parts]
    m = s[0].max(axis=-1, keepdims=True)
    for sp in s[1:]:
        m = jnp.maximum(m, sp.max(axis=-1, keepdims=True))
    o = None
    r0 = 0
    for sp in s:
        r1 = r0 + sp.shape[1]
        op = _dot(jnp.exp2(sp - m).astype(BF16), v_ext_ref[r0:r1, :])
        o = op if o is None else o + op
        r0 = r1
    return (o[:, 0:dv] / o[:, dv:2 * dv]).astype(BF16)


def _fill_v_ext(v_ext_ref, rows, v_ref, n_kv, dv):
    ones = jnp.ones((v_ref.shape[0], dv), BF16)
    for jk in range(n_kv):
        v_ext_ref[rows, 2 * jk * dv:(2 * jk + 1) * dv] = v_ref[:, jk * dv:(jk + 1) * dv]
        v_ext_ref[rows, (2 * jk + 1) * dv:(2 * jk + 2) * dv] = ones


ATTN_TQ = 512
ATTN_HEADS_PER_STEP = 8
ATTN_ROW_SPLIT = 2


def _attn_lat_kernel(q_ref, kc_ref, kn_ref, vc_ref, vn_ref, o_ref, v_ref, *k_scratch,
                     n_q, n_kv, dq, dv, past):
    @pl.when(pl.program_id(2) == 0)
    def _():
        _fill_v_ext(v_ref, slice(0, past), vc_ref, n_kv, dv)
        _fill_v_ext(v_ref, slice(past, None), vn_ref, n_kv, dv)
        if k_scratch:
            k_scratch[0][0:past, :] = kc_ref[...]
            k_scratch[0][past:, :] = kn_ref[...]

    rows = q_ref.shape[0] // ATTN_ROW_SPLIT
    for j in range(n_q):
        jk = j * n_kv // n_q
        ksl = slice(jk * dq, (jk + 1) * dq)
        vsl = slice(2 * jk * dv, (2 * jk + 2) * dv)
        for r in range(ATTN_ROW_SPLIT):
            rsl = slice(r * rows, (r + 1) * rows)
            q = q_ref[rsl, j * dq:(j + 1) * dq]
            if k_scratch:
                o = _attend(q, k_scratch[0][:, ksl], v_ref[:, vsl], dv)
            else:
                o = _attend_parts(q, [kc_ref[:, ksl], kn_ref[:, ksl]], v_ref.at[:, vsl], dv)
            o_ref[rsl, j * dv:(j + 1) * dv] = o


def _attn_lat(q, layer, kc, kn, vc, vn, n_units, n_q, n_kv, dq, dv, name, join_k, tq=ATTN_TQ):
    nb, seq, _ = q.shape
    past = kc.shape[2]
    scratch = [pltpu.VMEM((past + seq, n_kv * 2 * dv), BF16)]
    if join_k:
        scratch.append(pltpu.VMEM((past + seq, n_kv * dq), BF16))
    return pl.pallas_call(
        functools.partial(_attn_lat_kernel, n_q=n_q, n_kv=n_kv, dq=dq, dv=dv, past=past),
        grid=(nb, n_units, seq // tq),
        in_specs=[
            pl.BlockSpec((None, tq, n_q * dq), lambda b, u, i: (b, i, u)),
            pl.BlockSpec((None, None, past, n_kv * dq), lambda b, u, i: (layer, b, 0, u)),
            pl.BlockSpec((None, seq, n_kv * dq), lambda b, u, i: (b, 0, u)),
            pl.BlockSpec((None, None, past, n_kv * dv), lambda b, u, i: (layer, b, 0, u)),
            pl.BlockSpec((None, seq, n_kv * dv), lambda b, u, i: (b, 0, u)),
        ],
        out_specs=pl.BlockSpec((None, tq, n_q * dv), lambda b, u, i: (b, i, u)),
        out_shape=jax.ShapeDtypeStruct((nb, seq, n_units * n_q * dv), BF16),
        scratch_shapes=scratch,
        compiler_params=_cparams(3),
        name=name,
    )(q, kc, kn, vc, vn)


OUT_TM = 512


def _outproj_kernel(x_ref, oa_ref, ob_ref, mod_ref, w_ref, y_ref):
    y = _dot(oa_ref[...], w_ref[0:W_QA, :]) + _dot(ob_ref[...], w_ref[W_QA:, :])
    y_ref[...] = x_ref[...] + mod_ref[2:3, :] * y


def _outproj(x, oa, ob, layer, mods, mod_map, w_o, tm=OUT_TM):
    t = x.shape[0]
    return pl.pallas_call(
        _outproj_kernel,
        grid=(t // tm,),
        in_specs=[
            pl.BlockSpec((tm, D_MODEL), lambda i: (i, 0)),
            pl.BlockSpec((tm, W_QA), lambda i: (i, 0)),
            pl.BlockSpec((tm, W_VB), lambda i: (i, 0)),
            _mod_spec(layer, mod_map, D_MODEL),
            _layer_spec((W_QA + W_VB, D_MODEL), layer),
        ],
        out_specs=pl.BlockSpec((tm, D_MODEL), lambda i: (i, 0)),
        out_shape=jax.ShapeDtypeStruct((t, D_MODEL), F32),
        compiler_params=_cparams(1),
        name="outproj",
    )(x, oa, ob, mods, w_o)


FFN_TM = 1024
FFN_TF = 512
FFN_TN = 512
FFN_ROWS = 256


def _ffn_kernel(*refs, nf, tf, rc, n_tiles, n_cast):
    xr_ref, xc_ref, mod_ref, modc_ref, g_ref, wg_ref, wu_ref, wd_ref = refs[:8]
    cast_in = refs[8:8 + n_cast]
    y_ref = refs[8 + n_cast]
    cast_out = refs[9 + n_cast:9 + 2 * n_cast]
    h_ref, t_ref = refs[9 + 2 * n_cast:]
    i = pl.program_id(0)
    j = pl.program_id(1)

    def cast_chunks():
        for src, dst in zip(cast_in, cast_out):
            dst[...] = src[...].astype(BF16)
    cur = (i + 1) % 2
    nxt = i % 2

    def norm_chunk():
        h = _rms(xr_ref[...]) * g_ref[...]
        h = h * (1.0 + mod_ref[4:5, :]) + mod_ref[3:4, :]
        r0 = pl.multiple_of((j - nf) * rc, rc)
        h_ref[nxt, pl.ds(r0, rc), :] = h.astype(BF16)

    row_blocks = [slice(m, m + FFN_ROWS) for m in range(0, y_ref.shape[0], FFN_ROWS)]

    def down_proj():
        for rows in row_blocks:
            d = _dot(t_ref[0, rows, :], wd_ref[0:tf, :])
            for k in range(1, nf):
                d = d + _dot(t_ref[k, rows, :], wd_ref[k * tf:(k + 1) * tf, :])
            y_ref[rows, :] = xc_ref[rows, :] + modc_ref[5:6, :] * d

    @pl.when((j < nf) & (i == 0))
    def _():
        cast_chunks()

    @pl.when((j < nf) & (i >= 1))
    def _():
        cast_chunks()
        for rows in row_blocks:
            hb = h_ref[cur, rows, :]
            a = _dot(hb, wg_ref[...])
            b = _dot(hb, wu_ref[...])
            t_ref[j, rows, :] = (a * jax.nn.sigmoid(a) * b).astype(BF16)

    @pl.when((j >= nf) & (i >= 1) & (i < n_tiles))
    def _():
        cast_chunks()
        norm_chunk()
        down_proj()

    @pl.when((j >= nf) & (i == 0))
    def _():
        cast_chunks()
        norm_chunk()

    @pl.when((j >= nf) & (i == n_tiles))
    def _():
        cast_chunks()
        down_proj()


def _cast_rows(n_rows, n_steps):
    rows = 16
    while n_rows % rows or n_rows // rows > n_steps:
        rows += 16
    return rows


def _ffn(x, layer, mods, mod_map, g, wg, wu, wd, cast=(), tm=FFN_TM, tf=FFN_TF, tn=FFN_TN):
    t = x.shape[0]
    n_tiles = t // tm
    nf = D_FF // tf
    nn = D_MODEL // tn
    n_steps = (n_tiles + 1) * (nf + nn)
    cast_specs_in, cast_specs_out, cast_shapes = [], [], []
    for src, src_layer in cast:
        _, n_rows, n_cols = src.shape
        rows = _cast_rows(n_rows, n_steps)
        chunk = lambda i, j, last=n_rows // rows - 1: jnp.minimum(i * (nf + nn) + j, last)
        cast_specs_in.append(pl.BlockSpec(
            (None, rows, n_cols), lambda i, j, c=chunk, l=src_layer: (l, c(i, j), 0)))
        cast_specs_out.append(pl.BlockSpec((rows, n_cols), lambda i, j, c=chunk: (c(i, j), 0)))
        cast_shapes.append(jax.ShapeDtypeStruct((n_rows, n_cols), BF16))
    rc = tm // nn
    col = lambda j: jnp.maximum(j - nf, 0)
    comp = lambda i: jnp.maximum(i - 1, 0)
    norm = lambda i: jnp.minimum(i, n_tiles - 1)
    ccol = lambda i, j: jnp.where(i == 0, 0, col(j))
    ftile = lambda i, j: jnp.where(i == 0, 0, jnp.minimum(j, nf - 1))
    outs = pl.pallas_call(
        functools.partial(_ffn_kernel, nf=nf, tf=tf, rc=rc, n_tiles=n_tiles, n_cast=len(cast)),
        grid=(n_tiles + 1, nf + nn),
        in_specs=[
            pl.BlockSpec((rc, D_MODEL), lambda i, j: (norm(i) * nn + col(j), 0)),
            pl.BlockSpec((tm, tn), lambda i, j: (comp(i), ccol(i, j))),
            pl.BlockSpec((None, None, N_MOD, D_MODEL),
                         lambda i, j: (layer, mod_map(norm(i)), 0, 0)),
            pl.BlockSpec((None, None, N_MOD, tn),
                         lambda i, j: (layer, mod_map(comp(i)), 0, ccol(i, j))),
            _layer_spec((1, D_MODEL), layer),
            pl.BlockSpec((D_MODEL, tf), lambda i, j: (0, ftile(i, j))),
            pl.BlockSpec((D_MODEL, tf), lambda i, j: (0, ftile(i, j))),
            pl.BlockSpec((D_FF, tn), lambda i, j: (0, ccol(i, j))),
        ] + cast_specs_in,
        out_specs=[pl.BlockSpec((tm, tn), lambda i, j: (comp(i), ccol(i, j)))] + cast_specs_out,
        out_shape=[jax.ShapeDtypeStruct((t, D_MODEL), F32)] + cast_shapes,
        scratch_shapes=[pltpu.VMEM((2, tm, D_MODEL), BF16), pltpu.VMEM((nf, tm, tf), BF16)],
        compiler_params=_cparams(2),
        name="ffn",
    )(x, x, mods, mods, g, wg, wu, wd, *[src for src, _ in cast])
    return outs[0], outs[1:]


NORM_TM = 512


def _final_norm_kernel(x_ref, g_ref, y_ref):
    y_ref[...] = _rms(x_ref[...]) * g_ref[...]


def _final_norm(x, g, tm=NORM_TM):
    t = x.shape[0]
    return pl.pallas_call(
        _final_norm_kernel,
        grid=(t // tm,),
        in_specs=[pl.BlockSpec((tm, D_MODEL), lambda i: (i, 0)),
                  pl.BlockSpec((1, D_MODEL), lambda i: (0, 0))],
        out_specs=pl.BlockSpec((tm, D_MODEL), lambda i: (i, 0)),
        out_shape=jax.ShapeDtypeStruct((t, D_MODEL), F32),
        compiler_params=_cparams(1),
        name="final_norm",
    )(x, g)


def _rope_tables(n_tokens):
    n_rows = n_tokens // GRID_W
    row = jnp.repeat(jnp.arange(n_rows, dtype=F32), GRID_W)
    col = jnp.tile(jnp.arange(GRID_W, dtype=F32), n_rows)

    def cs(dim):
        n_freq = dim // 4
        inv = ROPE_THETA ** (-jnp.arange(n_freq, dtype=F32) / n_freq)
        ang = jnp.concatenate([row[:, None] * inv, col[:, None] * inv], axis=-1)
        return jnp.cos(ang), jnp.sin(ang)

    cos_a, sin_a = cs(HEAD_DIM)
    cos_b, sin_b = cs(ROPE_B)
    one = jnp.ones_like(cos_b)
    zero = jnp.zeros_like(sin_b)
    return (jnp.concatenate([cos_a, cos_a], axis=-1),
            jnp.concatenate([-sin_a, sin_a], axis=-1),
            jnp.concatenate([cos_b, one, cos_b, one], axis=-1),
            jnp.concatenate([-sin_b, zero, sin_b, zero], axis=-1))


def kernel(x_prompt, x_sample, cache_k_a, cache_v_a, cache_ckv_b, cache_krope_b, c, c_ctx, w_ada, b_ada, norm_attn, norm_ffn, w_in, qnorm_a, knorm_a, kvnorm_b, w_uk_b, w_uv_b, w_o, w_gate, w_up, w_down, norm_final):
    batch, seq, _ = x_prompt.shape
    dec_batch, dec_seq, _ = x_sample.shape
    assert dec_batch + 1 <= MOD_ROWS

    w_in_b = _wprep(w_in.astype(BF16))
    wuk_b = w_uk_b.astype(BF16)
    wuv_b = w_uv_b.astype(BF16)
    w_o_b = w_o.astype(BF16)
    wg_l = w_gate[0].astype(BF16)
    wu_l = w_up[0].astype(BF16)
    wd_l = w_down[0].astype(BF16)
    rope_tabs = _rope_tables(dec_seq)

    cond = jnp.concatenate(
        [c_ctx[None, :], c, jnp.zeros((MOD_ROWS - 1 - dec_batch, D_MODEL), F32)], axis=0)
    mods = _adaln(cond, w_ada, b_ada).reshape(DEPTH, MOD_ROWS, N_MOD, D_MODEL)

    kc_a, vc_a, kc_b, vc_b = _cache_prep(
        cache_k_a, cache_v_a, cache_ckv_b, cache_krope_b, wuk_b, wuv_b)

    xp = x_prompt.reshape(batch * seq, D_MODEL)
    xs = x_sample.reshape(dec_batch * dec_seq, D_MODEL)
    g1 = norm_attn.reshape(DEPTH, 1, D_MODEL)
    g2 = norm_ffn.reshape(DEPTH, 1, D_MODEL)
    qn = qnorm_a.reshape(DEPTH, 1, HEAD_DIM)
    kn = knorm_a.reshape(DEPTH, 1, HEAD_DIM)
    kvn = kvnorm_b.reshape(DEPTH, 1, KV_RANK)
    ctx_mod = lambda i: 0

    def lat_mod(tm):
        return lambda i: 1 + i // (dec_seq // tm)

    r3 = lambda a: a.reshape(dec_batch, dec_seq, a.shape[-1])
    hb = ATTN_HEADS_PER_STEP
    new_bufs = None

    for l in range(DEPTH):
        outs = _proj(xp, l, mods, ctx_mod, g1, w_in_b, qn, kn, kvn, wuk_b, wuv_b,
                     new_shape=(batch, seq), new_bufs=new_bufs)
        oa, ob = outs[:2]
        new_bufs = outs[2:]
        xp = _outproj(xp, oa, ob, l, mods, ctx_mod, w_o_b)
        nxt = l + 1 < DEPTH
        xp, gu_next = _ffn(xp, l, mods, ctx_mod, g2, wg_l, wu_l, wd_l,
                           cast=((w_gate, l + 1), (w_up, l + 1)) if nxt else ())

        qa, ka, va, qb, kb, vb = _proj(
            xs, l, mods, lat_mod(PROJ_TM), g1, w_in_b, qn, kn, kvn, wuk_b, wuv_b,
            rope_tabs=rope_tabs)
        oa = _attn_lat(r3(qa), l, kc_a, r3(ka), vc_a, r3(va),
                       1, N_HEADS_A, N_KV_A, HEAD_DIM, HEAD_DIM, "attn_lat_a", join_k=True)
        ob = _attn_lat(r3(qb), l, kc_b, r3(kb), vc_b, r3(vb),
                       N_HEADS_B // hb, hb, hb, QB_PAD, VDIM_B, "attn_lat_b", join_k=False)
        xs = _outproj(xs, oa.reshape(-1, W_QA), ob.reshape(-1, W_VB), l, mods, lat_mod(OUT_TM), w_o_b)
        xs, d_next = _ffn(xs, l, mods, lat_mod(FFN_TM), g2, wg_l, wu_l, wd_l,
                          cast=((w_down, l + 1),) if nxt else ())
        if nxt:
            (wg_l, wu_l), (wd_l,) = gu_next, d_next

    gf = norm_final.reshape(1, D_MODEL)
    y_prompt = _final_norm(xp, gf).reshape(batch, seq, D_MODEL)
    y_sample = _final_norm(xs, gf).reshape(dec_batch, dec_seq, D_MODEL)
    new_k, new_v, new_ckv, new_kr = new_bufs
    return (y_prompt, y_sample,
            new_k, new_v, new_ckv, new_kr)
```

```python
import functools

import jax
import jax.numpy as jnp
import numpy as np
from jax import lax
from jax.experimental import pallas as pl
from jax.experimental.pallas import tpu as pltpu

D_MODEL = 2048
DEPTH = 4
GRID_W = 64
HEAD_DIM = 128
N_HEADS_A = 8
N_KV_A = 2
GROUP_A = N_HEADS_A // N_KV_A
N_HEADS_B = 8
NOPE_B = 128
ROPE_B = 64
VDIM_B = 128
KV_RANK = 256
D_FF = 5632
ROPE_THETA = 10000.0
EPS = 1e-6

W_QA = N_HEADS_A * HEAD_DIM
W_KA = N_KV_A * HEAD_DIM
QB_DIM = NOPE_B + ROPE_B
W_QB = N_HEADS_B * QB_DIM
QB_PAD = 256
KR_PAD = 128
W_QB_PAD = N_HEADS_B * QB_PAD
W_KNOPE = N_HEADS_B * NOPE_B
W_VB = N_HEADS_B * VDIM_B
N_MOD = 6
MOD_ROWS = 8

O_QB = W_QA + 2 * W_KA
O_CKV = O_QB + W_QB
O_KR = O_CKV + KV_RANK
IN_COLS = O_KR + ROPE_B

C_QA = 0
C_KA = C_QA + W_QA
C_VA = C_KA + W_KA
C_QB = C_VA + W_KA
C_CKV = C_QB + W_QB_PAD
C_KR = C_CKV + KV_RANK
IN_COLS_PAD = C_KR + KR_PAD

LOG2_E = 1.4426950408889634
SCALE_A = HEAD_DIM ** -0.5 * LOG2_E
SCALE_B = QB_DIM ** -0.5 * LOG2_E

VMEM_LIMIT = 56 * 1024 * 1024

BF16 = jnp.bfloat16
F32 = jnp.float32


def _cparams(n_axes):
    return pltpu.CompilerParams(
        dimension_semantics=("arbitrary",) * n_axes, vmem_limit_bytes=VMEM_LIMIT)


def _layer_spec(shape, layer):
    nd = len(shape)
    return pl.BlockSpec((None,) + shape, lambda *_: (layer,) + (0,) * nd,
                        pipeline_mode=pl.Buffered(1))


def _mod_spec(layer, mod_map, width, col_map=None):
    if col_map is None:
        return pl.BlockSpec((None, None, N_MOD, width), lambda i, *_: (layer, mod_map(i), 0, 0))
    return pl.BlockSpec((None, None, N_MOD, width),
                        lambda i, j: (layer, mod_map(i), 0, col_map(j)))


def _dot(a, b):
    return jnp.dot(a, b, preferred_element_type=F32)


def _dot_t(a, b):
    return lax.dot_general(a, b, (((1,), (1,)), ((), ())), preferred_element_type=F32)


def _rms(x):
    return x * lax.rsqrt(jnp.mean(x * x, axis=-1, keepdims=True) + EPS)


def _cast_rows(n_rows, n_steps):
    rows = 16
    while n_rows % rows or n_rows // rows > n_steps:
        rows += 16
    return rows


def _side_cast(cast, n_steps, step_of):
    in_specs, out_specs, shapes = [], [], []
    for src, src_layer in cast:
        _, n_rows, n_cols = src.shape
        rows = _cast_rows(n_rows, n_steps)
        chunk = lambda *ids, last=n_rows // rows - 1: jnp.minimum(step_of(*ids), last)
        in_specs.append(pl.BlockSpec(
            (None, rows, n_cols), lambda *ids, c=chunk, l=src_layer: (l, c(*ids), 0)))
        out_specs.append(pl.BlockSpec((rows, n_cols), lambda *ids, c=chunk: (c(*ids), 0)))
        shapes.append(jax.ShapeDtypeStruct((n_rows, n_cols), BF16))
    return in_specs, out_specs, shapes, [src for src, _ in cast]


def _cast_chunks(cast_in, cast_out):
    for src, dst in zip(cast_in, cast_out):
        dst[...] = src[...].astype(BF16)


ADA_TN = 1024


def _adaln_kernel(s_ref, w_ref, b_ref, o_ref):
    s = s_ref[...]
    s = s * jax.nn.sigmoid(s)
    o_ref[...] = _dot(s.astype(BF16), w_ref[...].astype(BF16)) + b_ref[...]


def _adaln(s, w_ada, b_ada):
    n = N_MOD * D_MODEL
    return pl.pallas_call(
        _adaln_kernel,
        grid=(DEPTH, n // ADA_TN),
        in_specs=[
            pl.BlockSpec((MOD_ROWS, D_MODEL), lambda l, j: (0, 0)),
            pl.BlockSpec((None, D_MODEL, ADA_TN), lambda l, j: (l, 0, j)),
            pl.BlockSpec((None, 1, ADA_TN), lambda l, j: (l, 0, j)),
        ],
        out_specs=pl.BlockSpec((None, MOD_ROWS, ADA_TN), lambda l, j: (l, 0, j)),
        out_shape=jax.ShapeDtypeStruct((DEPTH, MOD_ROWS, n), F32),
        compiler_params=_cparams(2),
        name="adaln",
    )(s, w_ada, b_ada.reshape(DEPTH, 1, n))


WPREP_TR = 512
PAIR_IN = 2 * QB_DIM
PAIR_OUT = 2 * QB_PAD


def _wprep_kernel(w_ref, pq_ref, pk_ref, o_ref):
    o_ref[:, 0:O_QB] = w_ref[:, 0:O_QB]
    pq = pq_ref[...]
    for i in range(N_HEADS_B // 2):
        src = w_ref[:, O_QB + i * PAIR_IN:O_QB + (i + 1) * PAIR_IN]
        o_ref[:, C_QB + i * PAIR_OUT:C_QB + (i + 1) * PAIR_OUT] = _dot(src, pq).astype(BF16)
    o_ref[:, C_CKV:C_CKV + KV_RANK] = w_ref[:, O_CKV:O_CKV + KV_RANK]
    o_ref[:, C_KR:C_KR + KR_PAD] = _dot(w_ref[:, O_KR:O_KR + ROPE_B], pk_ref[...]).astype(BF16)


def _rope_pad_src():
    half = ROPE_B // 2
    src = np.full((KR_PAD,), -1, np.int32)
    src[0:half] = np.arange(half)
    src[2 * half:3 * half] = half + np.arange(half)
    return src


def _selection(src, n_in):
    return (jnp.arange(n_in, dtype=jnp.int32)[:, None] == jnp.asarray(src)[None, :]).astype(BF16)


def _wprep(w_in):
    rsrc = _rope_pad_src()
    head = np.concatenate([np.arange(NOPE_B), np.where(rsrc >= 0, NOPE_B + rsrc, -1)])
    pair = np.concatenate([head, np.where(head >= 0, QB_DIM + head, -1)]).astype(np.int32)
    pq = _selection(pair, PAIR_IN)
    pk = _selection(rsrc, ROPE_B)
    return pl.pallas_call(
        _wprep_kernel,
        grid=(DEPTH, D_MODEL // WPREP_TR),
        in_specs=[
            pl.BlockSpec((None, WPREP_TR, IN_COLS), lambda l, r: (l, r, 0)),
            pl.BlockSpec((PAIR_IN, PAIR_OUT), lambda l, r: (0, 0)),
            pl.BlockSpec((ROPE_B, KR_PAD), lambda l, r: (0, 0)),
        ],
        out_specs=pl.BlockSpec((None, WPREP_TR, IN_COLS_PAD), lambda l, r: (l, r, 0)),
        out_shape=jax.ShapeDtypeStruct((DEPTH, D_MODEL, IN_COLS_PAD), BF16),
        compiler_params=_cparams(2),
        name="wprep",
    )(w_in, pq, pk)


def _cache_kernel(ka_ref, va_ref, ckv_ref, kr_ref, wuk_ref, wuv_ref, pk_ref,
                  ka_o, va_o, kb_o, vb_o):
    ka_o[...] = ka_ref[...].astype(BF16)
    va_o[...] = va_ref[...].astype(BF16)
    ckv = ckv_ref[...].astype(BF16)
    kn = _dot(ckv, wuk_ref[...])
    vb_o[...] = _dot(ckv, wuv_ref[...]).astype(BF16)
    kr = _dot(kr_ref[...].astype(BF16), pk_ref[...]).astype(BF16)
    for h in range(N_HEADS_B):
        kb_o[:, h * QB_PAD:h * QB_PAD + NOPE_B] = kn[:, h * NOPE_B:(h + 1) * NOPE_B].astype(BF16)
        kb_o[:, h * QB_PAD + NOPE_B:(h + 1) * QB_PAD] = kr


def _cache_prep(cache_k_a, cache_v_a, cache_ckv_b, cache_krope_b, wuk, wuv):
    nb, _, past = cache_ckv_b.shape[:3]
    ka = cache_k_a.reshape(nb, DEPTH, past, W_KA)
    va = cache_v_a.reshape(nb, DEPTH, past, W_KA)
    pk = _selection(_rope_pad_src(), ROPE_B)

    def in4(width):
        return pl.BlockSpec((None, None, past, width), lambda l, b: (b, l, 0, 0))

    def out4(width):
        return pl.BlockSpec((None, None, past, width), lambda l, b: (l, b, 0, 0))

    def w3(width):
        return pl.BlockSpec((None, KV_RANK, width), lambda l, b: (l, 0, 0))

    return pl.pallas_call(
        _cache_kernel,
        grid=(DEPTH, nb),
        in_specs=[in4(W_KA), in4(W_KA), in4(KV_RANK), in4(ROPE_B), w3(W_KNOPE), w3(W_VB),
                  pl.BlockSpec((ROPE_B, KR_PAD), lambda l, b: (0, 0))],
        out_specs=[out4(W_KA), out4(W_KA), out4(W_QB_PAD), out4(W_VB)],
        out_shape=[
            jax.ShapeDtypeStruct((DEPTH, nb, past, W_KA), BF16),
            jax.ShapeDtypeStruct((DEPTH, nb, past, W_KA), BF16),
            jax.ShapeDtypeStruct((DEPTH, nb, past, W_QB_PAD), BF16),
            jax.ShapeDtypeStruct((DEPTH, nb, past, W_VB), BF16),
        ],
        compiler_params=_cparams(2),
        name="cache_prep",
    )(ka, va, cache_ckv_b, cache_krope_b, wuk, wuv, pk)


PROJ_TM = 512
PROJ_ROWS = 256
N_NEW = 4


def _proj_kernel(*refs, rope, emit_new, n_alias, n_cast):
    it = iter(refs)
    x_ref, mod_ref, g_ref, w_ref, qn_ref, kn_ref, kvn_ref, wuk_ref, wuv_ref = (
        next(it) for _ in range(9))
    if rope:
        ca_ref, sa_ref, cb_ref, sb_ref = (next(it) for _ in range(4))
    for _ in range(n_alias):
        next(it)
    cast_in = [next(it) for _ in range(n_cast)]
    if emit_new:
        oa_o, ob_o, nk_o, nv_o, nckv_o, nkr_o = (next(it) for _ in range(2 + N_NEW))
        cast_out = [next(it) for _ in range(n_cast)]
    qa_o, ka_o, va_o, qb_o, kb_o, vb_o = (next(it) for _ in range(6))
    if emit_new:
        vae_s, vbe_s = next(it), next(it)
    else:
        cast_out = [next(it) for _ in range(n_cast)]
    _cast_chunks(cast_in, cast_out)

    for r in range(x_ref.shape[0] // PROJ_ROWS):
        rows = slice(r * PROJ_ROWS, (r + 1) * PROJ_ROWS)

        h = _rms(x_ref[rows, :]) * g_ref[...]
        h = h * (1.0 + mod_ref[1:2, :]) + mod_ref[0:1, :]
        hb = h.astype(BF16)

        def rope_a(t):
            if not rope:
                return t
            return t * ca_ref[rows, :] + pltpu.roll(t, HEAD_DIM // 2, 1) * sa_ref[rows, :]

        def rope_b(t):
            if not rope:
                return t
            return t * cb_ref[rows, :] + pltpu.roll(t, KR_PAD // 2, 1) * sb_ref[rows, :]

        pq = _dot(hb, w_ref[:, C_QA:C_QA + W_QA])
        qn = qn_ref[...]
        for i in range(N_HEADS_A):
            sl = slice(i * HEAD_DIM, (i + 1) * HEAD_DIM)
            t = rope_a(_rms(pq[:, sl]) * qn)
            qa_o[rows, sl] = (t * SCALE_A).astype(BF16)

        pk = _dot(hb, w_ref[:, C_KA:C_KA + W_KA])
        pv = _dot(hb, w_ref[:, C_VA:C_VA + W_KA])
        kn = kn_ref[...]
        for i in range(N_KV_A):
            sl = slice(i * HEAD_DIM, (i + 1) * HEAD_DIM)
            t = _rms(pk[:, sl]) * kn
            if emit_new:
                nk_o[r, :, i, :] = t
            ka_o[rows, sl] = rope_a(t).astype(BF16)
        va_o[rows, :] = pv.astype(BF16)
        if emit_new:
            for i in range(N_KV_A):
                nv_o[r, :, i, :] = pv[:, i * HEAD_DIM:(i + 1) * HEAD_DIM]

        pqb = _dot(hb, w_ref[:, C_QB:C_QB + W_QB_PAD])
        for i in range(N_HEADS_B):
            s0 = i * QB_PAD
            qb_o[rows, s0:s0 + NOPE_B] = (pqb[:, s0:s0 + NOPE_B] * SCALE_B).astype(BF16)
            t = rope_b(pqb[:, s0 + NOPE_B:s0 + QB_PAD])
            qb_o[rows, s0 + NOPE_B:s0 + QB_PAD] = (t * SCALE_B).astype(BF16)

        pc = _dot(hb, w_ref[:, C_CKV:C_CKV + KV_RANK])
        ckv = _rms(pc) * kvn_ref[...]
        pkr = _dot(hb, w_ref[:, C_KR:C_KR + KR_PAD])
        if emit_new:
            half = ROPE_B // 2
            nckv_o[r] = ckv
            nkr_o[r] = jnp.concatenate([pkr[:, 0:half], pkr[:, 2 * half:3 * half]], axis=-1)
        ckvb = ckv.astype(BF16)
        krb = rope_b(pkr).astype(BF16)
        kno = _dot(ckvb, wuk_ref[...])
        vb_o[rows, :] = _dot(ckvb, wuv_ref[...]).astype(BF16)
        for i in range(N_HEADS_B):
            s0 = i * QB_PAD
            kb_o[rows, s0:s0 + NOPE_B] = kno[:, i * NOPE_B:(i + 1) * NOPE_B].astype(BF16)
            kb_o[rows, s0 + NOPE_B:s0 + QB_PAD] = krb

        if emit_new:
            _fill_v_ext(vae_s, rows, va_o.at[rows, :], N_KV_A, HEAD_DIM)
            _fill_v_ext(vbe_s, rows, vb_o.at[rows, :], N_HEADS_B, VDIM_B)
            for hq in range(N_HEADS_A):
                g = hq // GROUP_A
                qsl = slice(hq * HEAD_DIM, (hq + 1) * HEAD_DIM)
                oa_o[rows, qsl] = _attend(
                    qa_o[rows, qsl], ka_o[rows, g * HEAD_DIM:(g + 1) * HEAD_DIM],
                    vae_s[rows, 2 * g * HEAD_DIM:(2 * g + 2) * HEAD_DIM], HEAD_DIM)
            for hq in range(N_HEADS_B):
                qsl = slice(hq * QB_PAD, (hq + 1) * QB_PAD)
                ob_o[rows, hq * VDIM_B:(hq + 1) * VDIM_B] = _attend(
                    qb_o[rows, qsl], kb_o[rows, qsl],
                    vbe_s[rows, 2 * hq * VDIM_B:(2 * hq + 2) * VDIM_B], VDIM_B)


def _proj(x, layer, mods, mod_map, g, w_in, qn, kn, kvn, wuk, wuv, rope_tabs=None,
          new_shape=None, new_bufs=None, cast=(), tm=PROJ_TM):
    t = x.shape[0]
    rope = rope_tabs is not None
    emit_new = new_shape is not None

    def row(width):
        return pl.BlockSpec((tm, width), lambda i: (i, 0))

    in_specs = [
        row(D_MODEL),
        _mod_spec(layer, mod_map, D_MODEL),
        _layer_spec((1, D_MODEL), layer),
        _layer_spec((D_MODEL, IN_COLS_PAD), layer),
        _layer_spec((1, HEAD_DIM), layer),
        _layer_spec((1, HEAD_DIM), layer),
        _layer_spec((1, KV_RANK), layer),
        _layer_spec((KV_RANK, W_KNOPE), layer),
        _layer_spec((KV_RANK, W_VB), layer),
    ]
    args = [x, mods, g, w_in, qn, kn, kvn, wuk, wuv]
    if rope:
        seq_tiles = rope_tabs[0].shape[0] // tm
        for tab in rope_tabs:
            in_specs.append(pl.BlockSpec((tm, tab.shape[1]), lambda i: (i % seq_tiles, 0)))
            args.append(tab)
    qkv_widths = [W_QA, W_KA, W_KA, W_QB_PAD, W_QB_PAD, W_VB]
    widths = [W_QA, W_VB] if emit_new else qkv_widths
    out_specs = [row(w) for w in widths]
    out_shape = [jax.ShapeDtypeStruct((t, w), BF16) for w in widths]
    scratch = []
    aliases = {}
    n_alias = 0
    if emit_new:
        scratch = [pltpu.VMEM((tm, w), BF16) for w in qkv_widths + [2 * W_KA, 2 * W_VB]]
        batch, seq = new_shape
        assert seq == PROJ_ROWS and t == batch * seq
        nb = tm // seq
        tails = [(N_KV_A, HEAD_DIM), (N_KV_A, HEAD_DIM), (KV_RANK,), (ROPE_B,)]
        if new_bufs is not None:
            n_alias = N_NEW
            for k, buf in enumerate(new_bufs):
                aliases[len(args)] = len(widths) + k
                in_specs.append(pl.BlockSpec(memory_space=pl.ANY))
                args.append(buf)
        for tail in tails:
            zeros = (0,) * len(tail)
            out_specs.append(pl.BlockSpec((nb, None, seq) + tail,
                                          lambda i, zeros=zeros: (i, layer, 0) + zeros))
            out_shape.append(jax.ShapeDtypeStruct((batch, DEPTH, seq) + tail, F32))
    cast_specs_in, cast_specs_out, cast_shapes, cast_srcs = _side_cast(
        cast, t // tm, lambda i: i)
    outs = pl.pallas_call(
        functools.partial(_proj_kernel, rope=rope, emit_new=emit_new, n_alias=n_alias,
                          n_cast=len(cast)),
        grid=(t // tm,),
        in_specs=in_specs + cast_specs_in,
        out_specs=out_specs + cast_specs_out,
        out_shape=out_shape + cast_shapes,
        input_output_aliases=aliases,
        scratch_shapes=scratch,
        compiler_params=_cparams(1),
        name="proj_lat" if rope else "proj_attn_ctx",
    )(*args, *cast_srcs)
    n_main = len(out_shape)
    return outs[:n_main], outs[n_main:]


def _attend(q, k, v_ext, dv):
    s = _dot_t(q, k)
    p = jnp.exp2(s - s.max(axis=-1, keepdims=True))
    o = _dot(p.astype(BF16), v_ext)
    return (o[:, 0:dv] / o[:, dv:2 * dv]).astype(BF16)


def _attend_parts(q, k_parts, v_ext_ref, dv):
    s = [_dot_t(q, k) for k in k_parts]
    m = s[0].max(axis=-1, keepdims=True)
    for sp in s[1:]:
        m = jnp.maximum(m, sp.max(axis=-1, keepdims=True))
    o = None
    r0 = 0
    for sp in s:
        r1 = r0 + sp.shape[1]
        op = _dot(jnp.exp2(sp - m).astype(BF16), v_ext_ref[r0:r1, :])
        o = op if o is None else o + op
        r0 = r1
    return (o[:, 0:dv] / o[:, dv:2 * dv]).astype(BF16)


def _fill_v_ext(v_ext_ref, rows, v_ref, n_kv, dv):
    ones = jnp.ones((v_ref.shape[0], dv), BF16)
    for jk in range(n_kv):
        v_ext_ref[rows, 2 * jk * dv:(2 * jk + 1) * dv] = v_ref[:, jk * dv:(jk + 1) * dv]
        v_ext_ref[rows, (2 * jk + 1) * dv:(2 * jk + 2) * dv] = ones


ATTN_TQ = 512
ATTN_HEADS_PER_STEP = 8
ATTN_ROW_SPLIT = 2


def _attn_lat_kernel(*refs, n_q, n_kv, dq, dv, past, n_cast):
    q_ref, kc_ref, kn_ref, vc_ref, vn_ref = refs[:5]
    o_ref = refs[5 + n_cast]
    v_ref = refs[6 + 2 * n_cast]
    k_scratch = refs[7 + 2 * n_cast:]
    _attn_lat_body(q_ref, kc_ref, kn_ref, vc_ref, vn_ref, o_ref, v_ref, k_scratch,
                   n_q=n_q, n_kv=n_kv, dq=dq, dv=dv, past=past)
    _cast_chunks(refs[5:5 + n_cast], refs[6 + n_cast:6 + 2 * n_cast])


def _attn_lat_body(q_ref, kc_ref, kn_ref, vc_ref, vn_ref, o_ref, v_ref, k_scratch,
                   *, n_q, n_kv, dq, dv, past):
    @pl.when(pl.program_id(2) == 0)
    def _():
        _fill_v_ext(v_ref, slice(0, past), vc_ref, n_kv, dv)
        _fill_v_ext(v_ref, slice(past, None), vn_ref, n_kv, dv)
        if k_scratch:
            k_scratch[0][0:past, :] = kc_ref[...]
            k_scratch[0][past:, :] = kn_ref[...]

    rows = q_ref.shape[0] // ATTN_ROW_SPLIT
    for j in range(n_q):
        jk = j * n_kv // n_q
        ksl = slice(jk * dq, (jk + 1) * dq)
        vsl = slice(2 * jk * dv, (2 * jk + 2) * dv)
        for r in range(ATTN_ROW_SPLIT):
            rsl = slice(r * rows, (r + 1) * rows)
            q = q_ref[rsl, j * dq:(j + 1) * dq]
            if k_scratch:
                o = _attend(q, k_scratch[0][:, ksl], v_ref[:, vsl], dv)
            else:
                o = _attend_parts(q, [kc_ref[:, ksl], kn_ref[:, ksl]], v_ref.at[:, vsl], dv)
            o_ref[rsl, j * dv:(j + 1) * dv] = o


def _attn_lat(q, layer, kc, kn, vc, vn, n_units, n_q, n_kv, dq, dv, name, join_k, cast=(),
              tq=ATTN_TQ):
    nb, seq, _ = q.shape
    past = kc.shape[2]
    scratch = [pltpu.VMEM((past + seq, n_kv * 2 * dv), BF16)]
    if join_k:
        scratch.append(pltpu.VMEM((past + seq, n_kv * dq), BF16))
    n_qt = seq // tq
    cast_specs_in, cast_specs_out, cast_shapes, cast_srcs = _side_cast(
        cast, nb * n_units * n_qt, lambda b, u, i: (b * n_units + u) * n_qt + i)
    outs = pl.pallas_call(
        functools.partial(_attn_lat_kernel, n_q=n_q, n_kv=n_kv, dq=dq, dv=dv, past=past,
                          n_cast=len(cast)),
        grid=(nb, n_units, n_qt),
        in_specs=[
            pl.BlockSpec((None, tq, n_q * dq), lambda b, u, i: (b, i, u)),
            pl.BlockSpec((None, None, past, n_kv * dq), lambda b, u, i: (layer, b, 0, u)),
            pl.BlockSpec((None, seq, n_kv * dq), lambda b, u, i: (b, 0, u)),
            pl.BlockSpec((None, None, past, n_kv * dv), lambda b, u, i: (layer, b, 0, u)),
            pl.BlockSpec((None, seq, n_kv * dv), lambda b, u, i: (b, 0, u)),
        ] + cast_specs_in,
        out_specs=[pl.BlockSpec((None, tq, n_q * dv), lambda b, u, i: (b, i, u))] + cast_specs_out,
        out_shape=[jax.ShapeDtypeStruct((nb, seq, n_units * n_q * dv), BF16)] + cast_shapes,
        scratch_shapes=scratch,
        compiler_params=_cparams(3),
        name=name,
    )(q, kc, kn, vc, vn, *cast_srcs)
    return outs[0], outs[1:]


OUT_TM = 512


def _outproj_kernel(x_ref, oa_ref, ob_ref, mod_ref, w_ref, y_ref):
    y = _dot(oa_ref[...], w_ref[0:W_QA, :]) + _dot(ob_ref[...], w_ref[W_QA:, :])
    y_ref[...] = x_ref[...] + mod_ref[2:3, :] * y


def _outproj(x, oa, ob, layer, mods, mod_map, w_o, tm=OUT_TM):
    t = x.shape[0]
    return pl.pallas_call(
        _outproj_kernel,
        grid=(t // tm,),
        in_specs=[
            pl.BlockSpec((tm, D_MODEL), lambda i: (i, 0)),
            pl.BlockSpec((tm, W_QA), lambda i: (i, 0)),
            pl.BlockSpec((tm, W_VB), lambda i: (i, 0)),
            _mod_spec(layer, mod_map, D_MODEL),
            pl.BlockSpec((W_QA + W_VB, D_MODEL), lambda i: (0, 0), pipeline_mode=pl.Buffered(1)),
        ],
        out_specs=pl.BlockSpec((tm, D_MODEL), lambda i: (i, 0)),
        out_shape=jax.ShapeDtypeStruct((t, D_MODEL), F32),
        compiler_params=_cparams(1),
        name="outproj",
    )(x, oa, ob, mods, w_o)


FFN_TM = 1024
FFN_TF = 512
FFN_TN = 512
FFN_ROWS = 256


def _ffn_kernel(*refs, nf, tf, rc, n_tiles, n_cast):
    xr_ref, xc_ref, mod_ref, modc_ref, g_ref, wg_ref, wu_ref, wd_ref = refs[:8]
    cast_in = refs[8:8 + n_cast]
    y_ref = refs[8 + n_cast]
    cast_out = refs[9 + n_cast:9 + 2 * n_cast]
    h_ref, t_ref = refs[9 + 2 * n_cast:]
    i = pl.program_id(0)
    j = pl.program_id(1)

    def cast_chunks():
        _cast_chunks(cast_in, cast_out)
    cur = (i + 1) % 2
    nxt = i % 2

    def norm_chunk():
        h = _rms(xr_ref[...]) * g_ref[...]
        h = h * (1.0 + mod_ref[4:5, :]) + mod_ref[3:4, :]
        r0 = pl.multiple_of((j - nf) * rc, rc)
        h_ref[nxt, pl.ds(r0, rc), :] = h.astype(BF16)

    row_blocks = [slice(m, m + FFN_ROWS) for m in range(0, y_ref.shape[0], FFN_ROWS)]

    def down_proj():
        for rows in row_blocks:
            d = _dot(t_ref[0, rows, :], wd_ref[0:tf, :])
            for k in range(1, nf):
                d = d + _dot(t_ref[k, rows, :], wd_ref[k * tf:(k + 1) * tf, :])
            y_ref[rows, :] = xc_ref[rows, :] + modc_ref[5:6, :] * d

    @pl.when((j < nf) & (i == 0))
    def _():
        cast_chunks()

    @pl.when((j < nf) & (i >= 1))
    def _():
        cast_chunks()
        for rows in row_blocks:
            hb = h_ref[cur, rows, :]
            a = _dot(hb, wg_ref[...])
            b = _dot(hb, wu_ref[...])
            t_ref[j, rows, :] = (a * jax.nn.sigmoid(a) * b).astype(BF16)

    @pl.when((j >= nf) & (i >= 1) & (i < n_tiles))
    def _():
        cast_chunks()
        norm_chunk()
        down_proj()

    @pl.when((j >= nf) & (i == 0))
    def _():
        cast_chunks()
        norm_chunk()

    @pl.when((j >= nf) & (i == n_tiles))
    def _():
        cast_chunks()
        down_proj()


def _ffn(x, layer, mods, mod_map, g, wg, wu, wd, cast=(), tm=FFN_TM, tf=FFN_TF, tn=FFN_TN):
    t = x.shape[0]
    n_tiles = t // tm
    nf = D_FF // tf
    nn = D_MODEL // tn
    cast_specs_in, cast_specs_out, cast_shapes, cast_srcs = _side_cast(
        cast, (n_tiles + 1) * (nf + nn), lambda i, j: i * (nf + nn) + j)
    rc = tm // nn
    col = lambda j: jnp.maximum(j - nf, 0)
    comp = lambda i: jnp.maximum(i - 1, 0)
    norm = lambda i: jnp.minimum(i, n_tiles - 1)
    ccol = lambda i, j: jnp.where(i == 0, 0, col(j))
    ftile = lambda i, j: jnp.where(i == 0, 0, jnp.minimum(j, nf - 1))
    outs = pl.pallas_call(
        functools.partial(_ffn_kernel, nf=nf, tf=tf, rc=rc, n_tiles=n_tiles, n_cast=len(cast)),
        grid=(n_tiles + 1, nf + nn),
        in_specs=[
            pl.BlockSpec((rc, D_MODEL), lambda i, j: (norm(i) * nn + col(j), 0)),
            pl.BlockSpec((tm, tn), lambda i, j: (comp(i), ccol(i, j))),
            pl.BlockSpec((None, None, N_MOD, D_MODEL),
                         lambda i, j: (layer, mod_map(norm(i)), 0, 0)),
            pl.BlockSpec((None, None, N_MOD, tn),
                         lambda i, j: (layer, mod_map(comp(i)), 0, ccol(i, j))),
            _layer_spec((1, D_MODEL), layer),
            pl.BlockSpec((D_MODEL, tf), lambda i, j: (0, ftile(i, j))),
            pl.BlockSpec((D_MODEL, tf), lambda i, j: (0, ftile(i, j))),
            pl.BlockSpec((D_FF, tn), lambda i, j: (0, ccol(i, j))),
        ] + cast_specs_in,
        out_specs=[pl.BlockSpec((tm, tn), lambda i, j: (comp(i), ccol(i, j)))] + cast_specs_out,
        out_shape=[jax.ShapeDtypeStruct((t, D_MODEL), F32)] + cast_shapes,
        scratch_shapes=[pltpu.VMEM((2, tm, D_MODEL), BF16), pltpu.VMEM((nf, tm, tf), BF16)],
        compiler_params=_cparams(2),
        name="ffn",
    )(x, x, mods, mods, g, wg, wu, wd, *cast_srcs)
    return outs[0], outs[1:]


NORM_TM = 512


def _final_norm_kernel(x_ref, g_ref, y_ref):
    y_ref[...] = _rms(x_ref[...]) * g_ref[...]


def _final_norm(x, g, tm=NORM_TM):
    t = x.shape[0]
    return pl.pallas_call(
        _final_norm_kernel,
        grid=(t // tm,),
        in_specs=[pl.BlockSpec((tm, D_MODEL), lambda i: (i, 0)),
                  pl.BlockSpec((1, D_MODEL), lambda i: (0, 0))],
        out_specs=pl.BlockSpec((tm, D_MODEL), lambda i: (i, 0)),
        out_shape=jax.ShapeDtypeStruct((t, D_MODEL), F32),
        compiler_params=_cparams(1),
        name="final_norm",
    )(x, g)


def _rope_tables(n_tokens):
    n_rows = n_tokens // GRID_W
    row = jnp.repeat(jnp.arange(n_rows, dtype=F32), GRID_W)
    col = jnp.tile(jnp.arange(GRID_W, dtype=F32), n_rows)

    def cs(dim):
        n_freq = dim // 4
        inv = ROPE_THETA ** (-jnp.arange(n_freq, dtype=F32) / n_freq)
        ang = jnp.concatenate([row[:, None] * inv, col[:, None] * inv], axis=-1)
        return jnp.cos(ang), jnp.sin(ang)

    cos_a, sin_a = cs(HEAD_DIM)
    cos_b, sin_b = cs(ROPE_B)
    one = jnp.ones_like(cos_b)
    zero = jnp.zeros_like(sin_b)
    return (jnp.concatenate([cos_a, cos_a], axis=-1),
            jnp.concatenate([-sin_a, sin_a], axis=-1),
            jnp.concatenate([cos_b, one, cos_b, one], axis=-1),
            jnp.concatenate([-sin_b, zero, sin_b, zero], axis=-1))


def kernel(x_prompt, x_sample, cache_k_a, cache_v_a, cache_ckv_b, cache_krope_b, c, c_ctx, w_ada, b_ada, norm_attn, norm_ffn, w_in, qnorm_a, knorm_a, kvnorm_b, w_uk_b, w_uv_b, w_o, w_gate, w_up, w_down, norm_final):
    batch, seq, _ = x_prompt.shape
    dec_batch, dec_seq, _ = x_sample.shape
    assert dec_batch + 1 <= MOD_ROWS

    w_in_b = _wprep(w_in.astype(BF16))
    wuk_b = w_uk_b.astype(BF16)
    wuv_b = w_uv_b.astype(BF16)
    wo_l = w_o[0].astype(BF16)
    rope_tabs = _rope_tables(dec_seq)

    cond = jnp.concatenate(
        [c_ctx[None, :], c, jnp.zeros((MOD_ROWS - 1 - dec_batch, D_MODEL), F32)], axis=0)
    mods = _adaln(cond, w_ada, b_ada).reshape(DEPTH, MOD_ROWS, N_MOD, D_MODEL)

    kc_a, vc_a, kc_b, vc_b = _cache_prep(
        cache_k_a, cache_v_a, cache_ckv_b, cache_krope_b, wuk_b, wuv_b)

    xp = x_prompt.reshape(batch * seq, D_MODEL)
    xs = x_sample.reshape(dec_batch * dec_seq, D_MODEL)
    g1 = norm_attn.reshape(DEPTH, 1, D_MODEL)
    g2 = norm_ffn.reshape(DEPTH, 1, D_MODEL)
    qn = qnorm_a.reshape(DEPTH, 1, HEAD_DIM)
    kn = knorm_a.reshape(DEPTH, 1, HEAD_DIM)
    kvn = kvnorm_b.reshape(DEPTH, 1, KV_RANK)
    ctx_mod = lambda i: 0

    def lat_mod(tm):
        return lambda i: 1 + i // (dec_seq // tm)

    r3 = lambda a: a.reshape(dec_batch, dec_seq, a.shape[-1])
    hb = ATTN_HEADS_PER_STEP
    new_bufs = None

    for l in range(DEPTH):
        first, nxt = l == 0, l + 1 < DEPTH

        (oa_c, ob_c, *new_bufs), _ = _proj(
            xp, l, mods, ctx_mod, g1, w_in_b, qn, kn, kvn, wuk_b, wuv_b,
            new_shape=(batch, seq), new_bufs=new_bufs)
        xp = _outproj(xp, oa_c, ob_c, l, mods, ctx_mod, wo_l)

        (qa, ka, va, qb, kb, vb), cast0 = _proj(
            xs, l, mods, lat_mod(PROJ_TM), g1, w_in_b, qn, kn, kvn, wuk_b, wuv_b,
            rope_tabs=rope_tabs, cast=((w_gate, 0),) if first else ())
        oa, cast1 = _attn_lat(r3(qa), l, kc_a, r3(ka), vc_a, r3(va),
                              1, N_HEADS_A, N_KV_A, HEAD_DIM, HEAD_DIM, "attn_lat_a", join_k=True,
                              cast=((w_up, 0), (w_down, 0)) if first else ())
        ob, _ = _attn_lat(r3(qb), l, kc_b, r3(kb), vc_b, r3(vb),
                          N_HEADS_B // hb, hb, hb, QB_PAD, VDIM_B, "attn_lat_b", join_k=False)
        xs = _outproj(xs, oa.reshape(-1, W_QA), ob.reshape(-1, W_VB), l, mods, lat_mod(OUT_TM), wo_l)
        if first:
            (wg_l,), (wu_l, wd_l) = cast0, cast1

        xp, gu_next = _ffn(xp, l, mods, ctx_mod, g2, wg_l, wu_l, wd_l,
                           cast=((w_gate, l + 1), (w_up, l + 1)) if nxt else ())
        xs, do_next = _ffn(xs, l, mods, lat_mod(FFN_TM), g2, wg_l, wu_l, wd_l,
                           cast=((w_down, l + 1), (w_o, l + 1)) if nxt else ())
        if nxt:
            (wg_l, wu_l), (wd_l, wo_l) = gu_next, do_next

    gf = norm_final.reshape(1, D_MODEL)
    y_prompt = _final_norm(xp, gf).reshape(batch, seq, D_MODEL)
    y_sample = _final_norm(xs, gf).reshape(dec_batch, dec_seq, D_MODEL)
    new_k, new_v, new_ckv, new_kr = new_bufs
    return (y_prompt, y_sample,
            new_k, new_v, new_ckv, new_kr)
```

```python
import functools

import jax
import jax.numpy as jnp
import numpy as np
from jax import lax
from jax.experimental import pallas as pl
from jax.experimental.pallas import tpu as pltpu

D_MODEL = 2048
DEPTH = 4
GRID_W = 64
HEAD_DIM = 128
N_HEADS_A = 8
N_KV_A = 2
GROUP_A = N_HEADS_A // N_KV_A
N_HEADS_B = 8
NOPE_B = 128
ROPE_B = 64
VDIM_B = 128
KV_RANK = 256
D_FF = 5632
ROPE_THETA = 10000.0
EPS = 1e-6

W_QA = N_HEADS_A * HEAD_DIM
W_KA = N_KV_A * HEAD_DIM
QB_DIM = NOPE_B + ROPE_B
W_QB = N_HEADS_B * QB_DIM
QB_PAD = 256
KR_PAD = 128
W_QB_PAD = N_HEADS_B * QB_PAD
W_KNOPE = N_HEADS_B * NOPE_B
W_VB = N_HEADS_B * VDIM_B
N_MOD = 6
MOD_ROWS = 8

O_QB = W_QA + 2 * W_KA
O_CKV = O_QB + W_QB
O_KR = O_CKV + KV_RANK
IN_COLS = O_KR + ROPE_B

C_QA = 0
C_KA = C_QA + W_QA
C_VA = C_KA + W_KA
C_QB = C_VA + W_KA
C_CKV = C_QB + W_QB_PAD
C_KR = C_CKV + KV_RANK
IN_COLS_PAD = C_KR + KR_PAD

LOG2_E = 1.4426950408889634
SCALE_A = HEAD_DIM ** -0.5 * LOG2_E
SCALE_B = QB_DIM ** -0.5 * LOG2_E

VMEM_LIMIT = 60 * 1024 * 1024

BF16 = jnp.bfloat16
F32 = jnp.float32


def _cparams(n_axes):
    return pltpu.CompilerParams(
        dimension_semantics=("arbitrary",) * n_axes, vmem_limit_bytes=VMEM_LIMIT)


def _layer_spec(shape, layer):
    nd = len(shape)
    return pl.BlockSpec((None,) + shape, lambda *_: (layer,) + (0,) * nd,
                        pipeline_mode=pl.Buffered(1))


def _mod_spec(layer, mod_map, width, col_map=None):
    if col_map is None:
        return pl.BlockSpec((None, None, N_MOD, width), lambda i, *_: (layer, mod_map(i), 0, 0))
    return pl.BlockSpec((None, None, N_MOD, width),
                        lambda i, j: (layer, mod_map(i), 0, col_map(j)))


def _dot(a, b):
    return jnp.dot(a, b, preferred_element_type=F32)


def _dot_t(a, b):
    return lax.dot_general(a, b, (((1,), (1,)), ((), ())), preferred_element_type=F32)


def _rms(x):
    return x * lax.rsqrt(jnp.mean(x * x, axis=-1, keepdims=True) + EPS)


def _cast_rows(n_rows, n_steps):
    rows = 16
    while n_rows % rows or n_rows // rows > n_steps:
        rows += 16
    return rows


def _side_cast(cast, n_steps, step_of):
    in_specs, out_specs, shapes = [], [], []
    for src, src_layer in cast:
        _, n_rows, n_cols = src.shape
        rows = _cast_rows(n_rows, n_steps)
        chunk = lambda *ids, last=n_rows // rows - 1: jnp.minimum(step_of(*ids), last)
        in_specs.append(pl.BlockSpec(
            (None, rows, n_cols), lambda *ids, c=chunk, l=src_layer: (l, c(*ids), 0)))
        out_specs.append(pl.BlockSpec((rows, n_cols), lambda *ids, c=chunk: (c(*ids), 0)))
        shapes.append(jax.ShapeDtypeStruct((n_rows, n_cols), BF16))
    return in_specs, out_specs, shapes, [src for src, _ in cast]


def _cast_chunks(cast_in, cast_out):
    for src, dst in zip(cast_in, cast_out):
        dst[...] = src[...].astype(BF16)


ADA_TN = 1024


def _adaln_kernel(s_ref, w_ref, b_ref, o_ref):
    s = s_ref[...]
    s = s * jax.nn.sigmoid(s)
    o_ref[...] = _dot(s.astype(BF16), w_ref[...].astype(BF16)) + b_ref[...]


def _adaln(s, w_ada, b_ada):
    n = N_MOD * D_MODEL
    return pl.pallas_call(
        _adaln_kernel,
        grid=(DEPTH, n // ADA_TN),
        in_specs=[
            pl.BlockSpec((MOD_ROWS, D_MODEL), lambda l, j: (0, 0)),
            pl.BlockSpec((None, D_MODEL, ADA_TN), lambda l, j: (l, 0, j)),
            pl.BlockSpec((None, 1, ADA_TN), lambda l, j: (l, 0, j)),
        ],
        out_specs=pl.BlockSpec((None, MOD_ROWS, ADA_TN), lambda l, j: (l, 0, j)),
        out_shape=jax.ShapeDtypeStruct((DEPTH, MOD_ROWS, n), F32),
        compiler_params=_cparams(2),
        name="adaln",
    )(s, w_ada, b_ada.reshape(DEPTH, 1, n))


WPREP_TR = 512
PAIR_IN = 2 * QB_DIM
PAIR_OUT = 2 * QB_PAD


def _wprep_kernel(w_ref, pq_ref, pk_ref, o_ref):
    o_ref[:, 0:O_QB] = w_ref[:, 0:O_QB]
    pq = pq_ref[...]
    for i in range(N_HEADS_B // 2):
        src = w_ref[:, O_QB + i * PAIR_IN:O_QB + (i + 1) * PAIR_IN]
        o_ref[:, C_QB + i * PAIR_OUT:C_QB + (i + 1) * PAIR_OUT] = _dot(src, pq).astype(BF16)
    o_ref[:, C_CKV:C_CKV + KV_RANK] = w_ref[:, O_CKV:O_CKV + KV_RANK]
    o_ref[:, C_KR:C_KR + KR_PAD] = _dot(w_ref[:, O_KR:O_KR + ROPE_B], pk_ref[...]).astype(BF16)


def _rope_pad_src():
    half = ROPE_B // 2
    src = np.full((KR_PAD,), -1, np.int32)
    src[0:half] = np.arange(half)
    src[2 * half:3 * half] = half + np.arange(half)
    return src


def _selection(src, n_in):
    return (jnp.arange(n_in, dtype=jnp.int32)[:, None] == jnp.asarray(src)[None, :]).astype(BF16)


def _wprep(w_in):
    rsrc = _rope_pad_src()
    head = np.concatenate([np.arange(NOPE_B), np.where(rsrc >= 0, NOPE_B + rsrc, -1)])
    pair = np.concatenate([head, np.where(head >= 0, QB_DIM + head, -1)]).astype(np.int32)
    pq = _selection(pair, PAIR_IN)
    pk = _selection(rsrc, ROPE_B)
    return pl.pallas_call(
        _wprep_kernel,
        grid=(DEPTH, D_MODEL // WPREP_TR),
        in_specs=[
            pl.BlockSpec((None, WPREP_TR, IN_COLS), lambda l, r: (l, r, 0)),
            pl.BlockSpec((PAIR_IN, PAIR_OUT), lambda l, r: (0, 0)),
            pl.BlockSpec((ROPE_B, KR_PAD), lambda l, r: (0, 0)),
        ],
        out_specs=pl.BlockSpec((None, WPREP_TR, IN_COLS_PAD), lambda l, r: (l, r, 0)),
        out_shape=jax.ShapeDtypeStruct((DEPTH, D_MODEL, IN_COLS_PAD), BF16),
        compiler_params=_cparams(2),
        name="wprep",
    )(w_in, pq, pk)


def _cache_kernel(ka_ref, va_ref, ckv_ref, kr_ref, wuk_ref, wuv_ref, pk_ref,
                  ka_o, va_o, kb_o, vb_o):
    ka_o[...] = ka_ref[...].astype(BF16)
    va_o[...] = va_ref[...].astype(BF16)
    ckv = ckv_ref[...].astype(BF16)
    kn = _dot(ckv, wuk_ref[...])
    vb_o[...] = _dot(ckv, wuv_ref[...]).astype(BF16)
    kr = _dot(kr_ref[...].astype(BF16), pk_ref[...]).astype(BF16)
    for h in range(N_HEADS_B):
        kb_o[:, h * QB_PAD:h * QB_PAD + NOPE_B] = kn[:, h * NOPE_B:(h + 1) * NOPE_B].astype(BF16)
        kb_o[:, h * QB_PAD + NOPE_B:(h + 1) * QB_PAD] = kr


def _cache_prep(cache_k_a, cache_v_a, cache_ckv_b, cache_krope_b, wuk, wuv):
    nb, _, past = cache_ckv_b.shape[:3]
    ka = cache_k_a.reshape(nb, DEPTH, past, W_KA)
    va = cache_v_a.reshape(nb, DEPTH, past, W_KA)
    pk = _selection(_rope_pad_src(), ROPE_B)

    def in4(width):
        return pl.BlockSpec((None, None, past, width), lambda l, b: (b, l, 0, 0))

    def out4(width):
        return pl.BlockSpec((None, None, past, width), lambda l, b: (l, b, 0, 0))

    def w3(width):
        return pl.BlockSpec((None, KV_RANK, width), lambda l, b: (l, 0, 0))

    return pl.pallas_call(
        _cache_kernel,
        grid=(DEPTH, nb),
        in_specs=[in4(W_KA), in4(W_KA), in4(KV_RANK), in4(ROPE_B), w3(W_KNOPE), w3(W_VB),
                  pl.BlockSpec((ROPE_B, KR_PAD), lambda l, b: (0, 0))],
        out_specs=[out4(W_KA), out4(W_KA), out4(W_QB_PAD), out4(W_VB)],
        out_shape=[
            jax.ShapeDtypeStruct((DEPTH, nb, past, W_KA), BF16),
            jax.ShapeDtypeStruct((DEPTH, nb, past, W_KA), BF16),
            jax.ShapeDtypeStruct((DEPTH, nb, past, W_QB_PAD), BF16),
            jax.ShapeDtypeStruct((DEPTH, nb, past, W_VB), BF16),
        ],
        compiler_params=_cparams(2),
        name="cache_prep",
    )(ka, va, cache_ckv_b, cache_krope_b, wuk, wuv, pk)


PROJ_TM = 512
PROJ_ROWS = 256
N_NEW = 4


def _proj_kernel(*refs, rope, emit_new, n_alias, n_cast):
    it = iter(refs)
    x_ref, mod_ref, g_ref, w_ref, qn_ref, kn_ref, kvn_ref, wuk_ref, wuv_ref = (
        next(it) for _ in range(9))
    if rope:
        ca_ref, sa_ref, cb_ref, sb_ref = (next(it) for _ in range(4))
    if emit_new:
        wo_ref = next(it)
    for _ in range(n_alias):
        next(it)
    cast_in = [next(it) for _ in range(n_cast)]
    if emit_new:
        xo_o, nk_o, nv_o, nckv_o, nkr_o = (next(it) for _ in range(1 + N_NEW))
        cast_out = [next(it) for _ in range(n_cast)]
    qa_o, ka_o, va_o, qb_o, kb_o, vb_o = (next(it) for _ in range(6))
    if emit_new:
        vae_s, vbe_s, oa_o, ob_o = (next(it) for _ in range(4))
    else:
        cast_out = [next(it) for _ in range(n_cast)]
    _cast_chunks(cast_in, cast_out)

    for r in range(x_ref.shape[0] // PROJ_ROWS):
        rows = slice(r * PROJ_ROWS, (r + 1) * PROJ_ROWS)

        h = _rms(x_ref[rows, :]) * g_ref[...]
        h = h * (1.0 + mod_ref[1:2, :]) + mod_ref[0:1, :]
        hb = h.astype(BF16)

        def rope_a(t):
            if not rope:
                return t
            return t * ca_ref[rows, :] + pltpu.roll(t, HEAD_DIM // 2, 1) * sa_ref[rows, :]

        def rope_b(t):
            if not rope:
                return t
            return t * cb_ref[rows, :] + pltpu.roll(t, KR_PAD // 2, 1) * sb_ref[rows, :]

        pq = _dot(hb, w_ref[:, C_QA:C_QA + W_QA])
        qn = qn_ref[...]
        for i in range(N_HEADS_A):
            sl = slice(i * HEAD_DIM, (i + 1) * HEAD_DIM)
            t = rope_a(_rms(pq[:, sl]) * qn)
            qa_o[rows, sl] = (t * SCALE_A).astype(BF16)

        pk = _dot(hb, w_ref[:, C_KA:C_KA + W_KA])
        pv = _dot(hb, w_ref[:, C_VA:C_VA + W_KA])
        kn = kn_ref[...]
        for i in range(N_KV_A):
            sl = slice(i * HEAD_DIM, (i + 1) * HEAD_DIM)
            t = _rms(pk[:, sl]) * kn
            if emit_new:
                nk_o[r, :, i, :] = t
            ka_o[rows, sl] = rope_a(t).astype(BF16)
        va_o[rows, :] = pv.astype(BF16)
        if emit_new:
            for i in range(N_KV_A):
                nv_o[r, :, i, :] = pv[:, i * HEAD_DIM:(i + 1) * HEAD_DIM]

        pqb = _dot(hb, w_ref[:, C_QB:C_QB + W_QB_PAD])
        for i in range(N_HEADS_B):
            s0 = i * QB_PAD
            qb_o[rows, s0:s0 + NOPE_B] = (pqb[:, s0:s0 + NOPE_B] * SCALE_B).astype(BF16)
            t = rope_b(pqb[:, s0 + NOPE_B:s0 + QB_PAD])
            qb_o[rows, s0 + NOPE_B:s0 + QB_PAD] = (t * SCALE_B).astype(BF16)

        pc = _dot(hb, w_ref[:, C_CKV:C_CKV + KV_RANK])
        ckv = _rms(pc) * kvn_ref[...]
        pkr = _dot(hb, w_ref[:, C_KR:C_KR + KR_PAD])
        if emit_new:
            half = ROPE_B // 2
            nckv_o[r] = ckv
            nkr_o[r] = jnp.concatenate([pkr[:, 0:half], pkr[:, 2 * half:3 * half]], axis=-1)
        ckvb = ckv.astype(BF16)
        krb = rope_b(pkr).astype(BF16)
        kno = _dot(ckvb, wuk_ref[...])
        vb_o[rows, :] = _dot(ckvb, wuv_ref[...]).astype(BF16)
        for i in range(N_HEADS_B):
            s0 = i * QB_PAD
            kb_o[rows, s0:s0 + NOPE_B] = kno[:, i * NOPE_B:(i + 1) * NOPE_B].astype(BF16)
            kb_o[rows, s0 + NOPE_B:s0 + QB_PAD] = krb

        if emit_new:
            _fill_v_ext(vae_s, rows, va_o.at[rows, :], N_KV_A, HEAD_DIM)
            _fill_v_ext(vbe_s, rows, vb_o.at[rows, :], N_HEADS_B, VDIM_B)
            for hq in range(N_HEADS_A):
                g = hq // GROUP_A
                qsl = slice(hq * HEAD_DIM, (hq + 1) * HEAD_DIM)
                oa_o[rows, qsl] = _attend(
                    qa_o[rows, qsl], ka_o[rows, g * HEAD_DIM:(g + 1) * HEAD_DIM],
                    vae_s[rows, 2 * g * HEAD_DIM:(2 * g + 2) * HEAD_DIM], HEAD_DIM)
            for hq in range(N_HEADS_B):
                qsl = slice(hq * QB_PAD, (hq + 1) * QB_PAD)
                ob_o[rows, hq * VDIM_B:(hq + 1) * VDIM_B] = _attend(
                    qb_o[rows, qsl], kb_o[rows, qsl],
                    vbe_s[rows, 2 * hq * VDIM_B:(2 * hq + 2) * VDIM_B], VDIM_B)
            y = _dot(oa_o[rows, :], wo_ref[0:W_QA, :]) + _dot(ob_o[rows, :], wo_ref[W_QA:, :])
            xo_o[rows, :] = x_ref[rows, :] + mod_ref[2:3, :] * y


def _proj(x, layer, mods, mod_map, g, w_in, qn, kn, kvn, wuk, wuv, rope_tabs=None,
          new_shape=None, new_bufs=None, w_o=None, cast=(), tm=PROJ_TM):
    t = x.shape[0]
    rope = rope_tabs is not None
    emit_new = new_shape is not None

    def row(width):
        return pl.BlockSpec((tm, width), lambda i: (i, 0))

    in_specs = [
        row(D_MODEL),
        _mod_spec(layer, mod_map, D_MODEL),
        _layer_spec((1, D_MODEL), layer),
        _layer_spec((D_MODEL, IN_COLS_PAD), layer),
        _layer_spec((1, HEAD_DIM), layer),
        _layer_spec((1, HEAD_DIM), layer),
        _layer_spec((1, KV_RANK), layer),
        _layer_spec((KV_RANK, W_KNOPE), layer),
        _layer_spec((KV_RANK, W_VB), layer),
    ]
    args = [x, mods, g, w_in, qn, kn, kvn, wuk, wuv]
    if rope:
        seq_tiles = rope_tabs[0].shape[0] // tm
        for tab in rope_tabs:
            in_specs.append(pl.BlockSpec((tm, tab.shape[1]), lambda i: (i % seq_tiles, 0)))
            args.append(tab)
    qkv_widths = [W_QA, W_KA, W_KA, W_QB_PAD, W_QB_PAD, W_VB]
    scratch = []
    aliases = {}
    n_alias = 0
    if not emit_new:
        out_specs = [row(w) for w in qkv_widths]
        out_shape = [jax.ShapeDtypeStruct((t, w), BF16) for w in qkv_widths]
    else:
        in_specs.append(pl.BlockSpec((W_QA + W_VB, D_MODEL), lambda i: (0, 0),
                                     pipeline_mode=pl.Buffered(1)))
        args.append(w_o)
        out_specs = [row(D_MODEL)]
        out_shape = [jax.ShapeDtypeStruct((t, D_MODEL), F32)]
        scratch = [pltpu.VMEM((tm, w), BF16)
                   for w in qkv_widths + [2 * W_KA, 2 * W_VB, W_QA, W_VB]]
        batch, seq = new_shape
        assert seq == PROJ_ROWS and t == batch * seq
        nb = tm // seq
        tails = [(N_KV_A, HEAD_DIM), (N_KV_A, HEAD_DIM), (KV_RANK,), (ROPE_B,)]
        if new_bufs is not None:
            n_alias = N_NEW
            for k, buf in enumerate(new_bufs):
                aliases[len(args)] = len(out_shape) + k
                in_specs.append(pl.BlockSpec(memory_space=pl.ANY))
                args.append(buf)
        for tail in tails:
            zeros = (0,) * len(tail)
            out_specs.append(pl.BlockSpec((nb, None, seq) + tail,
                                          lambda i, zeros=zeros: (i, layer, 0) + zeros))
            out_shape.append(jax.ShapeDtypeStruct((batch, DEPTH, seq) + tail, F32))
    cast_specs_in, cast_specs_out, cast_shapes, cast_srcs = _side_cast(
        cast, t // tm, lambda i: i)
    outs = pl.pallas_call(
        functools.partial(_proj_kernel, rope=rope, emit_new=emit_new, n_alias=n_alias,
                          n_cast=len(cast)),
        grid=(t // tm,),
        in_specs=in_specs + cast_specs_in,
        out_specs=out_specs + cast_specs_out,
        out_shape=out_shape + cast_shapes,
        input_output_aliases=aliases,
        scratch_shapes=scratch,
        compiler_params=_cparams(1),
        name="proj_lat" if rope else "proj_attn_ctx",
    )(*args, *cast_srcs)
    n_main = len(out_shape)
    return outs[:n_main], outs[n_main:]


def _attend(q, k, v_ext, dv):
    s = _dot_t(q, k)
    p = jnp.exp2(s - s.max(axis=-1, keepdims=True))
    o = _dot(p.astype(BF16), v_ext)
    return (o[:, 0:dv] / o[:, dv:2 * dv]).astype(BF16)


def _attend_parts(q, k_parts, v_ext_ref, dv):
    s = [_dot_t(q, k) for k in k_parts]
    m = s[0].max(axis=-1, keepdims=True)
    for sp in s[1:]:
        m = jnp.maximum(m, sp.max(axis=-1, keepdims=True))
    o = None
    r0 = 0
    for sp in s:
        r1 = r0 + sp.shape[1]
        op = _dot(jnp.exp2(sp - m).astype(BF16), v_ext_ref[r0:r1, :])
        o = op if o is None else o + op
        r0 = r1
    return (o[:, 0:dv] / o[:, dv:2 * dv]).astype(BF16)


def _fill_v_ext(v_ext_ref, rows, v_ref, n_kv, dv):
    ones = jnp.ones((v_ref.shape[0], dv), BF16)
    for jk in range(n_kv):
        v_ext_ref[rows, 2 * jk * dv:(2 * jk + 1) * dv] = v_ref[:, jk * dv:(jk + 1) * dv]
        v_ext_ref[rows, (2 * jk + 1) * dv:(2 * jk + 2) * dv] = ones


ATTN_TQ = 512
ATTN_HEADS_PER_STEP = 8
ATTN_ROW_SPLIT = 2


def _attn_lat_kernel(*refs, n_q, n_kv, dq, dv, past, n_cast):
    q_ref, kc_ref, kn_ref, vc_ref, vn_ref = refs[:5]
    o_ref = refs[5 + n_cast]
    v_ref = refs[6 + 2 * n_cast]
    k_scratch = refs[7 + 2 * n_cast:]
    _attn_lat_body(q_ref, kc_ref, kn_ref, vc_ref, vn_ref, o_ref, v_ref, k_scratch,
                   n_q=n_q, n_kv=n_kv, dq=dq, dv=dv, past=past)
    _cast_chunks(refs[5:5 + n_cast], refs[6 + n_cast:6 + 2 * n_cast])


def _attn_lat_body(q_ref, kc_ref, kn_ref, vc_ref, vn_ref, o_ref, v_ref, k_scratch,
                   *, n_q, n_kv, dq, dv, past):
    @pl.when(pl.program_id(2) == 0)
    def _():
        _fill_v_ext(v_ref, slice(0, past), vc_ref, n_kv, dv)
        _fill_v_ext(v_ref, slice(past, None), vn_ref, n_kv, dv)
        if k_scratch:
            k_scratch[0][0:past, :] = kc_ref[...]
            k_scratch[0][past:, :] = kn_ref[...]

    rows = q_ref.shape[0] // ATTN_ROW_SPLIT
    for j in range(n_q):
        jk = j * n_kv // n_q
        ksl = slice(jk * dq, (jk + 1) * dq)
        vsl = slice(2 * jk * dv, (2 * jk + 2) * dv)
        for r in range(ATTN_ROW_SPLIT):
            rsl = slice(r * rows, (r + 1) * rows)
            q = q_ref[rsl, j * dq:(j + 1) * dq]
            if k_scratch:
                o = _attend(q, k_scratch[0][:, ksl], v_ref[:, vsl], dv)
            else:
                o = _attend_parts(q, [kc_ref[:, ksl], kn_ref[:, ksl]], v_ref.at[:, vsl], dv)
            o_ref[rsl, j * dv:(j + 1) * dv] = o


def _attn_lat(q, layer, kc, kn, vc, vn, n_units, n_q, n_kv, dq, dv, name, join_k, cast=(),
              tq=ATTN_TQ):
    nb, seq, _ = q.shape
    past = kc.shape[2]
    scratch = [pltpu.VMEM((past + seq, n_kv * 2 * dv), BF16)]
    if join_k:
        scratch.append(pltpu.VMEM((past + seq, n_kv * dq), BF16))
    n_qt = seq // tq
    cast_specs_in, cast_specs_out, cast_shapes, cast_srcs = _side_cast(
        cast, nb * n_units * n_qt, lambda b, u, i: (b * n_units + u) * n_qt + i)
    outs = pl.pallas_call(
        functools.partial(_attn_lat_kernel, n_q=n_q, n_kv=n_kv, dq=dq, dv=dv, past=past,
                          n_cast=len(cast)),
        grid=(nb, n_units, n_qt),
        in_specs=[
            pl.BlockSpec((None, tq, n_q * dq), lambda b, u, i: (b, i, u)),
            pl.BlockSpec((None, None, past, n_kv * dq), lambda b, u, i: (layer, b, 0, u)),
            pl.BlockSpec((None, seq, n_kv * dq), lambda b, u, i: (b, 0, u)),
            pl.BlockSpec((None, None, past, n_kv * dv), lambda b, u, i: (layer, b, 0, u)),
            pl.BlockSpec((None, seq, n_kv * dv), lambda b, u, i: (b, 0, u)),
        ] + cast_specs_in,
        out_specs=[pl.BlockSpec((None, tq, n_q * dv), lambda b, u, i: (b, i, u))] + cast_specs_out,
        out_shape=[jax.ShapeDtypeStruct((nb, seq, n_units * n_q * dv), BF16)] + cast_shapes,
        scratch_shapes=scratch,
        compiler_params=_cparams(3),
        name=name,
    )(q, kc, kn, vc, vn, *cast_srcs)
    return outs[0], outs[1:]


OUT_TM = 512


def _outproj_kernel(x_ref, oa_ref, ob_ref, mod_ref, w_ref, y_ref):
    y = _dot(oa_ref[...], w_ref[0:W_QA, :]) + _dot(ob_ref[...], w_ref[W_QA:, :])
    y_ref[...] = x_ref[...] + mod_ref[2:3, :] * y


def _outproj(x, oa, ob, layer, mods, mod_map, w_o, tm=OUT_TM):
    t = x.shape[0]
    return pl.pallas_call(
        _outproj_kernel,
        grid=(t // tm,),
        in_specs=[
            pl.BlockSpec((tm, D_MODEL), lambda i: (i, 0)),
            pl.BlockSpec((tm, W_QA), lambda i: (i, 0)),
            pl.BlockSpec((tm, W_VB), lambda i: (i, 0)),
            _mod_spec(layer, mod_map, D_MODEL),
            pl.BlockSpec((W_QA + W_VB, D_MODEL), lambda i: (0, 0), pipeline_mode=pl.Buffered(1)),
        ],
        out_specs=pl.BlockSpec((tm, D_MODEL), lambda i: (i, 0)),
        out_shape=jax.ShapeDtypeStruct((t, D_MODEL), F32),
        compiler_params=_cparams(1),
        name="outproj",
    )(x, oa, ob, mods, w_o)


FFN_TM = 1024
FFN_TF = 512
FFN_TN = 512
FFN_ROWS = 256


def _ffn_kernel(*refs, nf, tf, rc, n_tiles, n_cast):
    xr_ref, xc_ref, mod_ref, modc_ref, g_ref, wg_ref, wu_ref, wd_ref = refs[:8]
    cast_in = refs[8:8 + n_cast]
    y_ref = refs[8 + n_cast]
    cast_out = refs[9 + n_cast:9 + 2 * n_cast]
    h_ref, t_ref = refs[9 + 2 * n_cast:]
    i = pl.program_id(0)
    j = pl.program_id(1)

    def cast_chunks():
        _cast_chunks(cast_in, cast_out)
    cur = (i + 1) % 2
    nxt = i % 2

    def norm_chunk():
        h = _rms(xr_ref[...]) * g_ref[...]
        h = h * (1.0 + mod_ref[4:5, :]) + mod_ref[3:4, :]
        r0 = pl.multiple_of((j - nf) * rc, rc)
        h_ref[nxt, pl.ds(r0, rc), :] = h.astype(BF16)

    row_blocks = [slice(m, m + FFN_ROWS) for m in range(0, y_ref.shape[0], FFN_ROWS)]

    def down_proj():
        for rows in row_blocks:
            d = _dot(t_ref[0, rows, :], wd_ref[0:tf, :])
            for k in range(1, nf):
                d = d + _dot(t_ref[k, rows, :], wd_ref[k * tf:(k + 1) * tf, :])
            y_ref[rows, :] = xc_ref[rows, :] + modc_ref[5:6, :] * d

    @pl.when((j < nf) & (i == 0))
    def _():
        cast_chunks()

    @pl.when((j < nf) & (i >= 1))
    def _():
        cast_chunks()
        for rows in row_blocks:
            hb = h_ref[cur, rows, :]
            a = _dot(hb, wg_ref[...])
            b = _dot(hb, wu_ref[...])
            t_ref[j, rows, :] = (a * jax.nn.sigmoid(a) * b).astype(BF16)

    @pl.when((j >= nf) & (i >= 1) & (i < n_tiles))
    def _():
        cast_chunks()
        norm_chunk()
        down_proj()

    @pl.when((j >= nf) & (i == 0))
    def _():
        cast_chunks()
        norm_chunk()

    @pl.when((j >= nf) & (i == n_tiles))
    def _():
        cast_chunks()
        down_proj()


def _ffn(x, layer, mods, mod_map, g, wg, wu, wd, cast=(), tm=FFN_TM, tf=FFN_TF, tn=FFN_TN):
    t = x.shape[0]
    n_tiles = t // tm
    nf = D_FF // tf
    nn = D_MODEL // tn
    cast_specs_in, cast_specs_out, cast_shapes, cast_srcs = _side_cast(
        cast, (n_tiles + 1) * (nf + nn), lambda i, j: i * (nf + nn) + j)
    rc = tm // nn
    col = lambda j: jnp.maximum(j - nf, 0)
    comp = lambda i: jnp.maximum(i - 1, 0)
    norm = lambda i: jnp.minimum(i, n_tiles - 1)
    ccol = lambda i, j: jnp.where(i == 0, 0, col(j))
    ftile = lambda i, j: jnp.where(i == 0, 0, jnp.minimum(j, nf - 1))
    outs = pl.pallas_call(
        functools.partial(_ffn_kernel, nf=nf, tf=tf, rc=rc, n_tiles=n_tiles, n_cast=len(cast)),
        grid=(n_tiles + 1, nf + nn),
        in_specs=[
            pl.BlockSpec((rc, D_MODEL), lambda i, j: (norm(i) * nn + col(j), 0)),
            pl.BlockSpec((tm, tn), lambda i, j: (comp(i), ccol(i, j))),
            pl.BlockSpec((None, None, N_MOD, D_MODEL),
                         lambda i, j: (layer, mod_map(norm(i)), 0, 0)),
            pl.BlockSpec((None, None, N_MOD, tn),
                         lambda i, j: (layer, mod_map(comp(i)), 0, ccol(i, j))),
            _layer_spec((1, D_MODEL), layer),
            pl.BlockSpec((D_MODEL, tf), lambda i, j: (0, ftile(i, j))),
            pl.BlockSpec((D_MODEL, tf), lambda i, j: (0, ftile(i, j))),
            pl.BlockSpec((D_FF, tn), lambda i, j: (0, ccol(i, j))),
        ] + cast_specs_in,
        out_specs=[pl.BlockSpec((tm, tn), lambda i, j: (comp(i), ccol(i, j)))] + cast_specs_out,
        out_shape=[jax.ShapeDtypeStruct((t, D_MODEL), F32)] + cast_shapes,
        scratch_shapes=[pltpu.VMEM((2, tm, D_MODEL), BF16), pltpu.VMEM((nf, tm, tf), BF16)],
        compiler_params=_cparams(2),
        name="ffn",
    )(x, x, mods, mods, g, wg, wu, wd, *cast_srcs)
    return outs[0], outs[1:]


NORM_TM = 512


def _final_norm_kernel(x_ref, g_ref, y_ref):
    y_ref[...] = _rms(x_ref[...]) * g_ref[...]


def _final_norm(x, g, tm=NORM_TM):
    t = x.shape[0]
    return pl.pallas_call(
        _final_norm_kernel,
        grid=(t // tm,),
        in_specs=[pl.BlockSpec((tm, D_MODEL), lambda i: (i, 0)),
                  pl.BlockSpec((1, D_MODEL), lambda i: (0, 0))],
        out_specs=pl.BlockSpec((tm, D_MODEL), lambda i: (i, 0)),
        out_shape=jax.ShapeDtypeStruct((t, D_MODEL), F32),
        compiler_params=_cparams(1),
        name="final_norm",
    )(x, g)


def _rope_tables(n_tokens):
    n_rows = n_tokens // GRID_W
    row = jnp.repeat(jnp.arange(n_rows, dtype=F32), GRID_W)
    col = jnp.tile(jnp.arange(GRID_W, dtype=F32), n_rows)

    def cs(dim):
        n_freq = dim // 4
        inv = ROPE_THETA ** (-jnp.arange(n_freq, dtype=F32) / n_freq)
        ang = jnp.concatenate([row[:, None] * inv, col[:, None] * inv], axis=-1)
        return jnp.cos(ang), jnp.sin(ang)

    cos_a, sin_a = cs(HEAD_DIM)
    cos_b, sin_b = cs(ROPE_B)
    one = jnp.ones_like(cos_b)
    zero = jnp.zeros_like(sin_b)
    return (jnp.concatenate([cos_a, cos_a], axis=-1),
            jnp.concatenate([-sin_a, sin_a], axis=-1),
            jnp.concatenate([cos_b, one, cos_b, one], axis=-1),
            jnp.concatenate([-sin_b, zero, sin_b, zero], axis=-1))


def kernel(x_prompt, x_sample, cache_k_a, cache_v_a, cache_ckv_b, cache_krope_b, c, c_ctx, w_ada, b_ada, norm_attn, norm_ffn, w_in, qnorm_a, knorm_a, kvnorm_b, w_uk_b, w_uv_b, w_o, w_gate, w_up, w_down, norm_final):
    batch, seq, _ = x_prompt.shape
    dec_batch, dec_seq, _ = x_sample.shape
    assert dec_batch + 1 <= MOD_ROWS

    w_in_b = _wprep(w_in.astype(BF16))
    wuk_b = w_uk_b.astype(BF16)
    wuv_b = w_uv_b.astype(BF16)
    wo_l = w_o[0].astype(BF16)
    rope_tabs = _rope_tables(dec_seq)

    cond = jnp.concatenate(
        [c_ctx[None, :], c, jnp.zeros((MOD_ROWS - 1 - dec_batch, D_MODEL), F32)], axis=0)
    mods = _adaln(cond, w_ada, b_ada).reshape(DEPTH, MOD_ROWS, N_MOD, D_MODEL)

    kc_a, vc_a, kc_b, vc_b = _cache_prep(
        cache_k_a, cache_v_a, cache_ckv_b, cache_krope_b, wuk_b, wuv_b)

    xp = x_prompt.reshape(batch * seq, D_MODEL)
    xs = x_sample.reshape(dec_batch * dec_seq, D_MODEL)
    g1 = norm_attn.reshape(DEPTH, 1, D_MODEL)
    g2 = norm_ffn.reshape(DEPTH, 1, D_MODEL)
    qn = qnorm_a.reshape(DEPTH, 1, HEAD_DIM)
    kn = knorm_a.reshape(DEPTH, 1, HEAD_DIM)
    kvn = kvnorm_b.reshape(DEPTH, 1, KV_RANK)
    ctx_mod = lambda i: 0

    def lat_mod(tm):
        return lambda i: 1 + i // (dec_seq // tm)

    r3 = lambda a: a.reshape(dec_batch, dec_seq, a.shape[-1])
    hb = ATTN_HEADS_PER_STEP
    new_bufs = None

    for l in range(DEPTH):
        first, nxt = l == 0, l + 1 < DEPTH

        (xp, *new_bufs), _ = _proj(
            xp, l, mods, ctx_mod, g1, w_in_b, qn, kn, kvn, wuk_b, wuv_b,
            new_shape=(batch, seq), new_bufs=new_bufs, w_o=wo_l)

        (qa, ka, va, qb, kb, vb), cast0 = _proj(
            xs, l, mods, lat_mod(PROJ_TM), g1, w_in_b, qn, kn, kvn, wuk_b, wuv_b,
            rope_tabs=rope_tabs, cast=((w_gate, 0),) if first else ())
        oa, cast1 = _attn_lat(r3(qa), l, kc_a, r3(ka), vc_a, r3(va),
                              1, N_HEADS_A, N_KV_A, HEAD_DIM, HEAD_DIM, "attn_lat_a", join_k=True,
                              cast=((w_up, 0), (w_down, 0)) if first else ())
        ob, _ = _attn_lat(r3(qb), l, kc_b, r3(kb), vc_b, r3(vb),
                          N_HEADS_B // hb, hb, hb, QB_PAD, VDIM_B, "attn_lat_b", join_k=False)
        xs = _outproj(xs, oa.reshape(-1, W_QA), ob.reshape(-1, W_VB), l, mods, lat_mod(OUT_TM), wo_l)
        if first:
            (wg_l,), (wu_l, wd_l) = cast0, cast1

        xp, gu_next = _ffn(xp, l, mods, ctx_mod, g2, wg_l, wu_l, wd_l,
                           cast=((w_gate, l + 1), (w_up, l + 1)) if nxt else ())
        xs, do_next = _ffn(xs, l, mods, lat_mod(FFN_TM), g2, wg_l, wu_l, wd_l,
                           cast=((w_down, l + 1), (w_o, l + 1)) if nxt else ())
        if nxt:
            (wg_l, wu_l), (wd_l, wo_l) = gu_next, do_next

    gf = norm_final.reshape(1, D_MODEL)
    y_prompt = _final_norm(xp, gf).reshape(batch, seq, D_MODEL)
    y_sample = _final_norm(xs, gf).reshape(dec_batch, dec_seq, D_MODEL)
    new_k, new_v, new_ckv, new_kr = new_bufs
    return (y_prompt, y_sample,
            new_k, new_v, new_ckv, new_kr)
```

```python
import functools

import jax
import jax.numpy as jnp
import numpy as np
from jax import lax
from jax.experimental import pallas as pl
from jax.experimental.pallas import tpu as pltpu

D_MODEL = 2048
DEPTH = 4
GRID_W = 64
HEAD_DIM = 128
N_HEADS_A = 8
N_KV_A = 2
GROUP_A = N_HEADS_A // N_KV_A
N_HEADS_B = 8
NOPE_B = 128
ROPE_B = 64
VDIM_B = 128
KV_RANK = 256
D_FF = 5632
ROPE_THETA = 10000.0
EPS = 1e-6

W_QA = N_HEADS_A * HEAD_DIM
W_KA = N_KV_A * HEAD_DIM
QB_DIM = NOPE_B + ROPE_B
W_QB = N_HEADS_B * QB_DIM
QB_PAD = 256
KR_PAD = 128
W_QB_PAD = N_HEADS_B * QB_PAD
W_KNOPE = N_HEADS_B * NOPE_B
W_VB = N_HEADS_B * VDIM_B
N_MOD = 6
MOD_ROWS = 8

O_QB = W_QA + 2 * W_KA
O_CKV = O_QB + W_QB
O_KR = O_CKV + KV_RANK
IN_COLS = O_KR + ROPE_B

C_QA = 0
C_KA = C_QA + W_QA
C_VA = C_KA + W_KA
C_QB = C_VA + W_KA
C_CKV = C_QB + W_QB_PAD
C_KR = C_CKV + KV_RANK
IN_COLS_PAD = C_KR + KR_PAD

LOG2_E = 1.4426950408889634
SCALE_A = HEAD_DIM ** -0.5 * LOG2_E
SCALE_B = QB_DIM ** -0.5 * LOG2_E

VMEM_LIMIT = 60 * 1024 * 1024

BF16 = jnp.bfloat16
F32 = jnp.float32


def _cparams(n_axes):
    return pltpu.CompilerParams(
        dimension_semantics=("arbitrary",) * n_axes, vmem_limit_bytes=VMEM_LIMIT)


def _layer_spec(shape, layer):
    nd = len(shape)
    return pl.BlockSpec((None,) + shape, lambda *_: (layer,) + (0,) * nd,
                        pipeline_mode=pl.Buffered(1))


def _mod_spec(layer, mod_map, width, col_map=None):
    if col_map is None:
        return pl.BlockSpec((None, None, N_MOD, width), lambda i, *_: (layer, mod_map(i), 0, 0))
    return pl.BlockSpec((None, None, N_MOD, width),
                        lambda i, j: (layer, mod_map(i), 0, col_map(j)))


def _dot(a, b):
    return jnp.dot(a, b, preferred_element_type=F32)


def _dot_t(a, b):
    return lax.dot_general(a, b, (((1,), (1,)), ((), ())), preferred_element_type=F32)


def _rms(x):
    return x * lax.rsqrt(jnp.mean(x * x, axis=-1, keepdims=True) + EPS)


def _cast_rows(n_rows, n_steps):
    rows = 16
    while n_rows % rows or n_rows // rows > n_steps:
        rows += 16
    return rows


def _side_cast(cast, n_steps, step_of):
    in_specs, out_specs, shapes = [], [], []
    for src, src_layer in cast:
        _, n_rows, n_cols = src.shape
        rows = _cast_rows(n_rows, n_steps)
        chunk = lambda *ids, last=n_rows // rows - 1: jnp.minimum(step_of(*ids), last)
        in_specs.append(pl.BlockSpec(
            (None, rows, n_cols), lambda *ids, c=chunk, l=src_layer: (l, c(*ids), 0)))
        out_specs.append(pl.BlockSpec((rows, n_cols), lambda *ids, c=chunk: (c(*ids), 0)))
        shapes.append(jax.ShapeDtypeStruct((n_rows, n_cols), BF16))
    return in_specs, out_specs, shapes, [src for src, _ in cast]


def _cast_chunks(cast_in, cast_out):
    for src, dst in zip(cast_in, cast_out):
        dst[...] = src[...].astype(BF16)


ADA_TN = 1024


def _adaln_kernel(s_ref, w_ref, b_ref, o_ref):
    s = s_ref[...]
    s = s * jax.nn.sigmoid(s)
    o_ref[...] = _dot(s.astype(BF16), w_ref[...].astype(BF16)) + b_ref[...]


def _adaln(s, w_ada, b_ada):
    n = N_MOD * D_MODEL
    return pl.pallas_call(
        _adaln_kernel,
        grid=(DEPTH, n // ADA_TN),
        in_specs=[
            pl.BlockSpec((MOD_ROWS, D_MODEL), lambda l, j: (0, 0)),
            pl.BlockSpec((None, D_MODEL, ADA_TN), lambda l, j: (l, 0, j)),
            pl.BlockSpec((None, 1, ADA_TN), lambda l, j: (l, 0, j)),
        ],
        out_specs=pl.BlockSpec((None, MOD_ROWS, ADA_TN), lambda l, j: (l, 0, j)),
        out_shape=jax.ShapeDtypeStruct((DEPTH, MOD_ROWS, n), F32),
        compiler_params=_cparams(2),
        name="adaln",
    )(s, w_ada, b_ada.reshape(DEPTH, 1, n))


WPREP_TR = 512
PAIR_IN = 2 * QB_DIM
PAIR_OUT = 2 * QB_PAD


def _wprep_kernel(w_ref, pq_ref, pk_ref, o_ref):
    o_ref[:, 0:O_QB] = w_ref[:, 0:O_QB]
    pq = pq_ref[...]
    for i in range(N_HEADS_B // 2):
        src = w_ref[:, O_QB + i * PAIR_IN:O_QB + (i + 1) * PAIR_IN]
        o_ref[:, C_QB + i * PAIR_OUT:C_QB + (i + 1) * PAIR_OUT] = _dot(src, pq).astype(BF16)
    o_ref[:, C_CKV:C_CKV + KV_RANK] = w_ref[:, O_CKV:O_CKV + KV_RANK]
    o_ref[:, C_KR:C_KR + KR_PAD] = _dot(w_ref[:, O_KR:O_KR + ROPE_B], pk_ref[...]).astype(BF16)


def _rope_pad_src():
    half = ROPE_B // 2
    src = np.full((KR_PAD,), -1, np.int32)
    src[0:half] = np.arange(half)
    src[2 * half:3 * half] = half + np.arange(half)
    return src


def _selection(src, n_in):
    return (jnp.arange(n_in, dtype=jnp.int32)[:, None] == jnp.asarray(src)[None, :]).astype(BF16)


def _wprep(w_in):
    rsrc = _rope_pad_src()
    head = np.concatenate([np.arange(NOPE_B), np.where(rsrc >= 0, NOPE_B + rsrc, -1)])
    pair = np.concatenate([head, np.where(head >= 0, QB_DIM + head, -1)]).astype(np.int32)
    pq = _selection(pair, PAIR_IN)
    pk = _selection(rsrc, ROPE_B)
    return pl.pallas_call(
        _wprep_kernel,
        grid=(DEPTH, D_MODEL // WPREP_TR),
        in_specs=[
            pl.BlockSpec((None, WPREP_TR, IN_COLS), lambda l, r: (l, r, 0)),
            pl.BlockSpec((PAIR_IN, PAIR_OUT), lambda l, r: (0, 0)),
            pl.BlockSpec((ROPE_B, KR_PAD), lambda l, r: (0, 0)),
        ],
        out_specs=pl.BlockSpec((None, WPREP_TR, IN_COLS_PAD), lambda l, r: (l, r, 0)),
        out_shape=jax.ShapeDtypeStruct((DEPTH, D_MODEL, IN_COLS_PAD), BF16),
        compiler_params=_cparams(2),
        name="wprep",
    )(w_in, pq, pk)


def _cache_kernel(ka_ref, va_ref, ckv_ref, kr_ref, wuk_ref, wuv_ref, pk_ref,
                  ka_o, va_o, kb_o, vb_o):
    ka_o[...] = ka_ref[...].astype(BF16)
    va_o[...] = va_ref[...].astype(BF16)
    ckv = ckv_ref[...].astype(BF16)
    kn = _dot(ckv, wuk_ref[...])
    vb_o[...] = _dot(ckv, wuv_ref[...]).astype(BF16)
    kr = _dot(kr_ref[...].astype(BF16), pk_ref[...]).astype(BF16)
    for h in range(N_HEADS_B):
        kb_o[:, h * QB_PAD:h * QB_PAD + NOPE_B] = kn[:, h * NOPE_B:(h + 1) * NOPE_B].astype(BF16)
        kb_o[:, h * QB_PAD + NOPE_B:(h + 1) * QB_PAD] = kr


def _cache_prep(cache_k_a, cache_v_a, cache_ckv_b, cache_krope_b, wuk, wuv):
    nb, _, past = cache_ckv_b.shape[:3]
    ka = cache_k_a.reshape(nb, DEPTH, past, W_KA)
    va = cache_v_a.reshape(nb, DEPTH, past, W_KA)
    pk = _selection(_rope_pad_src(), ROPE_B)

    def in4(width):
        return pl.BlockSpec((None, None, past, width), lambda l, b: (b, l, 0, 0))

    def out4(width):
        return pl.BlockSpec((None, None, past, width), lambda l, b: (l, b, 0, 0))

    def w3(width):
        return pl.BlockSpec((None, KV_RANK, width), lambda l, b: (l, 0, 0))

    return pl.pallas_call(
        _cache_kernel,
        grid=(DEPTH, nb),
        in_specs=[in4(W_KA), in4(W_KA), in4(KV_RANK), in4(ROPE_B), w3(W_KNOPE), w3(W_VB),
                  pl.BlockSpec((ROPE_B, KR_PAD), lambda l, b: (0, 0))],
        out_specs=[out4(W_KA), out4(W_KA), out4(W_QB_PAD), out4(W_VB)],
        out_shape=[
            jax.ShapeDtypeStruct((DEPTH, nb, past, W_KA), BF16),
            jax.ShapeDtypeStruct((DEPTH, nb, past, W_KA), BF16),
            jax.ShapeDtypeStruct((DEPTH, nb, past, W_QB_PAD), BF16),
            jax.ShapeDtypeStruct((DEPTH, nb, past, W_VB), BF16),
        ],
        compiler_params=_cparams(2),
        name="cache_prep",
    )(ka, va, cache_ckv_b, cache_krope_b, wuk, wuv, pk)


PROJ_TM = 512
PROJ_ROWS = 256
N_NEW = 4


def _proj_kernel(*refs, rope, emit_new, n_alias, n_cast):
    it = iter(refs)
    x_ref, mod_ref, g_ref, w_ref, qn_ref, kn_ref, kvn_ref, wuk_ref, wuv_ref = (
        next(it) for _ in range(9))
    if rope:
        ca_ref, sa_ref, cb_ref, sb_ref = (next(it) for _ in range(4))
    if emit_new:
        wo_ref = next(it)
    for _ in range(n_alias):
        next(it)
    cast_in = [next(it) for _ in range(n_cast)]
    if emit_new:
        xo_o, nk_o, nv_o, nckv_o, nkr_o = (next(it) for _ in range(1 + N_NEW))
        cast_out = [next(it) for _ in range(n_cast)]
    qa_o, ka_o, va_o, qb_o, kb_o, vb_o = (next(it) for _ in range(6))
    if emit_new:
        vae_s, vbe_s, oa_o, ob_o = (next(it) for _ in range(4))
    else:
        cast_out = [next(it) for _ in range(n_cast)]
    _cast_chunks(cast_in, cast_out)

    for r in range(x_ref.shape[0] // PROJ_ROWS):
        rows = slice(r * PROJ_ROWS, (r + 1) * PROJ_ROWS)

        h = _rms(x_ref[rows, :]) * g_ref[...]
        h = h * (1.0 + mod_ref[1:2, :]) + mod_ref[0:1, :]
        hb = h.astype(BF16)

        def rope_a(t):
            if not rope:
                return t
            return t * ca_ref[rows, :] + pltpu.roll(t, HEAD_DIM // 2, 1) * sa_ref[rows, :]

        def rope_b(t):
            if not rope:
                return t
            return t * cb_ref[rows, :] + pltpu.roll(t, KR_PAD // 2, 1) * sb_ref[rows, :]

        p = _dot(hb, w_ref[...])

        pq = p[:, C_QA:C_QA + W_QA]
        qn = qn_ref[...]
        for i in range(N_HEADS_A):
            sl = slice(i * HEAD_DIM, (i + 1) * HEAD_DIM)
            t = rope_a(_rms(pq[:, sl]) * qn)
            qa_o[rows, sl] = (t * SCALE_A).astype(BF16)

        pk = p[:, C_KA:C_KA + W_KA]
        pv = p[:, C_VA:C_VA + W_KA]
        kn = kn_ref[...]
        for i in range(N_KV_A):
            sl = slice(i * HEAD_DIM, (i + 1) * HEAD_DIM)
            t = _rms(pk[:, sl]) * kn
            if emit_new:
                nk_o[r, :, i, :] = t
            ka_o[rows, sl] = rope_a(t).astype(BF16)
        va_o[rows, :] = pv.astype(BF16)
        if emit_new:
            for i in range(N_KV_A):
                nv_o[r, :, i, :] = pv[:, i * HEAD_DIM:(i + 1) * HEAD_DIM]

        pqb = p[:, C_QB:C_QB + W_QB_PAD]
        for i in range(N_HEADS_B):
            s0 = i * QB_PAD
            qb_o[rows, s0:s0 + NOPE_B] = (pqb[:, s0:s0 + NOPE_B] * SCALE_B).astype(BF16)
            t = rope_b(pqb[:, s0 + NOPE_B:s0 + QB_PAD])
            qb_o[rows, s0 + NOPE_B:s0 + QB_PAD] = (t * SCALE_B).astype(BF16)

        pc = p[:, C_CKV:C_CKV + KV_RANK]
        ckv = _rms(pc) * kvn_ref[...]
        pkr = p[:, C_KR:C_KR + KR_PAD]
        if emit_new:
            half = ROPE_B // 2
            nckv_o[r] = ckv
            nkr_o[r] = jnp.concatenate([pkr[:, 0:half], pkr[:, 2 * half:3 * half]], axis=-1)
        ckvb = ckv.astype(BF16)
        krb = rope_b(pkr).astype(BF16)
        kno = _dot(ckvb, wuk_ref[...])
        vb_o[rows, :] = _dot(ckvb, wuv_ref[...]).astype(BF16)
        for i in range(N_HEADS_B):
            s0 = i * QB_PAD
            kb_o[rows, s0:s0 + NOPE_B] = kno[:, i * NOPE_B:(i + 1) * NOPE_B].astype(BF16)
            kb_o[rows, s0 + NOPE_B:s0 + QB_PAD] = krb

        if emit_new:
            _fill_v_ext(vae_s, rows, va_o.at[rows, :], N_KV_A, HEAD_DIM)
            _fill_v_ext(vbe_s, rows, vb_o.at[rows, :], N_HEADS_B, VDIM_B)
            for hq in range(N_HEADS_A):
                g = hq // GROUP_A
                qsl = slice(hq * HEAD_DIM, (hq + 1) * HEAD_DIM)
                oa_o[rows, qsl] = _attend(
                    qa_o[rows, qsl], ka_o[rows, g * HEAD_DIM:(g + 1) * HEAD_DIM],
                    vae_s[rows, 2 * g * HEAD_DIM:(2 * g + 2) * HEAD_DIM], HEAD_DIM)
            for hq in range(N_HEADS_B):
                qsl = slice(hq * QB_PAD, (hq + 1) * QB_PAD)
                ob_o[rows, hq * VDIM_B:(hq + 1) * VDIM_B] = _attend(
                    qb_o[rows, qsl], kb_o[rows, qsl],
                    vbe_s[rows, 2 * hq * VDIM_B:(2 * hq + 2) * VDIM_B], VDIM_B)
            y = _dot(oa_o[rows, :], wo_ref[0:W_QA, :]) + _dot(ob_o[rows, :], wo_ref[W_QA:, :])
            xo_o[rows, :] = x_ref[rows, :] + mod_ref[2:3, :] * y


def _proj(x, layer, mods, mod_map, g, w_in, qn, kn, kvn, wuk, wuv, rope_tabs=None,
          new_shape=None, new_bufs=None, w_o=None, cast=(), tm=PROJ_TM):
    t = x.shape[0]
    rope = rope_tabs is not None
    emit_new = new_shape is not None

    def row(width):
        return pl.BlockSpec((tm, width), lambda i: (i, 0))

    in_specs = [
        row(D_MODEL),
        _mod_spec(layer, mod_map, D_MODEL),
        _layer_spec((1, D_MODEL), layer),
        _layer_spec((D_MODEL, IN_COLS_PAD), layer),
        _layer_spec((1, HEAD_DIM), layer),
        _layer_spec((1, HEAD_DIM), layer),
        _layer_spec((1, KV_RANK), layer),
        _layer_spec((KV_RANK, W_KNOPE), layer),
        _layer_spec((KV_RANK, W_VB), layer),
    ]
    args = [x, mods, g, w_in, qn, kn, kvn, wuk, wuv]
    if rope:
        seq_tiles = rope_tabs[0].shape[0] // tm
        for tab in rope_tabs:
            in_specs.append(pl.BlockSpec((tm, tab.shape[1]), lambda i: (i % seq_tiles, 0)))
            args.append(tab)
    qkv_widths = [W_QA, W_KA, W_KA, W_QB_PAD, W_QB_PAD, W_VB]
    scratch = []
    aliases = {}
    n_alias = 0
    if not emit_new:
        out_specs = [row(w) for w in qkv_widths]
        out_shape = [jax.ShapeDtypeStruct((t, w), BF16) for w in qkv_widths]
    else:
        in_specs.append(pl.BlockSpec((W_QA + W_VB, D_MODEL), lambda i: (0, 0),
                                     pipeline_mode=pl.Buffered(1)))
        args.append(w_o)
        out_specs = [row(D_MODEL)]
        out_shape = [jax.ShapeDtypeStruct((t, D_MODEL), F32)]
        scratch = [pltpu.VMEM((tm, w), BF16)
                   for w in qkv_widths + [2 * W_KA, 2 * W_VB, W_QA, W_VB]]
        batch, seq = new_shape
        assert seq == PROJ_ROWS and t == batch * seq
        nb = tm // seq
        tails = [(N_KV_A, HEAD_DIM), (N_KV_A, HEAD_DIM), (KV_RANK,), (ROPE_B,)]
        if new_bufs is not None:
            n_alias = N_NEW
            for k, buf in enumerate(new_bufs):
                aliases[len(args)] = len(out_shape) + k
                in_specs.append(pl.BlockSpec(memory_space=pl.ANY))
                args.append(buf)
        for tail in tails:
            zeros = (0,) * len(tail)
            out_specs.append(pl.BlockSpec((nb, None, seq) + tail,
                                          lambda i, zeros=zeros: (i, layer, 0) + zeros))
            out_shape.append(jax.ShapeDtypeStruct((batch, DEPTH, seq) + tail, F32))
    cast_specs_in, cast_specs_out, cast_shapes, cast_srcs = _side_cast(
        cast, t // tm, lambda i: i)
    outs = pl.pallas_call(
        functools.partial(_proj_kernel, rope=rope, emit_new=emit_new, n_alias=n_alias,
                          n_cast=len(cast)),
        grid=(t // tm,),
        in_specs=in_specs + cast_specs_in,
        out_specs=out_specs + cast_specs_out,
        out_shape=out_shape + cast_shapes,
        input_output_aliases=aliases,
        scratch_shapes=scratch,
        compiler_params=_cparams(1),
        name="proj_lat" if rope else "proj_attn_ctx",
    )(*args, *cast_srcs)
    n_main = len(out_shape)
    return outs[:n_main], outs[n_main:]


def _attend(q, k, v_ext, dv):
    s = _dot_t(q, k)
    p = jnp.exp2(s - s.max(axis=-1, keepdims=True))
    o = _dot(p.astype(BF16), v_ext)
    return (o[:, 0:dv] / o[:, dv:2 * dv]).astype(BF16)


def _attend_parts(q, k_parts, v_ext_ref, dv):
    s = [_dot_t(q, k) for k in k_parts]
    m = s[0].max(axis=-1, keepdims=True)
    for sp in s[1:]:
        m = jnp.maximum(m, sp.max(axis=-1, keepdims=True))
    o = None
    r0 = 0
    for sp in s:
        r1 = r0 + sp.shape[1]
        op = _dot(jnp.exp2(sp - m).astype(BF16), v_ext_ref[r0:r1, :])
        o = op if o is None else o + op
        r0 = r1
    return (o[:, 0:dv] / o[:, dv:2 * dv]).astype(BF16)


def _fill_v_ext(v_ext_ref, rows, v_ref, n_kv, dv):
    ones = jnp.ones((v_ref.shape[0], dv), BF16)
    for jk in range(n_kv):
        v_ext_ref[rows, 2 * jk * dv:(2 * jk + 1) * dv] = v_ref[:, jk * dv:(jk + 1) * dv]
        v_ext_ref[rows, (2 * jk + 1) * dv:(2 * jk + 2) * dv] = ones


ATTN_TQ_A = 1024
ATTN_TQ_B = 512
ATTN_HEADS_PER_STEP = 8
ATTN_ROWS = 256


def _attn_lat_kernel(*refs, n_q, n_kv, dq, dv, past, n_cast):
    q_ref, kc_ref, kn_ref, vc_ref, vn_ref = refs[:5]
    o_ref = refs[5 + n_cast]
    v_ref = refs[6 + 2 * n_cast]
    k_scratch = refs[7 + 2 * n_cast:]
    _attn_lat_body(q_ref, kc_ref, kn_ref, vc_ref, vn_ref, o_ref, v_ref, k_scratch,
                   n_q=n_q, n_kv=n_kv, dq=dq, dv=dv, past=past)
    _cast_chunks(refs[5:5 + n_cast], refs[6 + n_cast:6 + 2 * n_cast])


def _attn_lat_body(q_ref, kc_ref, kn_ref, vc_ref, vn_ref, o_ref, v_ref, k_scratch,
                   *, n_q, n_kv, dq, dv, past):
    @pl.when(pl.program_id(2) == 0)
    def _():
        _fill_v_ext(v_ref, slice(0, past), vc_ref, n_kv, dv)
        _fill_v_ext(v_ref, slice(past, None), vn_ref, n_kv, dv)
        if k_scratch:
            k_scratch[0][0:past, :] = kc_ref[...]
            k_scratch[0][past:, :] = kn_ref[...]

    rows = ATTN_ROWS
    for j in range(n_q):
        jk = j * n_kv // n_q
        ksl = slice(jk * dq, (jk + 1) * dq)
        vsl = slice(2 * jk * dv, (2 * jk + 2) * dv)
        for r in range(q_ref.shape[0] // rows):
            rsl = slice(r * rows, (r + 1) * rows)
            q = q_ref[rsl, j * dq:(j + 1) * dq]
            if k_scratch:
                o = _attend(q, k_scratch[0][:, ksl], v_ref[:, vsl], dv)
            else:
                o = _attend_parts(q, [kc_ref[:, ksl], kn_ref[:, ksl]], v_ref.at[:, vsl], dv)
            o_ref[rsl, j * dv:(j + 1) * dv] = o


def _attn_lat(q, layer, kc, kn, vc, vn, n_units, n_q, n_kv, dq, dv, name, join_k, tq, cast=()):
    nb, seq, _ = q.shape
    past = kc.shape[2]
    scratch = [pltpu.VMEM((past + seq, n_kv * 2 * dv), BF16)]
    if join_k:
        scratch.append(pltpu.VMEM((past + seq, n_kv * dq), BF16))
    n_qt = seq // tq
    cast_specs_in, cast_specs_out, cast_shapes, cast_srcs = _side_cast(
        cast, nb * n_units * n_qt, lambda b, u, i: (b * n_units + u) * n_qt + i)
    outs = pl.pallas_call(
        functools.partial(_attn_lat_kernel, n_q=n_q, n_kv=n_kv, dq=dq, dv=dv, past=past,
                          n_cast=len(cast)),
        grid=(nb, n_units, n_qt),
        in_specs=[
            pl.BlockSpec((None, tq, n_q * dq), lambda b, u, i: (b, i, u)),
            pl.BlockSpec((None, None, past, n_kv * dq), lambda b, u, i: (layer, b, 0, u)),
            pl.BlockSpec((None, seq, n_kv * dq), lambda b, u, i: (b, 0, u)),
            pl.BlockSpec((None, None, past, n_kv * dv), lambda b, u, i: (layer, b, 0, u)),
            pl.BlockSpec((None, seq, n_kv * dv), lambda b, u, i: (b, 0, u)),
        ] + cast_specs_in,
        out_specs=[pl.BlockSpec((None, tq, n_q * dv), lambda b, u, i: (b, i, u))] + cast_specs_out,
        out_shape=[jax.ShapeDtypeStruct((nb, seq, n_units * n_q * dv), BF16)] + cast_shapes,
        scratch_shapes=scratch,
        compiler_params=_cparams(3),
        name=name,
    )(q, kc, kn, vc, vn, *cast_srcs)
    return outs[0], outs[1:]


OUT_TM = 512


def _outproj_kernel(x_ref, oa_ref, ob_ref, mod_ref, w_ref, y_ref):
    y = _dot(oa_ref[...], w_ref[0:W_QA, :]) + _dot(ob_ref[...], w_ref[W_QA:, :])
    y_ref[...] = x_ref[...] + mod_ref[2:3, :] * y


def _outproj(x, oa, ob, layer, mods, mod_map, w_o, tm=OUT_TM):
    t = x.shape[0]
    return pl.pallas_call(
        _outproj_kernel,
        grid=(t // tm,),
        in_specs=[
            pl.BlockSpec((tm, D_MODEL), lambda i: (i, 0)),
            pl.BlockSpec((tm, W_QA), lambda i: (i, 0)),
            pl.BlockSpec((tm, W_VB), lambda i: (i, 0)),
            _mod_spec(layer, mod_map, D_MODEL),
            pl.BlockSpec((W_QA + W_VB, D_MODEL), lambda i: (0, 0), pipeline_mode=pl.Buffered(1)),
        ],
        out_specs=pl.BlockSpec((tm, D_MODEL), lambda i: (i, 0)),
        out_shape=jax.ShapeDtypeStruct((t, D_MODEL), F32),
        compiler_params=_cparams(1),
        name="outproj",
    )(x, oa, ob, mods, w_o)


FFN_TM = 1024
FFN_TF = 512
FFN_TN = 512
FFN_ROWS = 256


def _ffn_kernel(*refs, nf, tf, rc, n_tiles, n_cast):
    xr_ref, xc_ref, mod_ref, modc_ref, g_ref, wg_ref, wu_ref, wd_ref = refs[:8]
    cast_in = refs[8:8 + n_cast]
    y_ref = refs[8 + n_cast]
    cast_out = refs[9 + n_cast:9 + 2 * n_cast]
    h_ref, t_ref = refs[9 + 2 * n_cast:]
    i = pl.program_id(0)
    j = pl.program_id(1)

    def cast_chunks():
        _cast_chunks(cast_in, cast_out)
    cur = (i + 1) % 2
    nxt = i % 2

    def norm_chunk():
        h = _rms(xr_ref[...]) * g_ref[...]
        h = h * (1.0 + mod_ref[4:5, :]) + mod_ref[3:4, :]
        r0 = pl.multiple_of((j - nf) * rc, rc)
        h_ref[nxt, pl.ds(r0, rc), :] = h.astype(BF16)

    row_blocks = [slice(m, m + FFN_ROWS) for m in range(0, y_ref.shape[0], FFN_ROWS)]

    def down_proj():
        for rows in row_blocks:
            d = _dot(t_ref[0, rows, :], wd_ref[0:tf, :])
            for k in range(1, nf):
                d = d + _dot(t_ref[k, rows, :], wd_ref[k * tf:(k + 1) * tf, :])
            y_ref[rows, :] = xc_ref[rows, :] + modc_ref[5:6, :] * d

    @pl.when((j < nf) & (i == 0))
    def _():
        cast_chunks()

    @pl.when((j < nf) & (i >= 1))
    def _():
        cast_chunks()
        for rows in row_blocks:
            hb = h_ref[cur, rows, :]
            a = _dot(hb, wg_ref[...])
            b = _dot(hb, wu_ref[...])
            t_ref[j, rows, :] = (a * jax.nn.sigmoid(a) * b).astype(BF16)

    @pl.when((j >= nf) & (i >= 1) & (i < n_tiles))
    def _():
        cast_chunks()
        norm_chunk()
        down_proj()

    @pl.when((j >= nf) & (i == 0))
    def _():
        cast_chunks()
        norm_chunk()

    @pl.when((j >= nf) & (i == n_tiles))
    def _():
        cast_chunks()
        down_proj()


def _ffn(x, layer, mods, mod_map, g, wg, wu, wd, cast=(), tm=FFN_TM, tf=FFN_TF, tn=FFN_TN):
    t = x.shape[0]
    n_tiles = t // tm
    nf = D_FF // tf
    nn = D_MODEL // tn
    cast_specs_in, cast_specs_out, cast_shapes, cast_srcs = _side_cast(
        cast, (n_tiles + 1) * (nf + nn), lambda i, j: i * (nf + nn) + j)
    rc = tm // nn
    col = lambda j: jnp.maximum(j - nf, 0)
    comp = lambda i: jnp.maximum(i - 1, 0)
    norm = lambda i: jnp.minimum(i, n_tiles - 1)
    ccol = lambda i, j: jnp.where(i == 0, 0, col(j))
    ftile = lambda i, j: jnp.where(i == 0, 0, jnp.minimum(j, nf - 1))
    outs = pl.pallas_call(
        functools.partial(_ffn_kernel, nf=nf, tf=tf, rc=rc, n_tiles=n_tiles, n_cast=len(cast)),
        grid=(n_tiles + 1, nf + nn),
        in_specs=[
            pl.BlockSpec((rc, D_MODEL), lambda i, j: (norm(i) * nn + col(j), 0)),
            pl.BlockSpec((tm, tn), lambda i, j: (comp(i), ccol(i, j))),
            pl.BlockSpec((None, None, N_MOD, D_MODEL),
                         lambda i, j: (layer, mod_map(norm(i)), 0, 0)),
            pl.BlockSpec((None, None, N_MOD, tn),
                         lambda i, j: (layer, mod_map(comp(i)), 0, ccol(i, j))),
            _layer_spec((1, D_MODEL), layer),
            pl.BlockSpec((D_MODEL, tf), lambda i, j: (0, ftile(i, j))),
            pl.BlockSpec((D_MODEL, tf), lambda i, j: (0, ftile(i, j))),
            pl.BlockSpec((D_FF, tn), lambda i, j: (0, ccol(i, j))),
        ] + cast_specs_in,
        out_specs=[pl.BlockSpec((tm, tn), lambda i, j: (comp(i), ccol(i, j)))] + cast_specs_out,
        out_shape=[jax.ShapeDtypeStruct((t, D_MODEL), F32)] + cast_shapes,
        scratch_shapes=[pltpu.VMEM((2, tm, D_MODEL), BF16), pltpu.VMEM((nf, tm, tf), BF16)],
        compiler_params=_cparams(2),
        name="ffn",
    )(x, x, mods, mods, g, wg, wu, wd, *cast_srcs)
    return outs[0], outs[1:]


NORM_TM = 512


def _final_norm_kernel(x_ref, g_ref, y_ref):
    y_ref[...] = _rms(x_ref[...]) * g_ref[...]


def _final_norm(x, g, tm=NORM_TM):
    t = x.shape[0]
    return pl.pallas_call(
        _final_norm_kernel,
        grid=(t // tm,),
        in_specs=[pl.BlockSpec((tm, D_MODEL), lambda i: (i, 0)),
                  pl.BlockSpec((1, D_MODEL), lambda i: (0, 0))],
        out_specs=pl.BlockSpec((tm, D_MODEL), lambda i: (i, 0)),
        out_shape=jax.ShapeDtypeStruct((t, D_MODEL), F32),
        compiler_params=_cparams(1),
        name="final_norm",
    )(x, g)


def _rope_tables(n_tokens):
    n_rows = n_tokens // GRID_W
    row = jnp.repeat(jnp.arange(n_rows, dtype=F32), GRID_W)
    col = jnp.tile(jnp.arange(GRID_W, dtype=F32), n_rows)

    def cs(dim):
        n_freq = dim // 4
        inv = ROPE_THETA ** (-jnp.arange(n_freq, dtype=F32) / n_freq)
        ang = jnp.concatenate([row[:, None] * inv, col[:, None] * inv], axis=-1)
        return jnp.cos(ang), jnp.sin(ang)

    cos_a, sin_a = cs(HEAD_DIM)
    cos_b, sin_b = cs(ROPE_B)
    one = jnp.ones_like(cos_b)
    zero = jnp.zeros_like(sin_b)
    return (jnp.concatenate([cos_a, cos_a], axis=-1),
            jnp.concatenate([-sin_a, sin_a], axis=-1),
            jnp.concatenate([cos_b, one, cos_b, one], axis=-1),
            jnp.concatenate([-sin_b, zero, sin_b, zero], axis=-1))


def kernel(x_prompt, x_sample, cache_k_a, cache_v_a, cache_ckv_b, cache_krope_b, c, c_ctx, w_ada, b_ada, norm_attn, norm_ffn, w_in, qnorm_a, knorm_a, kvnorm_b, w_uk_b, w_uv_b, w_o, w_gate, w_up, w_down, norm_final):
    batch, seq, _ = x_prompt.shape
    dec_batch, dec_seq, _ = x_sample.shape
    assert dec_batch + 1 <= MOD_ROWS

    w_in_b = _wprep(w_in.astype(BF16))
    wuk_b = w_uk_b.astype(BF16)
    wuv_b = w_uv_b.astype(BF16)
    wo_l = w_o[0].astype(BF16)
    rope_tabs = _rope_tables(dec_seq)

    cond = jnp.concatenate(
        [c_ctx[None, :], c, jnp.zeros((MOD_ROWS - 1 - dec_batch, D_MODEL), F32)], axis=0)
    mods = _adaln(cond, w_ada, b_ada).reshape(DEPTH, MOD_ROWS, N_MOD, D_MODEL)

    kc_a, vc_a, kc_b, vc_b = _cache_prep(
        cache_k_a, cache_v_a, cache_ckv_b, cache_krope_b, wuk_b, wuv_b)

    xp = x_prompt.reshape(batch * seq, D_MODEL)
    xs = x_sample.reshape(dec_batch * dec_seq, D_MODEL)
    g1 = norm_attn.reshape(DEPTH, 1, D_MODEL)
    g2 = norm_ffn.reshape(DEPTH, 1, D_MODEL)
    qn = qnorm_a.reshape(DEPTH, 1, HEAD_DIM)
    kn = knorm_a.reshape(DEPTH, 1, HEAD_DIM)
    kvn = kvnorm_b.reshape(DEPTH, 1, KV_RANK)
    ctx_mod = lambda i: 0

    def lat_mod(tm):
        return lambda i: 1 + i // (dec_seq // tm)

    r3 = lambda a: a.reshape(dec_batch, dec_seq, a.shape[-1])
    hb = ATTN_HEADS_PER_STEP
    new_bufs = None

    for l in range(DEPTH):
        first, nxt = l == 0, l + 1 < DEPTH

        (xp, *new_bufs), _ = _proj(
            xp, l, mods, ctx_mod, g1, w_in_b, qn, kn, kvn, wuk_b, wuv_b,
            new_shape=(batch, seq), new_bufs=new_bufs, w_o=wo_l)

        (qa, ka, va, qb, kb, vb), cast0 = _proj(
            xs, l, mods, lat_mod(PROJ_TM), g1, w_in_b, qn, kn, kvn, wuk_b, wuv_b,
            rope_tabs=rope_tabs, cast=((w_gate, 0),) if first else ())
        oa, cast1 = _attn_lat(r3(qa), l, kc_a, r3(ka), vc_a, r3(va),
                              1, N_HEADS_A, N_KV_A, HEAD_DIM, HEAD_DIM, "attn_lat_a", join_k=True,
                              tq=ATTN_TQ_A,
                              cast=((w_up, 0), (w_down, 0)) if first else ())
        ob, _ = _attn_lat(r3(qb), l, kc_b, r3(kb), vc_b, r3(vb),
                          N_HEADS_B // hb, hb, hb, QB_PAD, VDIM_B, "attn_lat_b", join_k=False,
                          tq=ATTN_TQ_B)
        xs = _outproj(xs, oa.reshape(-1, W_QA), ob.reshape(-1, W_VB), l, mods, lat_mod(OUT_TM), wo_l)
        if first:
            (wg_l,), (wu_l, wd_l) = cast0, cast1

        xp, gu_next = _ffn(xp, l, mods, ctx_mod, g2, wg_l, wu_l, wd_l,
                           cast=((w_gate, l + 1), (w_up, l + 1)) if nxt else ())
        xs, do_next = _ffn(xs, l, mods, lat_mod(FFN_TM), g2, wg_l, wu_l, wd_l,
                           cast=((w_down, l + 1), (w_o, l + 1)) if nxt else ())
        if nxt:
            (wg_l, wu_l), (wd_l, wo_l) = gu_next, do_next

    gf = norm_final.reshape(1, D_MODEL)
    y_prompt = _final_norm(xp, gf).reshape(batch, seq, D_MODEL)
    y_sample = _final_norm(xs, gf).reshape(dec_batch, dec_seq, D_MODEL)
    new_k, new_v, new_ckv, new_kr = new_bufs
    return (y_prompt, y_sample,
            new_k, new_v, new_ckv, new_kr)
```

```python
import functools

import jax
import jax.numpy as jnp
import numpy as np
from jax import lax
from jax.experimental import pallas as pl
from jax.experimental.pallas import tpu as pltpu

D_MODEL = 2048
DEPTH = 4
GRID_W = 64
HEAD_DIM = 128
N_HEADS_A = 8
N_KV_A = 2
GROUP_A = N_HEADS_A // N_KV_A
N_HEADS_B = 8
NOPE_B = 128
ROPE_B = 64
VDIM_B = 128
KV_RANK = 256
D_FF = 5632
ROPE_THETA = 10000.0
EPS = 1e-6

W_QA = N_HEADS_A * HEAD_DIM
W_KA = N_KV_A * HEAD_DIM
QB_DIM = NOPE_B + ROPE_B
W_QB = N_HEADS_B * QB_DIM
QB_PAD = 256
KR_PAD = 128
W_QB_PAD = N_HEADS_B * QB_PAD
W_KNOPE = N_HEADS_B * NOPE_B
W_VB = N_HEADS_B * VDIM_B
N_MOD = 6
MOD_ROWS = 8

O_QB = W_QA + 2 * W_KA
O_CKV = O_QB + W_QB
O_KR = O_CKV + KV_RANK
IN_COLS = O_KR + ROPE_B
IN_COLS_128 = -(-IN_COLS // 128) * 128

C_QA = 0
C_KA = C_QA + W_QA
C_VA = C_KA + W_KA
C_QB = C_VA + W_KA
C_CKV = C_QB + W_QB_PAD
C_KR = C_CKV + KV_RANK
IN_COLS_PAD = C_KR + KR_PAD

LOG2_E = 1.4426950408889634
SCALE_A = HEAD_DIM ** -0.5 * LOG2_E
SCALE_B = QB_DIM ** -0.5 * LOG2_E

VMEM_LIMIT = 60 * 1024 * 1024

BF16 = jnp.bfloat16
F32 = jnp.float32


def _cparams(n_axes):
    return pltpu.CompilerParams(
        dimension_semantics=("arbitrary",) * n_axes, vmem_limit_bytes=VMEM_LIMIT)


def _layer_spec(shape, layer):
    nd = len(shape)
    return pl.BlockSpec((None,) + shape, lambda *_: (layer,) + (0,) * nd,
                        pipeline_mode=pl.Buffered(1))


def _mod_spec(layer, mod_map, width, col_map=None):
    if col_map is None:
        return pl.BlockSpec((None, None, N_MOD, width), lambda i, *_: (layer, mod_map(i), 0, 0))
    return pl.BlockSpec((None, None, N_MOD, width),
                        lambda i, j: (layer, mod_map(i), 0, col_map(j)))


def _dot(a, b):
    return jnp.dot(a, b, preferred_element_type=F32)


def _dot_t(a, b):
    return lax.dot_general(a, b, (((1,), (1,)), ((), ())), preferred_element_type=F32)


def _rms(x):
    return x * lax.rsqrt(jnp.mean(x * x, axis=-1, keepdims=True) + EPS)


def _cast_rows(n_rows, n_steps):
    rows = 16
    while n_rows % rows or n_rows // rows > n_steps:
        rows += 16
    return rows


def _side_cast(cast, n_steps, step_of):
    in_specs, out_specs, shapes = [], [], []
    for src, src_layer in cast:
        _, n_rows, n_cols = src.shape
        rows = _cast_rows(n_rows, n_steps)
        chunk = lambda *ids, last=n_rows // rows - 1: jnp.minimum(step_of(*ids), last)
        in_specs.append(pl.BlockSpec(
            (None, rows, n_cols), lambda *ids, c=chunk, l=src_layer: (l, c(*ids), 0)))
        out_specs.append(pl.BlockSpec((rows, n_cols), lambda *ids, c=chunk: (c(*ids), 0)))
        shapes.append(jax.ShapeDtypeStruct((n_rows, n_cols), BF16))
    return in_specs, out_specs, shapes, [src for src, _ in cast]


def _cast_chunks(cast_in, cast_out):
    for src, dst in zip(cast_in, cast_out):
        dst[...] = src[...].astype(BF16)


ADA_TN = 1024


def _adaln_kernel(s_ref, w_ref, b_ref, o_ref):
    s = s_ref[...]
    s = s * jax.nn.sigmoid(s)
    o_ref[...] = _dot(s.astype(BF16), w_ref[...].astype(BF16)) + b_ref[...]


def _adaln(s, w_ada, b_ada):
    n = N_MOD * D_MODEL
    return pl.pallas_call(
        _adaln_kernel,
        grid=(DEPTH, n // ADA_TN),
        in_specs=[
            pl.BlockSpec((MOD_ROWS, D_MODEL), lambda l, j: (0, 0)),
            pl.BlockSpec((None, D_MODEL, ADA_TN), lambda l, j: (l, 0, j)),
            pl.BlockSpec((None, 1, ADA_TN), lambda l, j: (l, 0, j)),
        ],
        out_specs=pl.BlockSpec((None, MOD_ROWS, ADA_TN), lambda l, j: (l, 0, j)),
        out_shape=jax.ShapeDtypeStruct((DEPTH, MOD_ROWS, n), F32),
        compiler_params=_cparams(2),
        name="adaln",
    )(s, w_ada, b_ada.reshape(DEPTH, 1, n))


WPREP_TR = 512
PAIR_IN = 2 * QB_DIM
PAIR_OUT = 2 * QB_PAD


def _wprep_kernel(w_ref, pq_ref, pk_ref, o_ref):
    o_ref[:, 0:O_QB] = w_ref[:, 0:O_QB]
    pq = pq_ref[...]
    for i in range(N_HEADS_B // 2):
        src = w_ref[:, O_QB + i * PAIR_IN:O_QB + (i + 1) * PAIR_IN]
        o_ref[:, C_QB + i * PAIR_OUT:C_QB + (i + 1) * PAIR_OUT] = _dot(src, pq).astype(BF16)
    o_ref[:, C_CKV:C_CKV + KV_RANK] = w_ref[:, O_CKV:O_CKV + KV_RANK]
    o_ref[:, C_KR:C_KR + KR_PAD] = _dot(w_ref[:, O_KR:O_KR + ROPE_B], pk_ref[...]).astype(BF16)


def _rope_pad_src():
    half = ROPE_B // 2
    src = np.full((KR_PAD,), -1, np.int32)
    src[0:half] = np.arange(half)
    src[2 * half:3 * half] = half + np.arange(half)
    return src


def _selection(src, n_in):
    return (jnp.arange(n_in, dtype=jnp.int32)[:, None] == jnp.asarray(src)[None, :]).astype(BF16)


def _wprep(w_in):
    rsrc = _rope_pad_src()
    head = np.concatenate([np.arange(NOPE_B), np.where(rsrc >= 0, NOPE_B + rsrc, -1)])
    pair = np.concatenate([head, np.where(head >= 0, QB_DIM + head, -1)]).astype(np.int32)
    pq = _selection(pair, PAIR_IN)
    pk = _selection(rsrc, ROPE_B)
    return pl.pallas_call(
        _wprep_kernel,
        grid=(DEPTH, D_MODEL // WPREP_TR),
        in_specs=[
            pl.BlockSpec((None, WPREP_TR, IN_COLS_128), lambda l, r: (l, r, 0)),
            pl.BlockSpec((PAIR_IN, PAIR_OUT), lambda l, r: (0, 0)),
            pl.BlockSpec((ROPE_B, KR_PAD), lambda l, r: (0, 0)),
        ],
        out_specs=pl.BlockSpec((None, WPREP_TR, IN_COLS_PAD), lambda l, r: (l, r, 0)),
        out_shape=jax.ShapeDtypeStruct((DEPTH, D_MODEL, IN_COLS_PAD), BF16),
        compiler_params=_cparams(2),
        name="wprep",
    )(w_in, pq, pk)


def _cache_kernel(ka_ref, va_ref, ckv_ref, kr_ref, wuk_ref, wuv_ref, pk_ref,
                  ka_o, va_o, kb_o, vb_o):
    ka_o[...] = ka_ref[...].astype(BF16)
    va_o[...] = va_ref[...].astype(BF16)
    ckv = ckv_ref[...].astype(BF16)
    kn = _dot(ckv, wuk_ref[...])
    vb_o[...] = _dot(ckv, wuv_ref[...]).astype(BF16)
    kr = _dot(kr_ref[...].astype(BF16), pk_ref[...]).astype(BF16)
    for h in range(N_HEADS_B):
        kb_o[:, h * QB_PAD:h * QB_PAD + NOPE_B] = kn[:, h * NOPE_B:(h + 1) * NOPE_B].astype(BF16)
        kb_o[:, h * QB_PAD + NOPE_B:(h + 1) * QB_PAD] = kr


def _cache_prep(cache_k_a, cache_v_a, cache_ckv_b, cache_krope_b, wuk, wuv):
    nb, _, past = cache_ckv_b.shape[:3]
    ka = cache_k_a.reshape(nb, DEPTH, past, W_KA)
    va = cache_v_a.reshape(nb, DEPTH, past, W_KA)
    pk = _selection(_rope_pad_src(), ROPE_B)

    def in4(width):
        return pl.BlockSpec((None, None, past, width), lambda l, b: (b, l, 0, 0))

    def out4(width):
        return pl.BlockSpec((None, None, past, width), lambda l, b: (l, b, 0, 0))

    def w3(width):
        return pl.BlockSpec((None, KV_RANK, width), lambda l, b: (l, 0, 0))

    return pl.pallas_call(
        _cache_kernel,
        grid=(DEPTH, nb),
        in_specs=[in4(W_KA), in4(W_KA), in4(KV_RANK), in4(ROPE_B), w3(W_KNOPE), w3(W_VB),
                  pl.BlockSpec((ROPE_B, KR_PAD), lambda l, b: (0, 0))],
        out_specs=[out4(W_KA), out4(W_KA), out4(W_QB_PAD), out4(W_VB)],
        out_shape=[
            jax.ShapeDtypeStruct((DEPTH, nb, past, W_KA), BF16),
            jax.ShapeDtypeStruct((DEPTH, nb, past, W_KA), BF16),
            jax.ShapeDtypeStruct((DEPTH, nb, past, W_QB_PAD), BF16),
            jax.ShapeDtypeStruct((DEPTH, nb, past, W_VB), BF16),
        ],
        compiler_params=_cparams(2),
        name="cache_prep",
    )(ka, va, cache_ckv_b, cache_krope_b, wuk, wuv, pk)


PROJ_TM = 512
PROJ_ROWS = 256
N_NEW = 4


def _proj_kernel(*refs, rope, emit_new, n_alias, n_cast):
    it = iter(refs)
    x_ref, mod_ref, g_ref, w_ref, qn_ref, kn_ref, kvn_ref, wuk_ref, wuv_ref = (
        next(it) for _ in range(9))
    if rope:
        ca_ref, sa_ref, cb_ref, sb_ref = (next(it) for _ in range(4))
    if emit_new:
        wo_ref = next(it)
    for _ in range(n_alias):
        next(it)
    cast_in = [next(it) for _ in range(n_cast)]
    if emit_new:
        xo_o, nk_o, nv_o, nckv_o, nkr_o = (next(it) for _ in range(1 + N_NEW))
        cast_out = [next(it) for _ in range(n_cast)]
    qa_o, ka_o, va_o, qb_o, kb_o, vb_o = (next(it) for _ in range(6))
    if emit_new:
        vae_s, vbe_s, oa_o, ob_o = (next(it) for _ in range(4))
    else:
        cast_out = [next(it) for _ in range(n_cast)]
    _cast_chunks(cast_in, cast_out)

    for r in range(x_ref.shape[0] // PROJ_ROWS):
        rows = slice(r * PROJ_ROWS, (r + 1) * PROJ_ROWS)

        h = _rms(x_ref[rows, :]) * g_ref[...]
        h = h * (1.0 + mod_ref[1:2, :]) + mod_ref[0:1, :]
        hb = h.astype(BF16)

        def rope_a(t):
            if not rope:
                return t
            return t * ca_ref[rows, :] + pltpu.roll(t, HEAD_DIM // 2, 1) * sa_ref[rows, :]

        def rope_b(t):
            if not rope:
                return t
            return t * cb_ref[rows, :] + pltpu.roll(t, KR_PAD // 2, 1) * sb_ref[rows, :]

        p = _dot(hb, w_ref[...])

        pq = p[:, C_QA:C_QA + W_QA]
        qn = qn_ref[...]
        for i in range(N_HEADS_A):
            sl = slice(i * HEAD_DIM, (i + 1) * HEAD_DIM)
            t = rope_a(_rms(pq[:, sl]) * qn)
            qa_o[rows, sl] = (t * SCALE_A).astype(BF16)

        pk = p[:, C_KA:C_KA + W_KA]
        pv = p[:, C_VA:C_VA + W_KA]
        kn = kn_ref[...]
        for i in range(N_KV_A):
            sl = slice(i * HEAD_DIM, (i + 1) * HEAD_DIM)
            t = _rms(pk[:, sl]) * kn
            if emit_new:
                nk_o[r, :, i, :] = t
            ka_o[rows, sl] = rope_a(t).astype(BF16)
        va_o[rows, :] = pv.astype(BF16)
        if emit_new:
            for i in range(N_KV_A):
                nv_o[r, :, i, :] = pv[:, i * HEAD_DIM:(i + 1) * HEAD_DIM]

        pqb = p[:, C_QB:C_QB + W_QB_PAD]
        for i in range(N_HEADS_B):
            s0 = i * QB_PAD
            qb_o[rows, s0:s0 + NOPE_B] = (pqb[:, s0:s0 + NOPE_B] * SCALE_B).astype(BF16)
            t = rope_b(pqb[:, s0 + NOPE_B:s0 + QB_PAD])
            qb_o[rows, s0 + NOPE_B:s0 + QB_PAD] = (t * SCALE_B).astype(BF16)

        pc = p[:, C_CKV:C_CKV + KV_RANK]
        ckv = _rms(pc) * kvn_ref[...]
        pkr = p[:, C_KR:C_KR + KR_PAD]
        if emit_new:
            half = ROPE_B // 2
            nckv_o[r] = ckv
            nkr_o[r] = jnp.concatenate([pkr[:, 0:half], pkr[:, 2 * half:3 * half]], axis=-1)
        ckvb = ckv.astype(BF16)
        krb = rope_b(pkr).astype(BF16)
        kno = _dot(ckvb, wuk_ref[...])
        vb_o[rows, :] = _dot(ckvb, wuv_ref[...]).astype(BF16)
        for i in range(N_HEADS_B):
            s0 = i * QB_PAD
            kb_o[rows, s0:s0 + NOPE_B] = kno[:, i * NOPE_B:(i + 1) * NOPE_B].astype(BF16)
            kb_o[rows, s0 + NOPE_B:s0 + QB_PAD] = krb

        if emit_new:
            _fill_v_ext(vae_s, rows, va_o.at[rows, :], N_KV_A, HEAD_DIM)
            _fill_v_ext(vbe_s, rows, vb_o.at[rows, :], N_HEADS_B, VDIM_B)
            for hq in range(N_HEADS_A):
                g = hq // GROUP_A
                qsl = slice(hq * HEAD_DIM, (hq + 1) * HEAD_DIM)
                oa_o[rows, qsl] = _attend(
                    qa_o[rows, qsl], ka_o[rows, g * HEAD_DIM:(g + 1) * HEAD_DIM],
                    vae_s[rows, 2 * g * HEAD_DIM:(2 * g + 2) * HEAD_DIM], HEAD_DIM)
            for hq in range(N_HEADS_B):
                qsl = slice(hq * QB_PAD, (hq + 1) * QB_PAD)
                ob_o[rows, hq * VDIM_B:(hq + 1) * VDIM_B] = _attend(
                    qb_o[rows, qsl], kb_o[rows, qsl],
                    vbe_s[rows, 2 * hq * VDIM_B:(2 * hq + 2) * VDIM_B], VDIM_B)
            y = _dot(oa_o[rows, :], wo_ref[0:W_QA, :]) + _dot(ob_o[rows, :], wo_ref[W_QA:, :])
            xo_o[rows, :] = x_ref[rows, :] + mod_ref[2:3, :] * y


def _proj(x, layer, mods, mod_map, g, w_in, qn, kn, kvn, wuk, wuv, rope_tabs=None,
          new_shape=None, new_bufs=None, w_o=None, cast=(), tm=PROJ_TM):
    t = x.shape[0]
    rope = rope_tabs is not None
    emit_new = new_shape is not None

    def row(width):
        return pl.BlockSpec((tm, width), lambda i: (i, 0))

    in_specs = [
        row(D_MODEL),
        _mod_spec(layer, mod_map, D_MODEL),
        _layer_spec((1, D_MODEL), layer),
        _layer_spec((D_MODEL, IN_COLS_PAD), layer),
        _layer_spec((1, HEAD_DIM), layer),
        _layer_spec((1, HEAD_DIM), layer),
        _layer_spec((1, KV_RANK), layer),
        _layer_spec((KV_RANK, W_KNOPE), layer),
        _layer_spec((KV_RANK, W_VB), layer),
    ]
    args = [x, mods, g, w_in, qn, kn, kvn, wuk, wuv]
    if rope:
        seq_tiles = rope_tabs[0].shape[0] // tm
        for tab in rope_tabs:
            in_specs.append(pl.BlockSpec((tm, tab.shape[1]), lambda i: (i % seq_tiles, 0)))
            args.append(tab)
    qkv_widths = [W_QA, W_KA, W_KA, W_QB_PAD, W_QB_PAD, W_VB]
    scratch = []
    aliases = {}
    n_alias = 0
    if not emit_new:
        out_specs = [row(w) for w in qkv_widths]
        out_shape = [jax.ShapeDtypeStruct((t, w), BF16) for w in qkv_widths]
    else:
        in_specs.append(pl.BlockSpec((W_QA + W_VB, D_MODEL), lambda i: (0, 0),
                                     pipeline_mode=pl.Buffered(1)))
        args.append(w_o)
        out_specs = [row(D_MODEL)]
        out_shape = [jax.ShapeDtypeStruct((t, D_MODEL), F32)]
        scratch = [pltpu.VMEM((tm, w), BF16)
                   for w in qkv_widths + [2 * W_KA, 2 * W_VB, W_QA, W_VB]]
        batch, seq = new_shape
        assert seq == PROJ_ROWS and t == batch * seq
        nb = tm // seq
        tails = [(N_KV_A, HEAD_DIM), (N_KV_A, HEAD_DIM), (KV_RANK,), (ROPE_B,)]
        if new_bufs is not None:
            n_alias = N_NEW
            for k, buf in enumerate(new_bufs):
                aliases[len(args)] = len(out_shape) + k
                in_specs.append(pl.BlockSpec(memory_space=pl.ANY))
                args.append(buf)
        for tail in tails:
            zeros = (0,) * len(tail)
            out_specs.append(pl.BlockSpec((nb, None, seq) + tail,
                                          lambda i, zeros=zeros: (i, layer, 0) + zeros))
            out_shape.append(jax.ShapeDtypeStruct((batch, DEPTH, seq) + tail, F32))
    cast_specs_in, cast_specs_out, cast_shapes, cast_srcs = _side_cast(
        cast, t // tm, lambda i: i)
    outs = pl.pallas_call(
        functools.partial(_proj_kernel, rope=rope, emit_new=emit_new, n_alias=n_alias,
                          n_cast=len(cast)),
        grid=(t // tm,),
        in_specs=in_specs + cast_specs_in,
        out_specs=out_specs + cast_specs_out,
        out_shape=out_shape + cast_shapes,
        input_output_aliases=aliases,
        scratch_shapes=scratch,
        compiler_params=_cparams(1),
        name="proj_lat" if rope else "proj_attn_ctx",
    )(*args, *cast_srcs)
    n_main = len(out_shape)
    return outs[:n_main], outs[n_main:]


def _attend(q, k, v_ext, dv):
    s = _dot_t(q, k)
    p = jnp.exp2(s - s.max(axis=-1, keepdims=True))
    o = _dot(p.astype(BF16), v_ext)
    return (o[:, 0:dv] / o[:, dv:2 * dv]).astype(BF16)


def _attend_parts(q, k_parts, v_ext_ref, dv):
    s = [_dot_t(q, k) for k in k_parts]
    m = s[0].max(axis=-1, keepdims=True)
    for sp in s[1:]:
        m = jnp.maximum(m, sp.max(axis=-1, keepdims=True))
    o = None
    r0 = 0
    for sp in s:
        r1 = r0 + sp.shape[1]
        op = _dot(jnp.exp2(sp - m).astype(BF16), v_ext_ref[r0:r1, :])
        o = op if o is None else o + op
        r0 = r1
    return (o[:, 0:dv] / o[:, dv:2 * dv]).astype(BF16)


def _fill_v_ext(v_ext_ref, rows, v_ref, n_kv, dv):
    ones = jnp.ones((v_ref.shape[0], dv), BF16)
    for jk in range(n_kv):
        v_ext_ref[rows, 2 * jk * dv:(2 * jk + 1) * dv] = v_ref[:, jk * dv:(jk + 1) * dv]
        v_ext_ref[rows, (2 * jk + 1) * dv:(2 * jk + 2) * dv] = ones


ATTN_TQ_A = 1024
ATTN_TQ_B = 512
ATTN_HEADS_PER_STEP = 8
ATTN_ROWS = 256


def _attn_lat_kernel(*refs, n_q, n_kv, dq, dv, past, n_cast):
    q_ref, kc_ref, kn_ref, vc_ref, vn_ref = refs[:5]
    o_ref = refs[5 + n_cast]
    v_ref = refs[6 + 2 * n_cast]
    k_scratch = refs[7 + 2 * n_cast:]
    _attn_lat_body(q_ref, kc_ref, kn_ref, vc_ref, vn_ref, o_ref, v_ref, k_scratch,
                   n_q=n_q, n_kv=n_kv, dq=dq, dv=dv, past=past)
    _cast_chunks(refs[5:5 + n_cast], refs[6 + n_cast:6 + 2 * n_cast])


def _attn_lat_body(q_ref, kc_ref, kn_ref, vc_ref, vn_ref, o_ref, v_ref, k_scratch,
                   *, n_q, n_kv, dq, dv, past):
    @pl.when(pl.program_id(2) == 0)
    def _():
        _fill_v_ext(v_ref, slice(0, past), vc_ref, n_kv, dv)
        _fill_v_ext(v_ref, slice(past, None), vn_ref, n_kv, dv)
        if k_scratch:
            k_scratch[0][0:past, :] = kc_ref[...]
            k_scratch[0][past:, :] = kn_ref[...]

    rows = ATTN_ROWS
    for j in range(n_q):
        jk = j * n_kv // n_q
        ksl = slice(jk * dq, (jk + 1) * dq)
        vsl = slice(2 * jk * dv, (2 * jk + 2) * dv)
        for r in range(q_ref.shape[0] // rows):
            rsl = slice(r * rows, (r + 1) * rows)
            q = q_ref[rsl, j * dq:(j + 1) * dq]
            if k_scratch:
                o = _attend(q, k_scratch[0][:, ksl], v_ref[:, vsl], dv)
            else:
                o = _attend_parts(q, [kc_ref[:, ksl], kn_ref[:, ksl]], v_ref.at[:, vsl], dv)
            o_ref[rsl, j * dv:(j + 1) * dv] = o


def _attn_lat(q, layer, kc, kn, vc, vn, n_units, n_q, n_kv, dq, dv, name, join_k, tq, cast=()):
    nb, seq, _ = q.shape
    past = kc.shape[2]
    scratch = [pltpu.VMEM((past + seq, n_kv * 2 * dv), BF16)]
    if join_k:
        scratch.append(pltpu.VMEM((past + seq, n_kv * dq), BF16))
    n_qt = seq // tq
    cast_specs_in, cast_specs_out, cast_shapes, cast_srcs = _side_cast(
        cast, nb * n_units * n_qt, lambda b, u, i: (b * n_units + u) * n_qt + i)
    outs = pl.pallas_call(
        functools.partial(_attn_lat_kernel, n_q=n_q, n_kv=n_kv, dq=dq, dv=dv, past=past,
                          n_cast=len(cast)),
        grid=(nb, n_units, n_qt),
        in_specs=[
            pl.BlockSpec((None, tq, n_q * dq), lambda b, u, i: (b, i, u)),
            pl.BlockSpec((None, None, past, n_kv * dq), lambda b, u, i: (layer, b, 0, u)),
            pl.BlockSpec((None, seq, n_kv * dq), lambda b, u, i: (b, 0, u)),
            pl.BlockSpec((None, None, past, n_kv * dv), lambda b, u, i: (layer, b, 0, u)),
            pl.BlockSpec((None, seq, n_kv * dv), lambda b, u, i: (b, 0, u)),
        ] + cast_specs_in,
        out_specs=[pl.BlockSpec((None, tq, n_q * dv), lambda b, u, i: (b, i, u))] + cast_specs_out,
        out_shape=[jax.ShapeDtypeStruct((nb, seq, n_units * n_q * dv), BF16)] + cast_shapes,
        scratch_shapes=scratch,
        compiler_params=_cparams(3),
        name=name,
    )(q, kc, kn, vc, vn, *cast_srcs)
    return outs[0], outs[1:]


OUT_TM = 512


def _outproj_kernel(x_ref, oa_ref, ob_ref, mod_ref, w_ref, y_ref):
    y = _dot(oa_ref[...], w_ref[0:W_QA, :]) + _dot(ob_ref[...], w_ref[W_QA:, :])
    y_ref[...] = x_ref[...] + mod_ref[2:3, :] * y


def _outproj(x, oa, ob, layer, mods, mod_map, w_o, tm=OUT_TM):
    t = x.shape[0]
    return pl.pallas_call(
        _outproj_kernel,
        grid=(t // tm,),
        in_specs=[
            pl.BlockSpec((tm, D_MODEL), lambda i: (i, 0)),
            pl.BlockSpec((tm, W_QA), lambda i: (i, 0)),
            pl.BlockSpec((tm, W_VB), lambda i: (i, 0)),
            _mod_spec(layer, mod_map, D_MODEL),
            pl.BlockSpec((W_QA + W_VB, D_MODEL), lambda i: (0, 0), pipeline_mode=pl.Buffered(1)),
        ],
        out_specs=pl.BlockSpec((tm, D_MODEL), lambda i: (i, 0)),
        out_shape=jax.ShapeDtypeStruct((t, D_MODEL), F32),
        compiler_params=_cparams(1),
        name="outproj",
    )(x, oa, ob, mods, w_o)


FFN_TM = 1024
FFN_TF = 512
FFN_TN = 512
FFN_ROWS = 256


def _ffn_kernel(*refs, nf, tf, rc, n_tiles, n_cast):
    xr_ref, xc_ref, mod_ref, modc_ref, g_ref, wg_ref, wu_ref, wd_ref = refs[:8]
    cast_in = refs[8:8 + n_cast]
    y_ref = refs[8 + n_cast]
    cast_out = refs[9 + n_cast:9 + 2 * n_cast]
    h_ref, t_ref = refs[9 + 2 * n_cast:]
    i = pl.program_id(0)
    j = pl.program_id(1)

    def cast_chunks():
        _cast_chunks(cast_in, cast_out)
    cur = (i + 1) % 2
    nxt = i % 2

    def norm_chunk():
        h = _rms(xr_ref[...]) * g_ref[...]
        h = h * (1.0 + mod_ref[4:5, :]) + mod_ref[3:4, :]
        r0 = pl.multiple_of((j - nf) * rc, rc)
        h_ref[nxt, pl.ds(r0, rc), :] = h.astype(BF16)

    row_blocks = [slice(m, m + FFN_ROWS) for m in range(0, y_ref.shape[0], FFN_ROWS)]

    def down_proj():
        for rows in row_blocks:
            d = _dot(t_ref[0, rows, :], wd_ref[0:tf, :])
            for k in range(1, nf):
                d = d + _dot(t_ref[k, rows, :], wd_ref[k * tf:(k + 1) * tf, :])
            y_ref[rows, :] = xc_ref[rows, :] + modc_ref[5:6, :] * d

    @pl.when((j < nf) & (i == 0))
    def _():
        cast_chunks()

    @pl.when((j < nf) & (i >= 1))
    def _():
        cast_chunks()
        for rows in row_blocks:
            hb = h_ref[cur, rows, :]
            a = _dot(hb, wg_ref[...])
            b = _dot(hb, wu_ref[...])
            t_ref[j, rows, :] = (a * jax.nn.sigmoid(a) * b).astype(BF16)

    @pl.when((j >= nf) & (i >= 1) & (i < n_tiles))
    def _():
        cast_chunks()
        norm_chunk()
        down_proj()

    @pl.when((j >= nf) & (i == 0))
    def _():
        cast_chunks()
        norm_chunk()

    @pl.when((j >= nf) & (i == n_tiles))
    def _():
        cast_chunks()
        down_proj()


def _ffn(x, layer, mods, mod_map, g, wg, wu, wd, cast=(), tm=FFN_TM, tf=FFN_TF, tn=FFN_TN):
    t = x.shape[0]
    n_tiles = t // tm
    nf = D_FF // tf
    nn = D_MODEL // tn
    cast_specs_in, cast_specs_out, cast_shapes, cast_srcs = _side_cast(
        cast, (n_tiles + 1) * (nf + nn), lambda i, j: i * (nf + nn) + j)
    rc = tm // nn
    col = lambda j: jnp.maximum(j - nf, 0)
    comp = lambda i: jnp.maximum(i - 1, 0)
    norm = lambda i: jnp.minimum(i, n_tiles - 1)
    ccol = lambda i, j: jnp.where(i == 0, 0, col(j))
    ftile = lambda i, j: jnp.where(i == 0, 0, jnp.minimum(j, nf - 1))
    outs = pl.pallas_call(
        functools.partial(_ffn_kernel, nf=nf, tf=tf, rc=rc, n_tiles=n_tiles, n_cast=len(cast)),
        grid=(n_tiles + 1, nf + nn),
        in_specs=[
            pl.BlockSpec((rc, D_MODEL), lambda i, j: (norm(i) * nn + col(j), 0)),
            pl.BlockSpec((tm, tn), lambda i, j: (comp(i), ccol(i, j))),
            pl.BlockSpec((None, None, N_MOD, D_MODEL),
                         lambda i, j: (layer, mod_map(norm(i)), 0, 0)),
            pl.BlockSpec((None, None, N_MOD, tn),
                         lambda i, j: (layer, mod_map(comp(i)), 0, ccol(i, j))),
            _layer_spec((1, D_MODEL), layer),
            pl.BlockSpec((D_MODEL, tf), lambda i, j: (0, ftile(i, j))),
            pl.BlockSpec((D_MODEL, tf), lambda i, j: (0, ftile(i, j))),
            pl.BlockSpec((D_FF, tn), lambda i, j: (0, ccol(i, j))),
        ] + cast_specs_in,
        out_specs=[pl.BlockSpec((tm, tn), lambda i, j: (comp(i), ccol(i, j)))] + cast_specs_out,
        out_shape=[jax.ShapeDtypeStruct((t, D_MODEL), F32)] + cast_shapes,
        scratch_shapes=[pltpu.VMEM((2, tm, D_MODEL), BF16), pltpu.VMEM((nf, tm, tf), BF16)],
        compiler_params=_cparams(2),
        name="ffn",
    )(x, x, mods, mods, g, wg, wu, wd, *cast_srcs)
    return outs[0], outs[1:]


NORM_TM = 512


def _final_norm_kernel(x_ref, g_ref, y_ref):
    y_ref[...] = _rms(x_ref[...]) * g_ref[...]


def _final_norm(x, g, tm=NORM_TM):
    t = x.shape[0]
    return pl.pallas_call(
        _final_norm_kernel,
        grid=(t // tm,),
        in_specs=[pl.BlockSpec((tm, D_MODEL), lambda i: (i, 0)),
                  pl.BlockSpec((1, D_MODEL), lambda i: (0, 0))],
        out_specs=pl.BlockSpec((tm, D_MODEL), lambda i: (i, 0)),
        out_shape=jax.ShapeDtypeStruct((t, D_MODEL), F32),
        compiler_params=_cparams(1),
        name="final_norm",
    )(x, g)


def _rope_tables(n_tokens):
    n_rows = n_tokens // GRID_W
    row = jnp.repeat(jnp.arange(n_rows, dtype=F32), GRID_W)
    col = jnp.tile(jnp.arange(GRID_W, dtype=F32), n_rows)

    def cs(dim):
        n_freq = dim // 4
        inv = ROPE_THETA ** (-jnp.arange(n_freq, dtype=F32) / n_freq)
        ang = jnp.concatenate([row[:, None] * inv, col[:, None] * inv], axis=-1)
        return jnp.cos(ang), jnp.sin(ang)

    cos_a, sin_a = cs(HEAD_DIM)
    cos_b, sin_b = cs(ROPE_B)
    one = jnp.ones_like(cos_b)
    zero = jnp.zeros_like(sin_b)
    return (jnp.concatenate([cos_a, cos_a], axis=-1),
            jnp.concatenate([-sin_a, sin_a], axis=-1),
            jnp.concatenate([cos_b, one, cos_b, one], axis=-1),
            jnp.concatenate([-sin_b, zero, sin_b, zero], axis=-1))


def kernel(x_prompt, x_sample, cache_k_a, cache_v_a, cache_ckv_b, cache_krope_b, c, c_ctx, w_ada, b_ada, norm_attn, norm_ffn, w_in, qnorm_a, knorm_a, kvnorm_b, w_uk_b, w_uv_b, w_o, w_gate, w_up, w_down, norm_final):
    batch, seq, _ = x_prompt.shape
    dec_batch, dec_seq, _ = x_sample.shape
    assert dec_batch + 1 <= MOD_ROWS

    w_in_b = _wprep(jnp.pad(w_in.astype(BF16), ((0, 0), (0, 0), (0, IN_COLS_128 - IN_COLS))))
    wuk_b = w_uk_b.astype(BF16)
    wuv_b = w_uv_b.astype(BF16)
    wo_l = w_o[0].astype(BF16)
    rope_tabs = _rope_tables(dec_seq)

    cond = jnp.concatenate(
        [c_ctx[None, :], c, jnp.zeros((MOD_ROWS - 1 - dec_batch, D_MODEL), F32)], axis=0)
    mods = _adaln(cond, w_ada, b_ada).reshape(DEPTH, MOD_ROWS, N_MOD, D_MODEL)

    kc_a, vc_a, kc_b, vc_b = _cache_prep(
        cache_k_a, cache_v_a, cache_ckv_b, cache_krope_b, wuk_b, wuv_b)

    xp = x_prompt.reshape(batch * seq, D_MODEL)
    xs = x_sample.reshape(dec_batch * dec_seq, D_MODEL)
    g1 = norm_attn.reshape(DEPTH, 1, D_MODEL)
    g2 = norm_ffn.reshape(DEPTH, 1, D_MODEL)
    qn = qnorm_a.reshape(DEPTH, 1, HEAD_DIM)
    kn = knorm_a.reshape(DEPTH, 1, HEAD_DIM)
    kvn = kvnorm_b.reshape(DEPTH, 1, KV_RANK)
    ctx_mod = lambda i: 0

    def lat_mod(tm):
        return lambda i: 1 + i // (dec_seq // tm)

    r3 = lambda a: a.reshape(dec_batch, dec_seq, a.shape[-1])
    hb = ATTN_HEADS_PER_STEP
    new_bufs = None

    for l in range(DEPTH):
        first, nxt = l == 0, l + 1 < DEPTH

        (xp, *new_bufs), _ = _proj(
            xp, l, mods, ctx_mod, g1, w_in_b, qn, kn, kvn, wuk_b, wuv_b,
            new_shape=(batch, seq), new_bufs=new_bufs, w_o=wo_l)

        (qa, ka, va, qb, kb, vb), cast0 = _proj(
            xs, l, mods, lat_mod(PROJ_TM), g1, w_in_b, qn, kn, kvn, wuk_b, wuv_b,
            rope_tabs=rope_tabs, cast=((w_gate, 0),) if first else ())
        oa, cast1 = _attn_lat(r3(qa), l, kc_a, r3(ka), vc_a, r3(va),
                              1, N_HEADS_A, N_KV_A, HEAD_DIM, HEAD_DIM, "attn_lat_a", join_k=True,
                              tq=ATTN_TQ_A,
                              cast=((w_up, 0), (w_down, 0)) if first else ())
        ob, _ = _attn_lat(r3(qb), l, kc_b, r3(kb), vc_b, r3(vb),
                          N_HEADS_B // hb, hb, hb, QB_PAD, VDIM_B, "attn_lat_b", join_k=False,
                          tq=ATTN_TQ_B)
        xs = _outproj(xs, oa.reshape(-1, W_QA), ob.reshape(-1, W_VB), l, mods, lat_mod(OUT_TM), wo_l)
        if first:
            (wg_l,), (wu_l, wd_l) = cast0, cast1

        xp, gu_next = _ffn(xp, l, mods, ctx_mod, g2, wg_l, wu_l, wd_l,
                           cast=((w_gate, l + 1), (w_up, l + 1)) if nxt else ())
        xs, do_next = _ffn(xs, l, mods, lat_mod(FFN_TM), g2, wg_l, wu_l, wd_l,
                           cast=((w_down, l + 1), (w_o, l + 1)) if nxt else ())
        if nxt:
            (wg_l, wu_l), (wd_l, wo_l) = gu_next, do_next

    gf = norm_final.reshape(1, D_MODEL)
    y_prompt = _final_norm(xp, gf).reshape(batch, seq, D_MODEL)
    y_sample = _final_norm(xs, gf).reshape(dec_batch, dec_seq, D_MODEL)
    new_k, new_v, new_ckv, new_kr = new_bufs
    return (y_prompt, y_sample,
            new_k, new_v, new_ckv, new_kr)
```

```python
import functools

import jax
import jax.numpy as jnp
import numpy as np
from jax import lax
from jax.experimental import pallas as pl
from jax.experimental.pallas import tpu as pltpu

D_MODEL = 2048
DEPTH = 4
GRID_W = 64
HEAD_DIM = 128
N_HEADS_A = 8
N_KV_A = 2
GROUP_A = N_HEADS_A // N_KV_A
N_HEADS_B = 8
NOPE_B = 128
ROPE_B = 64
VDIM_B = 128
KV_RANK = 256
D_FF = 5632
ROPE_THETA = 10000.0
EPS = 1e-6

W_QA = N_HEADS_A * HEAD_DIM
W_KA = N_KV_A * HEAD_DIM
QB_DIM = NOPE_B + ROPE_B
W_QB = N_HEADS_B * QB_DIM
QB_PAD = 256
KR_PAD = 128
W_QB_PAD = N_HEADS_B * QB_PAD
W_KNOPE = N_HEADS_B * NOPE_B
W_VB = N_HEADS_B * VDIM_B
N_MOD = 6
MOD_ROWS = 8

O_QB = W_QA + 2 * W_KA
O_CKV = O_QB + W_QB
O_KR = O_CKV + KV_RANK
IN_COLS = O_KR + ROPE_B

C_QA = 0
C_KA = C_QA + W_QA
C_VA = C_KA + W_KA
C_QB = C_VA + W_KA
C_CKV = C_QB + W_QB_PAD
C_KR = C_CKV + KV_RANK
IN_COLS_PAD = C_KR + KR_PAD

LOG2_E = 1.4426950408889634
SCALE_A = HEAD_DIM ** -0.5 * LOG2_E
SCALE_B = QB_DIM ** -0.5 * LOG2_E

VMEM_LIMIT = 60 * 1024 * 1024

BF16 = jnp.bfloat16
F32 = jnp.float32


def _cparams(n_axes):
    return pltpu.CompilerParams(
        dimension_semantics=("arbitrary",) * n_axes, vmem_limit_bytes=VMEM_LIMIT)


def _layer_spec(shape, layer):
    nd = len(shape)
    return pl.BlockSpec((None,) + shape, lambda *_: (layer,) + (0,) * nd,
                        pipeline_mode=pl.Buffered(1))


def _mod_spec(layer, mod_map, width, col_map=None):
    if col_map is None:
        return pl.BlockSpec((None, None, N_MOD, width), lambda i, *_: (layer, mod_map(i), 0, 0))
    return pl.BlockSpec((None, None, N_MOD, width),
                        lambda i, j: (layer, mod_map(i), 0, col_map(j)))


def _dot(a, b):
    return jnp.dot(a, b, preferred_element_type=F32)


def _dot_t(a, b):
    return lax.dot_general(a, b, (((1,), (1,)), ((), ())), preferred_element_type=F32)


def _rms(x):
    return x * lax.rsqrt(jnp.mean(x * x, axis=-1, keepdims=True) + EPS)


def _cast_rows(n_rows, n_steps):
    rows = 16
    while n_rows % rows or n_rows // rows > n_steps:
        rows += 16
    return rows


def _side_cast(cast, n_steps, step_of):
    in_specs, out_specs, shapes = [], [], []
    for src, src_layer in cast:
        _, n_rows, n_cols = src.shape
        rows = _cast_rows(n_rows, n_steps)
        chunk = lambda *ids, last=n_rows // rows - 1: jnp.minimum(step_of(*ids), last)
        in_specs.append(pl.BlockSpec(
            (None, rows, n_cols), lambda *ids, c=chunk, l=src_layer: (l, c(*ids), 0)))
        out_specs.append(pl.BlockSpec((rows, n_cols), lambda *ids, c=chunk: (c(*ids), 0)))
        shapes.append(jax.ShapeDtypeStruct((n_rows, n_cols), BF16))
    return in_specs, out_specs, shapes, [src for src, _ in cast]


def _cast_chunks(cast_in, cast_out):
    for src, dst in zip(cast_in, cast_out):
        dst[...] = src[...].astype(BF16)


ADA_TN = 1024


def _adaln_kernel(s_ref, w_ref, b_ref, o_ref):
    s = s_ref[...]
    s = s * jax.nn.sigmoid(s)
    o_ref[...] = _dot(s.astype(BF16), w_ref[...].astype(BF16)) + b_ref[...]


def _adaln(s, w_ada, b_ada):
    n = N_MOD * D_MODEL
    return pl.pallas_call(
        _adaln_kernel,
        grid=(DEPTH, n // ADA_TN),
        in_specs=[
            pl.BlockSpec((MOD_ROWS, D_MODEL), lambda l, j: (0, 0)),
            pl.BlockSpec((None, D_MODEL, ADA_TN), lambda l, j: (l, 0, j)),
            pl.BlockSpec((None, 1, ADA_TN), lambda l, j: (l, 0, j)),
        ],
        out_specs=pl.BlockSpec((None, MOD_ROWS, ADA_TN), lambda l, j: (l, 0, j)),
        out_shape=jax.ShapeDtypeStruct((DEPTH, MOD_ROWS, n), F32),
        compiler_params=_cparams(2),
        name="adaln",
    )(s, w_ada, b_ada.reshape(DEPTH, 1, n))


WPREP_TR = 512
PAIR_IN = 2 * QB_DIM
PAIR_OUT = 2 * QB_PAD


def _wprep_kernel(w_ref, pq_ref, pk_ref, o_ref):
    o_ref[:, 0:O_QB] = w_ref[:, 0:O_QB]
    pq = pq_ref[...]
    for i in range(N_HEADS_B // 2):
        src = w_ref[:, O_QB + i * PAIR_IN:O_QB + (i + 1) * PAIR_IN]
        o_ref[:, C_QB + i * PAIR_OUT:C_QB + (i + 1) * PAIR_OUT] = _dot(src, pq).astype(BF16)
    o_ref[:, C_CKV:C_CKV + KV_RANK] = w_ref[:, O_CKV:O_CKV + KV_RANK]
    o_ref[:, C_KR:C_KR + KR_PAD] = _dot(w_ref[:, O_KR:O_KR + ROPE_B], pk_ref[...]).astype(BF16)


def _rope_pad_src():
    half = ROPE_B // 2
    src = np.full((KR_PAD,), -1, np.int32)
    src[0:half] = np.arange(half)
    src[2 * half:3 * half] = half + np.arange(half)
    return src


def _selection(src, n_in):
    return (jnp.arange(n_in, dtype=jnp.int32)[:, None] == jnp.asarray(src)[None, :]).astype(BF16)


def _wprep(w_in):
    rsrc = _rope_pad_src()
    head = np.concatenate([np.arange(NOPE_B), np.where(rsrc >= 0, NOPE_B + rsrc, -1)])
    pair = np.concatenate([head, np.where(head >= 0, QB_DIM + head, -1)]).astype(np.int32)
    pq = _selection(pair, PAIR_IN)
    pk = _selection(rsrc, ROPE_B)
    return pl.pallas_call(
        _wprep_kernel,
        grid=(D_MODEL // WPREP_TR,),
        in_specs=[
            pl.BlockSpec((WPREP_TR, IN_COLS), lambda r: (r, 0)),
            pl.BlockSpec((PAIR_IN, PAIR_OUT), lambda r: (0, 0)),
            pl.BlockSpec((ROPE_B, KR_PAD), lambda r: (0, 0)),
        ],
        out_specs=pl.BlockSpec((WPREP_TR, IN_COLS_PAD), lambda r: (r, 0)),
        out_shape=jax.ShapeDtypeStruct((D_MODEL, IN_COLS_PAD), BF16),
        compiler_params=_cparams(1),
        name="wprep",
    )(w_in, pq, pk)


def _cache_kernel(ka_ref, va_ref, ckv_ref, kr_ref, wuk_ref, wuv_ref, pk_ref,
                  ka_o, va_o, kb_o, vb_o):
    ka_o[...] = ka_ref[...].astype(BF16)
    va_o[...] = va_ref[...].astype(BF16)
    ckv = ckv_ref[...].astype(BF16)
    kn = _dot(ckv, wuk_ref[...])
    vb_o[...] = _dot(ckv, wuv_ref[...]).astype(BF16)
    kr = _dot(kr_ref[...].astype(BF16), pk_ref[...]).astype(BF16)
    for h in range(N_HEADS_B):
        kb_o[:, h * QB_PAD:h * QB_PAD + NOPE_B] = kn[:, h * NOPE_B:(h + 1) * NOPE_B].astype(BF16)
        kb_o[:, h * QB_PAD + NOPE_B:(h + 1) * QB_PAD] = kr


def _cache_prep(cache_k_a, cache_v_a, cache_ckv_b, cache_krope_b, wuk, wuv):
    nb, _, past = cache_ckv_b.shape[:3]
    ka = cache_k_a.reshape(nb, DEPTH, past, W_KA)
    va = cache_v_a.reshape(nb, DEPTH, past, W_KA)
    pk = _selection(_rope_pad_src(), ROPE_B)

    def in4(width):
        return pl.BlockSpec((None, None, past, width), lambda l, b: (b, l, 0, 0))

    def out4(width):
        return pl.BlockSpec((None, None, past, width), lambda l, b: (l, b, 0, 0))

    def w3(width):
        return pl.BlockSpec((None, KV_RANK, width), lambda l, b: (l, 0, 0))

    return pl.pallas_call(
        _cache_kernel,
        grid=(DEPTH, nb),
        in_specs=[in4(W_KA), in4(W_KA), in4(KV_RANK), in4(ROPE_B), w3(W_KNOPE), w3(W_VB),
                  pl.BlockSpec((ROPE_B, KR_PAD), lambda l, b: (0, 0))],
        out_specs=[out4(W_KA), out4(W_KA), out4(W_QB_PAD), out4(W_VB)],
        out_shape=[
            jax.ShapeDtypeStruct((DEPTH, nb, past, W_KA), BF16),
            jax.ShapeDtypeStruct((DEPTH, nb, past, W_KA), BF16),
            jax.ShapeDtypeStruct((DEPTH, nb, past, W_QB_PAD), BF16),
            jax.ShapeDtypeStruct((DEPTH, nb, past, W_VB), BF16),
        ],
        compiler_params=_cparams(2),
        name="cache_prep",
    )(ka, va, cache_ckv_b, cache_krope_b, wuk, wuv, pk)


PROJ_TM = 512
PROJ_ROWS = 256
N_NEW = 4


def _proj_kernel(*refs, rope, emit_new, n_alias, n_cast):
    it = iter(refs)
    x_ref, mod_ref, g_ref, w_ref, qn_ref, kn_ref, kvn_ref, wuk_ref, wuv_ref = (
        next(it) for _ in range(9))
    if rope:
        ca_ref, sa_ref, cb_ref, sb_ref = (next(it) for _ in range(4))
    if emit_new:
        wo_ref = next(it)
    for _ in range(n_alias):
        next(it)
    cast_in = [next(it) for _ in range(n_cast)]
    if emit_new:
        xo_o, nk_o, nv_o, nckv_o, nkr_o = (next(it) for _ in range(1 + N_NEW))
        cast_out = [next(it) for _ in range(n_cast)]
    qa_o, ka_o, va_o, qb_o, kb_o, vb_o = (next(it) for _ in range(6))
    if emit_new:
        vae_s, vbe_s, oa_o, ob_o = (next(it) for _ in range(4))
    else:
        cast_out = [next(it) for _ in range(n_cast)]
    _cast_chunks(cast_in, cast_out)

    for r in range(x_ref.shape[0] // PROJ_ROWS):
        rows = slice(r * PROJ_ROWS, (r + 1) * PROJ_ROWS)

        h = _rms(x_ref[rows, :]) * g_ref[...]
        h = h * (1.0 + mod_ref[1:2, :]) + mod_ref[0:1, :]
        hb = h.astype(BF16)

        def rope_a(t):
            if not rope:
                return t
            return t * ca_ref[rows, :] + pltpu.roll(t, HEAD_DIM // 2, 1) * sa_ref[rows, :]

        def rope_b(t):
            if not rope:
                return t
            return t * cb_ref[rows, :] + pltpu.roll(t, KR_PAD // 2, 1) * sb_ref[rows, :]

        p = _dot(hb, w_ref[...])

        pq = p[:, C_QA:C_QA + W_QA]
        qn = qn_ref[...]
        for i in range(N_HEADS_A):
            sl = slice(i * HEAD_DIM, (i + 1) * HEAD_DIM)
            t = rope_a(_rms(pq[:, sl]) * qn)
            qa_o[rows, sl] = (t * SCALE_A).astype(BF16)

        pk = p[:, C_KA:C_KA + W_KA]
        pv = p[:, C_VA:C_VA + W_KA]
        kn = kn_ref[...]
        for i in range(N_KV_A):
            sl = slice(i * HEAD_DIM, (i + 1) * HEAD_DIM)
            t = _rms(pk[:, sl]) * kn
            if emit_new:
                nk_o[r, :, i, :] = t
            ka_o[rows, sl] = rope_a(t).astype(BF16)
        va_o[rows, :] = pv.astype(BF16)
        if emit_new:
            for i in range(N_KV_A):
                nv_o[r, :, i, :] = pv[:, i * HEAD_DIM:(i + 1) * HEAD_DIM]

        pqb = p[:, C_QB:C_QB + W_QB_PAD]
        for i in range(N_HEADS_B):
            s0 = i * QB_PAD
            qb_o[rows, s0:s0 + NOPE_B] = (pqb[:, s0:s0 + NOPE_B] * SCALE_B).astype(BF16)
            t = rope_b(pqb[:, s0 + NOPE_B:s0 + QB_PAD])
            qb_o[rows, s0 + NOPE_B:s0 + QB_PAD] = (t * SCALE_B).astype(BF16)

        pc = p[:, C_CKV:C_CKV + KV_RANK]
        ckv = _rms(pc) * kvn_ref[...]
        pkr = p[:, C_KR:C_KR + KR_PAD]
        if emit_new:
            half = ROPE_B // 2
            nckv_o[r] = ckv
            nkr_o[r] = jnp.concatenate([pkr[:, 0:half], pkr[:, 2 * half:3 * half]], axis=-1)
        ckvb = ckv.astype(BF16)
        krb = rope_b(pkr).astype(BF16)
        kno = _dot(ckvb, wuk_ref[...])
        vb_o[rows, :] = _dot(ckvb, wuv_ref[...]).astype(BF16)
        for i in range(N_HEADS_B):
            s0 = i * QB_PAD
            kb_o[rows, s0:s0 + NOPE_B] = kno[:, i * NOPE_B:(i + 1) * NOPE_B].astype(BF16)
            kb_o[rows, s0 + NOPE_B:s0 + QB_PAD] = krb

        if emit_new:
            _fill_v_ext(vae_s, rows, va_o.at[rows, :], N_KV_A, HEAD_DIM)
            _fill_v_ext(vbe_s, rows, vb_o.at[rows, :], N_HEADS_B, VDIM_B)
            for hq in range(N_HEADS_A):
                g = hq // GROUP_A
                qsl = slice(hq * HEAD_DIM, (hq + 1) * HEAD_DIM)
                oa_o[rows, qsl] = _attend(
                    qa_o[rows, qsl], ka_o[rows, g * HEAD_DIM:(g + 1) * HEAD_DIM],
                    vae_s[rows, 2 * g * HEAD_DIM:(2 * g + 2) * HEAD_DIM], HEAD_DIM)
            for hq in range(N_HEADS_B):
                qsl = slice(hq * QB_PAD, (hq + 1) * QB_PAD)
                ob_o[rows, hq * VDIM_B:(hq + 1) * VDIM_B] = _attend(
                    qb_o[rows, qsl], kb_o[rows, qsl],
                    vbe_s[rows, 2 * hq * VDIM_B:(2 * hq + 2) * VDIM_B], VDIM_B)
            y = _dot(oa_o[rows, :], wo_ref[0:W_QA, :]) + _dot(ob_o[rows, :], wo_ref[W_QA:, :])
            xo_o[rows, :] = x_ref[rows, :] + mod_ref[2:3, :] * y


def _proj(x, layer, mods, mod_map, g, w_in, qn, kn, kvn, wuk, wuv, rope_tabs=None,
          new_shape=None, new_bufs=None, w_o=None, cast=(), tm=PROJ_TM):
    t = x.shape[0]
    rope = rope_tabs is not None
    emit_new = new_shape is not None

    def row(width):
        return pl.BlockSpec((tm, width), lambda i: (i, 0))

    in_specs = [
        row(D_MODEL),
        _mod_spec(layer, mod_map, D_MODEL),
        _layer_spec((1, D_MODEL), layer),
        pl.BlockSpec((D_MODEL, IN_COLS_PAD), lambda *_: (0, 0), pipeline_mode=pl.Buffered(1)),
        _layer_spec((1, HEAD_DIM), layer),
        _layer_spec((1, HEAD_DIM), layer),
        _layer_spec((1, KV_RANK), layer),
        _layer_spec((KV_RANK, W_KNOPE), layer),
        _layer_spec((KV_RANK, W_VB), layer),
    ]
    args = [x, mods, g, w_in, qn, kn, kvn, wuk, wuv]
    if rope:
        seq_tiles = rope_tabs[0].shape[0] // tm
        for tab in rope_tabs:
            in_specs.append(pl.BlockSpec((tm, tab.shape[1]), lambda i: (i % seq_tiles, 0)))
            args.append(tab)
    qkv_widths = [W_QA, W_KA, W_KA, W_QB_PAD, W_QB_PAD, W_VB]
    scratch = []
    aliases = {}
    n_alias = 0
    if not emit_new:
        out_specs = [row(w) for w in qkv_widths]
        out_shape = [jax.ShapeDtypeStruct((t, w), BF16) for w in qkv_widths]
    else:
        in_specs.append(pl.BlockSpec((W_QA + W_VB, D_MODEL), lambda i: (0, 0),
                                     pipeline_mode=pl.Buffered(1)))
        args.append(w_o)
        out_specs = [row(D_MODEL)]
        out_shape = [jax.ShapeDtypeStruct((t, D_MODEL), F32)]
        scratch = [pltpu.VMEM((tm, w), BF16)
                   for w in qkv_widths + [2 * W_KA, 2 * W_VB, W_QA, W_VB]]
        batch, seq = new_shape
        assert seq == PROJ_ROWS and t == batch * seq
        nb = tm // seq
        tails = [(N_KV_A, HEAD_DIM), (N_KV_A, HEAD_DIM), (KV_RANK,), (ROPE_B,)]
        if new_bufs is not None:
            n_alias = N_NEW
            for k, buf in enumerate(new_bufs):
                aliases[len(args)] = len(out_shape) + k
                in_specs.append(pl.BlockSpec(memory_space=pl.ANY))
                args.append(buf)
        for tail in tails:
            zeros = (0,) * len(tail)
            out_specs.append(pl.BlockSpec((nb, None, seq) + tail,
                                          lambda i, zeros=zeros: (i, layer, 0) + zeros))
            out_shape.append(jax.ShapeDtypeStruct((batch, DEPTH, seq) + tail, F32))
    cast_specs_in, cast_specs_out, cast_shapes, cast_srcs = _side_cast(
        cast, t // tm, lambda i: i)
    outs = pl.pallas_call(
        functools.partial(_proj_kernel, rope=rope, emit_new=emit_new, n_alias=n_alias,
                          n_cast=len(cast)),
        grid=(t // tm,),
        in_specs=in_specs + cast_specs_in,
        out_specs=out_specs + cast_specs_out,
        out_shape=out_shape + cast_shapes,
        input_output_aliases=aliases,
        scratch_shapes=scratch,
        compiler_params=_cparams(1),
        name="proj_lat" if rope else "proj_attn_ctx",
    )(*args, *cast_srcs)
    n_main = len(out_shape)
    return outs[:n_main], outs[n_main:]


def _attend(q, k, v_ext, dv):
    s = _dot_t(q, k)
    p = jnp.exp2(s - s.max(axis=-1, keepdims=True))
    o = _dot(p.astype(BF16), v_ext)
    return (o[:, 0:dv] / o[:, dv:2 * dv]).astype(BF16)


def _attend_parts(q, k_parts, v_ext_ref, dv):
    s = [_dot_t(q, k) for k in k_parts]
    m = s[0].max(axis=-1, keepdims=True)
    for sp in s[1:]:
        m = jnp.maximum(m, sp.max(axis=-1, keepdims=True))
    o = None
    r0 = 0
    for sp in s:
        r1 = r0 + sp.shape[1]
        op = _dot(jnp.exp2(sp - m).astype(BF16), v_ext_ref[r0:r1, :])
        o = op if o is None else o + op
        r0 = r1
    return (o[:, 0:dv] / o[:, dv:2 * dv]).astype(BF16)


def _fill_v_ext(v_ext_ref, rows, v_ref, n_kv, dv):
    ones = jnp.ones((v_ref.shape[0], dv), BF16)
    for jk in range(n_kv):
        v_ext_ref[rows, 2 * jk * dv:(2 * jk + 1) * dv] = v_ref[:, jk * dv:(jk + 1) * dv]
        v_ext_ref[rows, (2 * jk + 1) * dv:(2 * jk + 2) * dv] = ones


ATTN_TQ_A = 1024
ATTN_TQ_B = 512
ATTN_HEADS_PER_STEP = 8
ATTN_ROWS = 256


def _attn_lat_kernel(*refs, n_q, n_kv, dq, dv, past, n_cast):
    q_ref, kc_ref, kn_ref, vc_ref, vn_ref = refs[:5]
    o_ref = refs[5 + n_cast]
    v_ref = refs[6 + 2 * n_cast]
    k_scratch = refs[7 + 2 * n_cast:]
    _attn_lat_body(q_ref, kc_ref, kn_ref, vc_ref, vn_ref, o_ref, v_ref, k_scratch,
                   n_q=n_q, n_kv=n_kv, dq=dq, dv=dv, past=past)
    _cast_chunks(refs[5:5 + n_cast], refs[6 + n_cast:6 + 2 * n_cast])


def _attn_lat_body(q_ref, kc_ref, kn_ref, vc_ref, vn_ref, o_ref, v_ref, k_scratch,
                   *, n_q, n_kv, dq, dv, past):
    @pl.when(pl.program_id(2) == 0)
    def _():
        _fill_v_ext(v_ref, slice(0, past), vc_ref, n_kv, dv)
        _fill_v_ext(v_ref, slice(past, None), vn_ref, n_kv, dv)
        if k_scratch:
            k_scratch[0][0:past, :] = kc_ref[...]
            k_scratch[0][past:, :] = kn_ref[...]

    rows = ATTN_ROWS
    for j in range(n_q):
        jk = j * n_kv // n_q
        ksl = slice(jk * dq, (jk + 1) * dq)
        vsl = slice(2 * jk * dv, (2 * jk + 2) * dv)
        for r in range(q_ref.shape[0] // rows):
            rsl = slice(r * rows, (r + 1) * rows)
            q = q_ref[rsl, j * dq:(j + 1) * dq]
            if k_scratch:
                o = _attend(q, k_scratch[0][:, ksl], v_ref[:, vsl], dv)
            else:
                o = _attend_parts(q, [kc_ref[:, ksl], kn_ref[:, ksl]], v_ref.at[:, vsl], dv)
            o_ref[rsl, j * dv:(j + 1) * dv] = o


def _attn_lat(q, layer, kc, kn, vc, vn, n_units, n_q, n_kv, dq, dv, name, join_k, tq, cast=()):
    nb, seq, _ = q.shape
    past = kc.shape[2]
    scratch = [pltpu.VMEM((past + seq, n_kv * 2 * dv), BF16)]
    if join_k:
        scratch.append(pltpu.VMEM((past + seq, n_kv * dq), BF16))
    n_qt = seq // tq
    cast_specs_in, cast_specs_out, cast_shapes, cast_srcs = _side_cast(
        cast, nb * n_units * n_qt, lambda b, u, i: (b * n_units + u) * n_qt + i)
    outs = pl.pallas_call(
        functools.partial(_attn_lat_kernel, n_q=n_q, n_kv=n_kv, dq=dq, dv=dv, past=past,
                          n_cast=len(cast)),
        grid=(nb, n_units, n_qt),
        in_specs=[
            pl.BlockSpec((None, tq, n_q * dq), lambda b, u, i: (b, i, u)),
            pl.BlockSpec((None, None, past, n_kv * dq), lambda b, u, i: (layer, b, 0, u)),
            pl.BlockSpec((None, seq, n_kv * dq), lambda b, u, i: (b, 0, u)),
            pl.BlockSpec((None, None, past, n_kv * dv), lambda b, u, i: (layer, b, 0, u)),
            pl.BlockSpec((None, seq, n_kv * dv), lambda b, u, i: (b, 0, u)),
        ] + cast_specs_in,
        out_specs=[pl.BlockSpec((None, tq, n_q * dv), lambda b, u, i: (b, i, u))] + cast_specs_out,
        out_shape=[jax.ShapeDtypeStruct((nb, seq, n_units * n_q * dv), BF16)] + cast_shapes,
        scratch_shapes=scratch,
        compiler_params=_cparams(3),
        name=name,
    )(q, kc, kn, vc, vn, *cast_srcs)
    return outs[0], outs[1:]


OUT_TM = 512


def _outproj_kernel(x_ref, oa_ref, ob_ref, mod_ref, w_ref, y_ref):
    y = _dot(oa_ref[...], w_ref[0:W_QA, :]) + _dot(ob_ref[...], w_ref[W_QA:, :])
    y_ref[...] = x_ref[...] + mod_ref[2:3, :] * y


def _outproj(x, oa, ob, layer, mods, mod_map, w_o, tm=OUT_TM):
    t = x.shape[0]
    return pl.pallas_call(
        _outproj_kernel,
        grid=(t // tm,),
        in_specs=[
            pl.BlockSpec((tm, D_MODEL), lambda i: (i, 0)),
            pl.BlockSpec((tm, W_QA), lambda i: (i, 0)),
            pl.BlockSpec((tm, W_VB), lambda i: (i, 0)),
            _mod_spec(layer, mod_map, D_MODEL),
            pl.BlockSpec((W_QA + W_VB, D_MODEL), lambda i: (0, 0), pipeline_mode=pl.Buffered(1)),
        ],
        out_specs=pl.BlockSpec((tm, D_MODEL), lambda i: (i, 0)),
        out_shape=jax.ShapeDtypeStruct((t, D_MODEL), F32),
        compiler_params=_cparams(1),
        name="outproj",
    )(x, oa, ob, mods, w_o)


FFN_TM = 1024
FFN_TF = 512
FFN_TN = 512
FFN_ROWS = 256


def _ffn_kernel(*refs, nf, tf, rc, n_tiles, n_cast):
    xr_ref, xc_ref, mod_ref, modc_ref, g_ref, wg_ref, wu_ref, wd_ref = refs[:8]
    cast_in = refs[8:8 + n_cast]
    y_ref = refs[8 + n_cast]
    cast_out = refs[9 + n_cast:9 + 2 * n_cast]
    h_ref, t_ref = refs[9 + 2 * n_cast:]
    i = pl.program_id(0)
    j = pl.program_id(1)

    def cast_chunks():
        _cast_chunks(cast_in, cast_out)
    cur = (i + 1) % 2
    nxt = i % 2

    def norm_chunk():
        h = _rms(xr_ref[...]) * g_ref[...]
        h = h * (1.0 + mod_ref[4:5, :]) + mod_ref[3:4, :]
        r0 = pl.multiple_of((j - nf) * rc, rc)
        h_ref[nxt, pl.ds(r0, rc), :] = h.astype(BF16)

    row_blocks = [slice(m, m + FFN_ROWS) for m in range(0, y_ref.shape[0], FFN_ROWS)]

    def down_proj():
        for rows in row_blocks:
            d = _dot(t_ref[0, rows, :], wd_ref[0:tf, :])
            for k in range(1, nf):
                d = d + _dot(t_ref[k, rows, :], wd_ref[k * tf:(k + 1) * tf, :])
            y_ref[rows, :] = xc_ref[rows, :] + modc_ref[5:6, :] * d

    @pl.when((j < nf) & (i == 0))
    def _():
        cast_chunks()

    @pl.when((j < nf) & (i >= 1))
    def _():
        cast_chunks()
        for rows in row_blocks:
            hb = h_ref[cur, rows, :]
            a = _dot(hb, wg_ref[...])
            b = _dot(hb, wu_ref[...])
            t_ref[j, rows, :] = (a * jax.nn.sigmoid(a) * b).astype(BF16)

    @pl.when((j >= nf) & (i >= 1) & (i < n_tiles))
    def _():
        cast_chunks()
        norm_chunk()
        down_proj()

    @pl.when((j >= nf) & (i == 0))
    def _():
        cast_chunks()
        norm_chunk()

    @pl.when((j >= nf) & (i == n_tiles))
    def _():
        cast_chunks()
        down_proj()


def _ffn(x, layer, mods, mod_map, g, wg, wu, wd, cast=(), tm=FFN_TM, tf=FFN_TF, tn=FFN_TN):
    t = x.shape[0]
    n_tiles = t // tm
    nf = D_FF // tf
    nn = D_MODEL // tn
    cast_specs_in, cast_specs_out, cast_shapes, cast_srcs = _side_cast(
        cast, (n_tiles + 1) * (nf + nn), lambda i, j: i * (nf + nn) + j)
    rc = tm // nn
    col = lambda j: jnp.maximum(j - nf, 0)
    comp = lambda i: jnp.maximum(i - 1, 0)
    norm = lambda i: jnp.minimum(i, n_tiles - 1)
    ccol = lambda i, j: jnp.where(i == 0, 0, col(j))
    ftile = lambda i, j: jnp.where(i == 0, 0, jnp.minimum(j, nf - 1))
    outs = pl.pallas_call(
        functools.partial(_ffn_kernel, nf=nf, tf=tf, rc=rc, n_tiles=n_tiles, n_cast=len(cast)),
        grid=(n_tiles + 1, nf + nn),
        in_specs=[
            pl.BlockSpec((rc, D_MODEL), lambda i, j: (norm(i) * nn + col(j), 0)),
            pl.BlockSpec((tm, tn), lambda i, j: (comp(i), ccol(i, j))),
            pl.BlockSpec((None, None, N_MOD, D_MODEL),
                         lambda i, j: (layer, mod_map(norm(i)), 0, 0)),
            pl.BlockSpec((None, None, N_MOD, tn),
                         lambda i, j: (layer, mod_map(comp(i)), 0, ccol(i, j))),
            _layer_spec((1, D_MODEL), layer),
            pl.BlockSpec((D_MODEL, tf), lambda i, j: (0, ftile(i, j))),
            pl.BlockSpec((D_MODEL, tf), lambda i, j: (0, ftile(i, j))),
            pl.BlockSpec((D_FF, tn), lambda i, j: (0, ccol(i, j))),
        ] + cast_specs_in,
        out_specs=[pl.BlockSpec((tm, tn), lambda i, j: (comp(i), ccol(i, j)))] + cast_specs_out,
        out_shape=[jax.ShapeDtypeStruct((t, D_MODEL), F32)] + cast_shapes,
        scratch_shapes=[pltpu.VMEM((2, tm, D_MODEL), BF16), pltpu.VMEM((nf, tm, tf), BF16)],
        compiler_params=_cparams(2),
        name="ffn",
    )(x, x, mods, mods, g, wg, wu, wd, *cast_srcs)
    return outs[0], outs[1:]


NORM_TM = 512


def _final_norm_kernel(x_ref, g_ref, y_ref):
    y_ref[...] = _rms(x_ref[...]) * g_ref[...]


def _final_norm(x, g, tm=NORM_TM):
    t = x.shape[0]
    return pl.pallas_call(
        _final_norm_kernel,
        grid=(t // tm,),
        in_specs=[pl.BlockSpec((tm, D_MODEL), lambda i: (i, 0)),
                  pl.BlockSpec((1, D_MODEL), lambda i: (0, 0))],
        out_specs=pl.BlockSpec((tm, D_MODEL), lambda i: (i, 0)),
        out_shape=jax.ShapeDtypeStruct((t, D_MODEL), F32),
        compiler_params=_cparams(1),
        name="final_norm",
    )(x, g)


def _rope_tables(n_tokens):
    n_rows = n_tokens // GRID_W
    row = jnp.repeat(jnp.arange(n_rows, dtype=F32), GRID_W)
    col = jnp.tile(jnp.arange(GRID_W, dtype=F32), n_rows)

    def cs(dim):
        n_freq = dim // 4
        inv = ROPE_THETA ** (-jnp.arange(n_freq, dtype=F32) / n_freq)
        ang = jnp.concatenate([row[:, None] * inv, col[:, None] * inv], axis=-1)
        return jnp.cos(ang), jnp.sin(ang)

    cos_a, sin_a = cs(HEAD_DIM)
    cos_b, sin_b = cs(ROPE_B)
    one = jnp.ones_like(cos_b)
    zero = jnp.zeros_like(sin_b)
    return (jnp.concatenate([cos_a, cos_a], axis=-1),
            jnp.concatenate([-sin_a, sin_a], axis=-1),
            jnp.concatenate([cos_b, one, cos_b, one], axis=-1),
            jnp.concatenate([-sin_b, zero, sin_b, zero], axis=-1))


def kernel(x_prompt, x_sample, cache_k_a, cache_v_a, cache_ckv_b, cache_krope_b, c, c_ctx, w_ada, b_ada, norm_attn, norm_ffn, w_in, qnorm_a, knorm_a, kvnorm_b, w_uk_b, w_uv_b, w_o, w_gate, w_up, w_down, norm_final):
    batch, seq, _ = x_prompt.shape
    dec_batch, dec_seq, _ = x_sample.shape
    assert dec_batch + 1 <= MOD_ROWS

    win_l = _wprep(w_in[0].astype(BF16))
    wuk_b = w_uk_b.astype(BF16)
    wuv_b = w_uv_b.astype(BF16)
    wo_l = w_o[0].astype(BF16)
    rope_tabs = _rope_tables(dec_seq)

    cond = jnp.concatenate(
        [c_ctx[None, :], c, jnp.zeros((MOD_ROWS - 1 - dec_batch, D_MODEL), F32)], axis=0)
    mods = _adaln(cond, w_ada, b_ada).reshape(DEPTH, MOD_ROWS, N_MOD, D_MODEL)

    kc_a, vc_a, kc_b, vc_b = _cache_prep(
        cache_k_a, cache_v_a, cache_ckv_b, cache_krope_b, wuk_b, wuv_b)

    xp = x_prompt.reshape(batch * seq, D_MODEL)
    xs = x_sample.reshape(dec_batch * dec_seq, D_MODEL)
    g1 = norm_attn.reshape(DEPTH, 1, D_MODEL)
    g2 = norm_ffn.reshape(DEPTH, 1, D_MODEL)
    qn = qnorm_a.reshape(DEPTH, 1, HEAD_DIM)
    kn = knorm_a.reshape(DEPTH, 1, HEAD_DIM)
    kvn = kvnorm_b.reshape(DEPTH, 1, KV_RANK)
    ctx_mod = lambda i: 0

    def lat_mod(tm):
        return lambda i: 1 + i // (dec_seq // tm)

    r3 = lambda a: a.reshape(dec_batch, dec_seq, a.shape[-1])
    hb = ATTN_HEADS_PER_STEP
    new_bufs = None

    for l in range(DEPTH):
        first, nxt = l == 0, l + 1 < DEPTH

        (xp, *new_bufs), _ = _proj(
            xp, l, mods, ctx_mod, g1, win_l, qn, kn, kvn, wuk_b, wuv_b,
            new_shape=(batch, seq), new_bufs=new_bufs, w_o=wo_l)

        (qa, ka, va, qb, kb, vb), cast0 = _proj(
            xs, l, mods, lat_mod(PROJ_TM), g1, win_l, qn, kn, kvn, wuk_b, wuv_b,
            rope_tabs=rope_tabs, cast=((w_gate, 0),) if first else ())
        oa, cast1 = _attn_lat(r3(qa), l, kc_a, r3(ka), vc_a, r3(va),
                              1, N_HEADS_A, N_KV_A, HEAD_DIM, HEAD_DIM, "attn_lat_a", join_k=True,
                              tq=ATTN_TQ_A,
                              cast=((w_up, 0), (w_down, 0)) if first else ())
        ob, _ = _attn_lat(r3(qb), l, kc_b, r3(kb), vc_b, r3(vb),
                          N_HEADS_B // hb, hb, hb, QB_PAD, VDIM_B, "attn_lat_b", join_k=False,
                          tq=ATTN_TQ_B)
        xs = _outproj(xs, oa.reshape(-1, W_QA), ob.reshape(-1, W_VB), l, mods, lat_mod(OUT_TM), wo_l)
        if first:
            (wg_l,), (wu_l, wd_l) = cast0, cast1

        xp, gui_next = _ffn(xp, l, mods, ctx_mod, g2, wg_l, wu_l, wd_l,
                            cast=((w_gate, l + 1), (w_up, l + 1), (w_in, l + 1)) if nxt else ())
        xs, do_next = _ffn(xs, l, mods, lat_mod(FFN_TM), g2, wg_l, wu_l, wd_l,
                           cast=((w_down, l + 1), (w_o, l + 1)) if nxt else ())
        if nxt:
            (wg_l, wu_l, win_next), (wd_l, wo_l) = gui_next, do_next
            win_l = _wprep(win_next)

    gf = norm_final.reshape(1, D_MODEL)
    y_prompt = _final_norm(xp, gf).reshape(batch, seq, D_MODEL)
    y_sample = _final_norm(xs, gf).reshape(dec_batch, dec_seq, D_MODEL)
    new_k, new_v, new_ckv, new_kr = new_bufs
    return (y_prompt, y_sample,
            new_k, new_v, new_ckv, new_kr)
```

```python
import functools

import jax
import jax.numpy as jnp
import numpy as np
from jax import lax
from jax.experimental import pallas as pl
from jax.experimental.pallas import tpu as pltpu

D_MODEL = 2048
DEPTH = 4
GRID_W = 64
HEAD_DIM = 128
N_HEADS_A = 8
N_KV_A = 2
GROUP_A = N_HEADS_A // N_KV_A
N_HEADS_B = 8
NOPE_B = 128
ROPE_B = 64
VDIM_B = 128
KV_RANK = 256
D_FF = 5632
ROPE_THETA = 10000.0
EPS = 1e-6

W_QA = N_HEADS_A * HEAD_DIM
W_KA = N_KV_A * HEAD_DIM
QB_DIM = NOPE_B + ROPE_B
W_QB = N_HEADS_B * QB_DIM
QB_PAD = 256
KR_PAD = 128
W_QB_PAD = N_HEADS_B * QB_PAD
W_KNOPE = N_HEADS_B * NOPE_B
W_VB = N_HEADS_B * VDIM_B
N_MOD = 6
MOD_ROWS = 8

O_QB = W_QA + 2 * W_KA
O_CKV = O_QB + W_QB
O_KR = O_CKV + KV_RANK
IN_COLS = O_KR + ROPE_B
LANES = 128
IN_COLS_128 = -(-IN_COLS // LANES) * LANES

C_QA = 0
C_KA = C_QA + W_QA
C_VA = C_KA + W_KA
C_QB = C_VA + W_KA
C_CKV = C_QB + W_QB_PAD
C_KR = C_CKV + KV_RANK
IN_COLS_PAD = C_KR + KR_PAD

LOG2_E = 1.4426950408889634
SCALE_A = HEAD_DIM ** -0.5 * LOG2_E
SCALE_B = QB_DIM ** -0.5 * LOG2_E

VMEM_LIMIT = 60 * 1024 * 1024

BF16 = jnp.bfloat16
F32 = jnp.float32


def _cparams(n_axes):
    return pltpu.CompilerParams(
        dimension_semantics=("arbitrary",) * n_axes, vmem_limit_bytes=VMEM_LIMIT)


def _layer_spec(shape, layer):
    nd = len(shape)
    return pl.BlockSpec((None,) + shape, lambda *_: (layer,) + (0,) * nd,
                        pipeline_mode=pl.Buffered(1))


def _mod_spec(layer, mod_map, width, col_map=None):
    if col_map is None:
        return pl.BlockSpec((None, None, N_MOD, width), lambda i, *_: (layer, mod_map(i), 0, 0))
    return pl.BlockSpec((None, None, N_MOD, width),
                        lambda i, j: (layer, mod_map(i), 0, col_map(j)))


def _dot(a, b):
    return jnp.dot(a, b, preferred_element_type=F32)


def _dot_t(a, b):
    return lax.dot_general(a, b, (((1,), (1,)), ((), ())), preferred_element_type=F32)


def _rms(x):
    return x * lax.rsqrt(jnp.mean(x * x, axis=-1, keepdims=True) + EPS)


def _cast_rows(n_rows, n_steps):
    rows = 16
    while n_rows % rows or n_rows // rows > n_steps:
        rows += 16
    return rows


def _side_cast(cast, n_steps, step_of):
    in_specs, out_specs, shapes = [], [], []
    for src, src_layer in cast:
        _, n_rows, n_cols = src.shape
        out_cols = -(-n_cols // LANES) * LANES
        rows = _cast_rows(n_rows, n_steps)
        chunk = lambda *ids, last=n_rows // rows - 1: jnp.minimum(step_of(*ids), last)
        in_specs.append(pl.BlockSpec(
            (None, rows, n_cols), lambda *ids, c=chunk, l=src_layer: (l, c(*ids), 0)))
        out_specs.append(pl.BlockSpec((rows, out_cols), lambda *ids, c=chunk: (c(*ids), 0)))
        shapes.append(jax.ShapeDtypeStruct((n_rows, out_cols), BF16))
    return in_specs, out_specs, shapes, [src for src, _ in cast]


def _cast_chunks(cast_in, cast_out):
    for src, dst in zip(cast_in, cast_out):
        n_cols = src.shape[1]
        dst[:, 0:n_cols] = src[...].astype(BF16)
        if dst.shape[1] > n_cols:
            dst[:, n_cols:] = jnp.zeros((dst.shape[0], dst.shape[1] - n_cols), BF16)


ADA_TN = 1024


def _adaln_kernel(s_ref, w_ref, b_ref, o_ref):
    s = s_ref[...]
    s = s * jax.nn.sigmoid(s)
    o_ref[...] = _dot(s.astype(BF16), w_ref[...].astype(BF16)) + b_ref[...]


def _adaln(s, w_ada, b_ada):
    n = N_MOD * D_MODEL
    return pl.pallas_call(
        _adaln_kernel,
        grid=(DEPTH, n // ADA_TN),
        in_specs=[
            pl.BlockSpec((MOD_ROWS, D_MODEL), lambda l, j: (0, 0)),
            pl.BlockSpec((None, D_MODEL, ADA_TN), lambda l, j: (l, 0, j)),
            pl.BlockSpec((None, 1, ADA_TN), lambda l, j: (l, 0, j)),
        ],
        out_specs=pl.BlockSpec((None, MOD_ROWS, ADA_TN), lambda l, j: (l, 0, j)),
        out_shape=jax.ShapeDtypeStruct((DEPTH, MOD_ROWS, n), F32),
        compiler_params=_cparams(2),
        name="adaln",
    )(s, w_ada, b_ada.reshape(DEPTH, 1, n))


WPREP_TR = 512
PAIR_IN = 2 * QB_DIM
PAIR_OUT = 2 * QB_PAD


def _wprep_kernel(w_ref, pq_ref, pk_ref, o_ref):
    o_ref[:, 0:O_QB] = w_ref[:, 0:O_QB]
    pq = pq_ref[...]
    for i in range(N_HEADS_B // 2):
        src = w_ref[:, O_QB + i * PAIR_IN:O_QB + (i + 1) * PAIR_IN]
        o_ref[:, C_QB + i * PAIR_OUT:C_QB + (i + 1) * PAIR_OUT] = _dot(src, pq).astype(BF16)
    o_ref[:, C_CKV:C_CKV + KV_RANK] = w_ref[:, O_CKV:O_CKV + KV_RANK]
    o_ref[:, C_KR:C_KR + KR_PAD] = _dot(w_ref[:, O_KR:O_KR + ROPE_B], pk_ref[...]).astype(BF16)


def _rope_pad_src():
    half = ROPE_B // 2
    src = np.full((KR_PAD,), -1, np.int32)
    src[0:half] = np.arange(half)
    src[2 * half:3 * half] = half + np.arange(half)
    return src


def _selection(src, n_in):
    return (jnp.arange(n_in, dtype=jnp.int32)[:, None] == jnp.asarray(src)[None, :]).astype(BF16)


def _wprep(w_in):
    rsrc = _rope_pad_src()
    head = np.concatenate([np.arange(NOPE_B), np.where(rsrc >= 0, NOPE_B + rsrc, -1)])
    pair = np.concatenate([head, np.where(head >= 0, QB_DIM + head, -1)]).astype(np.int32)
    pq = _selection(pair, PAIR_IN)
    pk = _selection(rsrc, ROPE_B)
    return pl.pallas_call(
        _wprep_kernel,
        grid=(D_MODEL // WPREP_TR,),
        in_specs=[
            pl.BlockSpec((WPREP_TR, IN_COLS_128), lambda r: (r, 0)),
            pl.BlockSpec((PAIR_IN, PAIR_OUT), lambda r: (0, 0)),
            pl.BlockSpec((ROPE_B, KR_PAD), lambda r: (0, 0)),
        ],
        out_specs=pl.BlockSpec((WPREP_TR, IN_COLS_PAD), lambda r: (r, 0)),
        out_shape=jax.ShapeDtypeStruct((D_MODEL, IN_COLS_PAD), BF16),
        compiler_params=_cparams(1),
        name="wprep",
    )(w_in, pq, pk)


def _cache_kernel(ka_ref, va_ref, ckv_ref, kr_ref, wuk_ref, wuv_ref, pk_ref,
                  ka_o, va_o, kb_o, vb_o):
    ka_o[...] = ka_ref[...].astype(BF16)
    va_o[...] = va_ref[...].astype(BF16)
    ckv = ckv_ref[...].astype(BF16)
    kn = _dot(ckv, wuk_ref[...])
    vb_o[...] = _dot(ckv, wuv_ref[...]).astype(BF16)
    kr = _dot(kr_ref[...].astype(BF16), pk_ref[...]).astype(BF16)
    for h in range(N_HEADS_B):
        kb_o[:, h * QB_PAD:h * QB_PAD + NOPE_B] = kn[:, h * NOPE_B:(h + 1) * NOPE_B].astype(BF16)
        kb_o[:, h * QB_PAD + NOPE_B:(h + 1) * QB_PAD] = kr


def _cache_prep(cache_k_a, cache_v_a, cache_ckv_b, cache_krope_b, wuk, wuv):
    nb, _, past = cache_ckv_b.shape[:3]
    ka = cache_k_a.reshape(nb, DEPTH, past, W_KA)
    va = cache_v_a.reshape(nb, DEPTH, past, W_KA)
    pk = _selection(_rope_pad_src(), ROPE_B)

    def in4(width):
        return pl.BlockSpec((None, None, past, width), lambda l, b: (b, l, 0, 0))

    def out4(width):
        return pl.BlockSpec((None, None, past, width), lambda l, b: (l, b, 0, 0))

    def w3(width):
        return pl.BlockSpec((None, KV_RANK, width), lambda l, b: (l, 0, 0))

    return pl.pallas_call(
        _cache_kernel,
        grid=(DEPTH, nb),
        in_specs=[in4(W_KA), in4(W_KA), in4(KV_RANK), in4(ROPE_B), w3(W_KNOPE), w3(W_VB),
                  pl.BlockSpec((ROPE_B, KR_PAD), lambda l, b: (0, 0))],
        out_specs=[out4(W_KA), out4(W_KA), out4(W_QB_PAD), out4(W_VB)],
        out_shape=[
            jax.ShapeDtypeStruct((DEPTH, nb, past, W_KA), BF16),
            jax.ShapeDtypeStruct((DEPTH, nb, past, W_KA), BF16),
            jax.ShapeDtypeStruct((DEPTH, nb, past, W_QB_PAD), BF16),
            jax.ShapeDtypeStruct((DEPTH, nb, past, W_VB), BF16),
        ],
        compiler_params=_cparams(2),
        name="cache_prep",
    )(ka, va, cache_ckv_b, cache_krope_b, wuk, wuv, pk)


PROJ_TM = 512
PROJ_ROWS = 256
N_NEW = 4


def _proj_kernel(*refs, rope, emit_new, n_alias, n_cast):
    it = iter(refs)
    x_ref, mod_ref, g_ref, w_ref, qn_ref, kn_ref, kvn_ref, wuk_ref, wuv_ref = (
        next(it) for _ in range(9))
    if rope:
        ca_ref, sa_ref, cb_ref, sb_ref = (next(it) for _ in range(4))
    if emit_new:
        wo_ref = next(it)
    for _ in range(n_alias):
        next(it)
    cast_in = [next(it) for _ in range(n_cast)]
    if emit_new:
        xo_o, nk_o, nv_o, nckv_o, nkr_o = (next(it) for _ in range(1 + N_NEW))
        cast_out = [next(it) for _ in range(n_cast)]
    qa_o, ka_o, va_o, qb_o, kb_o, vb_o = (next(it) for _ in range(6))
    if emit_new:
        vae_s, vbe_s, oa_o, ob_o = (next(it) for _ in range(4))
    else:
        cast_out = [next(it) for _ in range(n_cast)]
    _cast_chunks(cast_in, cast_out)

    for r in range(x_ref.shape[0] // PROJ_ROWS):
        rows = slice(r * PROJ_ROWS, (r + 1) * PROJ_ROWS)

        h = _rms(x_ref[rows, :]) * g_ref[...]
        h = h * (1.0 + mod_ref[1:2, :]) + mod_ref[0:1, :]
        hb = h.astype(BF16)

        def rope_a(t):
            if not rope:
                return t
            return t * ca_ref[rows, :] + pltpu.roll(t, HEAD_DIM // 2, 1) * sa_ref[rows, :]

        def rope_b(t):
            if not rope:
                return t
            return t * cb_ref[rows, :] + pltpu.roll(t, KR_PAD // 2, 1) * sb_ref[rows, :]

        p = _dot(hb, w_ref[...])

        pq = p[:, C_QA:C_QA + W_QA]
        qn = qn_ref[...]
        for i in range(N_HEADS_A):
            sl = slice(i * HEAD_DIM, (i + 1) * HEAD_DIM)
            t = rope_a(_rms(pq[:, sl]) * qn)
            qa_o[rows, sl] = (t * SCALE_A).astype(BF16)

        pk = p[:, C_KA:C_KA + W_KA]
        pv = p[:, C_VA:C_VA + W_KA]
        kn = kn_ref[...]
        for i in range(N_KV_A):
            sl = slice(i * HEAD_DIM, (i + 1) * HEAD_DIM)
            t = _rms(pk[:, sl]) * kn
            if emit_new:
                nk_o[r, :, i, :] = t
            ka_o[rows, sl] = rope_a(t).astype(BF16)
        va_o[rows, :] = pv.astype(BF16)
        if emit_new:
            for i in range(N_KV_A):
                nv_o[r, :, i, :] = pv[:, i * HEAD_DIM:(i + 1) * HEAD_DIM]

        pqb = p[:, C_QB:C_QB + W_QB_PAD]
        for i in range(N_HEADS_B):
            s0 = i * QB_PAD
            qb_o[rows, s0:s0 + NOPE_B] = (pqb[:, s0:s0 + NOPE_B] * SCALE_B).astype(BF16)
            t = rope_b(pqb[:, s0 + NOPE_B:s0 + QB_PAD])
            qb_o[rows, s0 + NOPE_B:s0 + QB_PAD] = (t * SCALE_B).astype(BF16)

        pc = p[:, C_CKV:C_CKV + KV_RANK]
        ckv = _rms(pc) * kvn_ref[...]
        pkr = p[:, C_KR:C_KR + KR_PAD]
        if emit_new:
            half = ROPE_B // 2
            nckv_o[r] = ckv
            nkr_o[r] = jnp.concatenate([pkr[:, 0:half], pkr[:, 2 * half:3 * half]], axis=-1)
        ckvb = ckv.astype(BF16)
        krb = rope_b(pkr).astype(BF16)
        kno = _dot(ckvb, wuk_ref[...])
        vb_o[rows, :] = _dot(ckvb, wuv_ref[...]).astype(BF16)
        for i in range(N_HEADS_B):
            s0 = i * QB_PAD
            kb_o[rows, s0:s0 + NOPE_B] = kno[:, i * NOPE_B:(i + 1) * NOPE_B].astype(BF16)
            kb_o[rows, s0 + NOPE_B:s0 + QB_PAD] = krb

        if emit_new:
            _fill_v_ext(vae_s, rows, va_o.at[rows, :], N_KV_A, HEAD_DIM)
            _fill_v_ext(vbe_s, rows, vb_o.at[rows, :], N_HEADS_B, VDIM_B)
            for hq in range(N_HEADS_A):
                g = hq // GROUP_A
                qsl = slice(hq * HEAD_DIM, (hq + 1) * HEAD_DIM)
                oa_o[rows, qsl] = _attend(
                    qa_o[rows, qsl], ka_o[rows, g * HEAD_DIM:(g + 1) * HEAD_DIM],
                    vae_s[rows, 2 * g * HEAD_DIM:(2 * g + 2) * HEAD_DIM], HEAD_DIM)
            for hq in range(N_HEADS_B):
                qsl = slice(hq * QB_PAD, (hq + 1) * QB_PAD)
                ob_o[rows, hq * VDIM_B:(hq + 1) * VDIM_B] = _attend(
                    qb_o[rows, qsl], kb_o[rows, qsl],
                    vbe_s[rows, 2 * hq * VDIM_B:(2 * hq + 2) * VDIM_B], VDIM_B)
            y = _dot(oa_o[rows, :], wo_ref[0:W_QA, :]) + _dot(ob_o[rows, :], wo_ref[W_QA:, :])
            xo_o[rows, :] = x_ref[rows, :] + mod_ref[2:3, :] * y


def _proj(x, layer, mods, mod_map, g, w_in, qn, kn, kvn, wuk, wuv, rope_tabs=None,
          new_shape=None, new_bufs=None, w_o=None, cast=(), tm=PROJ_TM):
    t = x.shape[0]
    rope = rope_tabs is not None
    emit_new = new_shape is not None

    def row(width):
        return pl.BlockSpec((tm, width), lambda i: (i, 0))

    in_specs = [
        row(D_MODEL),
        _mod_spec(layer, mod_map, D_MODEL),
        _layer_spec((1, D_MODEL), layer),
        pl.BlockSpec((D_MODEL, IN_COLS_PAD), lambda *_: (0, 0), pipeline_mode=pl.Buffered(1)),
        _layer_spec((1, HEAD_DIM), layer),
        _layer_spec((1, HEAD_DIM), layer),
        _layer_spec((1, KV_RANK), layer),
        _layer_spec((KV_RANK, W_KNOPE), layer),
        _layer_spec((KV_RANK, W_VB), layer),
    ]
    args = [x, mods, g, w_in, qn, kn, kvn, wuk, wuv]
    if rope:
        seq_tiles = rope_tabs[0].shape[0] // tm
        for tab in rope_tabs:
            in_specs.append(pl.BlockSpec((tm, tab.shape[1]), lambda i: (i % seq_tiles, 0)))
            args.append(tab)
    qkv_widths = [W_QA, W_KA, W_KA, W_QB_PAD, W_QB_PAD, W_VB]
    scratch = []
    aliases = {}
    n_alias = 0
    if not emit_new:
        out_specs = [row(w) for w in qkv_widths]
        out_shape = [jax.ShapeDtypeStruct((t, w), BF16) for w in qkv_widths]
    else:
        in_specs.append(pl.BlockSpec((W_QA + W_VB, D_MODEL), lambda i: (0, 0),
                                     pipeline_mode=pl.Buffered(1)))
        args.append(w_o)
        out_specs = [row(D_MODEL)]
        out_shape = [jax.ShapeDtypeStruct((t, D_MODEL), F32)]
        scratch = [pltpu.VMEM((tm, w), BF16)
                   for w in qkv_widths + [2 * W_KA, 2 * W_VB, W_QA, W_VB]]
        batch, seq = new_shape
        assert seq == PROJ_ROWS and t == batch * seq
        nb = tm // seq
        tails = [(N_KV_A, HEAD_DIM), (N_KV_A, HEAD_DIM), (KV_RANK,), (ROPE_B,)]
        if new_bufs is not None:
            n_alias = N_NEW
            for k, buf in enumerate(new_bufs):
                aliases[len(args)] = len(out_shape) + k
                in_specs.append(pl.BlockSpec(memory_space=pl.ANY))
                args.append(buf)
        for tail in tails:
            zeros = (0,) * len(tail)
            out_specs.append(pl.BlockSpec((nb, None, seq) + tail,
                                          lambda i, zeros=zeros: (i, layer, 0) + zeros))
            out_shape.append(jax.ShapeDtypeStruct((batch, DEPTH, seq) + tail, F32))
    cast_specs_in, cast_specs_out, cast_shapes, cast_srcs = _side_cast(
        cast, t // tm, lambda i: i)
    outs = pl.pallas_call(
        functools.partial(_proj_kernel, rope=rope, emit_new=emit_new, n_alias=n_alias,
                          n_cast=len(cast)),
        grid=(t // tm,),
        in_specs=in_specs + cast_specs_in,
        out_specs=out_specs + cast_specs_out,
        out_shape=out_shape + cast_shapes,
        input_output_aliases=aliases,
        scratch_shapes=scratch,
        compiler_params=_cparams(1),
        name="proj_lat" if rope else "proj_attn_ctx",
    )(*args, *cast_srcs)
    n_main = len(out_shape)
    return outs[:n_main], outs[n_main:]


def _attend(q, k, v_ext, dv):
    s = _dot_t(q, k)
    p = jnp.exp2(s - s.max(axis=-1, keepdims=True))
    o = _dot(p.astype(BF16), v_ext)
    return (o[:, 0:dv] / o[:, dv:2 * dv]).astype(BF16)


def _attend_parts(q, k_parts, v_ext_ref, dv):
    s = [_dot_t(q, k) for k in k_parts]
    m = s[0].max(axis=-1, keepdims=True)
    for sp in s[1:]:
        m = jnp.maximum(m, sp.max(axis=-1, keepdims=True))
    o = None
    r0 = 0
    for sp in s:
        r1 = r0 + sp.shape[1]
        op = _dot(jnp.exp2(sp - m).astype(BF16), v_ext_ref[r0:r1, :])
        o = op if o is None else o + op
        r0 = r1
    return (o[:, 0:dv] / o[:, dv:2 * dv]).astype(BF16)


def _fill_v_ext(v_ext_ref, rows, v_ref, n_kv, dv):
    ones = jnp.ones((v_ref.shape[0], dv), BF16)
    for jk in range(n_kv):
        v_ext_ref[rows, 2 * jk * dv:(2 * jk + 1) * dv] = v_ref[:, jk * dv:(jk + 1) * dv]
        v_ext_ref[rows, (2 * jk + 1) * dv:(2 * jk + 2) * dv] = ones


ATTN_TQ_A = 1024
ATTN_TQ_B = 512
ATTN_HEADS_PER_STEP = 8
ATTN_ROWS = 256


def _attn_lat_kernel(*refs, n_q, n_kv, dq, dv, past, n_cast):
    q_ref, kc_ref, kn_ref, vc_ref, vn_ref = refs[:5]
    o_ref = refs[5 + n_cast]
    v_ref = refs[6 + 2 * n_cast]
    k_scratch = refs[7 + 2 * n_cast:]
    _attn_lat_body(q_ref, kc_ref, kn_ref, vc_ref, vn_ref, o_ref, v_ref, k_scratch,
                   n_q=n_q, n_kv=n_kv, dq=dq, dv=dv, past=past)
    _cast_chunks(refs[5:5 + n_cast], refs[6 + n_cast:6 + 2 * n_cast])


def _attn_lat_body(q_ref, kc_ref, kn_ref, vc_ref, vn_ref, o_ref, v_ref, k_scratch,
                   *, n_q, n_kv, dq, dv, past):
    @pl.when(pl.program_id(2) == 0)
    def _():
        _fill_v_ext(v_ref, slice(0, past), vc_ref, n_kv, dv)
        _fill_v_ext(v_ref, slice(past, None), vn_ref, n_kv, dv)
        if k_scratch:
            k_scratch[0][0:past, :] = kc_ref[...]
            k_scratch[0][past:, :] = kn_ref[...]

    rows = ATTN_ROWS
    for j in range(n_q):
        jk = j * n_kv // n_q
        ksl = slice(jk * dq, (jk + 1) * dq)
        vsl = slice(2 * jk * dv, (2 * jk + 2) * dv)
        for r in range(q_ref.shape[0] // rows):
            rsl = slice(r * rows, (r + 1) * rows)
            q = q_ref[rsl, j * dq:(j + 1) * dq]
            if k_scratch:
                o = _attend(q, k_scratch[0][:, ksl], v_ref[:, vsl], dv)
            else:
                o = _attend_parts(q, [kc_ref[:, ksl], kn_ref[:, ksl]], v_ref.at[:, vsl], dv)
            o_ref[rsl, j * dv:(j + 1) * dv] = o


def _attn_lat(q, layer, kc, kn, vc, vn, n_units, n_q, n_kv, dq, dv, name, join_k, tq, cast=()):
    nb, seq, _ = q.shape
    past = kc.shape[2]
    scratch = [pltpu.VMEM((past + seq, n_kv * 2 * dv), BF16)]
    if join_k:
        scratch.append(pltpu.VMEM((past + seq, n_kv * dq), BF16))
    n_qt = seq // tq
    cast_specs_in, cast_specs_out, cast_shapes, cast_srcs = _side_cast(
        cast, nb * n_units * n_qt, lambda b, u, i: (b * n_units + u) * n_qt + i)
    outs = pl.pallas_call(
        functools.partial(_attn_lat_kernel, n_q=n_q, n_kv=n_kv, dq=dq, dv=dv, past=past,
                          n_cast=len(cast)),
        grid=(nb, n_units, n_qt),
        in_specs=[
            pl.BlockSpec((None, tq, n_q * dq), lambda b, u, i: (b, i, u)),
            pl.BlockSpec((None, None, past, n_kv * dq), lambda b, u, i: (layer, b, 0, u)),
            pl.BlockSpec((None, seq, n_kv * dq), lambda b, u, i: (b, 0, u)),
            pl.BlockSpec((None, None, past, n_kv * dv), lambda b, u, i: (layer, b, 0, u)),
            pl.BlockSpec((None, seq, n_kv * dv), lambda b, u, i: (b, 0, u)),
        ] + cast_specs_in,
        out_specs=[pl.BlockSpec((None, tq, n_q * dv), lambda b, u, i: (b, i, u))] + cast_specs_out,
        out_shape=[jax.ShapeDtypeStruct((nb, seq, n_units * n_q * dv), BF16)] + cast_shapes,
        scratch_shapes=scratch,
        compiler_params=_cparams(3),
        name=name,
    )(q, kc, kn, vc, vn, *cast_srcs)
    return outs[0], outs[1:]


OUT_TM = 512


def _outproj_kernel(x_ref, oa_ref, ob_ref, mod_ref, w_ref, y_ref):
    y = _dot(oa_ref[...], w_ref[0:W_QA, :]) + _dot(ob_ref[...], w_ref[W_QA:, :])
    y_ref[...] = x_ref[...] + mod_ref[2:3, :] * y


def _outproj(x, oa, ob, layer, mods, mod_map, w_o, tm=OUT_TM):
    t = x.shape[0]
    return pl.pallas_call(
        _outproj_kernel,
        grid=(t // tm,),
        in_specs=[
            pl.BlockSpec((tm, D_MODEL), lambda i: (i, 0)),
            pl.BlockSpec((tm, W_QA), lambda i: (i, 0)),
            pl.BlockSpec((tm, W_VB), lambda i: (i, 0)),
            _mod_spec(layer, mod_map, D_MODEL),
            pl.BlockSpec((W_QA + W_VB, D_MODEL), lambda i: (0, 0), pipeline_mode=pl.Buffered(1)),
        ],
        out_specs=pl.BlockSpec((tm, D_MODEL), lambda i: (i, 0)),
        out_shape=jax.ShapeDtypeStruct((t, D_MODEL), F32),
        compiler_params=_cparams(1),
        name="outproj",
    )(x, oa, ob, mods, w_o)


FFN_TM = 1024
FFN_TF = 512
FFN_TN = 512
FFN_ROWS = 256


def _ffn_kernel(*refs, nf, tf, rc, n_tiles, n_cast):
    xr_ref, xc_ref, mod_ref, modc_ref, g_ref, wg_ref, wu_ref, wd_ref = refs[:8]
    cast_in = refs[8:8 + n_cast]
    y_ref = refs[8 + n_cast]
    cast_out = refs[9 + n_cast:9 + 2 * n_cast]
    h_ref, t_ref = refs[9 + 2 * n_cast:]
    i = pl.program_id(0)
    j = pl.program_id(1)

    def cast_chunks():
        _cast_chunks(cast_in, cast_out)
    cur = (i + 1) % 2
    nxt = i % 2

    def norm_chunk():
        h = _rms(xr_ref[...]) * g_ref[...]
        h = h * (1.0 + mod_ref[4:5, :]) + mod_ref[3:4, :]
        r0 = pl.multiple_of((j - nf) * rc, rc)
        h_ref[nxt, pl.ds(r0, rc), :] = h.astype(BF16)

    row_blocks = [slice(m, m + FFN_ROWS) for m in range(0, y_ref.shape[0], FFN_ROWS)]

    def down_proj():
        for rows in row_blocks:
            d = _dot(t_ref[0, rows, :], wd_ref[0:tf, :])
            for k in range(1, nf):
                d = d + _dot(t_ref[k, rows, :], wd_ref[k * tf:(k + 1) * tf, :])
            y_ref[rows, :] = xc_ref[rows, :] + modc_ref[5:6, :] * d

    @pl.when((j < nf) & (i == 0))
    def _():
        cast_chunks()

    @pl.when((j < nf) & (i >= 1))
    def _():
        cast_chunks()
        for rows in row_blocks:
            hb = h_ref[cur, rows, :]
            a = _dot(hb, wg_ref[...])
            b = _dot(hb, wu_ref[...])
            t_ref[j, rows, :] = (a * jax.nn.sigmoid(a) * b).astype(BF16)

    @pl.when((j >= nf) & (i >= 1) & (i < n_tiles))
    def _():
        cast_chunks()
        norm_chunk()
        down_proj()

    @pl.when((j >= nf) & (i == 0))
    def _():
        cast_chunks()
        norm_chunk()

    @pl.when((j >= nf) & (i == n_tiles))
    def _():
        cast_chunks()
        down_proj()


def _ffn(x, layer, mods, mod_map, g, wg, wu, wd, cast=(), tm=FFN_TM, tf=FFN_TF, tn=FFN_TN):
    t = x.shape[0]
    n_tiles = t // tm
    nf = D_FF // tf
    nn = D_MODEL // tn
    cast_specs_in, cast_specs_out, cast_shapes, cast_srcs = _side_cast(
        cast, (n_tiles + 1) * (nf + nn), lambda i, j: i * (nf + nn) + j)
    rc = tm // nn
    col = lambda j: jnp.maximum(j - nf, 0)
    comp = lambda i: jnp.maximum(i - 1, 0)
    norm = lambda i: jnp.minimum(i, n_tiles - 1)
    ccol = lambda i, j: jnp.where(i == 0, 0, col(j))
    ftile = lambda i, j: jnp.where(i == 0, 0, jnp.minimum(j, nf - 1))
    outs = pl.pallas_call(
        functools.partial(_ffn_kernel, nf=nf, tf=tf, rc=rc, n_tiles=n_tiles, n_cast=len(cast)),
        grid=(n_tiles + 1, nf + nn),
        in_specs=[
            pl.BlockSpec((rc, D_MODEL), lambda i, j: (norm(i) * nn + col(j), 0)),
            pl.BlockSpec((tm, tn), lambda i, j: (comp(i), ccol(i, j))),
            pl.BlockSpec((None, None, N_MOD, D_MODEL),
                         lambda i, j: (layer, mod_map(norm(i)), 0, 0)),
            pl.BlockSpec((None, None, N_MOD, tn),
                         lambda i, j: (layer, mod_map(comp(i)), 0, ccol(i, j))),
            _layer_spec((1, D_MODEL), layer),
            pl.BlockSpec((D_MODEL, tf), lambda i, j: (0, ftile(i, j))),
            pl.BlockSpec((D_MODEL, tf), lambda i, j: (0, ftile(i, j))),
            pl.BlockSpec((D_FF, tn), lambda i, j: (0, ccol(i, j))),
        ] + cast_specs_in,
        out_specs=[pl.BlockSpec((tm, tn), lambda i, j: (comp(i), ccol(i, j)))] + cast_specs_out,
        out_shape=[jax.ShapeDtypeStruct((t, D_MODEL), F32)] + cast_shapes,
        scratch_shapes=[pltpu.VMEM((2, tm, D_MODEL), BF16), pltpu.VMEM((nf, tm, tf), BF16)],
        compiler_params=_cparams(2),
        name="ffn",
    )(x, x, mods, mods, g, wg, wu, wd, *cast_srcs)
    return outs[0], outs[1:]


NORM_TM = 512


def _final_norm_kernel(x_ref, g_ref, y_ref):
    y_ref[...] = _rms(x_ref[...]) * g_ref[...]


def _final_norm(x, g, tm=NORM_TM):
    t = x.shape[0]
    return pl.pallas_call(
        _final_norm_kernel,
        grid=(t // tm,),
        in_specs=[pl.BlockSpec((tm, D_MODEL), lambda i: (i, 0)),
                  pl.BlockSpec((1, D_MODEL), lambda i: (0, 0))],
        out_specs=pl.BlockSpec((tm, D_MODEL), lambda i: (i, 0)),
        out_shape=jax.ShapeDtypeStruct((t, D_MODEL), F32),
        compiler_params=_cparams(1),
        name="final_norm",
    )(x, g)


def _rope_tables(n_tokens):
    n_rows = n_tokens // GRID_W
    row = jnp.repeat(jnp.arange(n_rows, dtype=F32), GRID_W)
    col = jnp.tile(jnp.arange(GRID_W, dtype=F32), n_rows)

    def cs(dim):
        n_freq = dim // 4
        inv = ROPE_THETA ** (-jnp.arange(n_freq, dtype=F32) / n_freq)
        ang = jnp.concatenate([row[:, None] * inv, col[:, None] * inv], axis=-1)
        return jnp.cos(ang), jnp.sin(ang)

    cos_a, sin_a = cs(HEAD_DIM)
    cos_b, sin_b = cs(ROPE_B)
    one = jnp.ones_like(cos_b)
    zero = jnp.zeros_like(sin_b)
    return (jnp.concatenate([cos_a, cos_a], axis=-1),
            jnp.concatenate([-sin_a, sin_a], axis=-1),
            jnp.concatenate([cos_b, one, cos_b, one], axis=-1),
            jnp.concatenate([-sin_b, zero, sin_b, zero], axis=-1))


def kernel(x_prompt, x_sample, cache_k_a, cache_v_a, cache_ckv_b, cache_krope_b, c, c_ctx, w_ada, b_ada, norm_attn, norm_ffn, w_in, qnorm_a, knorm_a, kvnorm_b, w_uk_b, w_uv_b, w_o, w_gate, w_up, w_down, norm_final):
    batch, seq, _ = x_prompt.shape
    dec_batch, dec_seq, _ = x_sample.shape
    assert dec_batch + 1 <= MOD_ROWS

    win_l = _wprep(jnp.pad(w_in[0].astype(BF16), ((0, 0), (0, IN_COLS_128 - IN_COLS))))
    wuk_b = w_uk_b.astype(BF16)
    wuv_b = w_uv_b.astype(BF16)
    wo_l = w_o[0].astype(BF16)
    rope_tabs = _rope_tables(dec_seq)

    cond = jnp.concatenate(
        [c_ctx[None, :], c, jnp.zeros((MOD_ROWS - 1 - dec_batch, D_MODEL), F32)], axis=0)
    mods = _adaln(cond, w_ada, b_ada).reshape(DEPTH, MOD_ROWS, N_MOD, D_MODEL)

    kc_a, vc_a, kc_b, vc_b = _cache_prep(
        cache_k_a, cache_v_a, cache_ckv_b, cache_krope_b, wuk_b, wuv_b)

    xp = x_prompt.reshape(batch * seq, D_MODEL)
    xs = x_sample.reshape(dec_batch * dec_seq, D_MODEL)
    g1 = norm_attn.reshape(DEPTH, 1, D_MODEL)
    g2 = norm_ffn.reshape(DEPTH, 1, D_MODEL)
    qn = qnorm_a.reshape(DEPTH, 1, HEAD_DIM)
    kn = knorm_a.reshape(DEPTH, 1, HEAD_DIM)
    kvn = kvnorm_b.reshape(DEPTH, 1, KV_RANK)
    ctx_mod = lambda i: 0

    def lat_mod(tm):
        return lambda i: 1 + i // (dec_seq // tm)

    r3 = lambda a: a.reshape(dec_batch, dec_seq, a.shape[-1])
    hb = ATTN_HEADS_PER_STEP
    new_bufs = None

    for l in range(DEPTH):
        first, nxt = l == 0, l + 1 < DEPTH

        (xp, *new_bufs), _ = _proj(
            xp, l, mods, ctx_mod, g1, win_l, qn, kn, kvn, wuk_b, wuv_b,
            new_shape=(batch, seq), new_bufs=new_bufs, w_o=wo_l)

        (qa, ka, va, qb, kb, vb), cast0 = _proj(
            xs, l, mods, lat_mod(PROJ_TM), g1, win_l, qn, kn, kvn, wuk_b, wuv_b,
            rope_tabs=rope_tabs, cast=((w_gate, 0),) if first else ())
        oa, cast1 = _attn_lat(r3(qa), l, kc_a, r3(ka), vc_a, r3(va),
                              1, N_HEADS_A, N_KV_A, HEAD_DIM, HEAD_DIM, "attn_lat_a", join_k=True,
                              tq=ATTN_TQ_A,
                              cast=((w_up, 0), (w_down, 0)) if first else ())
        ob, _ = _attn_lat(r3(qb), l, kc_b, r3(kb), vc_b, r3(vb),
                          N_HEADS_B // hb, hb, hb, QB_PAD, VDIM_B, "attn_lat_b", join_k=False,
                          tq=ATTN_TQ_B)
        xs = _outproj(xs, oa.reshape(-1, W_QA), ob.reshape(-1, W_VB), l, mods, lat_mod(OUT_TM), wo_l)
        if first:
            (wg_l,), (wu_l, wd_l) = cast0, cast1

        xp, gui_next = _ffn(xp, l, mods, ctx_mod, g2, wg_l, wu_l, wd_l,
                            cast=((w_gate, l + 1), (w_up, l + 1), (w_in, l + 1)) if nxt else ())
        xs, do_next = _ffn(xs, l, mods, lat_mod(FFN_TM), g2, wg_l, wu_l, wd_l,
                           cast=((w_down, l + 1), (w_o, l + 1)) if nxt else ())
        if nxt:
            (wg_l, wu_l, win_next), (wd_l, wo_l) = gui_next, do_next
            win_l = _wprep(win_next)

    gf = norm_final.reshape(1, D_MODEL)
    y_prompt = _final_norm(xp, gf).reshape(batch, seq, D_MODEL)
    y_sample = _final_norm(xs, gf).reshape(dec_batch, dec_seq, D_MODEL)
    new_k, new_v, new_ckv, new_kr = new_bufs
    return (y_prompt, y_sample,
            new_k, new_v, new_ckv, new_kr)
```

```python
import functools

import jax
import jax.numpy as jnp
import numpy as np
from jax import lax
from jax.experimental import pallas as pl
from jax.experimental.pallas import tpu as pltpu

D_MODEL = 2048
DEPTH = 4
GRID_W = 64
HEAD_DIM = 128
N_HEADS_A = 8
N_KV_A = 2
GROUP_A = N_HEADS_A // N_KV_A
N_HEADS_B = 8
NOPE_B = 128
ROPE_B = 64
VDIM_B = 128
KV_RANK = 256
D_FF = 5632
ROPE_THETA = 10000.0
EPS = 1e-6

W_QA = N_HEADS_A * HEAD_DIM
W_KA = N_KV_A * HEAD_DIM
QB_DIM = NOPE_B + ROPE_B
W_QB = N_HEADS_B * QB_DIM
QB_PAD = 256
KR_PAD = 128
W_QB_PAD = N_HEADS_B * QB_PAD
W_KNOPE = N_HEADS_B * NOPE_B
W_VB = N_HEADS_B * VDIM_B
N_MOD = 6
MOD_ROWS = 8

O_QB = W_QA + 2 * W_KA
O_CKV = O_QB + W_QB
O_KR = O_CKV + KV_RANK
IN_COLS = O_KR + ROPE_B

C_QA = 0
C_KA = C_QA + W_QA
C_VA = C_KA + W_KA
C_QB = C_VA + W_KA
C_CKV = C_QB + W_QB_PAD
C_KR = C_CKV + KV_RANK
IN_COLS_PAD = C_KR + KR_PAD

LOG2_E = 1.4426950408889634
SCALE_A = HEAD_DIM ** -0.5 * LOG2_E
SCALE_B = QB_DIM ** -0.5 * LOG2_E

VMEM_LIMIT = 60 * 1024 * 1024

BF16 = jnp.bfloat16
F32 = jnp.float32


def _cparams(n_axes):
    return pltpu.CompilerParams(
        dimension_semantics=("arbitrary",) * n_axes, vmem_limit_bytes=VMEM_LIMIT)


def _layer_spec(shape, layer):
    nd = len(shape)
    return pl.BlockSpec((None,) + shape, lambda *_: (layer,) + (0,) * nd,
                        pipeline_mode=pl.Buffered(1))


def _mod_spec(layer, mod_map, width, col_map=None):
    if col_map is None:
        return pl.BlockSpec((None, None, N_MOD, width), lambda i, *_: (layer, mod_map(i), 0, 0))
    return pl.BlockSpec((None, None, N_MOD, width),
                        lambda i, j: (layer, mod_map(i), 0, col_map(j)))


def _dot(a, b):
    return jnp.dot(a, b, preferred_element_type=F32)


def _dot_t(a, b):
    return lax.dot_general(a, b, (((1,), (1,)), ((), ())), preferred_element_type=F32)


def _rms(x):
    return x * lax.rsqrt(jnp.mean(x * x, axis=-1, keepdims=True) + EPS)


def _cast_rows(n_rows, n_steps):
    rows = 16
    while n_rows % rows or n_rows // rows > n_steps:
        rows += 16
    return rows


def _side_cast(cast, n_steps, step_of):
    in_specs, out_specs, shapes = [], [], []
    for src, src_layer in cast:
        _, n_rows, n_cols = src.shape
        rows = _cast_rows(n_rows, n_steps)
        chunk = lambda *ids, last=n_rows // rows - 1: jnp.minimum(step_of(*ids), last)
        in_specs.append(pl.BlockSpec(
            (None, rows, n_cols), lambda *ids, c=chunk, l=src_layer: (l, c(*ids), 0)))
        out_specs.append(pl.BlockSpec((rows, n_cols), lambda *ids, c=chunk: (c(*ids), 0)))
        shapes.append(jax.ShapeDtypeStruct((n_rows, n_cols), BF16))
    return in_specs, out_specs, shapes, [src for src, _ in cast]


def _cast_chunks(cast_in, cast_out):
    for src, dst in zip(cast_in, cast_out):
        dst[...] = src[...].astype(BF16)


ADA_TN = 1024


def _adaln_kernel(s_ref, w_ref, b_ref, o_ref):
    s = s_ref[...]
    s = s * jax.nn.sigmoid(s)
    o_ref[...] = _dot(s.astype(BF16), w_ref[...].astype(BF16)) + b_ref[...]


def _adaln(s, w_ada, b_ada):
    n = N_MOD * D_MODEL
    return pl.pallas_call(
        _adaln_kernel,
        grid=(DEPTH, n // ADA_TN),
        in_specs=[
            pl.BlockSpec((MOD_ROWS, D_MODEL), lambda l, j: (0, 0)),
            pl.BlockSpec((None, D_MODEL, ADA_TN), lambda l, j: (l, 0, j)),
            pl.BlockSpec((None, 1, ADA_TN), lambda l, j: (l, 0, j)),
        ],
        out_specs=pl.BlockSpec((None, MOD_ROWS, ADA_TN), lambda l, j: (l, 0, j)),
        out_shape=jax.ShapeDtypeStruct((DEPTH, MOD_ROWS, n), F32),
        compiler_params=_cparams(2),
        name="adaln",
    )(s, w_ada, b_ada.reshape(DEPTH, 1, n))


WPREP_TR = 512
PAIR_IN = 2 * QB_DIM
PAIR_OUT = 2 * QB_PAD


def _wprep_kernel(w_ref, pq_ref, pk_ref, o_ref):
    o_ref[:, 0:O_QB] = w_ref[:, 0:O_QB]
    pq = pq_ref[...]
    for i in range(N_HEADS_B // 2):
        src = w_ref[:, O_QB + i * PAIR_IN:O_QB + (i + 1) * PAIR_IN]
        o_ref[:, C_QB + i * PAIR_OUT:C_QB + (i + 1) * PAIR_OUT] = _dot(src, pq).astype(BF16)
    o_ref[:, C_CKV:C_CKV + KV_RANK] = w_ref[:, O_CKV:O_CKV + KV_RANK]
    o_ref[:, C_KR:C_KR + KR_PAD] = _dot(w_ref[:, O_KR:O_KR + ROPE_B], pk_ref[...]).astype(BF16)


def _rope_pad_src():
    half = ROPE_B // 2
    src = np.full((KR_PAD,), -1, np.int32)
    src[0:half] = np.arange(half)
    src[2 * half:3 * half] = half + np.arange(half)
    return src


def _selection(src, n_in):
    return (jnp.arange(n_in, dtype=jnp.int32)[:, None] == jnp.asarray(src)[None, :]).astype(BF16)


def _wprep(w_in):
    rsrc = _rope_pad_src()
    head = np.concatenate([np.arange(NOPE_B), np.where(rsrc >= 0, NOPE_B + rsrc, -1)])
    pair = np.concatenate([head, np.where(head >= 0, QB_DIM + head, -1)]).astype(np.int32)
    pq = _selection(pair, PAIR_IN)
    pk = _selection(rsrc, ROPE_B)
    return pl.pallas_call(
        _wprep_kernel,
        grid=(DEPTH, D_MODEL // WPREP_TR),
        in_specs=[
            pl.BlockSpec((None, WPREP_TR, IN_COLS), lambda l, r: (l, r, 0)),
            pl.BlockSpec((PAIR_IN, PAIR_OUT), lambda l, r: (0, 0)),
            pl.BlockSpec((ROPE_B, KR_PAD), lambda l, r: (0, 0)),
        ],
        out_specs=pl.BlockSpec((None, WPREP_TR, IN_COLS_PAD), lambda l, r: (l, r, 0)),
        out_shape=jax.ShapeDtypeStruct((DEPTH, D_MODEL, IN_COLS_PAD), BF16),
        compiler_params=_cparams(2),
        name="wprep",
    )(w_in, pq, pk)


def _cache_kernel(ka_ref, va_ref, ckv_ref, kr_ref, wuk_ref, wuv_ref, pk_ref,
                  ka_o, va_o, kb_o, vb_o):
    ka_o[...] = ka_ref[...].astype(BF16)
    va_o[...] = va_ref[...].astype(BF16)
    ckv = ckv_ref[...].astype(BF16)
    kn = _dot(ckv, wuk_ref[...])
    vb_o[...] = _dot(ckv, wuv_ref[...]).astype(BF16)
    kr = _dot(kr_ref[...].astype(BF16), pk_ref[...]).astype(BF16)
    for h in range(N_HEADS_B):
        kb_o[:, h * QB_PAD:h * QB_PAD + NOPE_B] = kn[:, h * NOPE_B:(h + 1) * NOPE_B].astype(BF16)
        kb_o[:, h * QB_PAD + NOPE_B:(h + 1) * QB_PAD] = kr


def _cache_prep(cache_k_a, cache_v_a, cache_ckv_b, cache_krope_b, wuk, wuv):
    nb, _, past = cache_ckv_b.shape[:3]
    ka = cache_k_a.reshape(nb, DEPTH, past, W_KA)
    va = cache_v_a.reshape(nb, DEPTH, past, W_KA)
    pk = _selection(_rope_pad_src(), ROPE_B)

    def in4(width):
        return pl.BlockSpec((None, None, past, width), lambda l, b: (b, l, 0, 0))

    def out4(width):
        return pl.BlockSpec((None, None, past, width), lambda l, b: (l, b, 0, 0))

    def w3(width):
        return pl.BlockSpec((None, KV_RANK, width), lambda l, b: (l, 0, 0))

    return pl.pallas_call(
        _cache_kernel,
        grid=(DEPTH, nb),
        in_specs=[in4(W_KA), in4(W_KA), in4(KV_RANK), in4(ROPE_B), w3(W_KNOPE), w3(W_VB),
                  pl.BlockSpec((ROPE_B, KR_PAD), lambda l, b: (0, 0))],
        out_specs=[out4(W_KA), out4(W_KA), out4(W_QB_PAD), out4(W_VB)],
        out_shape=[
            jax.ShapeDtypeStruct((DEPTH, nb, past, W_KA), BF16),
            jax.ShapeDtypeStruct((DEPTH, nb, past, W_KA), BF16),
            jax.ShapeDtypeStruct((DEPTH, nb, past, W_QB_PAD), BF16),
            jax.ShapeDtypeStruct((DEPTH, nb, past, W_VB), BF16),
        ],
        compiler_params=_cparams(2),
        name="cache_prep",
    )(ka, va, cache_ckv_b, cache_krope_b, wuk, wuv, pk)


PROJ_TM = 512
PROJ_ROWS = 256
N_NEW = 4


def _proj_kernel(*refs, rope, emit_new, n_alias, n_cast):
    it = iter(refs)
    x_ref, mod_ref, g_ref, w_ref, qn_ref, kn_ref, kvn_ref, wuk_ref, wuv_ref = (
        next(it) for _ in range(9))
    if rope:
        ca_ref, sa_ref, cb_ref, sb_ref = (next(it) for _ in range(4))
    if emit_new:
        wo_ref = next(it)
    for _ in range(n_alias):
        next(it)
    cast_in = [next(it) for _ in range(n_cast)]
    if emit_new:
        xo_o, nk_o, nv_o, nckv_o, nkr_o = (next(it) for _ in range(1 + N_NEW))
        cast_out = [next(it) for _ in range(n_cast)]
    qa_o, ka_o, va_o, qb_o, kb_o, vb_o = (next(it) for _ in range(6))
    if emit_new:
        vae_s, vbe_s, oa_o, ob_o = (next(it) for _ in range(4))
    else:
        cast_out = [next(it) for _ in range(n_cast)]
    _cast_chunks(cast_in, cast_out)

    for r in range(x_ref.shape[0] // PROJ_ROWS):
        rows = slice(r * PROJ_ROWS, (r + 1) * PROJ_ROWS)

        h = _rms(x_ref[rows, :]) * g_ref[...]
        h = h * (1.0 + mod_ref[1:2, :]) + mod_ref[0:1, :]
        hb = h.astype(BF16)

        def rope_a(t):
            if not rope:
                return t
            return t * ca_ref[rows, :] + pltpu.roll(t, HEAD_DIM // 2, 1) * sa_ref[rows, :]

        def rope_b(t):
            if not rope:
                return t
            return t * cb_ref[rows, :] + pltpu.roll(t, KR_PAD // 2, 1) * sb_ref[rows, :]

        p = _dot(hb, w_ref[...])

        pq = p[:, C_QA:C_QA + W_QA]
        qn = qn_ref[...]
        for i in range(N_HEADS_A):
            sl = slice(i * HEAD_DIM, (i + 1) * HEAD_DIM)
            t = rope_a(_rms(pq[:, sl]) * qn)
            qa_o[rows, sl] = (t * SCALE_A).astype(BF16)

        pk = p[:, C_KA:C_KA + W_KA]
        pv = p[:, C_VA:C_VA + W_KA]
        kn = kn_ref[...]
        for i in range(N_KV_A):
            sl = slice(i * HEAD_DIM, (i + 1) * HEAD_DIM)
            t = _rms(pk[:, sl]) * kn
            if emit_new:
                nk_o[r, :, i, :] = t
            ka_o[rows, sl] = rope_a(t).astype(BF16)
        va_o[rows, :] = pv.astype(BF16)
        if emit_new:
            for i in range(N_KV_A):
                nv_o[r, :, i, :] = pv[:, i * HEAD_DIM:(i + 1) * HEAD_DIM]

        pqb = p[:, C_QB:C_QB + W_QB_PAD]
        for i in range(N_HEADS_B):
            s0 = i * QB_PAD
            qb_o[rows, s0:s0 + NOPE_B] = (pqb[:, s0:s0 + NOPE_B] * SCALE_B).astype(BF16)
            t = rope_b(pqb[:, s0 + NOPE_B:s0 + QB_PAD])
            qb_o[rows, s0 + NOPE_B:s0 + QB_PAD] = (t * SCALE_B).astype(BF16)

        pc = p[:, C_CKV:C_CKV + KV_RANK]
        ckv = _rms(pc) * kvn_ref[...]
        pkr = p[:, C_KR:C_KR + KR_PAD]
        if emit_new:
            half = ROPE_B // 2
            nckv_o[r] = ckv
            nkr_o[r] = jnp.concatenate([pkr[:, 0:half], pkr[:, 2 * half:3 * half]], axis=-1)
        ckvb = ckv.astype(BF16)
        krb = rope_b(pkr).astype(BF16)
        kno = _dot(ckvb, wuk_ref[...])
        vb_o[rows, :] = _dot(ckvb, wuv_ref[...]).astype(BF16)
        for i in range(N_HEADS_B):
            s0 = i * QB_PAD
            kb_o[rows, s0:s0 + NOPE_B] = kno[:, i * NOPE_B:(i + 1) * NOPE_B].astype(BF16)
            kb_o[rows, s0 + NOPE_B:s0 + QB_PAD] = krb

        if emit_new:
            _fill_v_ext(vae_s, rows, va_o.at[rows, :], N_KV_A, HEAD_DIM)
            _fill_v_ext(vbe_s, rows, vb_o.at[rows, :], N_HEADS_B, VDIM_B)
            for hq in range(N_HEADS_A):
                g = hq // GROUP_A
                qsl = slice(hq * HEAD_DIM, (hq + 1) * HEAD_DIM)
                oa_o[rows, qsl] = _attend(
                    qa_o[rows, qsl], ka_o[rows, g * HEAD_DIM:(g + 1) * HEAD_DIM],
                    vae_s[rows, 2 * g * HEAD_DIM:(2 * g + 2) * HEAD_DIM], HEAD_DIM)
            for hq in range(N_HEADS_B):
                qsl = slice(hq * QB_PAD, (hq + 1) * QB_PAD)
                ob_o[rows, hq * VDIM_B:(hq + 1) * VDIM_B] = _attend(
                    qb_o[rows, qsl], kb_o[rows, qsl],
                    vbe_s[rows, 2 * hq * VDIM_B:(2 * hq + 2) * VDIM_B], VDIM_B)
            y = _dot(oa_o[rows, :], wo_ref[0:W_QA, :]) + _dot(ob_o[rows, :], wo_ref[W_QA:, :])
            xo_o[rows, :] = x_ref[rows, :] + mod_ref[2:3, :] * y


def _proj(x, layer, mods, mod_map, g, w_in, qn, kn, kvn, wuk, wuv, rope_tabs=None,
          new_shape=None, new_bufs=None, w_o=None, cast=(), tm=PROJ_TM):
    t = x.shape[0]
    rope = rope_tabs is not None
    emit_new = new_shape is not None

    def row(width):
        return pl.BlockSpec((tm, width), lambda i: (i, 0))

    in_specs = [
        row(D_MODEL),
        _mod_spec(layer, mod_map, D_MODEL),
        _layer_spec((1, D_MODEL), layer),
        _layer_spec((D_MODEL, IN_COLS_PAD), layer),
        _layer_spec((1, HEAD_DIM), layer),
        _layer_spec((1, HEAD_DIM), layer),
        _layer_spec((1, KV_RANK), layer),
        _layer_spec((KV_RANK, W_KNOPE), layer),
        _layer_spec((KV_RANK, W_VB), layer),
    ]
    args = [x, mods, g, w_in, qn, kn, kvn, wuk, wuv]
    if rope:
        seq_tiles = rope_tabs[0].shape[0] // tm
        for tab in rope_tabs:
            in_specs.append(pl.BlockSpec((tm, tab.shape[1]), lambda i: (i % seq_tiles, 0)))
            args.append(tab)
    qkv_widths = [W_QA, W_KA, W_KA, W_QB_PAD, W_QB_PAD, W_VB]
    scratch = []
    aliases = {}
    n_alias = 0
    if not emit_new:
        out_specs = [row(w) for w in qkv_widths]
        out_shape = [jax.ShapeDtypeStruct((t, w), BF16) for w in qkv_widths]
    else:
        in_specs.append(pl.BlockSpec((W_QA + W_VB, D_MODEL), lambda i: (0, 0),
                                     pipeline_mode=pl.Buffered(1)))
        args.append(w_o)
        out_specs = [row(D_MODEL)]
        out_shape = [jax.ShapeDtypeStruct((t, D_MODEL), F32)]
        scratch = [pltpu.VMEM((tm, w), BF16)
                   for w in qkv_widths + [2 * W_KA, 2 * W_VB, W_QA, W_VB]]
        batch, seq = new_shape
        assert seq == PROJ_ROWS and t == batch * seq
        nb = tm // seq
        tails = [(N_KV_A, HEAD_DIM), (N_KV_A, HEAD_DIM), (KV_RANK,), (ROPE_B,)]
        if new_bufs is not None:
            n_alias = N_NEW
            for k, buf in enumerate(new_bufs):
                aliases[len(args)] = len(out_shape) + k
                in_specs.append(pl.BlockSpec(memory_space=pl.ANY))
                args.append(buf)
        for tail in tails:
            zeros = (0,) * len(tail)
            out_specs.append(pl.BlockSpec((nb, None, seq) + tail,
                                          lambda i, zeros=zeros: (i, layer, 0) + zeros))
            out_shape.append(jax.ShapeDtypeStruct((batch, DEPTH, seq) + tail, F32))
    cast_specs_in, cast_specs_out, cast_shapes, cast_srcs = _side_cast(
        cast, t // tm, lambda i: i)
    outs = pl.pallas_call(
        functools.partial(_proj_kernel, rope=rope, emit_new=emit_new, n_alias=n_alias,
                          n_cast=len(cast)),
        grid=(t // tm,),
        in_specs=in_specs + cast_specs_in,
        out_specs=out_specs + cast_specs_out,
        out_shape=out_shape + cast_shapes,
        input_output_aliases=aliases,
        scratch_shapes=scratch,
        compiler_params=_cparams(1),
        name="proj_lat" if rope else "proj_attn_ctx",
    )(*args, *cast_srcs)
    n_main = len(out_shape)
    return outs[:n_main], outs[n_main:]


def _attend(q, k, v_ext, dv):
    s = _dot_t(q, k)
    p = jnp.exp2(s - s.max(axis=-1, keepdims=True))
    o = _dot(p.astype(BF16), v_ext)
    return (o[:, 0:dv] / o[:, dv:2 * dv]).astype(BF16)


def _attend_parts(q, k_parts, v_ext_ref, dv):
    s = [_dot_t(q, k) for k in k_parts]
    m = s[0].max(axis=-1, keepdims=True)
    for sp in s[1:]:
        m = jnp.maximum(m, sp.max(axis=-1, keepdims=True))
    o = None
    r0 = 0
    for sp in s:
        r1 = r0 + sp.shape[1]
        op = _dot(jnp.exp2(sp - m).astype(BF16), v_ext_ref[r0:r1, :])
        o = op if o is None else o + op
        r0 = r1
    return (o[:, 0:dv] / o[:, dv:2 * dv]).astype(BF16)


def _fill_v_ext(v_ext_ref, rows, v_ref, n_kv, dv):
    ones = jnp.ones((v_ref.shape[0], dv), BF16)
    for jk in range(n_kv):
        v_ext_ref[rows, 2 * jk * dv:(2 * jk + 1) * dv] = v_ref[:, jk * dv:(jk + 1) * dv]
        v_ext_ref[rows, (2 * jk + 1) * dv:(2 * jk + 2) * dv] = ones


ATTN_TQ_A = 1024
ATTN_TQ_B = 512
ATTN_HEADS_PER_STEP = 8
ATTN_ROWS = 256


def _attn_lat_kernel(*refs, n_q, n_kv, dq, dv, past, n_cast):
    q_ref, kc_ref, kn_ref, vc_ref, vn_ref = refs[:5]
    o_ref = refs[5 + n_cast]
    v_ref = refs[6 + 2 * n_cast]
    k_scratch = refs[7 + 2 * n_cast:]
    _attn_lat_body(q_ref, kc_ref, kn_ref, vc_ref, vn_ref, o_ref, v_ref, k_scratch,
                   n_q=n_q, n_kv=n_kv, dq=dq, dv=dv, past=past)
    _cast_chunks(refs[5:5 + n_cast], refs[6 + n_cast:6 + 2 * n_cast])


def _attn_lat_body(q_ref, kc_ref, kn_ref, vc_ref, vn_ref, o_ref, v_ref, k_scratch,
                   *, n_q, n_kv, dq, dv, past):
    @pl.when(pl.program_id(2) == 0)
    def _():
        _fill_v_ext(v_ref, slice(0, past), vc_ref, n_kv, dv)
        _fill_v_ext(v_ref, slice(past, None), vn_ref, n_kv, dv)
        if k_scratch:
            k_scratch[0][0:past, :] = kc_ref[...]
            k_scratch[0][past:, :] = kn_ref[...]

    rows = ATTN_ROWS
    for j in range(n_q):
        jk = j * n_kv // n_q
        ksl = slice(jk * dq, (jk + 1) * dq)
        vsl = slice(2 * jk * dv, (2 * jk + 2) * dv)
        for r in range(q_ref.shape[0] // rows):
            rsl = slice(r * rows, (r + 1) * rows)
            q = q_ref[rsl, j * dq:(j + 1) * dq]
            if k_scratch:
                o = _attend(q, k_scratch[0][:, ksl], v_ref[:, vsl], dv)
            else:
                o = _attend_parts(q, [kc_ref[:, ksl], kn_ref[:, ksl]], v_ref.at[:, vsl], dv)
            o_ref[rsl, j * dv:(j + 1) * dv] = o


def _attn_lat(q, layer, kc, kn, vc, vn, n_units, n_q, n_kv, dq, dv, name, join_k, tq, cast=()):
    nb, seq, _ = q.shape
    past = kc.shape[2]
    scratch = [pltpu.VMEM((past + seq, n_kv * 2 * dv), BF16)]
    if join_k:
        scratch.append(pltpu.VMEM((past + seq, n_kv * dq), BF16))
    n_qt = seq // tq
    cast_specs_in, cast_specs_out, cast_shapes, cast_srcs = _side_cast(
        cast, nb * n_units * n_qt, lambda b, u, i: (b * n_units + u) * n_qt + i)
    outs = pl.pallas_call(
        functools.partial(_attn_lat_kernel, n_q=n_q, n_kv=n_kv, dq=dq, dv=dv, past=past,
                          n_cast=len(cast)),
        grid=(nb, n_units, n_qt),
        in_specs=[
            pl.BlockSpec((None, tq, n_q * dq), lambda b, u, i: (b, i, u)),
            pl.BlockSpec((None, None, past, n_kv * dq), lambda b, u, i: (layer, b, 0, u)),
            pl.BlockSpec((None, seq, n_kv * dq), lambda b, u, i: (b, 0, u)),
            pl.BlockSpec((None, None, past, n_kv * dv), lambda b, u, i: (layer, b, 0, u)),
            pl.BlockSpec((None, seq, n_kv * dv), lambda b, u, i: (b, 0, u)),
        ] + cast_specs_in,
        out_specs=[pl.BlockSpec((None, tq, n_q * dv), lambda b, u, i: (b, i, u))] + cast_specs_out,
        out_shape=[jax.ShapeDtypeStruct((nb, seq, n_units * n_q * dv), BF16)] + cast_shapes,
        scratch_shapes=scratch,
        compiler_params=_cparams(3),
        name=name,
    )(q, kc, kn, vc, vn, *cast_srcs)
    return outs[0], outs[1:]


OUT_TM = 512


def _outproj_kernel(x_ref, oa_ref, ob_ref, mod_ref, w_ref, y_ref):
    y = _dot(oa_ref[...], w_ref[0:W_QA, :]) + _dot(ob_ref[...], w_ref[W_QA:, :])
    y_ref[...] = x_ref[...] + mod_ref[2:3, :] * y


def _outproj(x, oa, ob, layer, mods, mod_map, w_o, tm=OUT_TM):
    t = x.shape[0]
    return pl.pallas_call(
        _outproj_kernel,
        grid=(t // tm,),
        in_specs=[
            pl.BlockSpec((tm, D_MODEL), lambda i: (i, 0)),
            pl.BlockSpec((tm, W_QA), lambda i: (i, 0)),
            pl.BlockSpec((tm, W_VB), lambda i: (i, 0)),
            _mod_spec(layer, mod_map, D_MODEL),
            pl.BlockSpec((W_QA + W_VB, D_MODEL), lambda i: (0, 0), pipeline_mode=pl.Buffered(1)),
        ],
        out_specs=pl.BlockSpec((tm, D_MODEL), lambda i: (i, 0)),
        out_shape=jax.ShapeDtypeStruct((t, D_MODEL), F32),
        compiler_params=_cparams(1),
        name="outproj",
    )(x, oa, ob, mods, w_o)


FFN_TM = 1024
FFN_TF = 512
FFN_TN = 512
FFN_ROWS = 512


def _ffn_kernel(*refs, nf, tf, rc, n_tiles, n_cast):
    xr_ref, xc_ref, mod_ref, modc_ref, g_ref, wg_ref, wu_ref, wd_ref = refs[:8]
    cast_in = refs[8:8 + n_cast]
    y_ref = refs[8 + n_cast]
    cast_out = refs[9 + n_cast:9 + 2 * n_cast]
    h_ref, t_ref = refs[9 + 2 * n_cast:]
    i = pl.program_id(0)
    j = pl.program_id(1)

    def cast_chunks():
        _cast_chunks(cast_in, cast_out)
    cur = (i + 1) % 2
    nxt = i % 2

    def norm_chunk():
        h = _rms(xr_ref[...]) * g_ref[...]
        h = h * (1.0 + mod_ref[4:5, :]) + mod_ref[3:4, :]
        r0 = pl.multiple_of((j - nf) * rc, rc)
        h_ref[nxt, pl.ds(r0, rc), :] = h.astype(BF16)

    row_blocks = [slice(m, m + FFN_ROWS) for m in range(0, y_ref.shape[0], FFN_ROWS)]

    def down_proj():
        for rows in row_blocks:
            d = _dot(t_ref[0, rows, :], wd_ref[0:tf, :])
            for k in range(1, nf):
                d = d + _dot(t_ref[k, rows, :], wd_ref[k * tf:(k + 1) * tf, :])
            y_ref[rows, :] = xc_ref[rows, :] + modc_ref[5:6, :] * d

    @pl.when((j < nf) & (i == 0))
    def _():
        cast_chunks()

    @pl.when((j < nf) & (i >= 1))
    def _():
        cast_chunks()
        for rows in row_blocks:
            hb = h_ref[cur, rows, :]
            a = _dot(hb, wg_ref[...])
            b = _dot(hb, wu_ref[...])
            t_ref[j, rows, :] = (a * jax.nn.sigmoid(a) * b).astype(BF16)

    @pl.when((j >= nf) & (i >= 1) & (i < n_tiles))
    def _():
        cast_chunks()
        norm_chunk()
        down_proj()

    @pl.when((j >= nf) & (i == 0))
    def _():
        cast_chunks()
        norm_chunk()

    @pl.when((j >= nf) & (i == n_tiles))
    def _():
        cast_chunks()
        down_proj()


def _ffn(x, layer, mods, mod_map, g, wg, wu, wd, cast=(), tm=FFN_TM, tf=FFN_TF, tn=FFN_TN):
    t = x.shape[0]
    n_tiles = t // tm
    nf = D_FF // tf
    nn = D_MODEL // tn
    cast_specs_in, cast_specs_out, cast_shapes, cast_srcs = _side_cast(
        cast, (n_tiles + 1) * (nf + nn), lambda i, j: i * (nf + nn) + j)
    rc = tm // nn
    col = lambda j: jnp.maximum(j - nf, 0)
    comp = lambda i: jnp.maximum(i - 1, 0)
    norm = lambda i: jnp.minimum(i, n_tiles - 1)
    ccol = lambda i, j: jnp.where(i == 0, 0, col(j))
    ftile = lambda i, j: jnp.where(i == 0, 0, jnp.minimum(j, nf - 1))
    outs = pl.pallas_call(
        functools.partial(_ffn_kernel, nf=nf, tf=tf, rc=rc, n_tiles=n_tiles, n_cast=len(cast)),
        grid=(n_tiles + 1, nf + nn),
        in_specs=[
            pl.BlockSpec((rc, D_MODEL), lambda i, j: (norm(i) * nn + col(j), 0)),
            pl.BlockSpec((tm, tn), lambda i, j: (comp(i), ccol(i, j))),
            pl.BlockSpec((None, None, N_MOD, D_MODEL),
                         lambda i, j: (layer, mod_map(norm(i)), 0, 0)),
            pl.BlockSpec((None, None, N_MOD, tn),
                         lambda i, j: (layer, mod_map(comp(i)), 0, ccol(i, j))),
            _layer_spec((1, D_MODEL), layer),
            pl.BlockSpec((D_MODEL, tf), lambda i, j: (0, ftile(i, j))),
            pl.BlockSpec((D_MODEL, tf), lambda i, j: (0, ftile(i, j))),
            pl.BlockSpec((D_FF, tn), lambda i, j: (0, ccol(i, j))),
        ] + cast_specs_in,
        out_specs=[pl.BlockSpec((tm, tn), lambda i, j: (comp(i), ccol(i, j)))] + cast_specs_out,
        out_shape=[jax.ShapeDtypeStruct((t, D_MODEL), F32)] + cast_shapes,
        scratch_shapes=[pltpu.VMEM((2, tm, D_MODEL), BF16), pltpu.VMEM((nf, tm, tf), BF16)],
        compiler_params=_cparams(2),
        name="ffn",
    )(x, x, mods, mods, g, wg, wu, wd, *cast_srcs)
    return outs[0], outs[1:]


NORM_TM = 512


def _final_norm_kernel(x_ref, g_ref, y_ref):
    y_ref[...] = _rms(x_ref[...]) * g_ref[...]


def _final_norm(x, g, tm=NORM_TM):
    t = x.shape[0]
    return pl.pallas_call(
        _final_norm_kernel,
        grid=(t // tm,),
        in_specs=[pl.BlockSpec((tm, D_MODEL), lambda i: (i, 0)),
                  pl.BlockSpec((1, D_MODEL), lambda i: (0, 0))],
        out_specs=pl.BlockSpec((tm, D_MODEL), lambda i: (i, 0)),
        out_shape=jax.ShapeDtypeStruct((t, D_MODEL), F32),
        compiler_params=_cparams(1),
        name="final_norm",
    )(x, g)


def _rope_tables(n_tokens):
    n_rows = n_tokens // GRID_W
    row = jnp.repeat(jnp.arange(n_rows, dtype=F32), GRID_W)
    col = jnp.tile(jnp.arange(GRID_W, dtype=F32), n_rows)

    def cs(dim):
        n_freq = dim // 4
        inv = ROPE_THETA ** (-jnp.arange(n_freq, dtype=F32) / n_freq)
        ang = jnp.concatenate([row[:, None] * inv, col[:, None] * inv], axis=-1)
        return jnp.cos(ang), jnp.sin(ang)

    cos_a, sin_a = cs(HEAD_DIM)
    cos_b, sin_b = cs(ROPE_B)
    one = jnp.ones_like(cos_b)
    zero = jnp.zeros_like(sin_b)
    return (jnp.concatenate([cos_a, cos_a], axis=-1),
            jnp.concatenate([-sin_a, sin_a], axis=-1),
            jnp.concatenate([cos_b, one, cos_b, one], axis=-1),
            jnp.concatenate([-sin_b, zero, sin_b, zero], axis=-1))


def kernel(x_prompt, x_sample, cache_k_a, cache_v_a, cache_ckv_b, cache_krope_b, c, c_ctx, w_ada, b_ada, norm_attn, norm_ffn, w_in, qnorm_a, knorm_a, kvnorm_b, w_uk_b, w_uv_b, w_o, w_gate, w_up, w_down, norm_final):
    batch, seq, _ = x_prompt.shape
    dec_batch, dec_seq, _ = x_sample.shape
    assert dec_batch + 1 <= MOD_ROWS

    w_in_b = _wprep(w_in.astype(BF16))
    wuk_b = w_uk_b.astype(BF16)
    wuv_b = w_uv_b.astype(BF16)
    wo_l = w_o[0].astype(BF16)
    rope_tabs = _rope_tables(dec_seq)

    cond = jnp.concatenate(
        [c_ctx[None, :], c, jnp.zeros((MOD_ROWS - 1 - dec_batch, D_MODEL), F32)], axis=0)
    mods = _adaln(cond, w_ada, b_ada).reshape(DEPTH, MOD_ROWS, N_MOD, D_MODEL)

    kc_a, vc_a, kc_b, vc_b = _cache_prep(
        cache_k_a, cache_v_a, cache_ckv_b, cache_krope_b, wuk_b, wuv_b)

    xp = x_prompt.reshape(batch * seq, D_MODEL)
    xs = x_sample.reshape(dec_batch * dec_seq, D_MODEL)
    g1 = norm_attn.reshape(DEPTH, 1, D_MODEL)
    g2 = norm_ffn.reshape(DEPTH, 1, D_MODEL)
    qn = qnorm_a.reshape(DEPTH, 1, HEAD_DIM)
    kn = knorm_a.reshape(DEPTH, 1, HEAD_DIM)
    kvn = kvnorm_b.reshape(DEPTH, 1, KV_RANK)
    ctx_mod = lambda i: 0

    def lat_mod(tm):
        return lambda i: 1 + i // (dec_seq // tm)

    r3 = lambda a: a.reshape(dec_batch, dec_seq, a.shape[-1])
    hb = ATTN_HEADS_PER_STEP
    new_bufs = None

    for l in range(DEPTH):
        first, nxt = l == 0, l + 1 < DEPTH

        (xp, *new_bufs), _ = _proj(
            xp, l, mods, ctx_mod, g1, w_in_b, qn, kn, kvn, wuk_b, wuv_b,
            new_shape=(batch, seq), new_bufs=new_bufs, w_o=wo_l)

        (qa, ka, va, qb, kb, vb), cast0 = _proj(
            xs, l, mods, lat_mod(PROJ_TM), g1, w_in_b, qn, kn, kvn, wuk_b, wuv_b,
            rope_tabs=rope_tabs, cast=((w_gate, 0),) if first else ())
        oa, cast1 = _attn_lat(r3(qa), l, kc_a, r3(ka), vc_a, r3(va),
                              1, N_HEADS_A, N_KV_A, HEAD_DIM, HEAD_DIM, "attn_lat_a", join_k=True,
                              tq=ATTN_TQ_A,
                              cast=((w_up, 0), (w_down, 0)) if first else ())
        ob, _ = _attn_lat(r3(qb), l, kc_b, r3(kb), vc_b, r3(vb),
                          N_HEADS_B // hb, hb, hb, QB_PAD, VDIM_B, "attn_lat_b", join_k=False,
                          tq=ATTN_TQ_B)
        xs = _outproj(xs, oa.reshape(-1, W_QA), ob.reshape(-1, W_VB), l, mods, lat_mod(OUT_TM), wo_l)
        if first:
            (wg_l,), (wu_l, wd_l) = cast0, cast1

        xp, gu_next = _ffn(xp, l, mods, ctx_mod, g2, wg_l, wu_l, wd_l,
                           cast=((w_gate, l + 1), (w_up, l + 1)) if nxt else ())
        xs, do_next = _ffn(xs, l, mods, lat_mod(FFN_TM), g2, wg_l, wu_l, wd_l,
                           cast=((w_down, l + 1), (w_o, l + 1)) if nxt else ())
        if nxt:
            (wg_l, wu_l), (wd_l, wo_l) = gu_next, do_next

    gf = norm_final.reshape(1, D_MODEL)
    y_prompt = _final_norm(xp, gf).reshape(batch, seq, D_MODEL)
    y_sample = _final_norm(xs, gf).reshape(dec_batch, dec_seq, D_MODEL)
    new_k, new_v, new_ckv, new_kr = new_bufs
    return (y_prompt, y_sample,
            new_k, new_v, new_ckv, new_kr)
```

```python
import functools

import jax
import jax.numpy as jnp
import numpy as np
from jax import lax
from jax.experimental import pallas as pl
from jax.experimental.pallas import tpu as pltpu

D_MODEL = 2048
DEPTH = 4
GRID_W = 64
HEAD_DIM = 128
N_HEADS_A = 8
N_KV_A = 2
GROUP_A = N_HEADS_A // N_KV_A
N_HEADS_B = 8
NOPE_B = 128
ROPE_B = 64
VDIM_B = 128
KV_RANK = 256
D_FF = 5632
ROPE_THETA = 10000.0
EPS = 1e-6

W_QA = N_HEADS_A * HEAD_DIM
W_KA = N_KV_A * HEAD_DIM
QB_DIM = NOPE_B + ROPE_B
W_QB = N_HEADS_B * QB_DIM
QB_PAD = 256
KR_PAD = 128
W_QB_PAD = N_HEADS_B * QB_PAD
W_KNOPE = N_HEADS_B * NOPE_B
W_VB = N_HEADS_B * VDIM_B
N_MOD = 6
MOD_ROWS = 8

O_QB = W_QA + 2 * W_KA
O_CKV = O_QB + W_QB
O_KR = O_CKV + KV_RANK
IN_COLS = O_KR + ROPE_B

C_QA = 0
C_KA = C_QA + W_QA
C_VA = C_KA + W_KA
C_QB = C_VA + W_KA
C_CKV = C_QB + W_QB_PAD
C_KR = C_CKV + KV_RANK
IN_COLS_PAD = C_KR + KR_PAD

LOG2_E = 1.4426950408889634
SCALE_A = HEAD_DIM ** -0.5 * LOG2_E
SCALE_B = QB_DIM ** -0.5 * LOG2_E

VMEM_LIMIT = 60 * 1024 * 1024

BF16 = jnp.bfloat16
F32 = jnp.float32


def _cparams(n_axes):
    return pltpu.CompilerParams(
        dimension_semantics=("arbitrary",) * n_axes, vmem_limit_bytes=VMEM_LIMIT)


def _layer_spec(shape, layer):
    nd = len(shape)
    return pl.BlockSpec((None,) + shape, lambda *_: (layer,) + (0,) * nd,
                        pipeline_mode=pl.Buffered(1))


def _mod_spec(layer, mod_map, width, col_map=None):
    if col_map is None:
        return pl.BlockSpec((None, None, N_MOD, width), lambda i, *_: (layer, mod_map(i), 0, 0))
    return pl.BlockSpec((None, None, N_MOD, width),
                        lambda i, j: (layer, mod_map(i), 0, col_map(j)))


def _dot(a, b):
    return jnp.dot(a, b, preferred_element_type=F32)


def _dot_t(a, b):
    return lax.dot_general(a, b, (((1,), (1,)), ((), ())), preferred_element_type=F32)


def _rms(x):
    return x * lax.rsqrt(jnp.mean(x * x, axis=-1, keepdims=True) + EPS)


def _cast_rows(n_rows, n_steps):
    rows = 16
    while n_rows % rows or n_rows // rows > n_steps:
        rows += 16
    return rows


def _side_cast(cast, n_steps, step_of):
    in_specs, out_specs, shapes = [], [], []
    for src, src_layer in cast:
        _, n_rows, n_cols = src.shape
        rows = _cast_rows(n_rows, n_steps)
        chunk = lambda *ids, last=n_rows // rows - 1: jnp.minimum(step_of(*ids), last)
        in_specs.append(pl.BlockSpec(
            (None, rows, n_cols), lambda *ids, c=chunk, l=src_layer: (l, c(*ids), 0)))
        out_specs.append(pl.BlockSpec((rows, n_cols), lambda *ids, c=chunk: (c(*ids), 0)))
        shapes.append(jax.ShapeDtypeStruct((n_rows, n_cols), BF16))
    return in_specs, out_specs, shapes, [src for src, _ in cast]


def _cast_chunks(cast_in, cast_out):
    for src, dst in zip(cast_in, cast_out):
        dst[...] = src[...].astype(BF16)


ADA_TN = 1024


def _adaln_kernel(s_ref, w_ref, b_ref, o_ref):
    s = s_ref[...]
    s = s * jax.nn.sigmoid(s)
    o_ref[...] = _dot(s.astype(BF16), w_ref[...].astype(BF16)) + b_ref[...]


def _adaln(s, w_ada, b_ada):
    n = N_MOD * D_MODEL
    return pl.pallas_call(
        _adaln_kernel,
        grid=(DEPTH, n // ADA_TN),
        in_specs=[
            pl.BlockSpec((MOD_ROWS, D_MODEL), lambda l, j: (0, 0)),
            pl.BlockSpec((None, D_MODEL, ADA_TN), lambda l, j: (l, 0, j)),
            pl.BlockSpec((None, 1, ADA_TN), lambda l, j: (l, 0, j)),
        ],
        out_specs=pl.BlockSpec((None, MOD_ROWS, ADA_TN), lambda l, j: (l, 0, j)),
        out_shape=jax.ShapeDtypeStruct((DEPTH, MOD_ROWS, n), F32),
        compiler_params=_cparams(2),
        name="adaln",
    )(s, w_ada, b_ada.reshape(DEPTH, 1, n))


WPREP_TR = 512
PAIR_IN = 2 * QB_DIM
PAIR_OUT = 2 * QB_PAD


def _wprep_kernel(w_ref, pq_ref, pk_ref, o_ref):
    o_ref[:, 0:O_QB] = w_ref[:, 0:O_QB]
    pq = pq_ref[...]
    for i in range(N_HEADS_B // 2):
        src = w_ref[:, O_QB + i * PAIR_IN:O_QB + (i + 1) * PAIR_IN]
        o_ref[:, C_QB + i * PAIR_OUT:C_QB + (i + 1) * PAIR_OUT] = _dot(src, pq).astype(BF16)
    o_ref[:, C_CKV:C_CKV + KV_RANK] = w_ref[:, O_CKV:O_CKV + KV_RANK]
    o_ref[:, C_KR:C_KR + KR_PAD] = _dot(w_ref[:, O_KR:O_KR + ROPE_B], pk_ref[...]).astype(BF16)


def _rope_pad_src():
    half = ROPE_B // 2
    src = np.full((KR_PAD,), -1, np.int32)
    src[0:half] = np.arange(half)
    src[2 * half:3 * half] = half + np.arange(half)
    return src


def _selection(src, n_in):
    return (jnp.arange(n_in, dtype=jnp.int32)[:, None] == jnp.asarray(src)[None, :]).astype(BF16)


def _wprep(w_in):
    rsrc = _rope_pad_src()
    head = np.concatenate([np.arange(NOPE_B), np.where(rsrc >= 0, NOPE_B + rsrc, -1)])
    pair = np.concatenate([head, np.where(head >= 0, QB_DIM + head, -1)]).astype(np.int32)
    pq = _selection(pair, PAIR_IN)
    pk = _selection(rsrc, ROPE_B)
    return pl.pallas_call(
        _wprep_kernel,
        grid=(DEPTH, D_MODEL // WPREP_TR),
        in_specs=[
            pl.BlockSpec((None, WPREP_TR, IN_COLS), lambda l, r: (l, r, 0)),
            pl.BlockSpec((PAIR_IN, PAIR_OUT), lambda l, r: (0, 0)),
            pl.BlockSpec((ROPE_B, KR_PAD), lambda l, r: (0, 0)),
        ],
        out_specs=pl.BlockSpec((None, WPREP_TR, IN_COLS_PAD), lambda l, r: (l, r, 0)),
        out_shape=jax.ShapeDtypeStruct((DEPTH, D_MODEL, IN_COLS_PAD), BF16),
        compiler_params=_cparams(2),
        name="wprep",
    )(w_in, pq, pk)


def _cache_kernel(ka_ref, va_ref, ckv_ref, kr_ref, wuk_ref, wuv_ref, pk_ref,
                  ka_o, va_o, kb_o, vb_o):
    ka_o[...] = ka_ref[...].astype(BF16)
    va_o[...] = va_ref[...].astype(BF16)
    ckv = ckv_ref[...].astype(BF16)
    kn = _dot(ckv, wuk_ref[...])
    vb_o[...] = _dot(ckv, wuv_ref[...]).astype(BF16)
    kr = _dot(kr_ref[...].astype(BF16), pk_ref[...]).astype(BF16)
    for h in range(N_HEADS_B):
        kb_o[:, h * QB_PAD:h * QB_PAD + NOPE_B] = kn[:, h * NOPE_B:(h + 1) * NOPE_B].astype(BF16)
        kb_o[:, h * QB_PAD + NOPE_B:(h + 1) * QB_PAD] = kr


def _cache_prep(cache_k_a, cache_v_a, cache_ckv_b, cache_krope_b, wuk, wuv):
    nb, _, past = cache_ckv_b.shape[:3]
    ka = cache_k_a.reshape(nb, DEPTH, past, W_KA)
    va = cache_v_a.reshape(nb, DEPTH, past, W_KA)
    pk = _selection(_rope_pad_src(), ROPE_B)

    def in4(width):
        return pl.BlockSpec((None, None, past, width), lambda l, b: (b, l, 0, 0))

    def out4(width):
        return pl.BlockSpec((None, None, past, width), lambda l, b: (l, b, 0, 0))

    def w3(width):
        return pl.BlockSpec((None, KV_RANK, width), lambda l, b: (l, 0, 0))

    return pl.pallas_call(
        _cache_kernel,
        grid=(DEPTH, nb),
        in_specs=[in4(W_KA), in4(W_KA), in4(KV_RANK), in4(ROPE_B), w3(W_KNOPE), w3(W_VB),
                  pl.BlockSpec((ROPE_B, KR_PAD), lambda l, b: (0, 0))],
        out_specs=[out4(W_KA), out4(W_KA), out4(W_QB_PAD), out4(W_VB)],
        out_shape=[
            jax.ShapeDtypeStruct((DEPTH, nb, past, W_KA), BF16),
            jax.ShapeDtypeStruct((DEPTH, nb, past, W_KA), BF16),
            jax.ShapeDtypeStruct((DEPTH, nb, past, W_QB_PAD), BF16),
            jax.ShapeDtypeStruct((DEPTH, nb, past, W_VB), BF16),
        ],
        compiler_params=_cparams(2),
        name="cache_prep",
    )(ka, va, cache_ckv_b, cache_krope_b, wuk, wuv, pk)


PROJ_TM = 512
PROJ_ROWS = 256
N_NEW = 4


def _proj_kernel(*refs, rope, emit_new, n_alias, n_cast):
    it = iter(refs)
    x_ref, mod_ref, g_ref, w_ref, qn_ref, kn_ref, kvn_ref, wuk_ref, wuv_ref = (
        next(it) for _ in range(9))
    if rope:
        ca_ref, sa_ref, cb_ref, sb_ref = (next(it) for _ in range(4))
    if emit_new:
        wo_ref = next(it)
    for _ in range(n_alias):
        next(it)
    cast_in = [next(it) for _ in range(n_cast)]
    if emit_new:
        xo_o, nk_o, nv_o, nckv_o, nkr_o = (next(it) for _ in range(1 + N_NEW))
        cast_out = [next(it) for _ in range(n_cast)]
    qa_o, ka_o, va_o, qb_o, kb_o, vb_o = (next(it) for _ in range(6))
    if emit_new:
        vae_s, vbe_s, oa_o, ob_o = (next(it) for _ in range(4))
    else:
        cast_out = [next(it) for _ in range(n_cast)]
    _cast_chunks(cast_in, cast_out)

    for r in range(x_ref.shape[0] // PROJ_ROWS):
        rows = slice(r * PROJ_ROWS, (r + 1) * PROJ_ROWS)

        h = _rms(x_ref[rows, :]) * g_ref[...]
        h = h * (1.0 + mod_ref[1:2, :]) + mod_ref[0:1, :]
        hb = h.astype(BF16)

        def rope_a(t):
            if not rope:
                return t
            return t * ca_ref[rows, :] + pltpu.roll(t, HEAD_DIM // 2, 1) * sa_ref[rows, :]

        def rope_b(t):
            if not rope:
                return t
            return t * cb_ref[rows, :] + pltpu.roll(t, KR_PAD // 2, 1) * sb_ref[rows, :]

        p = _dot(hb, w_ref[...])

        pq = p[:, C_QA:C_QA + W_QA]
        qn = qn_ref[...]
        for i in range(N_HEADS_A):
            sl = slice(i * HEAD_DIM, (i + 1) * HEAD_DIM)
            t = rope_a(_rms(pq[:, sl]) * qn)
            qa_o[rows, sl] = (t * SCALE_A).astype(BF16)

        pk = p[:, C_KA:C_KA + W_KA]
        pv = p[:, C_VA:C_VA + W_KA]
        kn = kn_ref[...]
        for i in range(N_KV_A):
            sl = slice(i * HEAD_DIM, (i + 1) * HEAD_DIM)
            t = _rms(pk[:, sl]) * kn
            if emit_new:
                nk_o[r, :, i, :] = t
            ka_o[rows, sl] = rope_a(t).astype(BF16)
        va_o[rows, :] = pv.astype(BF16)
        if emit_new:
            for i in range(N_KV_A):
                nv_o[r, :, i, :] = pv[:, i * HEAD_DIM:(i + 1) * HEAD_DIM]

        pqb = p[:, C_QB:C_QB + W_QB_PAD]
        for i in range(N_HEADS_B):
            s0 = i * QB_PAD
            qb_o[rows, s0:s0 + NOPE_B] = (pqb[:, s0:s0 + NOPE_B] * SCALE_B).astype(BF16)
            t = rope_b(pqb[:, s0 + NOPE_B:s0 + QB_PAD])
            qb_o[rows, s0 + NOPE_B:s0 + QB_PAD] = (t * SCALE_B).astype(BF16)

        pc = p[:, C_CKV:C_CKV + KV_RANK]
        ckv = _rms(pc) * kvn_ref[...]
        pkr = p[:, C_KR:C_KR + KR_PAD]
        if emit_new:
            half = ROPE_B // 2
            nckv_o[r] = ckv
            nkr_o[r] = jnp.concatenate([pkr[:, 0:half], pkr[:, 2 * half:3 * half]], axis=-1)
        ckvb = ckv.astype(BF16)
        krb = rope_b(pkr).astype(BF16)
        kno = _dot(ckvb, wuk_ref[...])
        vb_o[rows, :] = _dot(ckvb, wuv_ref[...]).astype(BF16)
        for i in range(N_HEADS_B):
            s0 = i * QB_PAD
            kb_o[rows, s0:s0 + NOPE_B] = kno[:, i * NOPE_B:(i + 1) * NOPE_B].astype(BF16)
            kb_o[rows, s0 + NOPE_B:s0 + QB_PAD] = krb

        if emit_new:
            _fill_v_ext(vae_s, rows, va_o.at[rows, :], N_KV_A, HEAD_DIM)
            _fill_v_ext(vbe_s, rows, vb_o.at[rows, :], N_HEADS_B, VDIM_B)
            for hq in range(N_HEADS_A):
                g = hq // GROUP_A
                qsl = slice(hq * HEAD_DIM, (hq + 1) * HEAD_DIM)
                oa_o[rows, qsl] = _attend(
                    qa_o[rows, qsl], ka_o[rows, g * HEAD_DIM:(g + 1) * HEAD_DIM],
                    vae_s[rows, 2 * g * HEAD_DIM:(2 * g + 2) * HEAD_DIM], HEAD_DIM)
            for hq in range(N_HEADS_B):
                qsl = slice(hq * QB_PAD, (hq + 1) * QB_PAD)
                ob_o[rows, hq * VDIM_B:(hq + 1) * VDIM_B] = _attend(
                    qb_o[rows, qsl], kb_o[rows, qsl],
                    vbe_s[rows, 2 * hq * VDIM_B:(2 * hq + 2) * VDIM_B], VDIM_B)
            y = _dot(oa_o[rows, :], wo_ref[0:W_QA, :]) + _dot(ob_o[rows, :], wo_ref[W_QA:, :])
            xo_o[rows, :] = x_ref[rows, :] + mod_ref[2:3, :] * y


def _proj(x, layer, mods, mod_map, g, w_in, qn, kn, kvn, wuk, wuv, rope_tabs=None,
          new_shape=None, new_bufs=None, w_o=None, cast=(), tm=PROJ_TM):
    t = x.shape[0]
    rope = rope_tabs is not None
    emit_new = new_shape is not None

    def row(width):
        return pl.BlockSpec((tm, width), lambda i: (i, 0))

    in_specs = [
        row(D_MODEL),
        _mod_spec(layer, mod_map, D_MODEL),
        _layer_spec((1, D_MODEL), layer),
        _layer_spec((D_MODEL, IN_COLS_PAD), layer),
        _layer_spec((1, HEAD_DIM), layer),
        _layer_spec((1, HEAD_DIM), layer),
        _layer_spec((1, KV_RANK), layer),
        _layer_spec((KV_RANK, W_KNOPE), layer),
        _layer_spec((KV_RANK, W_VB), layer),
    ]
    args = [x, mods, g, w_in, qn, kn, kvn, wuk, wuv]
    if rope:
        seq_tiles = rope_tabs[0].shape[0] // tm
        for tab in rope_tabs:
            in_specs.append(pl.BlockSpec((tm, tab.shape[1]), lambda i: (i % seq_tiles, 0)))
            args.append(tab)
    qkv_widths = [W_QA, W_KA, W_KA, W_QB_PAD, W_QB_PAD, W_VB]
    scratch = []
    aliases = {}
    n_alias = 0
    if not emit_new:
        out_specs = [row(w) for w in qkv_widths]
        out_shape = [jax.ShapeDtypeStruct((t, w), BF16) for w in qkv_widths]
    else:
        in_specs.append(pl.BlockSpec((W_QA + W_VB, D_MODEL), lambda i: (0, 0),
                                     pipeline_mode=pl.Buffered(1)))
        args.append(w_o)
        out_specs = [row(D_MODEL)]
        out_shape = [jax.ShapeDtypeStruct((t, D_MODEL), F32)]
        scratch = [pltpu.VMEM((tm, w), BF16)
                   for w in qkv_widths + [2 * W_KA, 2 * W_VB, W_QA, W_VB]]
        batch, seq = new_shape
        assert seq == PROJ_ROWS and t == batch * seq
        nb = tm // seq
        tails = [(N_KV_A, HEAD_DIM), (N_KV_A, HEAD_DIM), (KV_RANK,), (ROPE_B,)]
        if new_bufs is not None:
            n_alias = N_NEW
            for k, buf in enumerate(new_bufs):
                aliases[len(args)] = len(out_shape) + k
                in_specs.append(pl.BlockSpec(memory_space=pl.ANY))
                args.append(buf)
        for tail in tails:
            zeros = (0,) * len(tail)
            out_specs.append(pl.BlockSpec((nb, None, seq) + tail,
                                          lambda i, zeros=zeros: (i, layer, 0) + zeros))
            out_shape.append(jax.ShapeDtypeStruct((batch, DEPTH, seq) + tail, F32))
    cast_specs_in, cast_specs_out, cast_shapes, cast_srcs = _side_cast(
        cast, t // tm, lambda i: i)
    outs = pl.pallas_call(
        functools.partial(_proj_kernel, rope=rope, emit_new=emit_new, n_alias=n_alias,
                          n_cast=len(cast)),
        grid=(t // tm,),
        in_specs=in_specs + cast_specs_in,
        out_specs=out_specs + cast_specs_out,
        out_shape=out_shape + cast_shapes,
        input_output_aliases=aliases,
        scratch_shapes=scratch,
        compiler_params=_cparams(1),
        name="proj_lat" if rope else "proj_attn_ctx",
    )(*args, *cast_srcs)
    n_main = len(out_shape)
    return outs[:n_main], outs[n_main:]


def _attend(q, k, v_ext, dv):
    s = _dot_t(q, k)
    p = jnp.exp2(s - s.max(axis=-1, keepdims=True))
    o = _dot(p.astype(BF16), v_ext)
    return (o[:, 0:dv] / o[:, dv:2 * dv]).astype(BF16)


def _attend_parts(q, k_parts, v_ext_ref, dv):
    s = [_dot_t(q, k) for k in k_parts]
    m = s[0].max(axis=-1, keepdims=True)
    for sp in s[1:]:
        m = jnp.maximum(m, sp.max(axis=-1, keepdims=True))
    o = None
    r0 = 0
    for sp in s:
        r1 = r0 + sp.shape[1]
        op = _dot(jnp.exp2(sp - m).astype(BF16), v_ext_ref[r0:r1, :])
        o = op if o is None else o + op
        r0 = r1
    return (o[:, 0:dv] / o[:, dv:2 * dv]).astype(BF16)


def _fill_v_ext(v_ext_ref, rows, v_ref, n_kv, dv):
    ones = jnp.ones((v_ref.shape[0], dv), BF16)
    for jk in range(n_kv):
        v_ext_ref[rows, 2 * jk * dv:(2 * jk + 1) * dv] = v_ref[:, jk * dv:(jk + 1) * dv]
        v_ext_ref[rows, (2 * jk + 1) * dv:(2 * jk + 2) * dv] = ones


ATTN_TQ_A = 1024
ATTN_TQ_B = 512
ATTN_HEADS_PER_STEP = 8
ATTN_ROWS = 256


def _attn_lat_kernel(*refs, n_q, n_kv, dq, dv, past, n_cast):
    q_ref, kc_ref, kn_ref, vc_ref, vn_ref = refs[:5]
    o_ref = refs[5 + n_cast]
    v_ref = refs[6 + 2 * n_cast]
    k_scratch = refs[7 + 2 * n_cast:]
    _attn_lat_body(q_ref, kc_ref, kn_ref, vc_ref, vn_ref, o_ref, v_ref, k_scratch,
                   n_q=n_q, n_kv=n_kv, dq=dq, dv=dv, past=past)
    _cast_chunks(refs[5:5 + n_cast], refs[6 + n_cast:6 + 2 * n_cast])


def _attn_lat_body(q_ref, kc_ref, kn_ref, vc_ref, vn_ref, o_ref, v_ref, k_scratch,
                   *, n_q, n_kv, dq, dv, past):
    @pl.when(pl.program_id(2) == 0)
    def _():
        _fill_v_ext(v_ref, slice(0, past), vc_ref, n_kv, dv)
        _fill_v_ext(v_ref, slice(past, None), vn_ref, n_kv, dv)
        if k_scratch:
            k_scratch[0][0:past, :] = kc_ref[...]
            k_scratch[0][past:, :] = kn_ref[...]

    rows = ATTN_ROWS
    for j in range(n_q):
        jk = j * n_kv // n_q
        ksl = slice(jk * dq, (jk + 1) * dq)
        vsl = slice(2 * jk * dv, (2 * jk + 2) * dv)
        for r in range(q_ref.shape[0] // rows):
            rsl = slice(r * rows, (r + 1) * rows)
            q = q_ref[rsl, j * dq:(j + 1) * dq]
            if k_scratch:
                o = _attend(q, k_scratch[0][:, ksl], v_ref[:, vsl], dv)
            else:
                o = _attend_parts(q, [kc_ref[:, ksl], kn_ref[:, ksl]], v_ref.at[:, vsl], dv)
            o_ref[rsl, j * dv:(j + 1) * dv] = o


def _attn_lat(q, layer, kc, kn, vc, vn, n_units, n_q, n_kv, dq, dv, name, join_k, tq, cast=()):
    nb, seq, _ = q.shape
    past = kc.shape[2]
    scratch = [pltpu.VMEM((past + seq, n_kv * 2 * dv), BF16)]
    if join_k:
        scratch.append(pltpu.VMEM((past + seq, n_kv * dq), BF16))
    n_qt = seq // tq
    cast_specs_in, cast_specs_out, cast_shapes, cast_srcs = _side_cast(
        cast, nb * n_units * n_qt, lambda b, u, i: (b * n_units + u) * n_qt + i)
    outs = pl.pallas_call(
        functools.partial(_attn_lat_kernel, n_q=n_q, n_kv=n_kv, dq=dq, dv=dv, past=past,
                          n_cast=len(cast)),
        grid=(nb, n_units, n_qt),
        in_specs=[
            pl.BlockSpec((None, tq, n_q * dq), lambda b, u, i: (b, i, u)),
            pl.BlockSpec((None, None, past, n_kv * dq), lambda b, u, i: (layer, b, 0, u)),
            pl.BlockSpec((None, seq, n_kv * dq), lambda b, u, i: (b, 0, u)),
            pl.BlockSpec((None, None, past, n_kv * dv), lambda b, u, i: (layer, b, 0, u)),
            pl.BlockSpec((None, seq, n_kv * dv), lambda b, u, i: (b, 0, u)),
        ] + cast_specs_in,
        out_specs=[pl.BlockSpec((None, tq, n_q * dv), lambda b, u, i: (b, i, u))] + cast_specs_out,
        out_shape=[jax.ShapeDtypeStruct((nb, seq, n_units * n_q * dv), BF16)] + cast_shapes,
        scratch_shapes=scratch,
        compiler_params=_cparams(3),
        name=name,
    )(q, kc, kn, vc, vn, *cast_srcs)
    return outs[0], outs[1:]


OUT_TM = 512


def _outproj_kernel(x_ref, oa_ref, ob_ref, mod_ref, w_ref, y_ref):
    y = _dot(oa_ref[...], w_ref[0:W_QA, :]) + _dot(ob_ref[...], w_ref[W_QA:, :])
    y_ref[...] = x_ref[...] + mod_ref[2:3, :] * y


def _outproj(x, oa, ob, layer, mods, mod_map, w_o, tm=OUT_TM):
    t = x.shape[0]
    return pl.pallas_call(
        _outproj_kernel,
        grid=(t // tm,),
        in_specs=[
            pl.BlockSpec((tm, D_MODEL), lambda i: (i, 0)),
            pl.BlockSpec((tm, W_QA), lambda i: (i, 0)),
            pl.BlockSpec((tm, W_VB), lambda i: (i, 0)),
            _mod_spec(layer, mod_map, D_MODEL),
            pl.BlockSpec((W_QA + W_VB, D_MODEL), lambda i: (0, 0), pipeline_mode=pl.Buffered(1)),
        ],
        out_specs=pl.BlockSpec((tm, D_MODEL), lambda i: (i, 0)),
        out_shape=jax.ShapeDtypeStruct((t, D_MODEL), F32),
        compiler_params=_cparams(1),
        name="outproj",
    )(x, oa, ob, mods, w_o)


FFN_TM = 1024
FFN_TF = 512
FFN_TN = 512
FFN_ROWS = 128


def _ffn_kernel(*refs, nf, tf, rc, n_tiles, n_cast):
    xr_ref, xc_ref, mod_ref, modc_ref, g_ref, wg_ref, wu_ref, wd_ref = refs[:8]
    cast_in = refs[8:8 + n_cast]
    y_ref = refs[8 + n_cast]
    cast_out = refs[9 + n_cast:9 + 2 * n_cast]
    h_ref, t_ref = refs[9 + 2 * n_cast:]
    i = pl.program_id(0)
    j = pl.program_id(1)

    def cast_chunks():
        _cast_chunks(cast_in, cast_out)
    cur = (i + 1) % 2
    nxt = i % 2

    def norm_chunk():
        h = _rms(xr_ref[...]) * g_ref[...]
        h = h * (1.0 + mod_ref[4:5, :]) + mod_ref[3:4, :]
        r0 = pl.multiple_of((j - nf) * rc, rc)
        h_ref[nxt, pl.ds(r0, rc), :] = h.astype(BF16)

    row_blocks = [slice(m, m + FFN_ROWS) for m in range(0, y_ref.shape[0], FFN_ROWS)]

    def down_proj():
        for rows in row_blocks:
            d = _dot(t_ref[0, rows, :], wd_ref[0:tf, :])
            for k in range(1, nf):
                d = d + _dot(t_ref[k, rows, :], wd_ref[k * tf:(k + 1) * tf, :])
            y_ref[rows, :] = xc_ref[rows, :] + modc_ref[5:6, :] * d

    @pl.when((j < nf) & (i == 0))
    def _():
        cast_chunks()

    @pl.when((j < nf) & (i >= 1))
    def _():
        cast_chunks()
        for rows in row_blocks:
            hb = h_ref[cur, rows, :]
            a = _dot(hb, wg_ref[...])
            b = _dot(hb, wu_ref[...])
            t_ref[j, rows, :] = (a * jax.nn.sigmoid(a) * b).astype(BF16)

    @pl.when((j >= nf) & (i >= 1) & (i < n_tiles))
    def _():
        cast_chunks()
        norm_chunk()
        down_proj()

    @pl.when((j >= nf) & (i == 0))
    def _():
        cast_chunks()
        norm_chunk()

    @pl.when((j >= nf) & (i == n_tiles))
    def _():
        cast_chunks()
        down_proj()


def _ffn(x, layer, mods, mod_map, g, wg, wu, wd, cast=(), tm=FFN_TM, tf=FFN_TF, tn=FFN_TN):
    t = x.shape[0]
    n_tiles = t // tm
    nf = D_FF // tf
    nn = D_MODEL // tn
    cast_specs_in, cast_specs_out, cast_shapes, cast_srcs = _side_cast(
        cast, (n_tiles + 1) * (nf + nn), lambda i, j: i * (nf + nn) + j)
    rc = tm // nn
    col = lambda j: jnp.maximum(j - nf, 0)
    comp = lambda i: jnp.maximum(i - 1, 0)
    norm = lambda i: jnp.minimum(i, n_tiles - 1)
    ccol = lambda i, j: jnp.where(i == 0, 0, col(j))
    ftile = lambda i, j: jnp.where(i == 0, 0, jnp.minimum(j, nf - 1))
    outs = pl.pallas_call(
        functools.partial(_ffn_kernel, nf=nf, tf=tf, rc=rc, n_tiles=n_tiles, n_cast=len(cast)),
        grid=(n_tiles + 1, nf + nn),
        in_specs=[
            pl.BlockSpec((rc, D_MODEL), lambda i, j: (norm(i) * nn + col(j), 0)),
            pl.BlockSpec((tm, tn), lambda i, j: (comp(i), ccol(i, j))),
            pl.BlockSpec((None, None, N_MOD, D_MODEL),
                         lambda i, j: (layer, mod_map(norm(i)), 0, 0)),
            pl.BlockSpec((None, None, N_MOD, tn),
                         lambda i, j: (layer, mod_map(comp(i)), 0, ccol(i, j))),
            _layer_spec((1, D_MODEL), layer),
            pl.BlockSpec((D_MODEL, tf), lambda i, j: (0, ftile(i, j))),
            pl.BlockSpec((D_MODEL, tf), lambda i, j: (0, ftile(i, j))),
            pl.BlockSpec((D_FF, tn), lambda i, j: (0, ccol(i, j))),
        ] + cast_specs_in,
        out_specs=[pl.BlockSpec((tm, tn), lambda i, j: (comp(i), ccol(i, j)))] + cast_specs_out,
        out_shape=[jax.ShapeDtypeStruct((t, D_MODEL), F32)] + cast_shapes,
        scratch_shapes=[pltpu.VMEM((2, tm, D_MODEL), BF16), pltpu.VMEM((nf, tm, tf), BF16)],
        compiler_params=_cparams(2),
        name="ffn",
    )(x, x, mods, mods, g, wg, wu, wd, *cast_srcs)
    return outs[0], outs[1:]


NORM_TM = 512


def _final_norm_kernel(x_ref, g_ref, y_ref):
    y_ref[...] = _rms(x_ref[...]) * g_ref[...]


def _final_norm(x, g, tm=NORM_TM):
    t = x.shape[0]
    return pl.pallas_call(
        _final_norm_kernel,
        grid=(t // tm,),
        in_specs=[pl.BlockSpec((tm, D_MODEL), lambda i: (i, 0)),
                  pl.BlockSpec((1, D_MODEL), lambda i: (0, 0))],
        out_specs=pl.BlockSpec((tm, D_MODEL), lambda i: (i, 0)),
        out_shape=jax.ShapeDtypeStruct((t, D_MODEL), F32),
        compiler_params=_cparams(1),
        name="final_norm",
    )(x, g)


def _rope_tables(n_tokens):
    n_rows = n_tokens // GRID_W
    row = jnp.repeat(jnp.arange(n_rows, dtype=F32), GRID_W)
    col = jnp.tile(jnp.arange(GRID_W, dtype=F32), n_rows)

    def cs(dim):
        n_freq = dim // 4
        inv = ROPE_THETA ** (-jnp.arange(n_freq, dtype=F32) / n_freq)
        ang = jnp.concatenate([row[:, None] * inv, col[:, None] * inv], axis=-1)
        return jnp.cos(ang), jnp.sin(ang)

    cos_a, sin_a = cs(HEAD_DIM)
    cos_b, sin_b = cs(ROPE_B)
    one = jnp.ones_like(cos_b)
    zero = jnp.zeros_like(sin_b)
    return (jnp.concatenate([cos_a, cos_a], axis=-1),
            jnp.concatenate([-sin_a, sin_a], axis=-1),
            jnp.concatenate([cos_b, one, cos_b, one], axis=-1),
            jnp.concatenate([-sin_b, zero, sin_b, zero], axis=-1))


def kernel(x_prompt, x_sample, cache_k_a, cache_v_a, cache_ckv_b, cache_krope_b, c, c_ctx, w_ada, b_ada, norm_attn, norm_ffn, w_in, qnorm_a, knorm_a, kvnorm_b, w_uk_b, w_uv_b, w_o, w_gate, w_up, w_down, norm_final):
    batch, seq, _ = x_prompt.shape
    dec_batch, dec_seq, _ = x_sample.shape
    assert dec_batch + 1 <= MOD_ROWS

    w_in_b = _wprep(w_in.astype(BF16))
    wuk_b = w_uk_b.astype(BF16)
    wuv_b = w_uv_b.astype(BF16)
    wo_l = w_o[0].astype(BF16)
    rope_tabs = _rope_tables(dec_seq)

    cond = jnp.concatenate(
        [c_ctx[None, :], c, jnp.zeros((MOD_ROWS - 1 - dec_batch, D_MODEL), F32)], axis=0)
    mods = _adaln(cond, w_ada, b_ada).reshape(DEPTH, MOD_ROWS, N_MOD, D_MODEL)

    kc_a, vc_a, kc_b, vc_b = _cache_prep(
        cache_k_a, cache_v_a, cache_ckv_b, cache_krope_b, wuk_b, wuv_b)

    xp = x_prompt.reshape(batch * seq, D_MODEL)
    xs = x_sample.reshape(dec_batch * dec_seq, D_MODEL)
    g1 = norm_attn.reshape(DEPTH, 1, D_MODEL)
    g2 = norm_ffn.reshape(DEPTH, 1, D_MODEL)
    qn = qnorm_a.reshape(DEPTH, 1, HEAD_DIM)
    kn = knorm_a.reshape(DEPTH, 1, HEAD_DIM)
    kvn = kvnorm_b.reshape(DEPTH, 1, KV_RANK)
    ctx_mod = lambda i: 0

    def lat_mod(tm):
        return lambda i: 1 + i // (dec_seq // tm)

    r3 = lambda a: a.reshape(dec_batch, dec_seq, a.shape[-1])
    hb = ATTN_HEADS_PER_STEP
    new_bufs = None

    for l in range(DEPTH):
        first, nxt = l == 0, l + 1 < DEPTH

        (xp, *new_bufs), _ = _proj(
            xp, l, mods, ctx_mod, g1, w_in_b, qn, kn, kvn, wuk_b, wuv_b,
            new_shape=(batch, seq), new_bufs=new_bufs, w_o=wo_l)

        (qa, ka, va, qb, kb, vb), cast0 = _proj(
            xs, l, mods, lat_mod(PROJ_TM), g1, w_in_b, qn, kn, kvn, wuk_b, wuv_b,
            rope_tabs=rope_tabs, cast=((w_gate, 0),) if first else ())
        oa, cast1 = _attn_lat(r3(qa), l, kc_a, r3(ka), vc_a, r3(va),
                              1, N_HEADS_A, N_KV_A, HEAD_DIM, HEAD_DIM, "attn_lat_a", join_k=True,
                              tq=ATTN_TQ_A,
                              cast=((w_up, 0), (w_down, 0)) if first else ())
        ob, _ = _attn_lat(r3(qb), l, kc_b, r3(kb), vc_b, r3(vb),
                          N_HEADS_B // hb, hb, hb, QB_PAD, VDIM_B, "attn_lat_b", join_k=False,
                          tq=ATTN_TQ_B)
        xs = _outproj(xs, oa.reshape(-1, W_QA), ob.reshape(-1, W_VB), l, mods, lat_mod(OUT_TM), wo_l)
        if first:
            (wg_l,), (wu_l, wd_l) = cast0, cast1

        xp, gu_next = _ffn(xp, l, mods, ctx_mod, g2, wg_l, wu_l, wd_l,
                           cast=((w_gate, l + 1), (w_up, l + 1)) if nxt else ())
        xs, do_next = _ffn(xs, l, mods, lat_mod(FFN_TM), g2, wg_l, wu_l, wd_l,
                           cast=((w_down, l + 1), (w_o, l + 1)) if nxt else ())
        if nxt:
            (wg_l, wu_l), (wd_l, wo_l) = gu_next, do_next

    gf = norm_final.reshape(1, D_MODEL)
    y_prompt = _final_norm(xp, gf).reshape(batch, seq, D_MODEL)
    y_sample = _final_norm(xs, gf).reshape(dec_batch, dec_seq, D_MODEL)
    new_k, new_v, new_ckv, new_kr = new_bufs
    return (y_prompt, y_sample,
            new_k, new_v, new_ckv, new_kr)
```

```python
import functools

import jax
import jax.numpy as jnp
import numpy as np
from jax import lax
from jax.experimental import pallas as pl
from jax.experimental.pallas import tpu as pltpu

D_MODEL = 2048
DEPTH = 4
GRID_W = 64
HEAD_DIM = 128
N_HEADS_A = 8
N_KV_A = 2
GROUP_A = N_HEADS_A // N_KV_A
N_HEADS_B = 8
NOPE_B = 128
ROPE_B = 64
VDIM_B = 128
KV_RANK = 256
D_FF = 5632
ROPE_THETA = 10000.0
EPS = 1e-6

W_QA = N_HEADS_A * HEAD_DIM
W_KA = N_KV_A * HEAD_DIM
QB_DIM = NOPE_B + ROPE_B
W_QB = N_HEADS_B * QB_DIM
QB_PAD = 256
KR_PAD = 128
W_QB_PAD = N_HEADS_B * QB_PAD
W_KNOPE = N_HEADS_B * NOPE_B
W_VB = N_HEADS_B * VDIM_B
N_MOD = 6
MOD_ROWS = 8

O_QB = W_QA + 2 * W_KA
O_CKV = O_QB + W_QB
O_KR = O_CKV + KV_RANK
IN_COLS = O_KR + ROPE_B

C_QA = 0
C_KA = C_QA + W_QA
C_VA = C_KA + W_KA
C_QB = C_VA + W_KA
C_CKV = C_QB + W_QB_PAD
C_KR = C_CKV + KV_RANK
IN_COLS_PAD = C_KR + KR_PAD

LOG2_E = 1.4426950408889634
SCALE_A = HEAD_DIM ** -0.5 * LOG2_E
SCALE_B = QB_DIM ** -0.5 * LOG2_E

VMEM_LIMIT = 60 * 1024 * 1024

BF16 = jnp.bfloat16
F32 = jnp.float32


def _cparams(n_axes):
    return pltpu.CompilerParams(
        dimension_semantics=("arbitrary",) * n_axes, vmem_limit_bytes=VMEM_LIMIT)


def _layer_spec(shape, layer):
    nd = len(shape)
    return pl.BlockSpec((None,) + shape, lambda *_: (layer,) + (0,) * nd,
                        pipeline_mode=pl.Buffered(1))


def _mod_spec(layer, mod_map, width, col_map=None):
    if col_map is None:
        return pl.BlockSpec((None, None, N_MOD, width), lambda i, *_: (layer, mod_map(i), 0, 0))
    return pl.BlockSpec((None, None, N_MOD, width),
                        lambda i, j: (layer, mod_map(i), 0, col_map(j)))


def _dot(a, b):
    return jnp.dot(a, b, preferred_element_type=F32)


def _dot_t(a, b):
    return lax.dot_general(a, b, (((1,), (1,)), ((), ())), preferred_element_type=F32)


def _rms(x):
    return x * lax.rsqrt(jnp.mean(x * x, axis=-1, keepdims=True) + EPS)


def _cast_rows(n_rows, n_steps):
    rows = 16
    while n_rows % rows or n_rows // rows > n_steps:
        rows += 16
    return rows


def _side_cast(cast, n_steps, step_of):
    in_specs, out_specs, shapes = [], [], []
    for src, src_layer in cast:
        _, n_rows, n_cols = src.shape
        rows = _cast_rows(n_rows, n_steps)
        chunk = lambda *ids, last=n_rows // rows - 1: jnp.minimum(step_of(*ids), last)
        in_specs.append(pl.BlockSpec(
            (None, rows, n_cols), lambda *ids, c=chunk, l=src_layer: (l, c(*ids), 0)))
        out_specs.append(pl.BlockSpec((rows, n_cols), lambda *ids, c=chunk: (c(*ids), 0)))
        shapes.append(jax.ShapeDtypeStruct((n_rows, n_cols), BF16))
    return in_specs, out_specs, shapes, [src for src, _ in cast]


def _cast_chunks(cast_in, cast_out):
    for src, dst in zip(cast_in, cast_out):
        dst[...] = src[...].astype(BF16)


ADA_TN = 1024


def _adaln_kernel(s_ref, w_ref, b_ref, o_ref):
    s = s_ref[...]
    s = s * jax.nn.sigmoid(s)
    o_ref[...] = _dot(s.astype(BF16), w_ref[...].astype(BF16)) + b_ref[...]


def _adaln(s, w_ada, b_ada):
    n = N_MOD * D_MODEL
    return pl.pallas_call(
        _adaln_kernel,
        grid=(DEPTH, n // ADA_TN),
        in_specs=[
            pl.BlockSpec((MOD_ROWS, D_MODEL), lambda l, j: (0, 0)),
            pl.BlockSpec((None, D_MODEL, ADA_TN), lambda l, j: (l, 0, j)),
            pl.BlockSpec((None, 1, ADA_TN), lambda l, j: (l, 0, j)),
        ],
        out_specs=pl.BlockSpec((None, MOD_ROWS, ADA_TN), lambda l, j: (l, 0, j)),
        out_shape=jax.ShapeDtypeStruct((DEPTH, MOD_ROWS, n), F32),
        compiler_params=_cparams(2),
        name="adaln",
    )(s, w_ada, b_ada.reshape(DEPTH, 1, n))


WPREP_TR = 512
PAIR_IN = 2 * QB_DIM
PAIR_OUT = 2 * QB_PAD


def _wprep_kernel(w_ref, pq_ref, pk_ref, o_ref):
    o_ref[:, 0:O_QB] = w_ref[:, 0:O_QB]
    pq = pq_ref[...]
    for i in range(N_HEADS_B // 2):
        src = w_ref[:, O_QB + i * PAIR_IN:O_QB + (i + 1) * PAIR_IN]
        o_ref[:, C_QB + i * PAIR_OUT:C_QB + (i + 1) * PAIR_OUT] = _dot(src, pq).astype(BF16)
    o_ref[:, C_CKV:C_CKV + KV_RANK] = w_ref[:, O_CKV:O_CKV + KV_RANK]
    o_ref[:, C_KR:C_KR + KR_PAD] = _dot(w_ref[:, O_KR:O_KR + ROPE_B], pk_ref[...]).astype(BF16)


def _rope_pad_src():
    half = ROPE_B // 2
    src = np.full((KR_PAD,), -1, np.int32)
    src[0:half] = np.arange(half)
    src[2 * half:3 * half] = half + np.arange(half)
    return src


def _selection(src, n_in):
    return (jnp.arange(n_in, dtype=jnp.int32)[:, None] == jnp.asarray(src)[None, :]).astype(BF16)


def _wprep(w_in):
    rsrc = _rope_pad_src()
    head = np.concatenate([np.arange(NOPE_B), np.where(rsrc >= 0, NOPE_B + rsrc, -1)])
    pair = np.concatenate([head, np.where(head >= 0, QB_DIM + head, -1)]).astype(np.int32)
    pq = _selection(pair, PAIR_IN)
    pk = _selection(rsrc, ROPE_B)
    return pl.pallas_call(
        _wprep_kernel,
        grid=(DEPTH, D_MODEL // WPREP_TR),
        in_specs=[
            pl.BlockSpec((None, WPREP_TR, IN_COLS), lambda l, r: (l, r, 0)),
            pl.BlockSpec((PAIR_IN, PAIR_OUT), lambda l, r: (0, 0)),
            pl.BlockSpec((ROPE_B, KR_PAD), lambda l, r: (0, 0)),
        ],
        out_specs=pl.BlockSpec((None, WPREP_TR, IN_COLS_PAD), lambda l, r: (l, r, 0)),
        out_shape=jax.ShapeDtypeStruct((DEPTH, D_MODEL, IN_COLS_PAD), BF16),
        compiler_params=_cparams(2),
        name="wprep",
    )(w_in, pq, pk)


def _cache_kernel(ka_ref, va_ref, ckv_ref, kr_ref, wuk_ref, wuv_ref, pk_ref,
                  ka_o, va_o, kb_o, vb_o):
    ka_o[...] = ka_ref[...].astype(BF16)
    va_o[...] = va_ref[...].astype(BF16)
    ckv = ckv_ref[...].astype(BF16)
    kn = _dot(ckv, wuk_ref[...])
    vb_o[...] = _dot(ckv, wuv_ref[...]).astype(BF16)
    kr = _dot(kr_ref[...].astype(BF16), pk_ref[...]).astype(BF16)
    for h in range(N_HEADS_B):
        kb_o[:, h * QB_PAD:h * QB_PAD + NOPE_B] = kn[:, h * NOPE_B:(h + 1) * NOPE_B].astype(BF16)
        kb_o[:, h * QB_PAD + NOPE_B:(h + 1) * QB_PAD] = kr


def _cache_prep(cache_k_a, cache_v_a, cache_ckv_b, cache_krope_b, wuk, wuv):
    nb, _, past = cache_ckv_b.shape[:3]
    ka = cache_k_a.reshape(nb, DEPTH, past, W_KA)
    va = cache_v_a.reshape(nb, DEPTH, past, W_KA)
    pk = _selection(_rope_pad_src(), ROPE_B)

    def in4(width):
        return pl.BlockSpec((None, None, past, width), lambda l, b: (b, l, 0, 0))

    def out4(width):
        return pl.BlockSpec((None, None, past, width), lambda l, b: (l, b, 0, 0))

    def w3(width):
        return pl.BlockSpec((None, KV_RANK, width), lambda l, b: (l, 0, 0))

    return pl.pallas_call(
        _cache_kernel,
        grid=(DEPTH, nb),
        in_specs=[in4(W_KA), in4(W_KA), in4(KV_RANK), in4(ROPE_B), w3(W_KNOPE), w3(W_VB),
                  pl.BlockSpec((ROPE_B, KR_PAD), lambda l, b: (0, 0))],
        out_specs=[out4(W_KA), out4(W_KA), out4(W_QB_PAD), out4(W_VB)],
        out_shape=[
            jax.ShapeDtypeStruct((DEPTH, nb, past, W_KA), BF16),
            jax.ShapeDtypeStruct((DEPTH, nb, past, W_KA), BF16),
            jax.ShapeDtypeStruct((DEPTH, nb, past, W_QB_PAD), BF16),
            jax.ShapeDtypeStruct((DEPTH, nb, past, W_VB), BF16),
        ],
        compiler_params=_cparams(2),
        name="cache_prep",
    )(ka, va, cache_ckv_b, cache_krope_b, wuk, wuv, pk)


PROJ_TM = 512
PROJ_ROWS = 256
N_NEW = 4


def _proj_kernel(*refs, rope, emit_new, n_alias, n_cast):
    it = iter(refs)
    x_ref, mod_ref, g_ref, w_ref, qn_ref, kn_ref, kvn_ref, wuk_ref, wuv_ref = (
        next(it) for _ in range(9))
    if rope:
        ca_ref, sa_ref, cb_ref, sb_ref = (next(it) for _ in range(4))
    if emit_new:
        wo_ref = next(it)
    for _ in range(n_alias):
        next(it)
    cast_in = [next(it) for _ in range(n_cast)]
    if emit_new:
        xo_o, nk_o, nv_o, nckv_o, nkr_o = (next(it) for _ in range(1 + N_NEW))
        cast_out = [next(it) for _ in range(n_cast)]
    qa_o, ka_o, va_o, qb_o, kb_o, vb_o = (next(it) for _ in range(6))
    if emit_new:
        vae_s, vbe_s, oa_o, ob_o = (next(it) for _ in range(4))
    else:
        cast_out = [next(it) for _ in range(n_cast)]
    _cast_chunks(cast_in, cast_out)

    for r in range(x_ref.shape[0] // PROJ_ROWS):
        rows = slice(r * PROJ_ROWS, (r + 1) * PROJ_ROWS)

        h = _rms(x_ref[rows, :]) * g_ref[...]
        h = h * (1.0 + mod_ref[1:2, :]) + mod_ref[0:1, :]
        hb = h.astype(BF16)

        def rope_a(t):
            if not rope:
                return t
            return t * ca_ref[rows, :] + pltpu.roll(t, HEAD_DIM // 2, 1) * sa_ref[rows, :]

        def rope_b(t):
            if not rope:
                return t
            return t * cb_ref[rows, :] + pltpu.roll(t, KR_PAD // 2, 1) * sb_ref[rows, :]

        p = _dot(hb, w_ref[...])

        pq = p[:, C_QA:C_QA + W_QA]
        qn = qn_ref[...]
        for i in range(N_HEADS_A):
            sl = slice(i * HEAD_DIM, (i + 1) * HEAD_DIM)
            t = rope_a(_rms(pq[:, sl]) * qn)
            qa_o[rows, sl] = (t * SCALE_A).astype(BF16)

        pk = p[:, C_KA:C_KA + W_KA]
        pv = p[:, C_VA:C_VA + W_KA]
        kn = kn_ref[...]
        for i in range(N_KV_A):
            sl = slice(i * HEAD_DIM, (i + 1) * HEAD_DIM)
            t = _rms(pk[:, sl]) * kn
            if emit_new:
                nk_o[r, :, i, :] = t
            ka_o[rows, sl] = rope_a(t).astype(BF16)
        va_o[rows, :] = pv.astype(BF16)
        if emit_new:
            for i in range(N_KV_A):
                nv_o[r, :, i, :] = pv[:, i * HEAD_DIM:(i + 1) * HEAD_DIM]

        pqb = p[:, C_QB:C_QB + W_QB_PAD]
        for i in range(N_HEADS_B):
            s0 = i * QB_PAD
            qb_o[rows, s0:s0 + NOPE_B] = (pqb[:, s0:s0 + NOPE_B] * SCALE_B).astype(BF16)
            t = rope_b(pqb[:, s0 + NOPE_B:s0 + QB_PAD])
            qb_o[rows, s0 + NOPE_B:s0 + QB_PAD] = (t * SCALE_B).astype(BF16)

        pc = p[:, C_CKV:C_CKV + KV_RANK]
        ckv = _rms(pc) * kvn_ref[...]
        pkr = p[:, C_KR:C_KR + KR_PAD]
        if emit_new:
            half = ROPE_B // 2
            nckv_o[r] = ckv
            nkr_o[r] = jnp.concatenate([pkr[:, 0:half], pkr[:, 2 * half:3 * half]], axis=-1)
        ckvb = ckv.astype(BF16)
        krb = rope_b(pkr).astype(BF16)
        kno = _dot(ckvb, wuk_ref[...])
        vb_o[rows, :] = _dot(ckvb, wuv_ref[...]).astype(BF16)
        for i in range(N_HEADS_B):
            s0 = i * QB_PAD
            kb_o[rows, s0:s0 + NOPE_B] = kno[:, i * NOPE_B:(i + 1) * NOPE_B].astype(BF16)
            kb_o[rows, s0 + NOPE_B:s0 + QB_PAD] = krb

        if emit_new:
            _fill_v_ext(vae_s, rows, va_o.at[rows, :], N_KV_A, HEAD_DIM)
            _fill_v_ext(vbe_s, rows, vb_o.at[rows, :], N_HEADS_B, VDIM_B)
            for hq in range(N_HEADS_A):
                g = hq // GROUP_A
                qsl = slice(hq * HEAD_DIM, (hq + 1) * HEAD_DIM)
                oa_o[rows, qsl] = _attend(
                    qa_o[rows, qsl], ka_o[rows, g * HEAD_DIM:(g + 1) * HEAD_DIM],
                    vae_s[rows, 2 * g * HEAD_DIM:(2 * g + 2) * HEAD_DIM], HEAD_DIM)
            for hq in range(N_HEADS_B):
                qsl = slice(hq * QB_PAD, (hq + 1) * QB_PAD)
                ob_o[rows, hq * VDIM_B:(hq + 1) * VDIM_B] = _attend(
                    qb_o[rows, qsl], kb_o[rows, qsl],
                    vbe_s[rows, 2 * hq * VDIM_B:(2 * hq + 2) * VDIM_B], VDIM_B)
            y = _dot(oa_o[rows, :], wo_ref[0:W_QA, :]) + _dot(ob_o[rows, :], wo_ref[W_QA:, :])
            xo_o[rows, :] = x_ref[rows, :] + mod_ref[2:3, :] * y


def _proj(x, layer, mods, mod_map, g, w_in, qn, kn, kvn, wuk, wuv, rope_tabs=None,
          new_shape=None, new_bufs=None, w_o=None, cast=(), tm=PROJ_TM):
    t = x.shape[0]
    rope = rope_tabs is not None
    emit_new = new_shape is not None

    def row(width):
        return pl.BlockSpec((tm, width), lambda i: (i, 0))

    in_specs = [
        row(D_MODEL),
        _mod_spec(layer, mod_map, D_MODEL),
        _layer_spec((1, D_MODEL), layer),
        _layer_spec((D_MODEL, IN_COLS_PAD), layer),
        _layer_spec((1, HEAD_DIM), layer),
        _layer_spec((1, HEAD_DIM), layer),
        _layer_spec((1, KV_RANK), layer),
        _layer_spec((KV_RANK, W_KNOPE), layer),
        _layer_spec((KV_RANK, W_VB), layer),
    ]
    args = [x, mods, g, w_in, qn, kn, kvn, wuk, wuv]
    if rope:
        seq_tiles = rope_tabs[0].shape[0] // tm
        for tab in rope_tabs:
            in_specs.append(pl.BlockSpec((tm, tab.shape[1]), lambda i: (i % seq_tiles, 0)))
            args.append(tab)
    qkv_widths = [W_QA, W_KA, W_KA, W_QB_PAD, W_QB_PAD, W_VB]
    scratch = []
    aliases = {}
    n_alias = 0
    if not emit_new:
        out_specs = [row(w) for w in qkv_widths]
        out_shape = [jax.ShapeDtypeStruct((t, w), BF16) for w in qkv_widths]
    else:
        in_specs.append(pl.BlockSpec((W_QA + W_VB, D_MODEL), lambda i: (0, 0),
                                     pipeline_mode=pl.Buffered(1)))
        args.append(w_o)
        out_specs = [row(D_MODEL)]
        out_shape = [jax.ShapeDtypeStruct((t, D_MODEL), F32)]
        scratch = [pltpu.VMEM((tm, w), BF16)
                   for w in qkv_widths + [2 * W_KA, 2 * W_VB, W_QA, W_VB]]
        batch, seq = new_shape
        assert seq == PROJ_ROWS and t == batch * seq
        nb = tm // seq
        tails = [(N_KV_A, HEAD_DIM), (N_KV_A, HEAD_DIM), (KV_RANK,), (ROPE_B,)]
        if new_bufs is not None:
            n_alias = N_NEW
            for k, buf in enumerate(new_bufs):
                aliases[len(args)] = len(out_shape) + k
                in_specs.append(pl.BlockSpec(memory_space=pl.ANY))
                args.append(buf)
        for tail in tails:
            zeros = (0,) * len(tail)
            out_specs.append(pl.BlockSpec((nb, None, seq) + tail,
                                          lambda i, zeros=zeros: (i, layer, 0) + zeros))
            out_shape.append(jax.ShapeDtypeStruct((batch, DEPTH, seq) + tail, F32))
    cast_specs_in, cast_specs_out, cast_shapes, cast_srcs = _side_cast(
        cast, t // tm, lambda i: i)
    outs = pl.pallas_call(
        functools.partial(_proj_kernel, rope=rope, emit_new=emit_new, n_alias=n_alias,
                          n_cast=len(cast)),
        grid=(t // tm,),
        in_specs=in_specs + cast_specs_in,
        out_specs=out_specs + cast_specs_out,
        out_shape=out_shape + cast_shapes,
        input_output_aliases=aliases,
        scratch_shapes=scratch,
        compiler_params=_cparams(1),
        name="proj_lat" if rope else "proj_attn_ctx",
    )(*args, *cast_srcs)
    n_main = len(out_shape)
    return outs[:n_main], outs[n_main:]


def _attend(q, k, v_ext, dv):
    s = _dot_t(q, k)
    p = jnp.exp2(s - s.max(axis=-1, keepdims=True))
    o = _dot(p.astype(BF16), v_ext)
    return (o[:, 0:dv] / o[:, dv:2 * dv]).astype(BF16)


def _attend_parts(q, k_parts, v_ext_ref, dv):
    s = [_dot_t(q, k) for k in k_parts]
    m = s[0].max(axis=-1, keepdims=True)
    for sp in s[1:]:
        m = jnp.maximum(m, sp.max(axis=-1, keepdims=True))
    o = None
    r0 = 0
    for sp in s:
        r1 = r0 + sp.shape[1]
        op = _dot(jnp.exp2(sp - m).astype(BF16), v_ext_ref[r0:r1, :])
        o = op if o is None else o + op
        r0 = r1
    return (o[:, 0:dv] / o[:, dv:2 * dv]).astype(BF16)


def _fill_v_ext(v_ext_ref, rows, v_ref, n_kv, dv):
    ones = jnp.ones((v_ref.shape[0], dv), BF16)
    for jk in range(n_kv):
        v_ext_ref[rows, 2 * jk * dv:(2 * jk + 1) * dv] = v_ref[:, jk * dv:(jk + 1) * dv]
        v_ext_ref[rows, (2 * jk + 1) * dv:(2 * jk + 2) * dv] = ones


ATTN_TQ_A = 1024
ATTN_TQ_B = 512
ATTN_HEADS_PER_STEP = 8
ATTN_ROWS = 256


def _attn_lat_kernel(*refs, n_q, n_kv, dq, dv, past, n_cast):
    q_ref, kc_ref, kn_ref, vc_ref, vn_ref = refs[:5]
    o_ref = refs[5 + n_cast]
    v_ref = refs[6 + 2 * n_cast]
    k_scratch = refs[7 + 2 * n_cast:]
    _attn_lat_body(q_ref, kc_ref, kn_ref, vc_ref, vn_ref, o_ref, v_ref, k_scratch,
                   n_q=n_q, n_kv=n_kv, dq=dq, dv=dv, past=past)
    _cast_chunks(refs[5:5 + n_cast], refs[6 + n_cast:6 + 2 * n_cast])


def _attn_lat_body(q_ref, kc_ref, kn_ref, vc_ref, vn_ref, o_ref, v_ref, k_scratch,
                   *, n_q, n_kv, dq, dv, past):
    @pl.when(pl.program_id(2) == 0)
    def _():
        _fill_v_ext(v_ref, slice(0, past), vc_ref, n_kv, dv)
        _fill_v_ext(v_ref, slice(past, None), vn_ref, n_kv, dv)
        if k_scratch:
            k_scratch[0][0:past, :] = kc_ref[...]
            k_scratch[0][past:, :] = kn_ref[...]

    rows = ATTN_ROWS
    group = n_q // n_kv
    if k_scratch and group > 1:
        for jk in range(n_kv):
            ksl = slice(jk * dq, (jk + 1) * dq)
            vsl = slice(2 * jk * dv, (2 * jk + 2) * dv)
            heads = range(jk * group, (jk + 1) * group)
            for r in range(q_ref.shape[0] // rows):
                rsl = slice(r * rows, (r + 1) * rows)
                q = jnp.concatenate([q_ref[rsl, j * dq:(j + 1) * dq] for j in heads], axis=0)
                o = _attend(q, k_scratch[0][:, ksl], v_ref[:, vsl], dv)
                for n, j in enumerate(heads):
                    o_ref[rsl, j * dv:(j + 1) * dv] = o[n * rows:(n + 1) * rows, :]
        return
    for j in range(n_q):
        jk = j * n_kv // n_q
        ksl = slice(jk * dq, (jk + 1) * dq)
        vsl = slice(2 * jk * dv, (2 * jk + 2) * dv)
        for r in range(q_ref.shape[0] // rows):
            rsl = slice(r * rows, (r + 1) * rows)
            q = q_ref[rsl, j * dq:(j + 1) * dq]
            if k_scratch:
                o = _attend(q, k_scratch[0][:, ksl], v_ref[:, vsl], dv)
            else:
                o = _attend_parts(q, [kc_ref[:, ksl], kn_ref[:, ksl]], v_ref.at[:, vsl], dv)
            o_ref[rsl, j * dv:(j + 1) * dv] = o


def _attn_lat(q, layer, kc, kn, vc, vn, n_units, n_q, n_kv, dq, dv, name, join_k, tq, cast=()):
    nb, seq, _ = q.shape
    past = kc.shape[2]
    scratch = [pltpu.VMEM((past + seq, n_kv * 2 * dv), BF16)]
    if join_k:
        scratch.append(pltpu.VMEM((past + seq, n_kv * dq), BF16))
    n_qt = seq // tq
    cast_specs_in, cast_specs_out, cast_shapes, cast_srcs = _side_cast(
        cast, nb * n_units * n_qt, lambda b, u, i: (b * n_units + u) * n_qt + i)
    outs = pl.pallas_call(
        functools.partial(_attn_lat_kernel, n_q=n_q, n_kv=n_kv, dq=dq, dv=dv, past=past,
                          n_cast=len(cast)),
        grid=(nb, n_units, n_qt),
        in_specs=[
            pl.BlockSpec((None, tq, n_q * dq), lambda b, u, i: (b, i, u)),
            pl.BlockSpec((None, None, past, n_kv * dq), lambda b, u, i: (layer, b, 0, u)),
            pl.BlockSpec((None, seq, n_kv * dq), lambda b, u, i: (b, 0, u)),
            pl.BlockSpec((None, None, past, n_kv * dv), lambda b, u, i: (layer, b, 0, u)),
            pl.BlockSpec((None, seq, n_kv * dv), lambda b, u, i: (b, 0, u)),
        ] + cast_specs_in,
        out_specs=[pl.BlockSpec((None, tq, n_q * dv), lambda b, u, i: (b, i, u))] + cast_specs_out,
        out_shape=[jax.ShapeDtypeStruct((nb, seq, n_units * n_q * dv), BF16)] + cast_shapes,
        scratch_shapes=scratch,
        compiler_params=_cparams(3),
        name=name,
    )(q, kc, kn, vc, vn, *cast_srcs)
    return outs[0], outs[1:]


OUT_TM = 512


def _outproj_kernel(x_ref, oa_ref, ob_ref, mod_ref, w_ref, y_ref):
    y = _dot(oa_ref[...], w_ref[0:W_QA, :]) + _dot(ob_ref[...], w_ref[W_QA:, :])
    y_ref[...] = x_ref[...] + mod_ref[2:3, :] * y


def _outproj(x, oa, ob, layer, mods, mod_map, w_o, tm=OUT_TM):
    t = x.shape[0]
    return pl.pallas_call(
        _outproj_kernel,
        grid=(t // tm,),
        in_specs=[
            pl.BlockSpec((tm, D_MODEL), lambda i: (i, 0)),
            pl.BlockSpec((tm, W_QA), lambda i: (i, 0)),
            pl.BlockSpec((tm, W_VB), lambda i: (i, 0)),
            _mod_spec(layer, mod_map, D_MODEL),
            pl.BlockSpec((W_QA + W_VB, D_MODEL), lambda i: (0, 0), pipeline_mode=pl.Buffered(1)),
        ],
        out_specs=pl.BlockSpec((tm, D_MODEL), lambda i: (i, 0)),
        out_shape=jax.ShapeDtypeStruct((t, D_MODEL), F32),
        compiler_params=_cparams(1),
        name="outproj",
    )(x, oa, ob, mods, w_o)


FFN_TM = 1024
FFN_TF = 512
FFN_TN = 512
FFN_ROWS = 256


def _ffn_kernel(*refs, nf, tf, rc, n_tiles, n_cast):
    xr_ref, xc_ref, mod_ref, modc_ref, g_ref, wg_ref, wu_ref, wd_ref = refs[:8]
    cast_in = refs[8:8 + n_cast]
    y_ref = refs[8 + n_cast]
    cast_out = refs[9 + n_cast:9 + 2 * n_cast]
    h_ref, t_ref = refs[9 + 2 * n_cast:]
    i = pl.program_id(0)
    j = pl.program_id(1)

    def cast_chunks():
        _cast_chunks(cast_in, cast_out)
    cur = (i + 1) % 2
    nxt = i % 2

    def norm_chunk():
        h = _rms(xr_ref[...]) * g_ref[...]
        h = h * (1.0 + mod_ref[4:5, :]) + mod_ref[3:4, :]
        r0 = pl.multiple_of((j - nf) * rc, rc)
        h_ref[nxt, pl.ds(r0, rc), :] = h.astype(BF16)

    row_blocks = [slice(m, m + FFN_ROWS) for m in range(0, y_ref.shape[0], FFN_ROWS)]

    def down_proj():
        for rows in row_blocks:
            d = _dot(t_ref[0, rows, :], wd_ref[0:tf, :])
            for k in range(1, nf):
                d = d + _dot(t_ref[k, rows, :], wd_ref[k * tf:(k + 1) * tf, :])
            y_ref[rows, :] = xc_ref[rows, :] + modc_ref[5:6, :] * d

    @pl.when((j < nf) & (i == 0))
    def _():
        cast_chunks()

    @pl.when((j < nf) & (i >= 1))
    def _():
        cast_chunks()
        for rows in row_blocks:
            hb = h_ref[cur, rows, :]
            a = _dot(hb, wg_ref[...])
            b = _dot(hb, wu_ref[...])
            t_ref[j, rows, :] = (a * jax.nn.sigmoid(a) * b).astype(BF16)

    @pl.when((j >= nf) & (i >= 1) & (i < n_tiles))
    def _():
        cast_chunks()
        norm_chunk()
        down_proj()

    @pl.when((j >= nf) & (i == 0))
    def _():
        cast_chunks()
        norm_chunk()

    @pl.when((j >= nf) & (i == n_tiles))
    def _():
        cast_chunks()
        down_proj()


def _ffn(x, layer, mods, mod_map, g, wg, wu, wd, cast=(), tm=FFN_TM, tf=FFN_TF, tn=FFN_TN):
    t = x.shape[0]
    n_tiles = t // tm
    nf = D_FF // tf
    nn = D_MODEL // tn
    cast_specs_in, cast_specs_out, cast_shapes, cast_srcs = _side_cast(
        cast, (n_tiles + 1) * (nf + nn), lambda i, j: i * (nf + nn) + j)
    rc = tm // nn
    col = lambda j: jnp.maximum(j - nf, 0)
    comp = lambda i: jnp.maximum(i - 1, 0)
    norm = lambda i: jnp.minimum(i, n_tiles - 1)
    ccol = lambda i, j: jnp.where(i == 0, 0, col(j))
    ftile = lambda i, j: jnp.where(i == 0, 0, jnp.minimum(j, nf - 1))
    outs = pl.pallas_call(
        functools.partial(_ffn_kernel, nf=nf, tf=tf, rc=rc, n_tiles=n_tiles, n_cast=len(cast)),
        grid=(n_tiles + 1, nf + nn),
        in_specs=[
            pl.BlockSpec((rc, D_MODEL), lambda i, j: (norm(i) * nn + col(j), 0)),
            pl.BlockSpec((tm, tn), lambda i, j: (comp(i), ccol(i, j))),
            pl.BlockSpec((None, None, N_MOD, D_MODEL),
                         lambda i, j: (layer, mod_map(norm(i)), 0, 0)),
            pl.BlockSpec((None, None, N_MOD, tn),
                         lambda i, j: (layer, mod_map(comp(i)), 0, ccol(i, j))),
            _layer_spec((1, D_MODEL), layer),
            pl.BlockSpec((D_MODEL, tf), lambda i, j: (0, ftile(i, j))),
            pl.BlockSpec((D_MODEL, tf), lambda i, j: (0, ftile(i, j))),
            pl.BlockSpec((D_FF, tn), lambda i, j: (0, ccol(i, j))),
        ] + cast_specs_in,
        out_specs=[pl.BlockSpec((tm, tn), lambda i, j: (comp(i), ccol(i, j)))] + cast_specs_out,
        out_shape=[jax.ShapeDtypeStruct((t, D_MODEL), F32)] + cast_shapes,
        scratch_shapes=[pltpu.VMEM((2, tm, D_MODEL), BF16), pltpu.VMEM((nf, tm, tf), BF16)],
        compiler_params=_cparams(2),
        name="ffn",
    )(x, x, mods, mods, g, wg, wu, wd, *cast_srcs)
    return outs[0], outs[1:]


NORM_TM = 512


def _final_norm_kernel(x_ref, g_ref, y_ref):
    y_ref[...] = _rms(x_ref[...]) * g_ref[...]


def _final_norm(x, g, tm=NORM_TM):
    t = x.shape[0]
    return pl.pallas_call(
        _final_norm_kernel,
        grid=(t // tm,),
        in_specs=[pl.BlockSpec((tm, D_MODEL), lambda i: (i, 0)),
                  pl.BlockSpec((1, D_MODEL), lambda i: (0, 0))],
        out_specs=pl.BlockSpec((tm, D_MODEL), lambda i: (i, 0)),
        out_shape=jax.ShapeDtypeStruct((t, D_MODEL), F32),
        compiler_params=_cparams(1),
        name="final_norm",
    )(x, g)


def _rope_tables(n_tokens):
    n_rows = n_tokens // GRID_W
    row = jnp.repeat(jnp.arange(n_rows, dtype=F32), GRID_W)
    col = jnp.tile(jnp.arange(GRID_W, dtype=F32), n_rows)

    def cs(dim):
        n_freq = dim // 4
        inv = ROPE_THETA ** (-jnp.arange(n_freq, dtype=F32) / n_freq)
        ang = jnp.concatenate([row[:, None] * inv, col[:, None] * inv], axis=-1)
        return jnp.cos(ang), jnp.sin(ang)

    cos_a, sin_a = cs(HEAD_DIM)
    cos_b, sin_b = cs(ROPE_B)
    one = jnp.ones_like(cos_b)
    zero = jnp.zeros_like(sin_b)
    return (jnp.concatenate([cos_a, cos_a], axis=-1),
            jnp.concatenate([-sin_a, sin_a], axis=-1),
            jnp.concatenate([cos_b, one, cos_b, one], axis=-1),
            jnp.concatenate([-sin_b, zero, sin_b, zero], axis=-1))


def kernel(x_prompt, x_sample, cache_k_a, cache_v_a, cache_ckv_b, cache_krope_b, c, c_ctx, w_ada, b_ada, norm_attn, norm_ffn, w_in, qnorm_a, knorm_a, kvnorm_b, w_uk_b, w_uv_b, w_o, w_gate, w_up, w_down, norm_final):
    batch, seq, _ = x_prompt.shape
    dec_batch, dec_seq, _ = x_sample.shape
    assert dec_batch + 1 <= MOD_ROWS

    w_in_b = _wprep(w_in.astype(BF16))
    wuk_b = w_uk_b.astype(BF16)
    wuv_b = w_uv_b.astype(BF16)
    wo_l = w_o[0].astype(BF16)
    rope_tabs = _rope_tables(dec_seq)

    cond = jnp.concatenate(
        [c_ctx[None, :], c, jnp.zeros((MOD_ROWS - 1 - dec_batch, D_MODEL), F32)], axis=0)
    mods = _adaln(cond, w_ada, b_ada).reshape(DEPTH, MOD_ROWS, N_MOD, D_MODEL)

    kc_a, vc_a, kc_b, vc_b = _cache_prep(
        cache_k_a, cache_v_a, cache_ckv_b, cache_krope_b, wuk_b, wuv_b)

    xp = x_prompt.reshape(batch * seq, D_MODEL)
    xs = x_sample.reshape(dec_batch * dec_seq, D_MODEL)
    g1 = norm_attn.reshape(DEPTH, 1, D_MODEL)
    g2 = norm_ffn.reshape(DEPTH, 1, D_MODEL)
    qn = qnorm_a.reshape(DEPTH, 1, HEAD_DIM)
    kn = knorm_a.reshape(DEPTH, 1, HEAD_DIM)
    kvn = kvnorm_b.reshape(DEPTH, 1, KV_RANK)
    ctx_mod = lambda i: 0

    def lat_mod(tm):
        return lambda i: 1 + i // (dec_seq // tm)

    r3 = lambda a: a.reshape(dec_batch, dec_seq, a.shape[-1])
    hb = ATTN_HEADS_PER_STEP
    new_bufs = None

    for l in range(DEPTH):
        first, nxt = l == 0, l + 1 < DEPTH

        (xp, *new_bufs), _ = _proj(
            xp, l, mods, ctx_mod, g1, w_in_b, qn, kn, kvn, wuk_b, wuv_b,
            new_shape=(batch, seq), new_bufs=new_bufs, w_o=wo_l)

        (qa, ka, va, qb, kb, vb), cast0 = _proj(
            xs, l, mods, lat_mod(PROJ_TM), g1, w_in_b, qn, kn, kvn, wuk_b, wuv_b,
            rope_tabs=rope_tabs, cast=((w_gate, 0),) if first else ())
        oa, cast1 = _attn_lat(r3(qa), l, kc_a, r3(ka), vc_a, r3(va),
                              1, N_HEADS_A, N_KV_A, HEAD_DIM, HEAD_DIM, "attn_lat_a", join_k=True,
                              tq=ATTN_TQ_A,
                              cast=((w_up, 0), (w_down, 0)) if first else ())
        ob, _ = _attn_lat(r3(qb), l, kc_b, r3(kb), vc_b, r3(vb),
                          N_HEADS_B // hb, hb, hb, QB_PAD, VDIM_B, "attn_lat_b", join_k=False,
                          tq=ATTN_TQ_B)
        xs = _outproj(xs, oa.reshape(-1, W_QA), ob.reshape(-1, W_VB), l, mods, lat_mod(OUT_TM), wo_l)
        if first:
            (wg_l,), (wu_l, wd_l) = cast0, cast1

        xp, gu_next = _ffn(xp, l, mods, ctx_mod, g2, wg_l, wu_l, wd_l,
                           cast=((w_gate, l + 1), (w_up, l + 1)) if nxt else ())
        xs, do_next = _ffn(xs, l, mods, lat_mod(FFN_TM), g2, wg_l, wu_l, wd_l,
                           cast=((w_down, l + 1), (w_o, l + 1)) if nxt else ())
        if nxt:
            (wg_l, wu_l), (wd_l, wo_l) = gu_next, do_next

    gf = norm_final.reshape(1, D_MODEL)
    y_prompt = _final_norm(xp, gf).reshape(batch, seq, D_MODEL)
    y_sample = _final_norm(xs, gf).reshape(dec_batch, dec_seq, D_MODEL)
    new_k, new_v, new_ckv, new_kr = new_bufs
    return (y_prompt, y_sample,
            new_k, new_v, new_ckv, new_kr)
```
